```python
import math, functools
import jax, jax.numpy as jnp
from jax import lax
import numpy as np

D_MODEL = 1024
BATCH = 8
SEQ = 2048
DEPTH = 1
DEC_BATCH = 128
DEC_SEQ = 1
PAST_LEN = 2048
PAGE_SIZE = 128

HEAD_DIM = 64
N_DIFF_HEADS = 4
N_FOX_HEADS = 8
DIFF_QK_W = N_DIFF_HEADS * 2 * HEAD_DIM
DIFF_V_W = N_DIFF_HEADS * 2 * HEAD_DIM
FOX_W = N_FOX_HEADS * HEAD_DIM
N_BRANCHES = 2
IN_COLS = 2 * DIFF_QK_W + DIFF_V_W + 3 * FOX_W + N_FOX_HEADS + N_BRANCHES * D_MODEL
D_FF = -(-8 * D_MODEL // (3 * 256)) * 256
N_BUCKETS = 32
MAX_DISTANCE = 128
Q_BLOCK = 128
EPS = 1e-6
ATTN_SCALE = HEAD_DIM ** -0.5
NEG_INF = -1e30

kernel_name = 'hybrid_diff_fox_decoder_step'


def rms_norm(x, g):
    xf = x.astype(jnp.float32)
    y = xf * lax.rsqrt(jnp.mean(xf * xf, axis=-1, keepdims=True) + EPS)
    return (y * g.astype(jnp.float32)).astype(x.dtype)


def rel_bucket(q_pos, k_pos):
    n = jnp.maximum(q_pos[:, None] - k_pos[None, :], 0)
    max_exact = N_BUCKETS // 2
    nf = jnp.maximum(n, 1).astype(jnp.float32)
    large = max_exact + (jnp.log(nf / max_exact) / math.log(MAX_DISTANCE / max_exact)
                         * (N_BUCKETS - max_exact)).astype(jnp.int32)
    return jnp.where(n < max_exact, n, jnp.minimum(large, N_BUCKETS - 1))


def lambda_value(lam_p, lambda_init):
    lp = lam_p.astype(jnp.float32)
    return jnp.exp(jnp.sum(lp[0] * lp[1])) - jnp.exp(jnp.sum(lp[2] * lp[3])) + lambda_init


def diff_attend(q, k, v, q_pos, k_pos, rel_bias, lam):
    s = jnp.einsum('bqhmd,bkhmd->bmhqk', q, k, preferred_element_type=jnp.float32) * ATTN_SCALE
    bias = jnp.transpose(rel_bias[rel_bucket(q_pos, k_pos)], (2, 0, 1)).astype(jnp.float32)
    mask = k_pos[None, :] <= q_pos[:, None]
    p = jax.nn.softmax(jnp.where(mask, s + bias, NEG_INF), axis=-1)
    a = p[:, 0] - lam * p[:, 1]
    return jnp.einsum('bhqk,bkhe->bqhe', a.astype(v.dtype), v)


def fox_attend(q, k, v, cq, ck, q_pos, k_pos):
    s = jnp.einsum('bqhd,bkhd->bhqk', q, k, preferred_element_type=jnp.float32) * ATTN_SCALE
    decay = jnp.swapaxes(cq, 1, 2)[:, :, :, None] - jnp.swapaxes(ck, 1, 2)[:, :, None, :]
    mask = k_pos[None, :] <= q_pos[:, None]
    p = jax.nn.softmax(jnp.where(mask, s + decay, NEG_INF), axis=-1)
    return jnp.einsum('bhqk,bkhe->bqhe', p.astype(v.dtype), v)


def project_inputs(h, w_in, diff_q_g, diff_k_g, fox_q_g, fox_k_g, fox_f_b, gate_b):
    B, T, _ = h.shape
    z = jnp.einsum('btd,dc->btc', h, w_in)
    o = [int(v) for v in np.cumsum([0, DIFF_QK_W, DIFF_QK_W, DIFF_V_W, FOX_W, FOX_W, FOX_W,
                                    N_FOX_HEADS, N_BRANCHES * D_MODEL])]
    dq = z[..., o[0]:o[1]].reshape(B, T, N_DIFF_HEADS, 2, HEAD_DIM)
    dk = z[..., o[1]:o[2]].reshape(B, T, N_DIFF_HEADS, 2, HEAD_DIM)
    dv = z[..., o[2]:o[3]].reshape(B, T, N_DIFF_HEADS, 2 * HEAD_DIM)
    fq = z[..., o[3]:o[4]].reshape(B, T, N_FOX_HEADS, HEAD_DIM)
    fk = z[..., o[4]:o[5]].reshape(B, T, N_FOX_HEADS, HEAD_DIM)
    fv = z[..., o[5]:o[6]].reshape(B, T, N_FOX_HEADS, HEAD_DIM)
    logf = jax.nn.log_sigmoid((z[..., o[6]:o[7]] + fox_f_b).astype(jnp.float32))
    gates = jax.nn.sigmoid(z[..., o[7]:o[8]].reshape(B, T, N_BRANCHES, D_MODEL) + gate_b)
    return (rms_norm(dq, diff_q_g), rms_norm(dk, diff_k_g), dv,
            rms_norm(fq, fox_q_g), rms_norm(fk, fox_k_g), fv, logf, gates)


def attend_prompt(dq, dk, dv, fq, fk, fv, logf, lam, rel_bias):
    B, S = dq.shape[:2]
    nblk = S // Q_BLOCK
    pos = jnp.arange(S, dtype=jnp.int32)
    c = jnp.cumsum(logf, axis=1)

    def to_blocks(a):
        return jnp.swapaxes(a.reshape((B, nblk, Q_BLOCK) + a.shape[2:]), 0, 1)

    def from_blocks(a):
        return jnp.swapaxes(a, 0, 1).reshape((B, S) + a.shape[3:])

    def block(args):
        qd, qf, cq, qpos = args
        return (diff_attend(qd, dk, dv, qpos, pos, rel_bias, lam),
                fox_attend(qf, fk, fv, cq, c, qpos, pos))

    od, of = lax.map(block, (to_blocks(dq), to_blocks(fq), to_blocks(c), pos.reshape(nblk, Q_BLOCK)))
    return from_blocks(od), from_blocks(of)


def attend_sample(past_dk, past_dv, past_fk, past_fv, past_logf,
                  dq, dk, dv, fq, fk, fv, logf, lam, rel_bias):
    P, T = past_dk.shape[1], dq.shape[1]
    kd = jnp.concatenate([past_dk.astype(dk.dtype), dk], axis=1)
    vd = jnp.concatenate([past_dv.astype(dv.dtype), dv], axis=1)
    kf = jnp.concatenate([past_fk.astype(fk.dtype), fk], axis=1)
    vf = jnp.concatenate([past_fv.astype(fv.dtype), fv], axis=1)
    c = jnp.cumsum(jnp.concatenate([past_logf.astype(jnp.float32), logf], axis=1), axis=1)
    kpos = jnp.arange(P + T, dtype=jnp.int32)
    qpos = P + jnp.arange(T, dtype=jnp.int32)
    od = diff_attend(dq, kd, vd, qpos, kpos, rel_bias, lam)
    of = fox_attend(fq, kf, vf, c[:, P:], c, qpos, kpos)
    return od, of


def gather_pages(cache, page_table):
    g = jnp.take(cache, page_table, axis=0)
    return g.reshape((page_table.shape[0], page_table.shape[1] * cache.shape[1]) + cache.shape[2:])


def swiglu(h, w_gate_up, w_down):
    gu = h @ w_gate_up
    return (jax.nn.silu(gu[..., :D_FF]) * gu[..., D_FF:]) @ w_down


def block_forward(x, attend, lw, rel_bias, lambda_init):
    (norm1_g, w_in, dqg, dkg, fqg, fkg, lam_p, fox_f_b, gate_b, subln_g,
     w_a, w_b, w_out, norm2_g, w_gu, w_dn) = lw
    B, T = x.shape[:2]
    h = rms_norm(x, norm1_g)
    dq, dk, dv, fq, fk, fv, logf, gates = project_inputs(h, w_in, dqg, dkg, fqg, fkg, fox_f_b, gate_b)
    lam = lambda_value(lam_p, lambda_init)
    od, of = attend(dq, dk, dv, fq, fk, fv, logf, lam, rel_bias)
    od = rms_norm(od, subln_g) * (1.0 - lambda_init)
    ya = od.reshape(B, T, DIFF_V_W) @ w_a
    yb = of.reshape(B, T, FOX_W) @ w_b
    x = x + (gates[:, :, 0] * ya + gates[:, :, 1] * yb) @ w_out
    x = x + swiglu(rms_norm(x, norm2_g), w_gu, w_dn)
    return x, (dk, dv, fk, fv, logf)


def stack_rows(rows, i):
    return jnp.stack([r[i] for r in rows], axis=0)


def setup_inputs(seed: int = 0) -> dict:
    key = jax.random.key(seed)
    ks = jax.random.split(key, 32)
    n_pages = PAST_LEN // PAGE_SIZE
    n_used = DEC_BATCH * n_pages
    n_phys = n_used + max(1, n_used // 4)

    def nrm(k, shape, s):
        return s * jax.random.normal(k, shape, jnp.float32)

    def gain(k, shape):
        return 1.0 + nrm(k, shape, 0.02)

    pool = (DEPTH, n_phys, PAGE_SIZE)
    return {
        'x_prompt': nrm(ks[0], (BATCH, SEQ, D_MODEL), 1.0),
        'x_sample': nrm(ks[1], (DEC_BATCH, DEC_SEQ, D_MODEL), 1.0),
        'cache_diff_k': nrm(ks[2], pool + (N_DIFF_HEADS, 2, HEAD_DIM), 1.0),
        'cache_diff_v': nrm(ks[3], pool + (N_DIFF_HEADS, 2 * HEAD_DIM), 1.0),
        'cache_fox_k': nrm(ks[4], pool + (N_FOX_HEADS, HEAD_DIM), 1.0),
        'cache_fox_v': nrm(ks[5], pool + (N_FOX_HEADS, HEAD_DIM), 1.0),
        'cache_fox_logf': jax.nn.log_sigmoid(2.0 + nrm(ks[6], pool + (N_FOX_HEADS,), 0.5)),
        'page_table': jax.random.permutation(ks[7], n_phys)[:n_used].reshape(DEC_BATCH, n_pages).astype(jnp.int32),
        'rel_bias': nrm(ks[8], (N_BUCKETS, N_DIFF_HEADS), 0.5),
        'norm1_g': gain(ks[9], (DEPTH, D_MODEL)),
        'w_in': nrm(ks[10], (DEPTH, D_MODEL, IN_COLS), D_MODEL ** -0.5),
        'diff_q_g': gain(ks[11], (DEPTH, HEAD_DIM)),
        'diff_k_g': gain(ks[12], (DEPTH, HEAD_DIM)),
        'fox_q_g': gain(ks[13], (DEPTH, HEAD_DIM)),
        'fox_k_g': gain(ks[14], (DEPTH, HEAD_DIM)),
        'diff_lambda': nrm(ks[15], (DEPTH, 4, HEAD_DIM), 0.1),
        'fox_f_b': 2.0 + nrm(ks[16], (DEPTH, N_FOX_HEADS), 0.5),
        'gate_b': nrm(ks[17], (DEPTH, N_BRANCHES, D_MODEL), 0.1),
        'diff_subln_g': gain(ks[18], (DEPTH, 2 * HEAD_DIM)),
        'w_branch_a': nrm(ks[19], (DEPTH, DIFF_V_W, D_MODEL), DIFF_V_W ** -0.5),
        'w_branch_b': nrm(ks[20], (DEPTH, FOX_W, D_MODEL), FOX_W ** -0.5),
        'w_out': nrm(ks[21], (DEPTH, D_MODEL, D_MODEL), D_MODEL ** -0.5),
        'norm2_g': gain(ks[22], (DEPTH, D_MODEL)),
        'w_gate_up': nrm(ks[23], (DEPTH, D_MODEL, 2 * D_FF), D_MODEL ** -0.5),
        'w_down': nrm(ks[24], (DEPTH, D_FF, D_MODEL), D_FF ** -0.5),
    }


def reference(x_prompt, x_sample, cache_diff_k, cache_diff_v, cache_fox_k, cache_fox_v, cache_fox_logf,
              page_table, rel_bias, norm1_g, w_in, diff_q_g, diff_k_g, fox_q_g, fox_k_g, diff_lambda,
              fox_f_b, gate_b, diff_subln_g, w_branch_a, w_branch_b, w_out, norm2_g, w_gate_up, w_down):
    yp, ys = x_prompt, x_sample
    rows_p, rows_s = [], []
    for l in range(DEPTH):
        lambda_init = 0.8 - 0.6 * math.exp(-0.3 * l)
        lw = (norm1_g[l], w_in[l], diff_q_g[l], diff_k_g[l], fox_q_g[l], fox_k_g[l], diff_lambda[l],
              fox_f_b[l], gate_b[l], diff_subln_g[l], w_branch_a[l], w_branch_b[l], w_out[l],
              norm2_g[l], w_gate_up[l], w_down[l])
        yp, rp = block_forward(yp, attend_prompt, lw, rel_bias, lambda_init)
        samp = functools.partial(attend_sample,
                                 gather_pages(cache_diff_k[l], page_table),
                                 gather_pages(cache_diff_v[l], page_table),
                                 gather_pages(cache_fox_k[l], page_table),
                                 gather_pages(cache_fox_v[l], page_table),
                                 gather_pages(cache_fox_logf[l], page_table))
        ys, rs = block_forward(ys, samp, lw, rel_bias, lambda_init)
        rows_p.append(rp)
        rows_s.append(rs)
    return (yp, ys,
            stack_rows(rows_p, 0), stack_rows(rows_p, 1), stack_rows(rows_p, 2), stack_rows(rows_p, 3), stack_rows(rows_p, 4),
            stack_rows(rows_s, 0), stack_rows(rows_s, 1), stack_rows(rows_s, 2), stack_rows(rows_s, 3), stack_rows(rows_s, 4))
```

```python
import functools
import math

import numpy as np
import jax
import jax.numpy as jnp
from jax import lax
from jax.experimental import pallas as pl
from jax.experimental.pallas import tpu as pltpu

F32 = jnp.float32
BF16 = jnp.bfloat16

N_BUCKETS = 32
MAX_DISTANCE = 128
EPS = 1e-6
NEG_INF = -1e30

LANES = 128
MXU_DIM = 256
VMEM_LIMIT = 56 * 1024 * 1024

ROW_TILE = 512
FFN_ROW_TILE = 256
ATTN_TILE = 256
PAGES_PER_STEP = 8

NT_DIMS = (((1,), (1,)), ((), ()))


def _resident(shape):
    zeros = (0,) * len(shape)
    return pl.BlockSpec(shape, lambda *_: zeros, pipeline_mode=pl.Buffered(1))


def _params(semantics, vmem=VMEM_LIMIT):
    return pltpu.CompilerParams(dimension_semantics=semantics, vmem_limit_bytes=vmem)


def _rms(x, g):
    return x * lax.rsqrt(jnp.mean(x * x, axis=-1, keepdims=True) + EPS) * g


def _dot(a, b):
    return jnp.dot(a, b, preferred_element_type=F32)


def _dot_nt(a, b, precision=None):
    return lax.dot_general(a, b, NT_DIMS, preferred_element_type=F32, precision=precision)


def _log_sigmoid(x):
    return jnp.minimum(x, 0.0) - jnp.log1p(jnp.exp(-jnp.abs(x)))


def _rel_bucket(n):
    max_exact = N_BUCKETS // 2
    nf = jnp.maximum(n, 1).astype(F32)
    large = max_exact + (jnp.log(nf / max_exact) / math.log(MAX_DISTANCE / max_exact)
                         * (N_BUCKETS - max_exact)).astype(jnp.int32)
    return jnp.where(n < max_exact, n, jnp.minimum(large, N_BUCKETS - 1))


def _lambda_value(lam_ref, lambda_init):
    lp = lam_ref[...]
    a = jnp.sum(lp[0:1] * lp[1:2], axis=-1, keepdims=True)
    b = jnp.sum(lp[2:3] * lp[3:4], axis=-1, keepdims=True)
    return jnp.exp(a) - jnp.exp(b) + lambda_init


def _proj_kernel(x_ref, g1_ref, wqkv_ref, wf_ref, gains_ref, fb_ref, gsum_ref,
                 dq_ref, dk_ref, dv_ref, fq_ref, fk_ref, fv_ref, logf_ref, *, head_dim, scale):
    h = _rms(x_ref[...], g1_ref[...]).astype(BF16)
    width = dq_ref.shape[-1]
    gsum = gsum_ref[...]
    half = gsum.shape[0]

    def section(i):
        return _dot(h, wqkv_ref[:, i * width:(i + 1) * width])

    def head_norm(z, gain_row):
        zz = (z * z).astype(BF16)
        ss = jnp.concatenate([_dot(zz[:, c:c + half], gsum) for c in range(0, width, half)], axis=1)
        return z * lax.rsqrt(ss * (1.0 / head_dim) + EPS) * gains_ref[gain_row:gain_row + 1, :]

    dq_ref[...] = (head_norm(section(0), 0) * scale).astype(dq_ref.dtype)
    dk_ref[...] = head_norm(section(1), 1)
    dv_ref[...] = section(2)
    fq_ref[...] = (head_norm(section(3), 2) * scale).astype(fq_ref.dtype)
    fk_ref[...] = head_norm(section(4), 3)
    fv_ref[...] = section(5)
    zf = _dot(h, wf_ref[...]) + fb_ref[...]
    logf_ref[...] = _log_sigmoid(zf)[:, :logf_ref.shape[-1]]


def _proj(x, g1, wqkv, wf, gains, fb, gsum, *, n_f, head_dim):
    m, d = x.shape
    width = wqkv.shape[1] // 6
    tm = min(ROW_TILE, m)
    assert m % tm == 0
    row = lambda c: pl.BlockSpec((tm, c), lambda i: (i, 0))
    out_shape = [jax.ShapeDtypeStruct((m, width), dt) for dt in (BF16, F32, F32, BF16, F32, F32)]
    out_shape.append(jax.ShapeDtypeStruct((m, n_f), F32))
    return pl.pallas_call(
        functools.partial(_proj_kernel, head_dim=head_dim, scale=head_dim ** -0.5),
        out_shape=out_shape,
        grid=(m // tm,),
        in_specs=[row(d), _resident(g1.shape), _resident(wqkv.shape), _resident(wf.shape),
                  _resident(gains.shape), _resident(fb.shape), _resident(gsum.shape)],
        out_specs=[row(width)] * 6 + [row(n_f)],
        compiler_params=_params(("arbitrary",)),
        name="proj",
    )(x, g1, wqkv, wf, gains, fb, gsum)


def _softmax_step(s, v, m_ref, l_ref, acc_ref, idx):
    m_prev = m_ref[idx]
    m_new = jnp.maximum(m_prev, jnp.max(s, axis=-1, keepdims=True))
    alpha = jnp.exp(m_prev - m_new)
    p = jnp.exp(s - m_new)
    l_ref[idx] = alpha * l_ref[idx] + jnp.sum(p, axis=-1, keepdims=True)
    acc_ref[idx] = alpha * acc_ref[idx] + _dot(p.astype(BF16), v)
    m_ref[idx] = m_new


def _init_state(m_ref, l_ref, acc_ref):
    m_ref[...] = jnp.full(m_ref.shape, NEG_INF, F32)
    l_ref[...] = jnp.zeros(l_ref.shape, F32)
    acc_ref[...] = jnp.zeros(acc_ref.shape, F32)


def _causal_mask(t):
    return lax.broadcasted_iota(jnp.int32, (t, t), 1) <= lax.broadcasted_iota(jnp.int32, (t, t), 0)


def _diff_attn_kernel(rb_ref, q_ref, k_ref, v_ref, lam_ref, subg_ref, o_ref,
                      kb_ref, vb_ref, bias_ref, m_ref, l_ref, acc_ref,
                      *, n_heads, head_dim, lambda_init):
    t = ATTN_TILE
    head = pl.program_id(0)
    seq = q_ref.shape[1]

    @pl.when(pl.program_id(1) == 0)
    def _():
        row = lax.broadcasted_iota(jnp.int32, (t, t), 0)
        col = lax.broadcasted_iota(jnp.int32, (t, t), 1)
        far = rb_ref[(N_BUCKETS - 1) * n_heads + head]
        for blk in range(2):
            bucket = _rel_bucket(jnp.maximum(row - col + blk * t, 0))
            val = jnp.zeros((t, t), F32)
            for b in range(N_BUCKETS):
                val = jnp.where(bucket == b, rb_ref[b * n_heads + head], val)
            bias_ref[blk] = val - far

    kb_ref[...] = k_ref[0].astype(BF16)
    vb_ref[...] = v_ref[0].astype(BF16)
    first = lax.broadcasted_iota(jnp.int32, (1, 2 * head_dim), 1) < head_dim
    lam = _lambda_value(lam_ref, lambda_init)
    mask = _causal_mask(t)

    def q_block(i, carry):
        qs = pl.multiple_of(i * t, t)
        q = q_ref[0, pl.ds(qs, t), :]
        q_maps = (jnp.where(first, q, jnp.zeros_like(q)), jnp.where(first, jnp.zeros_like(q), q))
        _init_state(m_ref, l_ref, acc_ref)

        def step(j, bias_blk, masked):
            ks = pl.multiple_of(j * t, t)
            k = kb_ref[pl.ds(ks, t), :]
            v = vb_ref[pl.ds(ks, t), :]
            for mp in range(2):
                s = _dot_nt(q_maps[mp], k)
                if bias_blk is not None:
                    s = s + bias_ref[bias_blk]
                if masked:
                    s = jnp.where(mask, s, NEG_INF)
                _softmax_step(s, v, m_ref, l_ref, acc_ref, mp)

        def far_step(j, c):
            step(j, None, False)
            return c

        lax.fori_loop(0, jnp.maximum(i - 1, 0), far_step, 0)

        @pl.when(i >= 1)
        def _():
            step(i - 1, 1, False)

        step(i, 0, True)
        o = acc_ref[0] / l_ref[0] - lam * (acc_ref[1] / l_ref[1])
        o = _rms(o, subg_ref[...]) * (1.0 - lambda_init)
        o_ref[0, pl.ds(qs, t), :] = o.astype(o_ref.dtype)
        return carry

    lax.fori_loop(0, seq // t, q_block, 0)


def _diff_attention(rb_flat, q, k, v, lam_p, subg, *, n_heads, head_dim, lambda_init):
    b, s, _ = q.shape
    e = 2 * head_dim
    t = ATTN_TILE
    assert s % t == 0 and t >= MAX_DISTANCE and e == LANES
    blk = pl.BlockSpec((1, s, e), lambda h, bi, *_: (bi, 0, h))
    return pl.pallas_call(
        functools.partial(_diff_attn_kernel, n_heads=n_heads, head_dim=head_dim, lambda_init=lambda_init),
        out_shape=jax.ShapeDtypeStruct((b, s, n_heads * e), BF16),
        grid_spec=pltpu.PrefetchScalarGridSpec(
            num_scalar_prefetch=1,
            grid=(n_heads, b),
            in_specs=[blk, blk, blk, _resident(lam_p.shape), _resident(subg.shape)],
            out_specs=blk,
            scratch_shapes=[pltpu.VMEM((s, e), BF16), pltpu.VMEM((s, e), BF16),
                            pltpu.VMEM((2, t, t), F32),
                            pltpu.VMEM((2, t, 1), F32), pltpu.VMEM((2, t, 1), F32),
                            pltpu.VMEM((2, t, e), F32)]),
        compiler_params=_params(("arbitrary", "arbitrary")),
        name="diff_attention",
    )(rb_flat, q, k, v, lam_p, subg)


def _fox_attn_kernel(q_ref, k_ref, v_ref, logf_ref, o_ref,
                     kb_ref, vb_ref, c_ref, m_ref, l_ref, acc_ref, *, head_dim):
    t = ATTN_TILE
    pair = pl.program_id(1)
    seq = q_ref.shape[1]
    n_f = logf_ref.shape[-1]

    @pl.when(pair == 0)
    def _():
        eye = (lax.broadcasted_iota(jnp.int32, (n_f, n_f), 0)
               == lax.broadcasted_iota(jnp.int32, (n_f, n_f), 1)).astype(F32)
        upper = (lax.broadcasted_iota(jnp.int32, (t, t), 0)
                 <= lax.broadcasted_iota(jnp.int32, (t, t), 1)).astype(F32)
        carry = jnp.zeros((n_f, 1), F32)
        for blk in range(seq // t):
            x_t = _dot_nt(eye, logf_ref[0, blk * t:(blk + 1) * t, :], lax.Precision.HIGHEST)
            c = jnp.dot(x_t, upper, preferred_element_type=F32, precision=lax.Precision.HIGHEST) + carry
            c_ref[:, blk * t:(blk + 1) * t] = c
            carry = c[:, t - 1:t]

    kb_ref[...] = k_ref[0].astype(BF16)
    vb_ref[...] = v_ref[0].astype(BF16)
    first = lax.broadcasted_iota(jnp.int32, (1, 2 * head_dim), 1) < head_dim
    mask = _causal_mask(t)

    def q_block(i, carry):
        qs = pl.multiple_of(i * t, t)
        q = q_ref[0, pl.ds(qs, t), :]
        q_heads = (jnp.where(first, q, jnp.zeros_like(q)), jnp.where(first, jnp.zeros_like(q), q))
        _init_state(m_ref, l_ref, acc_ref)

        def step(j, masked):
            ks = pl.multiple_of(j * t, t)
            k = kb_ref[pl.ds(ks, t), :]
            v = vb_ref[pl.ds(ks, t), :]
            for hh in range(2):
                s = _dot_nt(q_heads[hh], k) - c_ref[pl.ds(2 * pair + hh, 1), pl.ds(ks, t)]
                if masked:
                    s = jnp.where(mask, s, NEG_INF)
                _softmax_step(s, v, m_ref, l_ref, acc_ref, hh)

        def far_step(j, c):
            step(j, False)
            return c

        lax.fori_loop(0, i, far_step, 0)
        step(i, True)
        o = jnp.where(first, acc_ref[0] / l_ref[0], acc_ref[1] / l_ref[1])
        o_ref[0, pl.ds(qs, t), :] = o.astype(o_ref.dtype)
        return carry

    lax.fori_loop(0, seq // t, q_block, 0)


def _fox_attention(q, k, v, logf, *, head_dim):
    b, s, w = q.shape
    e = 2 * head_dim
    t = ATTN_TILE
    n_f = logf.shape[-1]
    assert s % t == 0 and e == LANES and w // e * 2 == n_f
    blk = pl.BlockSpec((1, s, e), lambda bi, g: (bi, 0, g))
    return pl.pallas_call(
        functools.partial(_fox_attn_kernel, head_dim=head_dim),
        out_shape=jax.ShapeDtypeStruct((b, s, w), BF16),
        grid=(b, w // e),
        in_specs=[blk, blk, blk, pl.BlockSpec((1, s, n_f), lambda bi, g: (bi, 0, 0))],
        out_specs=blk,
        scratch_shapes=[pltpu.VMEM((s, e), BF16), pltpu.VMEM((s, e), BF16),
                        pltpu.VMEM((n_f, s), F32),
                        pltpu.VMEM((2, t, 1), F32), pltpu.VMEM((2, t, 1), F32),
                        pltpu.VMEM((2, t, e), F32)],
        compiler_params=_params(("arbitrary", "arbitrary")),
        name="fox_attention",
    )(q, k, v, logf)


def _decode_kernel(pt_ref, qd_ref, kd_ref, vd_ref, qf_ref, kf_ref, vf_ref, lfn_ref,
                   rbt_ref, lam_ref, subg_ref, *rest,
                   n_pages, page_size, head_dim, lambda_init):
    pps = PAGES_PER_STEP
    pages = [rest[5 * p:5 * p + 5] for p in range(pps)]
    od_ref, of_ref = rest[5 * pps:5 * pps + 2]
    (md_ref, ld_ref, accd_ref, mf_ref, lf_ref, accf_ref, carry_ref, bias_ref) = rest[5 * pps + 2:]
    grp = pl.program_id(1)
    n_maps, width = accd_ref.shape
    past = n_pages * page_size

    map_of_lane = lax.broadcasted_iota(jnp.int32, (n_maps, width), 1) // head_dim
    row = lax.broadcasted_iota(jnp.int32, (n_maps, width), 0)
    own = map_of_lane == row
    own_d = map_of_lane // 2 == row // 2

    @pl.when(jnp.logical_and(pl.program_id(0) == 0, grp == 0))
    def _():
        lane = lax.broadcasted_iota(jnp.int32, (n_maps, page_size), 1)
        bucket = _rel_bucket(page_size - lane)
        val = jnp.zeros((n_maps, page_size), F32)
        for b in range(N_BUCKETS):
            val = jnp.where(bucket == b, rbt_ref[:, b:b + 1], val)
        bias_ref[...] = val - rbt_ref[:, N_BUCKETS - 1:N_BUCKETS]

    qd = jnp.where(own, qd_ref[0], 0.0).astype(BF16)
    qf = jnp.where(own, qf_ref[0], 0.0).astype(BF16)

    @pl.when(grp == 0)
    def _():
        def self_score(q, k_ref):
            k = k_ref[0].astype(BF16).astype(F32)
            return jnp.sum(q.astype(F32) * k, axis=-1, keepdims=True)
        md_ref[...] = self_score(qd, kd_ref) + (rbt_ref[:, 0:1] - rbt_ref[:, N_BUCKETS - 1:N_BUCKETS])
        mf_ref[...] = self_score(qf, kf_ref)
        ld_ref[...] = jnp.ones(ld_ref.shape, F32)
        lf_ref[...] = jnp.ones(lf_ref.shape, F32)
        accd_ref[...] = jnp.broadcast_to(vd_ref[0], accd_ref.shape)
        accf_ref[...] = jnp.broadcast_to(vf_ref[0], accf_ref.shape)
        carry_ref[...] = lfn_ref[0]

    def update(s, v_ref, m_ref, l_ref, acc_ref):
        m_prev = m_ref[...]
        m_new = jnp.maximum(m_prev, jnp.max(s, axis=-1, keepdims=True))
        alpha = jnp.exp(m_prev - m_new)
        p = jnp.exp(s - m_new)
        l_ref[...] = alpha * l_ref[...] + jnp.sum(p, axis=-1, keepdims=True)
        acc_ref[...] = alpha * acc_ref[...] + _dot(p.astype(BF16), v_ref[0].astype(BF16))
        m_ref[...] = m_new

    n_f = lfn_ref.shape[1]
    eye = (lax.broadcasted_iota(jnp.int32, (n_f, n_f), 0)
           == lax.broadcasted_iota(jnp.int32, (n_f, n_f), 1)).astype(F32)
    later = (lax.broadcasted_iota(jnp.int32, (page_size, page_size), 0)
             >= lax.broadcasted_iota(jnp.int32, (page_size, page_size), 1)).astype(F32)
    hi = lax.Precision.HIGHEST
    for p, (cdk_ref, cdv_ref, cfk_ref, cfv_ref, clf_ref) in enumerate(pages):
        s_d = _dot_nt(qd, cdk_ref[0].astype(BF16))
        if p == 0:
            s_d = s_d + jnp.where(grp == 0, bias_ref[...], 0.0)
        update(s_d, cdv_ref, md_ref, ld_ref, accd_ref)
        lf_t = _dot_nt(eye, clf_ref[0], hi)
        incl = jnp.dot(lf_t, later, preferred_element_type=F32, precision=hi)
        s_f = _dot_nt(qf, cfk_ref[0].astype(BF16)) + (incl - lf_t + carry_ref[...])
        update(s_f, cfv_ref, mf_ref, lf_ref, accf_ref)
        carry_ref[...] = carry_ref[...] + incl[:, 0:1]

    @pl.when(grp == pl.num_programs(1) - 1)
    def _():
        lam = _lambda_value(lam_ref, lambda_init)
        sign = jnp.where(row[:, 0:1] % 2 == 0, 1.0, -lam)
        o_d = jnp.sum(jnp.where(own_d, accd_ref[...] / ld_ref[...] * sign, 0.0), axis=0, keepdims=True)
        e = 2 * head_dim
        o_d = jnp.concatenate([_rms(o_d[:, c:c + e], subg_ref[...]) for c in range(0, width, e)], axis=1)
        od_ref[0] = o_d * (1.0 - lambda_init)
        of_ref[0] = jnp.sum(jnp.where(own, accf_ref[...] / lf_ref[...], 0.0), axis=0, keepdims=True)


def _decode_attention(page_table, qd, kd, vd, qf, kf, vf, logf_new, rbt, lam_p, subg,
                      cdk, cdv, cfk, cfv, clf, *, head_dim, lambda_init):
    r, n_pages = page_table.shape
    _, page_size, width = cdk.shape
    n_f = clf.shape[-1]
    n_maps = width // head_dim
    pps = PAGES_PER_STEP
    assert n_pages % pps == 0 and n_maps == n_f
    assert page_size >= MAX_DISTANCE
    row = pl.BlockSpec((1, 1, width), lambda i, g, pt: (i, 0, 0))

    def paged(p, last):
        def index(i, g, pt):
            return (pt[i, n_pages - 1 - (g * pps + p)], 0, 0)
        return pl.BlockSpec((1, page_size, last), index)

    in_specs = [row] * 6 + [pl.BlockSpec((1, n_f, 1), lambda i, g, pt: (i, 0, 0)),
                            _resident(rbt.shape), _resident(lam_p.shape), _resident(subg.shape)]
    operands = [qd, kd, vd, qf, kf, vf, logf_new, rbt, lam_p, subg]
    for p in range(pps):
        in_specs += [paged(p, width)] * 4 + [paged(p, n_f)]
        operands += [cdk, cdv, cfk, cfv, clf]
    out = jax.ShapeDtypeStruct((r, 1, width), F32)
    return pl.pallas_call(
        functools.partial(_decode_kernel, n_pages=n_pages, page_size=page_size,
                          head_dim=head_dim, lambda_init=lambda_init),
        out_shape=[out, out],
        grid_spec=pltpu.PrefetchScalarGridSpec(
            num_scalar_prefetch=1,
            grid=(r, n_pages // pps),
            in_specs=in_specs,
            out_specs=[row, row],
            scratch_shapes=[pltpu.VMEM((n_maps, 1), F32), pltpu.VMEM((n_maps, 1), F32),
                            pltpu.VMEM((n_maps, width), F32),
                            pltpu.VMEM((n_maps, 1), F32), pltpu.VMEM((n_maps, 1), F32),
                            pltpu.VMEM((n_maps, width), F32),
                            pltpu.VMEM((n_f, 1), F32), pltpu.VMEM((n_maps, page_size), F32)]),
        compiler_params=_params(("arbitrary", "arbitrary")),
        name="decode_attention",
    )(page_table, *operands)


def _merge_kernel(x_ref, od_ref, of_ref, g1_ref, wg_ref, gb_ref, wa_ref, wb_ref, wo_ref, o_ref):
    x = x_ref[...]
    d = x.shape[-1]
    h = _rms(x, g1_ref[...]).astype(BF16)
    ya = _dot(od_ref[...].astype(BF16), wa_ref[...])
    yb = _dot(of_ref[...].astype(BF16), wb_ref[...])
    gate_a = jax.nn.sigmoid(_dot(h, wg_ref[:, :d]) + gb_ref[:, :d])
    gate_b = jax.nn.sigmoid(_dot(h, wg_ref[:, d:]) + gb_ref[:, d:])
    merged = gate_a * ya + gate_b * yb
    o_ref[...] = x + _dot(merged.astype(BF16), wo_ref[...])


def _merge(x, od, of, g1, wg, gb, wa, wb, wo):
    m, d = x.shape
    tm = min(ROW_TILE, m)
    assert m % tm == 0
    row = lambda c: pl.BlockSpec((tm, c), lambda i: (i, 0))
    return pl.pallas_call(
        _merge_kernel,
        out_shape=jax.ShapeDtypeStruct((m, d), F32),
        grid=(m // tm,),
        in_specs=[row(d), row(od.shape[1]), row(of.shape[1])]
                 + [_resident(a.shape) for a in (g1, wg, gb, wa, wb, wo)],
        out_specs=row(d),
        compiler_params=_params(("arbitrary",)),
        name="merge",
    )(x, od, of, g1, wg, gb, wa, wb, wo)


def _ffn_kernel(x_ref, g2_ref, wgu_ref, wdn_ref, o_ref):
    x = x_ref[...]
    d_ff = wdn_ref.shape[0]
    h = _rms(x, g2_ref[...]).astype(BF16)
    gate = _dot(h, wgu_ref[:, :d_ff])
    up = _dot(h, wgu_ref[:, d_ff:])
    act = (gate * jax.nn.sigmoid(gate) * up).astype(BF16)
    o_ref[...] = x + _dot(act, wdn_ref[...])


def _ffn(x, g2, wgu, wdn):
    m, d = x.shape
    tm = min(FFN_ROW_TILE, m)
    assert m % tm == 0
    row = pl.BlockSpec((tm, d), lambda i: (i, 0))
    return pl.pallas_call(
        _ffn_kernel,
        out_shape=jax.ShapeDtypeStruct((m, d), F32),
        grid=(m // tm,),
        in_specs=[row, _resident(g2.shape), _resident(wgu.shape), _resident(wdn.shape)],
        out_specs=row,
        compiler_params=_params(("arbitrary",)),
        name="ffn",
    )(x, g2, wgu, wdn)


def kernel(x_prompt, x_sample, cache_diff_k, cache_diff_v, cache_fox_k, cache_fox_v, cache_fox_logf,
           page_table, rel_bias, norm1_g, w_in, diff_q_g, diff_k_g, fox_q_g, fox_k_g, diff_lambda,
           fox_f_b, gate_b, diff_subln_g, w_branch_a, w_branch_b, w_out, norm2_g, w_gate_up, w_down):
    depth = w_in.shape[0]
    batch, seq, d = x_prompt.shape
    dec_batch, dec_seq, _ = x_sample.shape
    assert dec_seq == 1
    _, n_phys, page_size, n_dh, _, head_dim = cache_diff_k.shape
    n_f = cache_fox_k.shape[3]
    width = n_dh * 2 * head_dim
    assert n_f * head_dim == width
    qkv_cols = 6 * width

    grp = np.arange(MXU_DIM) // head_dim
    gsum = jnp.asarray(grp[:, None] == grp[None, :], BF16)
    rb_flat = rel_bias.reshape(-1)
    rbt = jnp.repeat(rel_bias.T, 2, axis=0)

    yp = x_prompt.reshape(batch * seq, d)
    ys = x_sample.reshape(dec_batch, d)
    rows_p, rows_s = [], []
    for l in range(depth):
        lambda_init = 0.8 - 0.6 * math.exp(-0.3 * l)
        wl = w_in[l]
        wqkv = wl[:, :qkv_cols].astype(BF16)
        wf = jnp.pad(wl[:, qkv_cols:qkv_cols + n_f], ((0, 0), (0, LANES - n_f))).astype(BF16)
        wg = wl[:, qkv_cols + n_f:].astype(BF16)
        fb = jnp.pad(fox_f_b[l], (0, LANES - n_f)).reshape(1, LANES)
        gains = jnp.stack([jnp.tile(g[l], width // head_dim)
                           for g in (diff_q_g, diff_k_g, fox_q_g, fox_k_g)])
        g1 = norm1_g[l].reshape(1, d)
        g2 = norm2_g[l].reshape(1, d)
        gb = gate_b[l].reshape(1, 2 * d)
        subg = diff_subln_g[l].reshape(1, 2 * head_dim)
        wa, wb, wo = (w[l].astype(BF16) for w in (w_branch_a, w_branch_b, w_out))
        wgu, wdn = w_gate_up[l].astype(BF16), w_down[l].astype(BF16)
        lam_p = diff_lambda[l]

        def tail(x, od, of):
            return _ffn(_merge(x, od, of, g1, wg, gb, wa, wb, wo), g2, wgu, wdn)

        dq, dk, dv, fq, fk, fv, logf = _proj(yp, g1, wqkv, wf, gains, fb, gsum, n_f=n_f, head_dim=head_dim)
        as_seq = lambda a: a.reshape(batch, seq, a.shape[-1])
        od = _diff_attention(rb_flat, as_seq(dq), as_seq(dk), as_seq(dv), lam_p, subg,
                             n_heads=n_dh, head_dim=head_dim, lambda_init=lambda_init)
        of = _fox_attention(as_seq(fq), as_seq(fk), as_seq(fv), as_seq(logf), head_dim=head_dim)
        yp = tail(yp, od.reshape(batch * seq, width), of.reshape(batch * seq, width))
        rows_p.append((dk.reshape(batch, seq, n_dh, 2, head_dim), dv.reshape(batch, seq, n_dh, 2 * head_dim),
                       fk.reshape(batch, seq, n_f, head_dim), fv.reshape(batch, seq, n_f, head_dim),
                       logf.reshape(batch, seq, n_f)))

        dq, dk, dv, fq, fk, fv, logf = _proj(ys, g1, wqkv, wf, gains, fb, gsum, n_f=n_f, head_dim=head_dim)
        as_row = lambda a: a.astype(F32).reshape(dec_batch, 1, width)
        paged = lambda c: c[l].reshape(n_phys, page_size, -1)
        od, of = _decode_attention(
            page_table, as_row(dq), as_row(dk), as_row(dv), as_row(fq), as_row(fk), as_row(fv),
            logf.reshape(dec_batch, n_f, 1), rbt, lam_p, subg,
            paged(cache_diff_k), paged(cache_diff_v), paged(cache_fox_k), paged(cache_fox_v),
            paged(cache_fox_logf), head_dim=head_dim, lambda_init=lambda_init)
        ys = tail(ys, od.reshape(dec_batch, width), of.reshape(dec_batch, width))
        rows_s.append((dk.reshape(dec_batch, 1, n_dh, 2, head_dim), dv.reshape(dec_batch, 1, n_dh, 2 * head_dim),
                       fk.reshape(dec_batch, 1, n_f, head_dim), fv.reshape(dec_batch, 1, n_f, head_dim),
                       logf.reshape(dec_batch, 1, n_f)))

    stack = lambda rows, i: jnp.stack([r[i] for r in rows], axis=0)
    return (yp.reshape(batch, seq, d), ys.reshape(dec_batch, 1, d),
            *(stack(rows_p, i) for i in range(5)), *(stack(rows_s, i) for i in range(5)))
```

```python
import functools
import math

import jax
import jax.numpy as jnp
from jax import lax
from jax.experimental import pallas as pl
from jax.experimental.pallas import tpu as pltpu

F32 = jnp.float32
BF16 = jnp.bfloat16

N_BUCKETS = 32
MAX_DISTANCE = 128
EPS = 1e-6
NEG_INF = -1e30

LANES = 128
VMEM_LIMIT = 56 * 1024 * 1024

ROW_TILE = 512
FFN_ROW_TILE = 256
ATTN_TILE = 256
PAGES_PER_STEP = 8

NT_DIMS = (((1,), (1,)), ((), ()))
HIGHEST = lax.Precision.HIGHEST


def _resident(shape):
    zeros = (0,) * len(shape)
    return pl.BlockSpec(shape, lambda *_: zeros, pipeline_mode=pl.Buffered(1))


def _params(semantics, vmem=VMEM_LIMIT):
    return pltpu.CompilerParams(dimension_semantics=semantics, vmem_limit_bytes=vmem)


def _rms(x, g):
    return x * lax.rsqrt(jnp.mean(x * x, axis=-1, keepdims=True) + EPS) * g


def _dot(a, b, precision=None):
    return jnp.dot(a, b, preferred_element_type=F32, precision=precision)


def _dot_nt(a, b):
    return lax.dot_general(a, b, NT_DIMS, preferred_element_type=F32)


def _log_sigmoid(x):
    return jnp.minimum(x, 0.0) - jnp.log1p(jnp.exp(-jnp.abs(x)))


def _rel_bucket(n):
    max_exact = N_BUCKETS // 2
    nf = jnp.maximum(n, 1).astype(F32)
    large = max_exact + (jnp.log(nf / max_exact) / math.log(MAX_DISTANCE / max_exact)
                         * (N_BUCKETS - max_exact)).astype(jnp.int32)
    return jnp.where(n < max_exact, n, jnp.minimum(large, N_BUCKETS - 1))


def _lambda_value(lam_ref, lambda_init):
    lp = lam_ref[...]
    a = jnp.sum(lp[0:1] * lp[1:2], axis=-1, keepdims=True)
    b = jnp.sum(lp[2:3] * lp[3:4], axis=-1, keepdims=True)
    return jnp.exp(a) - jnp.exp(b) + lambda_init


def _proj_kernel(x_ref, g1_ref, wt_ref, wdv_ref, wft_ref, gains_ref, fb_ref,
                 qd_ref, kd_ref, vd_ref, qf_ref, kf_ref, vf_ref, logf_ref, *row_refs, head_dim, scale):
    h = _rms(x_ref[0], g1_ref[...]).astype(BF16)
    width = kd_ref.shape[1]
    n_heads = width // head_dim

    def section_t(i):
        return _dot_nt(wt_ref[i * width:(i + 1) * width, :], h)

    def head_norm_t(z, i):
        z3 = z.reshape(n_heads, head_dim, z.shape[-1])
        ss = jnp.sum(z3 * z3, axis=1, keepdims=True)
        gain = gains_ref[i].reshape(n_heads, head_dim, 1)
        return (z3 * lax.rsqrt(ss * (1.0 / head_dim) + EPS) * gain).reshape(z.shape)

    qd = head_norm_t(section_t(0), 0) * scale
    kd = head_norm_t(section_t(1), 1)
    qf = head_norm_t(section_t(2), 2) * scale
    kf = head_norm_t(section_t(3), 3)
    vf = section_t(4)
    qd_ref[0] = qd.astype(qd_ref.dtype)
    kd_ref[0] = kd
    vd_ref[0] = _dot(h, wdv_ref[...])
    qf_ref[0] = qf.astype(qf_ref.dtype)
    kf_ref[0] = kf
    vf_ref[0] = vf
    logf_ref[0] = _log_sigmoid(_dot_nt(wft_ref[...], h) + fb_ref[...])
    if row_refs:
        for ref, val in zip(row_refs, (qd, kd, qf, kf, vf)):
            ref[0] = val.T


def _proj(x, g1, wt, wdv, wft, gains, fb, *, head_dim, with_rows):
    b, s, d = x.shape
    width = wdv.shape[1]
    n_f = wft.shape[0]
    tm = min(ROW_TILE, s)
    assert s % tm == 0 and tm % LANES == 0
    t_blk = lambda r: pl.BlockSpec((1, r, tm), lambda bi, si: (bi, 0, si))
    r_blk = lambda c: pl.BlockSpec((1, tm, c), lambda bi, si: (bi, si, 0))
    t_shape = lambda dt: jax.ShapeDtypeStruct((b, width, s), dt)
    r_shape = jax.ShapeDtypeStruct((b, s, width), F32)
    out_shape = [t_shape(BF16), t_shape(F32), r_shape, t_shape(BF16), t_shape(F32), t_shape(F32),
                 jax.ShapeDtypeStruct((b, n_f, s), F32)]
    out_specs = [t_blk(width), t_blk(width), r_blk(width), t_blk(width), t_blk(width), t_blk(width), t_blk(n_f)]
    if with_rows:
        out_shape += [r_shape] * 5
        out_specs += [r_blk(width)] * 5
    return pl.pallas_call(
        functools.partial(_proj_kernel, head_dim=head_dim, scale=head_dim ** -0.5),
        out_shape=out_shape,
        grid=(b, s // tm),
        in_specs=[r_blk(d)] + [_resident(a.shape) for a in (g1, wt, wdv, wft, gains, fb)],
        out_specs=out_specs,
        compiler_params=_params(("arbitrary", "arbitrary")),
        name="proj",
    )(x, g1, wt, wdv, wft, gains, fb)


def _softmax_step_t(s_t, v_t, m_ref, l_ref, acc_ref, idx):
    m_prev = m_ref[idx]
    m_new = jnp.maximum(m_prev, jnp.max(s_t, axis=0, keepdims=True))
    alpha = jnp.exp(m_prev - m_new)
    p_t = jnp.exp(s_t - m_new)
    l_ref[idx] = alpha * l_ref[idx] + jnp.sum(p_t, axis=0, keepdims=True)
    acc_ref[idx] = alpha * acc_ref[idx] + _dot(v_t, p_t.astype(BF16))
    m_ref[idx] = m_new


def _init_state(m_ref, l_ref, acc_ref):
    m_ref[...] = jnp.full(m_ref.shape, NEG_INF, F32)
    l_ref[...] = jnp.zeros(l_ref.shape, F32)
    acc_ref[...] = jnp.zeros(acc_ref.shape, F32)


def _causal_mask_t(t):
    return lax.broadcasted_iota(jnp.int32, (t, t), 0) <= lax.broadcasted_iota(jnp.int32, (t, t), 1)


def _split_key_rows(k_t_ref, k_rows_ref, head_dim):
    k_rows = k_t_ref[0].T
    first = lax.broadcasted_iota(jnp.int32, (1, 2 * head_dim), 1) < head_dim
    k_rows_ref[0] = jnp.where(first, k_rows, 0.0).astype(BF16)
    k_rows_ref[1] = jnp.where(first, 0.0, k_rows).astype(BF16)


def _diff_attn_kernel(rb_ref, q_ref, k_ref, v_ref, lam_ref, subg_ref, o_ref,
                      kb_ref, vb_ref, bias_ref, m_ref, l_ref, acc_ref,
                      *, n_heads, head_dim, lambda_init):
    t = ATTN_TILE
    head = pl.program_id(0)
    seq = q_ref.shape[2]

    @pl.when(pl.program_id(1) == 0)
    def _():
        key = lax.broadcasted_iota(jnp.int32, (t, t), 0)
        qry = lax.broadcasted_iota(jnp.int32, (t, t), 1)
        far = rb_ref[(N_BUCKETS - 1) * n_heads + head]
        for blk in range(2):
            bucket = _rel_bucket(jnp.maximum(qry - key + blk * t, 0))
            val = jnp.zeros((t, t), F32)
            for b in range(N_BUCKETS):
                val = jnp.where(bucket == b, rb_ref[b * n_heads + head], val)
            bias_ref[blk] = val - far

    _split_key_rows(k_ref, kb_ref, head_dim)
    vb_ref[...] = v_ref[0].T.astype(BF16)
    lam = _lambda_value(lam_ref, lambda_init)
    mask = _causal_mask_t(t)

    def q_block(i, carry):
        qs = pl.multiple_of(i * t, t)
        q_t = q_ref[0, :, pl.ds(qs, t)]
        _init_state(m_ref, l_ref, acc_ref)

        def step(j, bias_blk, masked):
            ks = pl.multiple_of(j * t, t)
            v_t = vb_ref[:, pl.ds(ks, t)]
            for mp in range(2):
                s_t = _dot(kb_ref[mp, pl.ds(ks, t), :], q_t)
                if bias_blk is not None:
                    s_t = s_t + bias_ref[bias_blk]
                if masked:
                    s_t = jnp.where(mask, s_t, NEG_INF)
                _softmax_step_t(s_t, v_t, m_ref, l_ref, acc_ref, mp)

        def far_step(j, c):
            step(j, None, False)
            return c

        lax.fori_loop(0, jnp.maximum(i - 1, 0), far_step, 0)

        @pl.when(i >= 1)
        def _():
            step(i - 1, 1, False)

        step(i, 0, True)
        o_t = acc_ref[0] / l_ref[0] - lam * (acc_ref[1] / l_ref[1])
        o = _rms(o_t.T, subg_ref[...]) * (1.0 - lambda_init)
        o_ref[0, pl.ds(qs, t), :] = o.astype(o_ref.dtype)
        return carry

    lax.fori_loop(0, seq // t, q_block, 0)


def _diff_attention(rb_flat, q_t, k_t, v, lam_p, subg, *, n_heads, head_dim, lambda_init):
    b, w, s = q_t.shape
    e = 2 * head_dim
    t = ATTN_TILE
    assert s % t == 0 and t >= MAX_DISTANCE and e == LANES
    t_blk = pl.BlockSpec((1, e, s), lambda h, bi, *_: (bi, h, 0))
    r_blk = pl.BlockSpec((1, s, e), lambda h, bi, *_: (bi, 0, h))
    return pl.pallas_call(
        functools.partial(_diff_attn_kernel, n_heads=n_heads, head_dim=head_dim, lambda_init=lambda_init),
        out_shape=jax.ShapeDtypeStruct((b, s, w), BF16),
        grid_spec=pltpu.PrefetchScalarGridSpec(
            num_scalar_prefetch=1,
            grid=(n_heads, b),
            in_specs=[t_blk, t_blk, r_blk, _resident(lam_p.shape), _resident(subg.shape)],
            out_specs=r_blk,
            scratch_shapes=[pltpu.VMEM((2, s, e), BF16), pltpu.VMEM((e, s), BF16),
                            pltpu.VMEM((2, t, t), F32),
                            pltpu.VMEM((2, 1, t), F32), pltpu.VMEM((2, 1, t), F32),
                            pltpu.VMEM((2, e, t), F32)]),
        compiler_params=_params(("arbitrary", "arbitrary")),
        name="diff_attention",
    )(rb_flat, q_t, k_t, v, lam_p, subg)


def _fox_attn_kernel(q_ref, k_ref, v_ref, logf_ref, o_ref,
                     kb_ref, vb_ref, c_ref, cb_ref, m_ref, l_ref, acc_ref, *, head_dim):
    t = ATTN_TILE
    pair = pl.program_id(1)
    seq = q_ref.shape[2]

    @pl.when(pair == 0)
    def _():
        upper = (lax.broadcasted_iota(jnp.int32, (t, t), 0)
                 <= lax.broadcasted_iota(jnp.int32, (t, t), 1)).astype(F32)
        carry = jnp.zeros((logf_ref.shape[1], 1), F32)
        for blk in range(seq // t):
            c = _dot(logf_ref[0, :, blk * t:(blk + 1) * t], upper, HIGHEST) + carry
            c_ref[:, blk * t:(blk + 1) * t] = c
            carry = c[:, t - 1:t]

    _split_key_rows(k_ref, kb_ref, head_dim)
    vb_ref[...] = v_ref[0].astype(BF16)
    for hh in range(2):
        c_col = jnp.broadcast_to(c_ref[pl.ds(2 * pair + hh, 1), :], (LANES, seq)).T
        for lane0 in range(0, t, LANES):
            cb_ref[hh, :, lane0:lane0 + LANES] = c_col
    mask = _causal_mask_t(t)

    def q_block(i, carry):
        qs = pl.multiple_of(i * t, t)
        q_t = q_ref[0, :, pl.ds(qs, t)]
        _init_state(m_ref, l_ref, acc_ref)

        def step(j, masked):
            ks = pl.multiple_of(j * t, t)
            for hh in range(2):
                s_t = _dot(kb_ref[hh, pl.ds(ks, t), :], q_t) - cb_ref[hh, pl.ds(ks, t), :]
                if masked:
                    s_t = jnp.where(mask, s_t, NEG_INF)
                v_t = vb_ref[hh * head_dim:(hh + 1) * head_dim, pl.ds(ks, t)]
                _softmax_step_t(s_t, v_t, m_ref, l_ref, acc_ref, hh)

        def far_step(j, c):
            step(j, False)
            return c

        lax.fori_loop(0, i, far_step, 0)
        step(i, True)
        o_t = jnp.concatenate([acc_ref[0] / l_ref[0], acc_ref[1] / l_ref[1]], axis=0)
        o_ref[0, pl.ds(qs, t), :] = o_t.T.astype(o_ref.dtype)
        return carry

    lax.fori_loop(0, seq // t, q_block, 0)


def _fox_attention(q_t, k_t, v_t, logf_t, *, head_dim):
    b, w, s = q_t.shape
    e = 2 * head_dim
    t = ATTN_TILE
    n_f = logf_t.shape[1]
    assert s % t == 0 and e == LANES and w // e * 2 == n_f and t % LANES == 0
    t_blk = pl.BlockSpec((1, e, s), lambda bi, g: (bi, g, 0))
    return pl.pallas_call(
        functools.partial(_fox_attn_kernel, head_dim=head_dim),
        out_shape=jax.ShapeDtypeStruct((b, s, w), BF16),
        grid=(b, w // e),
        in_specs=[t_blk, t_blk, t_blk, pl.BlockSpec((1, n_f, s), lambda bi, g: (bi, 0, 0))],
        out_specs=pl.BlockSpec((1, s, e), lambda bi, g: (bi, 0, g)),
        scratch_shapes=[pltpu.VMEM((2, s, e), BF16), pltpu.VMEM((e, s), BF16),
                        pltpu.VMEM((n_f, s), F32), pltpu.VMEM((2, s, t), F32),
                        pltpu.VMEM((2, 1, t), F32), pltpu.VMEM((2, 1, t), F32),
                        pltpu.VMEM((2, head_dim, t), F32)],
        compiler_params=_params(("arbitrary", "arbitrary")),
        name="fox_attention",
    )(q_t, k_t, v_t, logf_t)


def _decode_kernel(pt_ref, qd_ref, kd_ref, vd_ref, qf_ref, kf_ref, vf_ref, lfn_ref,
                   rbt_ref, lam_ref, subg_ref, *rest,
                   n_pages, page_size, head_dim, lambda_init):
    pps = PAGES_PER_STEP
    pages = [rest[5 * p:5 * p + 5] for p in range(pps)]
    od_ref, of_ref = rest[5 * pps:5 * pps + 2]
    (md_ref, ld_ref, accd_ref, mf_ref, lf_ref, accf_ref, self_ref, carry_ref, bias_ref) = rest[5 * pps + 2:]
    grp = pl.program_id(1)
    n_maps = md_ref.shape[0]
    width = qd_ref.shape[-1]
    n_dh = n_maps // 2
    e = 2 * head_dim

    map_of_lane = lax.broadcasted_iota(jnp.int32, (n_maps, width), 1) // head_dim
    row = lax.broadcasted_iota(jnp.int32, (n_maps, width), 0)
    own = map_of_lane == row
    row_e = lax.broadcasted_iota(jnp.int32, (n_maps, e), 0)

    @pl.when(jnp.logical_and(pl.program_id(0) == 0, grp == 0))
    def _():
        lane = lax.broadcasted_iota(jnp.int32, (n_maps, page_size), 1)
        bucket = _rel_bucket(page_size - lane)
        val = jnp.zeros((n_maps, page_size), F32)
        for b in range(N_BUCKETS):
            val = jnp.where(bucket == b, rbt_ref[:, b:b + 1], val)
        bias_ref[...] = val - rbt_ref[:, N_BUCKETS - 1:N_BUCKETS]

    qd = jnp.where(own, qd_ref[0], 0.0).astype(BF16)
    qf = jnp.where(own, qf_ref[0], 0.0).astype(BF16)

    @pl.when(grp == 0)
    def _():
        def self_score(q, k_ref):
            k = k_ref[0].astype(BF16).astype(F32)
            return jnp.sum(q.astype(F32) * k, axis=-1, keepdims=True)
        md_ref[...] = self_score(qd, kd_ref) + (rbt_ref[:, 0:1] - rbt_ref[:, N_BUCKETS - 1:N_BUCKETS])
        mf_ref[...] = self_score(qf, kf_ref)
        ld_ref[...] = jnp.ones(ld_ref.shape, F32)
        lf_ref[...] = jnp.ones(lf_ref.shape, F32)
        v_new = vd_ref[0]
        acc0 = jnp.zeros(accd_ref.shape, F32)
        for h in range(n_dh):
            acc0 = jnp.where(row_e // 2 == h, v_new[h:h + 1, :], acc0)
        accd_ref[...] = acc0
        accf_ref[...] = jnp.zeros(accf_ref.shape, F32)
        self_ref[...] = jnp.ones(self_ref.shape, F32)
        carry_ref[...] = lfn_ref[0]

    later = (lax.broadcasted_iota(jnp.int32, (page_size, page_size), 0)
             >= lax.broadcasted_iota(jnp.int32, (page_size, page_size), 1)).astype(F32)

    def softmax_update(s, m_ref, l_ref):
        m_prev = m_ref[...]
        m_new = jnp.maximum(m_prev, jnp.max(s, axis=-1, keepdims=True))
        alpha = jnp.exp(m_prev - m_new)
        p = jnp.exp(s - m_new)
        l_ref[...] = alpha * l_ref[...] + jnp.sum(p, axis=-1, keepdims=True)
        m_ref[...] = m_new
        return alpha, p

    for p_idx, (cdk_ref, cdv_ref, cfk_ref, cfv_ref, clf_ref) in enumerate(pages):
        s_d = _dot(qd, cdk_ref[0].astype(BF16))
        if p_idx == 0:
            s_d = s_d + jnp.where(grp == 0, bias_ref[...], 0.0)
        alpha, p = softmax_update(s_d, md_ref, ld_ref)
        p = p.astype(BF16)
        pv = jnp.zeros(accd_ref.shape, F32)
        for h in range(n_dh):
            v_h = cdv_ref[0, pl.ds(h, page_size, stride=n_dh), :].astype(BF16)
            pv = jnp.where(row_e // 2 == h, _dot(p, v_h), pv)
        accd_ref[...] = alpha * accd_ref[...] + pv

        lf = clf_ref[0]
        incl = _dot(lf, later, HIGHEST)
        s_f = _dot(qf, cfk_ref[0].astype(BF16)) + (incl - lf + carry_ref[...])
        alpha, p = softmax_update(s_f, mf_ref, lf_ref)
        v3 = cfv_ref[0].reshape(n_maps, head_dim, page_size)
        accf_ref[...] = alpha[:, :, None] * accf_ref[...] + p[:, None, :] * v3
        self_ref[...] = alpha * self_ref[...]
        carry_ref[...] = carry_ref[...] + incl[:, 0:1]

    @pl.when(grp == pl.num_programs(1) - 1)
    def _():
        lam = _lambda_value(lam_ref, lambda_init)
        sign = jnp.where(row_e[:, 0:1] % 2 == 0, 1.0, -lam)
        a = accd_ref[...] / ld_ref[...] * sign
        pick = (lax.broadcasted_iota(jnp.int32, (n_dh, n_maps), 1) // 2
                == lax.broadcasted_iota(jnp.int32, (n_dh, n_maps), 0)).astype(F32)
        o_d = _dot(pick, a, HIGHEST)
        od_ref[0] = _rms(o_d, subg_ref[...]) * (1.0 - lambda_init)

        def spread(col):
            return jnp.sum(jnp.where(own, col, 0.0), axis=0, keepdims=True)
        past = jnp.sum(accf_ref[...].reshape(width, page_size).T, axis=0, keepdims=True)
        of_ref[0] = (past + spread(self_ref[...]) * vf_ref[0]) / spread(lf_ref[...])


def _decode_attention(page_table, qd, kd, vd, qf, kf, vf, logf_new, rbt, lam_p, subg,
                      cdk, cdv, cfk, cfv, clf, *, head_dim, lambda_init):
    r, n_pages = page_table.shape
    _, width, page_size = cdk.shape
    n_f = clf.shape[1]
    n_maps = width // head_dim
    n_dh = n_maps // 2
    e = 2 * head_dim
    pps = PAGES_PER_STEP
    assert n_pages % pps == 0 and n_maps == n_f
    assert page_size >= MAX_DISTANCE
    row = pl.BlockSpec((1, 1, width), lambda i, g, pt: (i, 0, 0))
    head_rows = pl.BlockSpec((1, n_dh, e), lambda i, g, pt: (i, 0, 0))

    def paged(p, rows):
        def index(i, g, pt):
            return (pt[i, n_pages - 1 - (g * pps + p)], 0, 0)
        return pl.BlockSpec((1, rows, cdk.shape[2]), index)

    in_specs = [row, row, head_rows, row, row, row, pl.BlockSpec((1, n_f, 1), lambda i, g, pt: (i, 0, 0)),
                _resident(rbt.shape), _resident(lam_p.shape), _resident(subg.shape)]
    operands = [qd, kd, vd, qf, kf, vf, logf_new, rbt, lam_p, subg]
    for p in range(pps):
        in_specs += [paged(p, width), paged(p, cdv.shape[1]), paged(p, width), paged(p, width), paged(p, n_f)]
        operands += [cdk, cdv, cfk, cfv, clf]
    return pl.pallas_call(
        functools.partial(_decode_kernel, n_pages=n_pages, page_size=page_size,
                          head_dim=head_dim, lambda_init=lambda_init),
        out_shape=[jax.ShapeDtypeStruct((r, n_dh, e), F32), jax.ShapeDtypeStruct((r, 1, width), F32)],
        grid_spec=pltpu.PrefetchScalarGridSpec(
            num_scalar_prefetch=1,
            grid=(r, n_pages // pps),
            in_specs=in_specs,
            out_specs=[head_rows, row],
            scratch_shapes=[pltpu.VMEM((n_maps, 1), F32), pltpu.VMEM((n_maps, 1), F32),
                            pltpu.VMEM((n_maps, e), F32),
                            pltpu.VMEM((n_maps, 1), F32), pltpu.VMEM((n_maps, 1), F32),
                            pltpu.VMEM((n_maps, head_dim, page_size), F32),
                            pltpu.VMEM((n_maps, 1), F32),
                            pltpu.VMEM((n_f, 1), F32), pltpu.VMEM((n_maps, page_size), F32)]),
        compiler_params=_params(("arbitrary", "arbitrary")),
        name="decode_attention",
    )(page_table, *operands)


def _merge_kernel(x_ref, od_ref, of_ref, g1_ref, wg_ref, gb_ref, wa_ref, wb_ref, wo_ref, o_ref):
    x = x_ref[...]
    d = x.shape[-1]
    h = _rms(x, g1_ref[...]).astype(BF16)
    ya = _dot(od_ref[...].astype(BF16), wa_ref[...])
    yb = _dot(of_ref[...].astype(BF16), wb_ref[...])
    gate_a = jax.nn.sigmoid(_dot(h, wg_ref[:, :d]) + gb_ref[:, :d])
    gate_b = jax.nn.sigmoid(_dot(h, wg_ref[:, d:]) + gb_ref[:, d:])
    merged = gate_a * ya + gate_b * yb
    o_ref[...] = x + _dot(merged.astype(BF16), wo_ref[...])


def _merge(x, od, of, g1, wg, gb, wa, wb, wo):
    m, d = x.shape
    tm = min(ROW_TILE, m)
    assert m % tm == 0
    row = lambda c: pl.BlockSpec((tm, c), lambda i: (i, 0))
    return pl.pallas_call(
        _merge_kernel,
        out_shape=jax.ShapeDtypeStruct((m, d), F32),
        grid=(m // tm,),
        in_specs=[row(d), row(od.shape[1]), row(of.shape[1])]
                 + [_resident(a.shape) for a in (g1, wg, gb, wa, wb, wo)],
        out_specs=row(d),
        compiler_params=_params(("arbitrary",)),
        name="merge",
    )(x, od, of, g1, wg, gb, wa, wb, wo)


def _ffn_kernel(x_ref, g2_ref, wgu_ref, wdn_ref, o_ref):
    x = x_ref[...]
    d_ff = wdn_ref.shape[0]
    h = _rms(x, g2_ref[...]).astype(BF16)
    gate = _dot(h, wgu_ref[:, :d_ff])
    up = _dot(h, wgu_ref[:, d_ff:])
    act = (gate * jax.nn.sigmoid(gate) * up).astype(BF16)
    o_ref[...] = x + _dot(act, wdn_ref[...])


def _ffn(x, g2, wgu, wdn):
    m, d = x.shape
    tm = min(FFN_ROW_TILE, m)
    assert m % tm == 0
    row = pl.BlockSpec((tm, d), lambda i: (i, 0))
    return pl.pallas_call(
        _ffn_kernel,
        out_shape=jax.ShapeDtypeStruct((m, d), F32),
        grid=(m // tm,),
        in_specs=[row, _resident(g2.shape), _resident(wgu.shape), _resident(wdn.shape)],
        out_specs=row,
        compiler_params=_params(("arbitrary",)),
        name="ffn",
    )(x, g2, wgu, wdn)


def kernel(x_prompt, x_sample, cache_diff_k, cache_diff_v, cache_fox_k, cache_fox_v, cache_fox_logf,
           page_table, rel_bias, norm1_g, w_in, diff_q_g, diff_k_g, fox_q_g, fox_k_g, diff_lambda,
           fox_f_b, gate_b, diff_subln_g, w_branch_a, w_branch_b, w_out, norm2_g, w_gate_up, w_down):
    depth = w_in.shape[0]
    batch, seq, d = x_prompt.shape
    dec_batch, dec_seq, _ = x_sample.shape
    assert dec_seq == 1
    _, n_phys, page_size, n_dh, _, head_dim = cache_diff_k.shape
    n_f = cache_fox_k.shape[3]
    e = 2 * head_dim
    width = n_dh * e
    assert n_f * head_dim == width

    rb_flat = rel_bias.reshape(-1)
    rbt = jnp.repeat(rel_bias.T, 2, axis=0)

    yp, ys = x_prompt, x_sample.reshape(1, dec_batch, d)
    rows_p, rows_s = [], []
    for l in range(depth):
        lambda_init = 0.8 - 0.6 * math.exp(-0.3 * l)
        w_t = w_in[l].T
        sec = lambda i: w_t[i * width:(i + 1) * width]
        wt = jnp.concatenate([sec(0), sec(1), sec(3), sec(4), sec(5)], axis=0).astype(BF16)
        wdv = sec(2).T.astype(BF16)
        wft = w_t[6 * width:6 * width + n_f].astype(BF16)
        wg = w_t[6 * width + n_f:].T.astype(BF16)
        fb = fox_f_b[l].reshape(n_f, 1)
        gains = jnp.stack([jnp.tile(g[l], width // head_dim)
                           for g in (diff_q_g, diff_k_g, fox_q_g, fox_k_g)]).reshape(4, width, 1)
        g1 = norm1_g[l].reshape(1, d)
        g2 = norm2_g[l].reshape(1, d)
        gb = gate_b[l].reshape(1, 2 * d)
        subg = diff_subln_g[l].reshape(1, e)
        wa, wb, wo = (w[l].astype(BF16) for w in (w_branch_a, w_branch_b, w_out))
        wgu, wdn = w_gate_up[l].astype(BF16), w_down[l].astype(BF16)
        lam_p = diff_lambda[l]

        def tail(x, od, of):
            return _ffn(_merge(x, od, of, g1, wg, gb, wa, wb, wo), g2, wgu, wdn)

        def cache_rows(kd_t, vd, kf_t, vf_t, logf_t):
            b, _, s = kd_t.shape
            return (jnp.transpose(kd_t.reshape(b, n_dh, 2, head_dim, s), (0, 4, 1, 2, 3)),
                    vd.reshape(b, s, n_dh, e),
                    jnp.transpose(kf_t.reshape(b, n_f, head_dim, s), (0, 3, 1, 2)),
                    jnp.transpose(vf_t.reshape(b, n_f, head_dim, s), (0, 3, 1, 2)),
                    jnp.transpose(logf_t, (0, 2, 1)))

        qd_t, kd_t, vd, qf_t, kf_t, vf_t, logf_t = _proj(
            yp, g1, wt, wdv, wft, gains, fb, head_dim=head_dim, with_rows=False)
        od = _diff_attention(rb_flat, qd_t, kd_t, vd, lam_p, subg,
                             n_heads=n_dh, head_dim=head_dim, lambda_init=lambda_init)
        of = _fox_attention(qf_t, kf_t, vf_t, logf_t, head_dim=head_dim)
        yp = tail(yp.reshape(batch * seq, d), od.reshape(batch * seq, width),
                  of.reshape(batch * seq, width)).reshape(batch, seq, d)
        rows_p.append(cache_rows(kd_t, vd, kf_t, vf_t, logf_t))

        (qd_t, kd_t, vd, qf_t, kf_t, vf_t, logf_t,
         qd_r, kd_r, qf_r, kf_r, vf_r) = _proj(ys, g1, wt, wdv, wft, gains, fb, head_dim=head_dim, with_rows=True)
        as_row = lambda a: a.reshape(dec_batch, 1, width)
        cdk = jnp.transpose(cache_diff_k[l], (0, 2, 3, 4, 1)).reshape(n_phys, width, page_size)
        cdv = cache_diff_v[l].reshape(n_phys, page_size * n_dh, e)
        cfk = jnp.transpose(cache_fox_k[l], (0, 2, 3, 1)).reshape(n_phys, width, page_size)
        cfv = jnp.transpose(cache_fox_v[l], (0, 2, 3, 1)).reshape(n_phys, width, page_size)
        clf = jnp.transpose(cache_fox_logf[l], (0, 2, 1))
        od, of = _decode_attention(
            page_table, as_row(qd_r), as_row(kd_r), vd.reshape(dec_batch, n_dh, e),
            as_row(qf_r), as_row(kf_r), as_row(vf_r),
            jnp.transpose(logf_t, (2, 1, 0)), rbt, lam_p, subg,
            cdk, cdv, cfk, cfv, clf, head_dim=head_dim, lambda_init=lambda_init)
        ys = tail(ys.reshape(dec_batch, d), od.reshape(dec_batch, width),
                  of.reshape(dec_batch, width)).reshape(1, dec_batch, d)
        rows_s.append(tuple(jnp.swapaxes(a, 0, 1) for a in cache_rows(kd_t, vd, kf_t, vf_t, logf_t)))

    stack = lambda rows, i: jnp.stack([r[i] for r in rows], axis=0)
    return (yp, ys.reshape(dec_batch, 1, d),
            *(stack(rows_p, i) for i in range(5)), *(stack(rows_s, i) for i in range(5)))
```

```python
import functools
import math

import jax
import jax.numpy as jnp
from jax import lax
from jax.experimental import pallas as pl
from jax.experimental.pallas import tpu as pltpu

F32 = jnp.float32
BF16 = jnp.bfloat16

N_BUCKETS = 32
MAX_DISTANCE = 128
EPS = 1e-6
NEG_INF = -1e30
LOG2E = math.log2(math.e)

LANES = 128
BF16_ROWS = 16
VMEM_LIMIT = 56 * 1024 * 1024

ROW_TILE = 512
FFN_ROW_TILE = 256
ATTN_TQ = 512
ATTN_TK = 256
PAGES_PER_STEP = 16
DECAY_PARTS = 3

NT_DIMS = (((1,), (1,)), ((), ()))
HIGHEST = lax.Precision.HIGHEST


def _resident(shape):
    zeros = (0,) * len(shape)
    return pl.BlockSpec(shape, lambda *_: zeros, pipeline_mode=pl.Buffered(1))


def _params(semantics, vmem=VMEM_LIMIT):
    return pltpu.CompilerParams(dimension_semantics=semantics, vmem_limit_bytes=vmem)


def _rms(x, g):
    return x * lax.rsqrt(jnp.mean(x * x, axis=-1, keepdims=True) + EPS) * g


def _dot(a, b, precision=None):
    return jnp.dot(a, b, preferred_element_type=F32, precision=precision)


def _dot_nt(a, b):
    return lax.dot_general(a, b, NT_DIMS, preferred_element_type=F32)


def _log_sigmoid(x):
    return jnp.minimum(x, 0.0) - jnp.log1p(jnp.exp(-jnp.abs(x)))


def _rel_bucket(n):
    max_exact = N_BUCKETS // 2
    nf = jnp.maximum(n, 1).astype(F32)
    large = max_exact + (jnp.log(nf / max_exact) / math.log(MAX_DISTANCE / max_exact)
                         * (N_BUCKETS - max_exact)).astype(jnp.int32)
    return jnp.where(n < max_exact, n, jnp.minimum(large, N_BUCKETS - 1))


def _lambda_value(lam_ref, lambda_init):
    lp = lam_ref[...]
    a = jnp.sum(lp[0:1] * lp[1:2], axis=-1, keepdims=True)
    b = jnp.sum(lp[2:3] * lp[3:4], axis=-1, keepdims=True)
    return jnp.exp(a) - jnp.exp(b) + lambda_init


def _proj_kernel(x_ref, g1_ref, wt_ref, wdv_ref, wft_ref, gains_ref, fb_ref,
                 qd_ref, kd_ref, vd_ref, qf_ref, kf_ref, vf_ref, logf_ref, *row_refs, head_dim, q_scale):
    h = _rms(x_ref[0], g1_ref[...]).astype(BF16)
    width = kd_ref.shape[1]
    n_heads = width // head_dim

    def section_t(i):
        return _dot_nt(wt_ref[i * width:(i + 1) * width, :], h)

    def head_norm_t(z, i):
        z3 = z.reshape(n_heads, head_dim, z.shape[-1])
        ss = jnp.sum(z3 * z3, axis=1, keepdims=True)
        gain = gains_ref[i].reshape(n_heads, head_dim, 1)
        return (z3 * lax.rsqrt(ss * (1.0 / head_dim) + EPS) * gain).reshape(z.shape)

    qd = head_norm_t(section_t(0), 0) * q_scale
    kd = head_norm_t(section_t(1), 1)
    qf = head_norm_t(section_t(2), 2) * q_scale
    kf = head_norm_t(section_t(3), 3)
    vf = section_t(4)
    qd_ref[0] = qd.astype(qd_ref.dtype)
    kd_ref[0] = kd
    vd_ref[0] = _dot(h, wdv_ref[...])
    qf_ref[0] = qf.astype(qf_ref.dtype)
    kf_ref[0] = kf
    vf_ref[0] = vf
    logf_ref[0] = _log_sigmoid(_dot_nt(wft_ref[...], h) + fb_ref[...])
    if row_refs:
        for ref, val in zip(row_refs, (qd, kd, qf, kf, vf)):
            ref[0] = val.T


def _proj(x, g1, wt, wdv, wft, gains, fb, *, head_dim, with_rows):
    b, s, d = x.shape
    width = wdv.shape[1]
    n_f = wft.shape[0]
    tm = min(ROW_TILE, s)
    assert s % tm == 0 and tm % LANES == 0
    t_blk = lambda r: pl.BlockSpec((1, r, tm), lambda bi, si: (bi, 0, si))
    r_blk = lambda c: pl.BlockSpec((1, tm, c), lambda bi, si: (bi, si, 0))
    t_shape = lambda dt: jax.ShapeDtypeStruct((b, width, s), dt)
    r_shape = jax.ShapeDtypeStruct((b, s, width), F32)
    out_shape = [t_shape(BF16), t_shape(F32), r_shape, t_shape(BF16), t_shape(F32), t_shape(F32),
                 jax.ShapeDtypeStruct((b, n_f, s), F32)]
    out_specs = [t_blk(width), t_blk(width), r_blk(width), t_blk(width), t_blk(width), t_blk(width), t_blk(n_f)]
    if with_rows:
        out_shape += [r_shape] * 5
        out_specs += [r_blk(width)] * 5
    return pl.pallas_call(
        functools.partial(_proj_kernel, head_dim=head_dim, q_scale=head_dim ** -0.5 * LOG2E),
        out_shape=out_shape,
        grid=(b, s // tm),
        in_specs=[r_blk(d)] + [_resident(a.shape) for a in (g1, wt, wdv, wft, gains, fb)],
        out_specs=out_specs,
        compiler_params=_params(("arbitrary", "arbitrary")),
        name="proj",
    )(x, g1, wt, wdv, wft, gains, fb)


def _flash_step(scores, values, state):
    new_state = []
    for s_t, v_t, (m_prev, acc_prev) in zip(scores, values, state):
        m_new = jnp.maximum(m_prev, jnp.max(s_t, axis=0, keepdims=True))
        alpha = jnp.exp2(m_prev - m_new)
        p_t = jnp.exp2(s_t - m_new).astype(BF16)
        new_state.append((m_new, alpha * acc_prev + _dot(v_t, p_t)))
    return tuple(new_state)


def _flash_init(n_slots, rows, tq):
    return tuple((jnp.full((1, tq), NEG_INF, F32), jnp.zeros((rows, tq), F32)) for _ in range(n_slots))


def _ones_rows(seq):
    return jnp.where(lax.broadcasted_iota(jnp.int32, (BF16_ROWS, seq), 0) == 0, 1.0, 0.0).astype(BF16)


def _key_minus_query(tk, tq):
    return lax.broadcasted_iota(jnp.int32, (tk, tq), 0) - lax.broadcasted_iota(jnp.int32, (tk, tq), 1)


class _ScoreRing:
    def __init__(self, s_ref, qk):
        self.s_ref, self.qk = s_ref, qk

    def fill(self, j, slot):
        for idx, s in enumerate(self.qk(j)):
            self.s_ref[slot, idx] = s

    def scores(self, slot):
        return [self.s_ref[slot, idx] for idx in range(self.s_ref.shape[1])]

    def far_sweep(self, softmax, state, n_pairs):
        self.fill(0, 0)

        def body(jj, st):
            j = 2 * jj
            self.fill(j + 1, 1)
            st = softmax(self.scores(0), j, st)
            self.fill(j + 2, 0)
            return softmax(self.scores(1), j + 1, st)
        return lax.fori_loop(0, n_pairs, body, state)

    def tail(self, softmax, state, first, kinds, more_follows):
        assert not more_follows or len(kinds) % 2 == 0
        for n, kind in enumerate(kinds):
            slot = n % 2
            if n + 1 < len(kinds) or more_follows:
                self.fill(first + n + 1, 1 - slot)
            state = softmax(self.scores(slot), first + n, state, kind)
        return state


def _diff_attn_kernel(rb_ref, q_ref, k_ref, v_ref, lam_ref, subg_ref, o_ref,
                      kb_ref, vb_ref, bias_ref, s_ref, *, n_heads, head_dim, lambda_init):
    tq, tk = ATTN_TQ, ATTN_TK
    ratio = tq // tk
    head = pl.program_id(0)
    seq = q_ref.shape[2]
    e = 2 * head_dim
    kmq = _key_minus_query(tk, tq)
    near = [tk] + [-d * tk for d in range(ratio)]

    @pl.when(pl.program_id(1) == 0)
    def _():
        far = rb_ref[(N_BUCKETS - 1) * n_heads + head]
        for idx, delta in enumerate(near):
            bucket = _rel_bucket(jnp.maximum(delta - kmq, 0))
            val = jnp.zeros((tk, tq), F32)
            for b in range(N_BUCKETS):
                val = jnp.where(bucket == b, rb_ref[b * n_heads + head], val)
            bias_ref[idx] = (val - far) * LOG2E

    k_rows = k_ref[0].T
    first = lax.broadcasted_iota(jnp.int32, (1, e), 1) < head_dim
    kb_ref[0] = jnp.where(first, k_rows, 0.0).astype(BF16)
    kb_ref[1] = jnp.where(first, 0.0, k_rows).astype(BF16)
    vb_ref[0:e, :] = v_ref[0].T.astype(BF16)
    vb_ref[e:, :] = _ones_rows(seq)
    lam = _lambda_value(lam_ref, lambda_init)

    def q_block(i, carry):
        qs = pl.multiple_of(i * tq, tq)
        q_t = q_ref[0, :, pl.ds(qs, tq)]

        def qk(j):
            ks = pl.multiple_of(j * tk, tk)
            return [_dot(kb_ref[mp, pl.ds(ks, tk), :], q_t) for mp in range(2)]

        def softmax(scores, j, state, near_idx=None):
            if near_idx is not None:
                scores = [s + bias_ref[near_idx] for s in scores]
                if near[near_idx] <= 0:
                    visible = kmq <= near[near_idx]
                    scores = [jnp.where(visible, s, NEG_INF) for s in scores]
            v_t = vb_ref[:, pl.ds(pl.multiple_of(j * tk, tk), tk)]
            return _flash_step(scores, [v_t, v_t], state)

        ring = _ScoreRing(s_ref, qk)
        first_near = ratio * (i - 1)
        state = ring.far_sweep(softmax, _flash_init(2, vb_ref.shape[0], tq),
                               jnp.maximum(first_near, 0) // 2)
        before = [None] * (ratio - 1) + [0]
        state = lax.cond(i >= 1, lambda st: ring.tail(softmax, st, first_near, before, True),
                         lambda st: st, state)
        state = ring.tail(softmax, state, ratio * i, [1 + d for d in range(ratio)], False)
        (_, acc0), (_, acc1) = state
        o_t = acc0[0:e] / acc0[e:e + 1] - lam * (acc1[0:e] / acc1[e:e + 1])
        o = _rms(o_t.T, subg_ref[...]) * (1.0 - lambda_init)
        o_ref[0, pl.ds(qs, tq), :] = o.astype(o_ref.dtype)
        return carry

    lax.fori_loop(0, seq // tq, q_block, 0)


def _diff_attention(rb_flat, q_t, k_t, v, lam_p, subg, *, n_heads, head_dim, lambda_init):
    b, w, s = q_t.shape
    e = 2 * head_dim
    tq, tk = ATTN_TQ, ATTN_TK
    assert s % tq == 0 and tq % tk == 0 and tk % LANES == 0 and tk >= MAX_DISTANCE and e == LANES
    t_blk = pl.BlockSpec((1, e, s), lambda h, bi, *_: (bi, h, 0))
    r_blk = pl.BlockSpec((1, s, e), lambda h, bi, *_: (bi, 0, h))
    return pl.pallas_call(
        functools.partial(_diff_attn_kernel, n_heads=n_heads, head_dim=head_dim, lambda_init=lambda_init),
        out_shape=jax.ShapeDtypeStruct((b, s, w), BF16),
        grid_spec=pltpu.PrefetchScalarGridSpec(
            num_scalar_prefetch=1,
            grid=(n_heads, b),
            in_specs=[t_blk, t_blk, r_blk, _resident(lam_p.shape), _resident(subg.shape)],
            out_specs=r_blk,
            scratch_shapes=[pltpu.VMEM((2, s, e), BF16), pltpu.VMEM((e + BF16_ROWS, s), BF16),
                            pltpu.VMEM((tq // tk + 1, tk, tq), F32),
                            pltpu.VMEM((2, 2, tk, tq), F32)]),
        compiler_params=_params(("arbitrary", "arbitrary")),
        name="diff_attention",
    )(rb_flat, q_t, k_t, v, lam_p, subg)


def _fox_attn_kernel(q_ref, k_ref, v_ref, logf_ref, o_ref, kb_ref, vb_ref, c_ref, s_ref, *, head_dim):
    tq, tk = ATTN_TQ, ATTN_TK
    ratio = tq // tk
    pair = pl.program_id(1)
    seq = q_ref.shape[2]
    e = 2 * head_dim
    kmq = _key_minus_query(tk, tq)

    @pl.when(pair == 0)
    def _():
        upper = (lax.broadcasted_iota(jnp.int32, (tk, tk), 0)
                 <= lax.broadcasted_iota(jnp.int32, (tk, tk), 1)).astype(F32)
        carry = jnp.zeros((logf_ref.shape[1], 1), F32)
        for blk in range(seq // tk):
            c = _dot(logf_ref[0, :, blk * tk:(blk + 1) * tk], upper, HIGHEST) + carry
            c_ref[:, blk * tk:(blk + 1) * tk] = c
            carry = c[:, tk - 1:tk]

    k_rows = k_ref[0].T
    lane = lax.broadcasted_iota(jnp.int32, (1, e), 1)
    for hh in range(2):
        spare = (1 - hh) * head_dim
        rest = jnp.broadcast_to(c_ref[pl.ds(2 * pair + hh, 1), :] * LOG2E, (LANES, seq)).T
        k_aug = jnp.where(lane // head_dim == hh, k_rows, 0.0)
        for part in range(DECAY_PARTS):
            piece = rest.astype(BF16).astype(F32)
            k_aug = jnp.where(lane == spare + part, piece, k_aug)
            rest = rest - piece
        kb_ref[hh] = k_aug.astype(BF16)
        vb_ref[hh, 0:head_dim, :] = v_ref[0, hh * head_dim:(hh + 1) * head_dim, :].astype(BF16)
        vb_ref[hh, head_dim:, :] = _ones_rows(seq)

    def q_block(i, carry):
        qs = pl.multiple_of(i * tq, tq)
        q_t = q_ref[0, :, pl.ds(qs, tq)]
        row = lax.broadcasted_iota(jnp.int32, (e, 1), 0)
        q_aug = []
        for hh in range(2):
            spare = (1 - hh) * head_dim
            minus_one = jnp.logical_and(row >= spare, row < spare + DECAY_PARTS)
            q_aug.append(jnp.where(row // head_dim == hh, q_t,
                                   jnp.where(minus_one, -1.0, 0.0).astype(BF16)))

        def qk(j):
            ks = pl.multiple_of(j * tk, tk)
            return [_dot(kb_ref[hh, pl.ds(ks, tk), :], q_aug[hh]) for hh in range(2)]

        def softmax(scores, j, state, diagonal=None):
            ks = pl.multiple_of(j * tk, tk)
            if diagonal is not None:
                visible = kmq <= -diagonal * tk
                scores = [jnp.where(visible, s, NEG_INF) for s in scores]
            values = [vb_ref[hh, :, pl.ds(ks, tk)] for hh in range(2)]
            return _flash_step(scores, values, state)

        ring = _ScoreRing(s_ref, qk)
        state = ring.far_sweep(softmax, _flash_init(2, vb_ref.shape[1], tq), ratio * i // 2)
        state = ring.tail(softmax, state, ratio * i, list(range(ratio)), False)
        o_t = jnp.concatenate([acc[0:head_dim] / acc[head_dim:head_dim + 1] for _, acc in state], axis=0)
        o_ref[0, pl.ds(qs, tq), :] = o_t.T.astype(o_ref.dtype)
        return carry

    lax.fori_loop(0, seq // tq, q_block, 0)


def _fox_attention(q_t, k_t, v_t, logf_t, *, head_dim):
    b, w, s = q_t.shape
    e = 2 * head_dim
    tq, tk = ATTN_TQ, ATTN_TK
    n_f = logf_t.shape[1]
    assert s % tq == 0 and tq % tk == 0 and tk % LANES == 0 and e == LANES and w // e * 2 == n_f
    assert DECAY_PARTS <= head_dim
    t_blk = pl.BlockSpec((1, e, s), lambda bi, g: (bi, g, 0))
    return pl.pallas_call(
        functools.partial(_fox_attn_kernel, head_dim=head_dim),
        out_shape=jax.ShapeDtypeStruct((b, s, w), BF16),
        grid=(b, w // e),
        in_specs=[t_blk, t_blk, t_blk, pl.BlockSpec((1, n_f, s), lambda bi, g: (bi, 0, 0))],
        out_specs=pl.BlockSpec((1, s, e), lambda bi, g: (bi, 0, g)),
        scratch_shapes=[pltpu.VMEM((2, s, e), BF16), pltpu.VMEM((2, head_dim + BF16_ROWS, s), BF16),
                        pltpu.VMEM((n_f, s), F32), pltpu.VMEM((2, 2, tk, tq), F32)],
        compiler_params=_params(("arbitrary", "arbitrary")),
        name="fox_attention",
    )(q_t, k_t, v_t, logf_t)


def _decode_kernel(pt_ref, qd_ref, kd_ref, vd_ref, qf_ref, kf_ref, vf_ref, lfn_ref,
                   rbt_ref, lam_ref, subg_ref, *rest,
                   n_pages, page_size, head_dim, lambda_init):
    pps = PAGES_PER_STEP
    pages = [rest[5 * p:5 * p + 5] for p in range(pps)]
    od_ref, of_ref = rest[5 * pps:5 * pps + 2]
    (md_ref, ld_ref, accd_ref, mf_ref, lf_ref, accf_ref, self_ref, carry_ref, bias_ref) = rest[5 * pps + 2:]
    grp = pl.program_id(1)
    n_maps = md_ref.shape[0]
    width = qd_ref.shape[-1]
    n_dh = n_maps // 2
    e = 2 * head_dim

    map_of_lane = lax.broadcasted_iota(jnp.int32, (n_maps, width), 1) // head_dim
    row = lax.broadcasted_iota(jnp.int32, (n_maps, width), 0)
    own = map_of_lane == row
    row_e = lax.broadcasted_iota(jnp.int32, (n_maps, e), 0)

    @pl.when(jnp.logical_and(pl.program_id(0) == 0, grp == 0))
    def _():
        lane = lax.broadcasted_iota(jnp.int32, (n_maps, page_size), 1)
        bucket = _rel_bucket(page_size - lane)
        val = jnp.zeros((n_maps, page_size), F32)
        for b in range(N_BUCKETS):
            val = jnp.where(bucket == b, rbt_ref[:, b:b + 1], val)
        bias_ref[...] = (val - rbt_ref[:, N_BUCKETS - 1:N_BUCKETS]) * LOG2E

    qd = jnp.where(own, qd_ref[0], 0.0).astype(BF16)
    qf = jnp.where(own, qf_ref[0], 0.0).astype(BF16)

    @pl.when(grp == 0)
    def _():
        def self_score(q, k_ref):
            k = k_ref[0].astype(BF16).astype(F32)
            return jnp.sum(q.astype(F32) * k, axis=-1, keepdims=True)
        bias0 = (rbt_ref[:, 0:1] - rbt_ref[:, N_BUCKETS - 1:N_BUCKETS]) * LOG2E
        md_ref[...] = self_score(qd, kd_ref) + bias0
        mf_ref[...] = self_score(qf, kf_ref)
        ld_ref[...] = jnp.ones(ld_ref.shape, F32)
        lf_ref[...] = jnp.ones(lf_ref.shape, F32)
        v_new = vd_ref[0]
        acc0 = jnp.zeros(accd_ref.shape, F32)
        for h in range(n_dh):
            acc0 = jnp.where(row_e // 2 == h, v_new[h:h + 1, :], acc0)
        accd_ref[...] = acc0
        accf_ref[...] = jnp.zeros(accf_ref.shape, F32)
        self_ref[...] = jnp.ones(self_ref.shape, F32)
        carry_ref[...] = lfn_ref[0] * LOG2E

    later = (lax.broadcasted_iota(jnp.int32, (page_size, page_size), 0)
             >= lax.broadcasted_iota(jnp.int32, (page_size, page_size), 1)).astype(F32)

    def softmax_update(s, m_ref, l_ref):
        m_prev = m_ref[...]
        m_new = jnp.maximum(m_prev, jnp.max(s, axis=-1, keepdims=True))
        alpha = jnp.exp2(m_prev - m_new)
        p = jnp.exp2(s - m_new)
        l_ref[...] = alpha * l_ref[...] + jnp.sum(p, axis=-1, keepdims=True)
        m_ref[...] = m_new
        return alpha, p

    cdk, cdv, cfk, cfv, clf = zip(*pages)

    def keys_side_by_side(refs):
        return jnp.concatenate([ref[0].astype(BF16) for ref in refs], axis=1)

    newest = jnp.where(grp == 0, bias_ref[...], 0.0)
    bias = jnp.concatenate([newest, jnp.zeros((n_maps, (pps - 1) * page_size), F32)], axis=1)
    alpha, p = softmax_update(_dot(qd, keys_side_by_side(cdk)) + bias, md_ref, ld_ref)
    p = p.astype(BF16)
    pv = jnp.zeros(accd_ref.shape, F32)
    for h in range(n_dh):
        v_h = jnp.concatenate([ref[0, pl.ds(h, page_size, stride=n_dh), :].astype(BF16) for ref in cdv],
                              axis=0)
        pv = jnp.where(row_e // 2 == h, _dot(p, v_h), pv)
    accd_ref[...] = alpha * accd_ref[...] + pv

    lf = jnp.concatenate([ref[0] for ref in clf], axis=0) * LOG2E
    incl = _dot(lf, later, HIGHEST)
    carry = carry_ref[...]
    decay = []
    for p_idx in range(pps):
        rows = slice(p_idx * n_maps, (p_idx + 1) * n_maps)
        decay.append(incl[rows] - lf[rows] + carry)
        carry = carry + incl[rows, 0:1]
    carry_ref[...] = carry
    s_f = _dot(qf, keys_side_by_side(cfk)) + jnp.concatenate(decay, axis=1)
    alpha, p = softmax_update(s_f, mf_ref, lf_ref)
    acc = alpha[:, :, None] * accf_ref[...]
    for p_idx in range(pps):
        p_page = p[:, p_idx * page_size:(p_idx + 1) * page_size]
        acc = acc + p_page[:, None, :] * cfv[p_idx][0].reshape(n_maps, head_dim, page_size)
    accf_ref[...] = acc
    self_ref[...] = alpha * self_ref[...]

    @pl.when(grp == pl.num_programs(1) - 1)
    def _():
        lam = _lambda_value(lam_ref, lambda_init)
        sign = jnp.where(row_e[:, 0:1] % 2 == 0, 1.0, -lam)
        a = accd_ref[...] / ld_ref[...] * sign
        pick = (lax.broadcasted_iota(jnp.int32, (n_dh, n_maps), 1) // 2
                == lax.broadcasted_iota(jnp.int32, (n_dh, n_maps), 0)).astype(F32)
        o_d = _dot(pick, a, HIGHEST)
        od_ref[0] = _rms(o_d, subg_ref[...]) * (1.0 - lambda_init)

        def spread(col):
            return jnp.sum(jnp.where(own, col, 0.0), axis=0, keepdims=True)
        past = jnp.sum(accf_ref[...].reshape(width, page_size).T, axis=0, keepdims=True)
        of_ref[0] = (past + spread(self_ref[...]) * vf_ref[0]) / spread(lf_ref[...])


def _decode_attention(page_table, qd, kd, vd, qf, kf, vf, logf_new, rbt, lam_p, subg,
                      cdk, cdv, cfk, cfv, clf, *, head_dim, lambda_init):
    r, n_pages = page_table.shape
    _, width, page_size = cdk.shape
    n_f = clf.shape[1]
    n_maps = width // head_dim
    n_dh = n_maps // 2
    e = 2 * head_dim
    pps = PAGES_PER_STEP
    assert n_pages % pps == 0 and n_maps == n_f
    assert page_size >= MAX_DISTANCE
    row = pl.BlockSpec((1, 1, width), lambda i, g, pt: (i, 0, 0))
    head_rows = pl.BlockSpec((1, n_dh, e), lambda i, g, pt: (i, 0, 0))

    def paged(p, rows):
        def index(i, g, pt):
            return (pt[i, n_pages - 1 - (g * pps + p)], 0, 0)
        return pl.BlockSpec((1, rows, cdk.shape[2]), index)

    in_specs = [row, row, head_rows, row, row, row, pl.BlockSpec((1, n_f, 1), lambda i, g, pt: (i, 0, 0)),
                _resident(rbt.shape), _resident(lam_p.shape), _resident(subg.shape)]
    operands = [qd, kd, vd, qf, kf, vf, logf_new, rbt, lam_p, subg]
    for p in range(pps):
        in_specs += [paged(p, width), paged(p, cdv.shape[1]), paged(p, width), paged(p, width), paged(p, n_f)]
        operands += [cdk, cdv, cfk, cfv, clf]
    return pl.pallas_call(
        functools.partial(_decode_kernel, n_pages=n_pages, page_size=page_size,
                          head_dim=head_dim, lambda_init=lambda_init),
        out_shape=[jax.ShapeDtypeStruct((r, n_dh, e), F32), jax.ShapeDtypeStruct((r, 1, width), F32)],
        grid_spec=pltpu.PrefetchScalarGridSpec(
            num_scalar_prefetch=1,
            grid=(r, n_pages // pps),
            in_specs=in_specs,
            out_specs=[head_rows, row],
            scratch_shapes=[pltpu.VMEM((n_maps, 1), F32), pltpu.VMEM((n_maps, 1), F32),
                            pltpu.VMEM((n_maps, e), F32),
                            pltpu.VMEM((n_maps, 1), F32), pltpu.VMEM((n_maps, 1), F32),
                            pltpu.VMEM((n_maps, head_dim, page_size), F32),
                            pltpu.VMEM((n_maps, 1), F32),
                            pltpu.VMEM((n_f, 1), F32), pltpu.VMEM((n_maps, page_size), F32)]),
        compiler_params=_params(("arbitrary", "arbitrary")),
        name="decode_attention",
    )(page_table, *operands)


def _merge_kernel(x_ref, od_ref, of_ref, g1_ref, wg_ref, gb_ref, wa_ref, wb_ref, wo_ref, o_ref):
    x = x_ref[...]
    d = x.shape[-1]
    h = _rms(x, g1_ref[...]).astype(BF16)
    ya = _dot(od_ref[...].astype(BF16), wa_ref[...])
    yb = _dot(of_ref[...].astype(BF16), wb_ref[...])
    gate_a = jax.nn.sigmoid(_dot(h, wg_ref[:, :d]) + gb_ref[:, :d])
    gate_b = jax.nn.sigmoid(_dot(h, wg_ref[:, d:]) + gb_ref[:, d:])
    merged = gate_a * ya + gate_b * yb
    o_ref[...] = x + _dot(merged.astype(BF16), wo_ref[...])


def _merge(x, od, of, g1, wg, gb, wa, wb, wo):
    m, d = x.shape
    tm = min(ROW_TILE, m)
    assert m % tm == 0
    row = lambda c: pl.BlockSpec((tm, c), lambda i: (i, 0))
    return pl.pallas_call(
        _merge_kernel,
        out_shape=jax.ShapeDtypeStruct((m, d), F32),
        grid=(m // tm,),
        in_specs=[row(d), row(od.shape[1]), row(of.shape[1])]
                 + [_resident(a.shape) for a in (g1, wg, gb, wa, wb, wo)],
        out_specs=row(d),
        compiler_params=_params(("arbitrary",)),
        name="merge",
    )(x, od, of, g1, wg, gb, wa, wb, wo)


def _ffn_kernel(x_ref, g2_ref, wgu_ref, wdn_ref, o_ref):
    x = x_ref[...]
    d_ff = wdn_ref.shape[0]
    h = _rms(x, g2_ref[...]).astype(BF16)
    gate = _dot(h, wgu_ref[:, :d_ff])
    up = _dot(h, wgu_ref[:, d_ff:])
    act = (gate * jax.nn.sigmoid(gate) * up).astype(BF16)
    o_ref[...] = x + _dot(act, wdn_ref[...])


def _ffn(x, g2, wgu, wdn):
    m, d = x.shape
    tm = min(FFN_ROW_TILE, m)
    assert m % tm == 0
    row = pl.BlockSpec((tm, d), lambda i: (i, 0))
    return pl.pallas_call(
        _ffn_kernel,
        out_shape=jax.ShapeDtypeStruct((m, d), F32),
        grid=(m // tm,),
        in_specs=[row, _resident(g2.shape), _resident(wgu.shape), _resident(wdn.shape)],
        out_specs=row,
        compiler_params=_params(("arbitrary",)),
        name="ffn",
    )(x, g2, wgu, wdn)


def kernel(x_prompt, x_sample, cache_diff_k, cache_diff_v, cache_fox_k, cache_fox_v, cache_fox_logf,
           page_table, rel_bias, norm1_g, w_in, diff_q_g, diff_k_g, fox_q_g, fox_k_g, diff_lambda,
           fox_f_b, gate_b, diff_subln_g, w_branch_a, w_branch_b, w_out, norm2_g, w_gate_up, w_down):
    depth = w_in.shape[0]
    batch, seq, d = x_prompt.shape
    dec_batch, dec_seq, _ = x_sample.shape
    assert dec_seq == 1
    _, n_phys, page_size, n_dh, _, head_dim = cache_diff_k.shape
    n_f = cache_fox_k.shape[3]
    e = 2 * head_dim
    width = n_dh * e
    assert n_f * head_dim == width

    rb_flat = rel_bias.reshape(-1)
    rbt = jnp.repeat(rel_bias.T, 2, axis=0)

    yp, ys = x_prompt, x_sample.reshape(1, dec_batch, d)
    rows_p, rows_s = [], []
    for l in range(depth):
        lambda_init = 0.8 - 0.6 * math.exp(-0.3 * l)
        w_t = w_in[l].T
        sec = lambda i: w_t[i * width:(i + 1) * width]
        wt = jnp.concatenate([sec(0), sec(1), sec(3), sec(4), sec(5)], axis=0).astype(BF16)
        wdv = sec(2).T.astype(BF16)
        wft = w_t[6 * width:6 * width + n_f].astype(BF16)
        wg = w_t[6 * width + n_f:].T.astype(BF16)
        fb = fox_f_b[l].reshape(n_f, 1)
        gains = jnp.stack([jnp.tile(g[l], width // head_dim)
                           for g in (diff_q_g, diff_k_g, fox_q_g, fox_k_g)]).reshape(4, width, 1)
        g1 = norm1_g[l].reshape(1, d)
        g2 = norm2_g[l].reshape(1, d)
        gb = gate_b[l].reshape(1, 2 * d)
        subg = diff_subln_g[l].reshape(1, e)
        wa, wb, wo = (w[l].astype(BF16) for w in (w_branch_a, w_branch_b, w_out))
        wgu, wdn = w_gate_up[l].astype(BF16), w_down[l].astype(BF16)
        lam_p = diff_lambda[l]

        def tail(x, od, of):
            return _ffn(_merge(x, od, of, g1, wg, gb, wa, wb, wo), g2, wgu, wdn)

        def cache_rows(kd_t, vd, kf_t, vf_t, logf_t):
            b, _, s = kd_t.shape
            return (jnp.transpose(kd_t.reshape(b, n_dh, 2, head_dim, s), (0, 4, 1, 2, 3)),
                    vd.reshape(b, s, n_dh, e),
                    jnp.transpose(kf_t.reshape(b, n_f, head_dim, s), (0, 3, 1, 2)),
                    jnp.transpose(vf_t.reshape(b, n_f, head_dim, s), (0, 3, 1, 2)),
                    jnp.transpose(logf_t, (0, 2, 1)))

        qd_t, kd_t, vd, qf_t, kf_t, vf_t, logf_t = _proj(
            yp, g1, wt, wdv, wft, gains, fb, head_dim=head_dim, with_rows=False)
        od = _diff_attention(rb_flat, qd_t, kd_t, vd, lam_p, subg,
                             n_heads=n_dh, head_dim=head_dim, lambda_init=lambda_init)
        of = _fox_attention(qf_t, kf_t, vf_t, logf_t, head_dim=head_dim)
        yp = tail(yp.reshape(batch * seq, d), od.reshape(batch * seq, width),
                  of.reshape(batch * seq, width)).reshape(batch, seq, d)
        rows_p.append(cache_rows(kd_t, vd, kf_t, vf_t, logf_t))

        (qd_t, kd_t, vd, qf_t, kf_t, vf_t, logf_t,
         qd_r, kd_r, qf_r, kf_r, vf_r) = _proj(ys, g1, wt, wdv, wft, gains, fb, head_dim=head_dim, with_rows=True)
        as_row = lambda a: a.reshape(dec_batch, 1, width)
        cdk = jnp.transpose(cache_diff_k[l], (0, 2, 3, 4, 1)).reshape(n_phys, width, page_size)
        cdv = cache_diff_v[l].reshape(n_phys, page_size * n_dh, e)
        cfk = jnp.transpose(cache_fox_k[l], (0, 2, 3, 1)).reshape(n_phys, width, page_size)
        cfv = jnp.transpose(cache_fox_v[l], (0, 2, 3, 1)).reshape(n_phys, width, page_size)
        clf = jnp.transpose(cache_fox_logf[l], (0, 2, 1))
        od, of = _decode_attention(
            page_table, as_row(qd_r), as_row(kd_r), vd.reshape(dec_batch, n_dh, e),
            as_row(qf_r), as_row(kf_r), as_row(vf_r),
            jnp.transpose(logf_t, (2, 1, 0)), rbt, lam_p, subg,
            cdk, cdv, cfk, cfv, clf, head_dim=head_dim, lambda_init=lambda_init)
        ys = tail(ys.reshape(dec_batch, d), od.reshape(dec_batch, width),
                  of.reshape(dec_batch, width)).reshape(1, dec_batch, d)
        rows_s.append(tuple(jnp.swapaxes(a, 0, 1) for a in cache_rows(kd_t, vd, kf_t, vf_t, logf_t)))

    stack = lambda rows, i: jnp.stack([r[i] for r in rows], axis=0)
    return (yp, ys.reshape(dec_batch, 1, d),
            *(stack(rows_p, i) for i in range(5)), *(stack(rows_s, i) for i in range(5)))
```

```python
import functools
import math

import jax
import jax.numpy as jnp
from jax import lax
from jax.experimental import pallas as pl
from jax.experimental.pallas import tpu as pltpu

F32 = jnp.float32
BF16 = jnp.bfloat16

N_BUCKETS = 32
MAX_DISTANCE = 128
EPS = 1e-6
NEG_INF = -1e30
LOG2E = math.log2(math.e)

LANES = 128
SUBLANES = 8
BF16_ROWS = 16
VMEM_LIMIT = 56 * 1024 * 1024

ROW_TILE = 512
FFN_ROW_TILE = 256
ATTN_TQ = 512
ATTN_TK = 256
PAGES_PER_STEP = 16
DECAY_PARTS = 3

NT_DIMS = (((1,), (1,)), ((), ()))
HIGHEST = lax.Precision.HIGHEST


def _resident(shape):
    zeros = (0,) * len(shape)
    return pl.BlockSpec(shape, lambda *_: zeros, pipeline_mode=pl.Buffered(1))


def _params(semantics, vmem=VMEM_LIMIT):
    return pltpu.CompilerParams(dimension_semantics=semantics, vmem_limit_bytes=vmem)


def _rms(x, g):
    return x * lax.rsqrt(jnp.mean(x * x, axis=-1, keepdims=True) + EPS) * g


def _dot(a, b, precision=None):
    return jnp.dot(a, b, preferred_element_type=F32, precision=precision)


def _dot_nt(a, b):
    return lax.dot_general(a, b, NT_DIMS, preferred_element_type=F32)


def _log_sigmoid(x):
    return jnp.minimum(x, 0.0) - jnp.log1p(jnp.exp(-jnp.abs(x)))


def _rel_bucket(n):
    max_exact = N_BUCKETS // 2
    nf = jnp.maximum(n, 1).astype(F32)
    large = max_exact + (jnp.log(nf / max_exact) / math.log(MAX_DISTANCE / max_exact)
                         * (N_BUCKETS - max_exact)).astype(jnp.int32)
    return jnp.where(n < max_exact, n, jnp.minimum(large, N_BUCKETS - 1))


def _lambda_value(lam_ref, lambda_init):
    lp = lam_ref[...]
    a = jnp.sum(lp[0:1] * lp[1:2], axis=-1, keepdims=True)
    b = jnp.sum(lp[2:3] * lp[3:4], axis=-1, keepdims=True)
    return jnp.exp(a) - jnp.exp(b) + lambda_init


def _proj_kernel(x_ref, g1_ref, wt_ref, gains_ref, fb_ref,
                 qd_ref, kd_ref, vd_ref, vdt_ref, qf_ref, kf_ref, vf_ref, logf_ref, *row_refs,
                 head_dim, q_scale):
    h = _rms(x_ref[0], g1_ref[...])
    width = kd_ref.shape[1]
    n_heads = width // head_dim
    n_f = logf_ref.shape[1]
    section = lambda i: _dot_nt(wt_ref[i * width:(i + 1) * width, :], h)

    def head_norm_t(z, i):
        z3 = z.reshape(n_heads, head_dim, z.shape[-1])
        ss = jnp.sum(z3 * z3, axis=1, keepdims=True)
        gain = gains_ref[i].reshape(n_heads, head_dim, 1)
        return (z3 * lax.rsqrt(ss * (1.0 / head_dim) + EPS) * gain).reshape(z.shape)

    qd = head_norm_t(section(0), 0) * q_scale
    kd = head_norm_t(section(1), 1)
    qf = head_norm_t(section(3), 2) * q_scale
    kf = head_norm_t(section(4), 3)
    vd_t = section(2)
    vd_rows = vd_t.T
    vf = section(5)
    qd_ref[0] = qd.astype(qd_ref.dtype)
    kd_ref[0] = kd
    e = vd_ref.shape[-1]
    for hd in range(width // e):
        vd_ref[0, pl.ds(hd, vd_rows.shape[0], stride=width // e), :] = vd_rows[:, hd * e:(hd + 1) * e]
    vdt_ref[0] = vd_t.astype(vdt_ref.dtype)
    qf_ref[0] = qf.astype(qf_ref.dtype)
    kf_ref[0] = kf
    vf_ref[0] = vf
    logf_ref[0] = _log_sigmoid(_dot_nt(wt_ref[6 * width:6 * width + n_f, :], h) + fb_ref[...])
    if row_refs:
        for ref, val in zip(row_refs, (qd, kd, qf, kf, vf)):
            ref[0] = val.T


def _proj(x, g1, wt, gains, fb, *, width, head_dim, with_rows):
    b, s, d = x.shape
    n_f = fb.shape[0]
    e = 2 * head_dim
    tm = min(ROW_TILE, s)
    assert s % tm == 0 and tm % LANES == 0 and width % e == 0
    t_blk = lambda r: pl.BlockSpec((1, r, tm), lambda bi, si: (bi, 0, si))
    r_blk = lambda c: pl.BlockSpec((1, tm, c), lambda bi, si: (bi, si, 0))
    t_shape = lambda dt: jax.ShapeDtypeStruct((b, width, s), dt)
    r_shape = jax.ShapeDtypeStruct((b, s, width), F32)
    heads_per_token = width // e
    out_shape = [t_shape(BF16), t_shape(F32), jax.ShapeDtypeStruct((b, s * heads_per_token, e), F32),
                 t_shape(BF16), t_shape(BF16), t_shape(F32), t_shape(F32),
                 jax.ShapeDtypeStruct((b, n_f, s), F32)]
    out_specs = [t_blk(width), t_blk(width),
                 pl.BlockSpec((1, tm * heads_per_token, e), lambda bi, si: (bi, si, 0)),
                 t_blk(width), t_blk(width), t_blk(width), t_blk(width), t_blk(n_f)]
    if with_rows:
        out_shape += [r_shape] * 5
        out_specs += [r_blk(width)] * 5
    return pl.pallas_call(
        functools.partial(_proj_kernel, head_dim=head_dim, q_scale=head_dim ** -0.5 * LOG2E),
        out_shape=out_shape,
        grid=(b, s // tm),
        in_specs=[r_blk(d)] + [_resident(a.shape) for a in (g1, wt, gains, fb)],
        out_specs=out_specs,
        compiler_params=_params(("arbitrary", "arbitrary")),
        name="proj",
    )(x, g1, wt, gains, fb)


def _flash_step(scores, values, state):
    new_state = []
    for s_t, v_t, (m_prev, acc_prev) in zip(scores, values, state):
        m_new = jnp.maximum(m_prev, jnp.max(s_t, axis=0, keepdims=True))
        alpha = jnp.exp2(m_prev - m_new)
        p_t = jnp.exp2(s_t - m_new).astype(BF16)
        new_state.append((m_new, alpha * acc_prev + _dot(v_t, p_t)))
    return tuple(new_state)


def _flash_init(n_slots, rows, tq):
    return tuple((jnp.full((1, tq), NEG_INF, F32), jnp.zeros((rows, tq), F32)) for _ in range(n_slots))


def _ones_rows(seq):
    return jnp.where(lax.broadcasted_iota(jnp.int32, (BF16_ROWS, seq), 0) == 0, 1.0, 0.0).astype(BF16)


def _key_minus_query(tk, tq):
    return lax.broadcasted_iota(jnp.int32, (tk, tq), 0) - lax.broadcasted_iota(jnp.int32, (tk, tq), 1)


class _ScoreRing:
    def __init__(self, s_ref, qk):
        self.s_ref, self.qk = s_ref, qk

    def fill(self, j, slot):
        for idx, s in enumerate(self.qk(j)):
            self.s_ref[slot, idx] = s

    def scores(self, slot):
        return [self.s_ref[slot, idx] for idx in range(self.s_ref.shape[1])]

    def far_sweep(self, softmax, state, n_pairs):
        self.fill(0, 0)

        def body(jj, st):
            j = 2 * jj
            self.fill(j + 1, 1)
            st = softmax(self.scores(0), j, st)
            self.fill(j + 2, 0)
            return softmax(self.scores(1), j + 1, st)
        return lax.fori_loop(0, n_pairs, body, state)

    def tail(self, softmax, state, first, kinds, more_follows):
        assert not more_follows or len(kinds) % 2 == 0
        for n, kind in enumerate(kinds):
            slot = n % 2
            if n + 1 < len(kinds) or more_follows:
                self.fill(first + n + 1, 1 - slot)
            state = softmax(self.scores(slot), first + n, state, kind)
        return state


def _diff_attn_kernel(rb_ref, q_ref, k_ref, v_ref, lam_ref, subg_ref, o_ref,
                      kb_ref, vb_ref, bias_ref, s_ref, *, n_heads, head_dim, lambda_init):
    tq, tk = ATTN_TQ, ATTN_TK
    ratio = tq // tk
    head = pl.program_id(0)
    seq = q_ref.shape[2]
    e = 2 * head_dim
    kmq = _key_minus_query(tk, tq)
    near = [tk] + [-d * tk for d in range(ratio)]

    @pl.when(pl.program_id(1) == 0)
    def _():
        far = rb_ref[(N_BUCKETS - 1) * n_heads + head]
        for idx, delta in enumerate(near):
            bucket = _rel_bucket(jnp.maximum(delta - kmq, 0))
            val = jnp.zeros((tk, tq), F32)
            for b in range(N_BUCKETS):
                val = jnp.where(bucket == b, rb_ref[b * n_heads + head], val)
            bias_ref[idx] = (val - far) * LOG2E

    k_rows = k_ref[0].T
    first = lax.broadcasted_iota(jnp.int32, (1, e), 1) < head_dim
    kb_ref[0] = jnp.where(first, k_rows, 0.0).astype(BF16)
    kb_ref[1] = jnp.where(first, 0.0, k_rows).astype(BF16)
    vb_ref[0:e, :] = v_ref[0]
    vb_ref[e:, :] = _ones_rows(seq)
    lam = _lambda_value(lam_ref, lambda_init)

    def q_block(i, carry):
        qs = pl.multiple_of(i * tq, tq)
        q_t = q_ref[0, :, pl.ds(qs, tq)]

        def qk(j):
            ks = pl.multiple_of(j * tk, tk)
            return [_dot(kb_ref[mp, pl.ds(ks, tk), :], q_t) for mp in range(2)]

        def softmax(scores, j, state, near_idx=None):
            if near_idx is not None:
                scores = [s + bias_ref[near_idx] for s in scores]
                if near[near_idx] <= 0:
                    visible = kmq <= near[near_idx]
                    scores = [jnp.where(visible, s, NEG_INF) for s in scores]
            v_t = vb_ref[:, pl.ds(pl.multiple_of(j * tk, tk), tk)]
            return _flash_step(scores, [v_t, v_t], state)

        ring = _ScoreRing(s_ref, qk)
        first_near = ratio * (i - 1)
        state = ring.far_sweep(softmax, _flash_init(2, vb_ref.shape[0], tq),
                               jnp.maximum(first_near, 0) // 2)
        before = [None] * (ratio - 1) + [0]
        state = lax.cond(i >= 1, lambda st: ring.tail(softmax, st, first_near, before, True),
                         lambda st: st, state)
        state = ring.tail(softmax, state, ratio * i, [1 + d for d in range(ratio)], False)
        (_, acc0), (_, acc1) = state
        o_t = acc0[0:e] / acc0[e:e + 1] - lam * (acc1[0:e] / acc1[e:e + 1])
        o = _rms(o_t.T, subg_ref[...]) * (1.0 - lambda_init)
        o_ref[0, pl.ds(qs, tq), :] = o.astype(o_ref.dtype)
        return carry

    lax.fori_loop(0, seq // tq, q_block, 0)


def _diff_attention(rb_flat, q_t, k_t, v, lam_p, subg, *, n_heads, head_dim, lambda_init):
    b, w, s = q_t.shape
    e = 2 * head_dim
    tq, tk = ATTN_TQ, ATTN_TK
    assert s % tq == 0 and tq % tk == 0 and tk % LANES == 0 and tk >= MAX_DISTANCE and e == LANES
    t_blk = pl.BlockSpec((1, e, s), lambda h, bi, *_: (bi, h, 0))
    r_blk = pl.BlockSpec((1, s, e), lambda h, bi, *_: (bi, 0, h))
    return pl.pallas_call(
        functools.partial(_diff_attn_kernel, n_heads=n_heads, head_dim=head_dim, lambda_init=lambda_init),
        out_shape=jax.ShapeDtypeStruct((b, s, w), BF16),
        grid_spec=pltpu.PrefetchScalarGridSpec(
            num_scalar_prefetch=1,
            grid=(n_heads, b),
            in_specs=[t_blk, t_blk, t_blk, _resident(lam_p.shape), _resident(subg.shape)],
            out_specs=r_blk,
            scratch_shapes=[pltpu.VMEM((2, s, e), BF16), pltpu.VMEM((e + BF16_ROWS, s), BF16),
                            pltpu.VMEM((tq // tk + 1, tk, tq), F32),
                            pltpu.VMEM((2, 2, tk, tq), F32)]),
        compiler_params=_params(("arbitrary", "arbitrary")),
        name="diff_attention",
    )(rb_flat, q_t, k_t, v, lam_p, subg)


def _fox_attn_kernel(q_ref, k_ref, v_ref, logf_ref, o_ref, kb_ref, vb_ref, c_ref, s_ref, *, head_dim):
    tq, tk = ATTN_TQ, ATTN_TK
    ratio = tq // tk
    pair = pl.program_id(1)
    seq = q_ref.shape[2]
    e = 2 * head_dim
    kmq = _key_minus_query(tk, tq)

    @pl.when(pair == 0)
    def _():
        upper = (lax.broadcasted_iota(jnp.int32, (tk, tk), 0)
                 <= lax.broadcasted_iota(jnp.int32, (tk, tk), 1)).astype(F32)
        carry = jnp.zeros((logf_ref.shape[1], 1), F32)
        for blk in range(seq // tk):
            c = _dot(logf_ref[0, :, blk * tk:(blk + 1) * tk], upper, HIGHEST) + carry
            c_ref[:, blk * tk:(blk + 1) * tk] = c
            carry = c[:, tk - 1:tk]

    k_rows = k_ref[0].T
    lane = lax.broadcasted_iota(jnp.int32, (1, e), 1)
    for hh in range(2):
        spare = (1 - hh) * head_dim
        rest = jnp.broadcast_to(c_ref[pl.ds(2 * pair + hh, 1), :] * LOG2E, (LANES, seq)).T
        k_aug = jnp.where(lane // head_dim == hh, k_rows, 0.0)
        for part in range(DECAY_PARTS):
            piece = rest.astype(BF16).astype(F32)
            k_aug = jnp.where(lane == spare + part, piece, k_aug)
            rest = rest - piece
        kb_ref[hh] = k_aug.astype(BF16)
        vb_ref[hh, 0:head_dim, :] = v_ref[0, hh * head_dim:(hh + 1) * head_dim, :].astype(BF16)
        vb_ref[hh, head_dim:, :] = _ones_rows(seq)

    def q_block(i, carry):
        qs = pl.multiple_of(i * tq, tq)
        q_t = q_ref[0, :, pl.ds(qs, tq)]
        row = lax.broadcasted_iota(jnp.int32, (e, 1), 0)
        q_aug = []
        for hh in range(2):
            spare = (1 - hh) * head_dim
            minus_one = jnp.logical_and(row >= spare, row < spare + DECAY_PARTS)
            q_aug.append(jnp.where(row // head_dim == hh, q_t,
                                   jnp.where(minus_one, -1.0, 0.0).astype(BF16)))

        def qk(j):
            ks = pl.multiple_of(j * tk, tk)
            return [_dot(kb_ref[hh, pl.ds(ks, tk), :], q_aug[hh]) for hh in range(2)]

        def softmax(scores, j, state, diagonal=None):
            ks = pl.multiple_of(j * tk, tk)
            if diagonal is not None:
                visible = kmq <= -diagonal * tk
                scores = [jnp.where(visible, s, NEG_INF) for s in scores]
            values = [vb_ref[hh, :, pl.ds(ks, tk)] for hh in range(2)]
            return _flash_step(scores, values, state)

        ring = _ScoreRing(s_ref, qk)
        state = ring.far_sweep(softmax, _flash_init(2, vb_ref.shape[1], tq), ratio * i // 2)
        state = ring.tail(softmax, state, ratio * i, list(range(ratio)), False)
        o_t = jnp.concatenate([acc[0:head_dim] / acc[head_dim:head_dim + 1] for _, acc in state], axis=0)
        o_ref[0, pl.ds(qs, tq), :] = o_t.T.astype(o_ref.dtype)
        return carry

    lax.fori_loop(0, seq // tq, q_block, 0)


def _fox_attention(q_t, k_t, v_t, logf_t, *, head_dim):
    b, w, s = q_t.shape
    e = 2 * head_dim
    tq, tk = ATTN_TQ, ATTN_TK
    n_f = logf_t.shape[1]
    assert s % tq == 0 and tq % tk == 0 and tk % LANES == 0 and e == LANES and w // e * 2 == n_f
    assert DECAY_PARTS <= head_dim
    t_blk = pl.BlockSpec((1, e, s), lambda bi, g: (bi, g, 0))
    return pl.pallas_call(
        functools.partial(_fox_attn_kernel, head_dim=head_dim),
        out_shape=jax.ShapeDtypeStruct((b, s, w), BF16),
        grid=(b, w // e),
        in_specs=[t_blk, t_blk, t_blk, pl.BlockSpec((1, n_f, s), lambda bi, g: (bi, 0, 0))],
        out_specs=pl.BlockSpec((1, s, e), lambda bi, g: (bi, 0, g)),
        scratch_shapes=[pltpu.VMEM((2, s, e), BF16), pltpu.VMEM((2, head_dim + BF16_ROWS, s), BF16),
                        pltpu.VMEM((n_f, s), F32), pltpu.VMEM((2, 2, tk, tq), F32)],
        compiler_params=_params(("arbitrary", "arbitrary")),
        name="fox_attention",
    )(q_t, k_t, v_t, logf_t)


def _decode_kernel(pt_ref, qd_ref, kd_ref, vd_ref, qf_ref, kf_ref, vf_ref, lfn_ref,
                   rbt_ref, lam_ref, subg_ref, *rest,
                   n_pages, page_size, head_dim, lambda_init):
    pps = PAGES_PER_STEP
    pages = [rest[5 * p:5 * p + 5] for p in range(pps)]
    od_ref, of_ref = rest[5 * pps:5 * pps + 2]
    (md_ref, ld_ref, accd_ref, mf_ref, lf_ref, accf_ref, self_ref, carry_ref, bias_ref) = rest[5 * pps + 2:]
    req, grp = pl.program_id(0), pl.program_id(1)
    n_maps = md_ref.shape[0]
    width = qd_ref.shape[-1]
    sub = req % qd_ref.shape[0]
    new_row = lambda ref: ref[pl.ds(sub, 1), :]
    n_dh = n_maps // 2
    e = 2 * head_dim

    map_of_lane = lax.broadcasted_iota(jnp.int32, (n_maps, width), 1) // head_dim
    row = lax.broadcasted_iota(jnp.int32, (n_maps, width), 0)
    own = map_of_lane == row
    row_e = lax.broadcasted_iota(jnp.int32, (n_maps, e), 0)

    @pl.when(jnp.logical_and(pl.program_id(0) == 0, grp == 0))
    def _():
        lane = lax.broadcasted_iota(jnp.int32, (n_maps, page_size), 1)
        bucket = _rel_bucket(page_size - lane)
        val = jnp.zeros((n_maps, page_size), F32)
        for b in range(N_BUCKETS):
            val = jnp.where(bucket == b, rbt_ref[:, b:b + 1], val)
        bias_ref[...] = (val - rbt_ref[:, N_BUCKETS - 1:N_BUCKETS]) * LOG2E

    qd = jnp.where(own, new_row(qd_ref), 0.0).astype(BF16)
    qf = jnp.where(own, new_row(qf_ref), 0.0).astype(BF16)

    @pl.when(grp == 0)
    def _():
        def self_score(q, k_ref):
            k = new_row(k_ref).astype(BF16).astype(F32)
            return jnp.sum(q.astype(F32) * k, axis=-1, keepdims=True)
        bias0 = (rbt_ref[:, 0:1] - rbt_ref[:, N_BUCKETS - 1:N_BUCKETS]) * LOG2E
        md_ref[...] = self_score(qd, kd_ref) + bias0
        mf_ref[...] = self_score(qf, kf_ref)
        ld_ref[...] = jnp.ones(ld_ref.shape, F32)
        lf_ref[...] = jnp.ones(lf_ref.shape, F32)
        v_new = vd_ref[pl.ds(sub * n_dh, n_dh), :]
        acc0 = jnp.zeros(accd_ref.shape, F32)
        for h in range(n_dh):
            acc0 = jnp.where(row_e // 2 == h, v_new[h:h + 1, :], acc0)
        accd_ref[...] = acc0
        accf_ref[...] = jnp.zeros(accf_ref.shape, F32)
        self_ref[...] = jnp.ones(self_ref.shape, F32)
        lane = lax.broadcasted_iota(jnp.int32, lfn_ref.shape[1:], 1)
        log_f_new = jnp.sum(jnp.where(lane == req, lfn_ref[0], 0.0), axis=1, keepdims=True)
        carry_ref[...] = log_f_new * LOG2E

    later = (lax.broadcasted_iota(jnp.int32, (page_size, page_size), 0)
             >= lax.broadcasted_iota(jnp.int32, (page_size, page_size), 1)).astype(F32)

    def softmax_update(s, m_ref, l_ref):
        m_prev = m_ref[...]
        m_new = jnp.maximum(m_prev, jnp.max(s, axis=-1, keepdims=True))
        alpha = jnp.exp2(m_prev - m_new)
        p = jnp.exp2(s - m_new)
        l_ref[...] = alpha * l_ref[...] + jnp.sum(p, axis=-1, keepdims=True)
        m_ref[...] = m_new
        return alpha, p

    cdk, cdv, cfk, cfv, clf = zip(*pages)

    def keys_side_by_side(refs):
        return jnp.concatenate([ref[0].astype(BF16) for ref in refs], axis=1)

    newest = jnp.where(grp == 0, bias_ref[...], 0.0)
    bias = jnp.concatenate([newest, jnp.zeros((n_maps, (pps - 1) * page_size), F32)], axis=1)
    alpha, p = softmax_update(_dot(qd, keys_side_by_side(cdk)) + bias, md_ref, ld_ref)
    p = p.astype(BF16)
    pv = jnp.zeros(accd_ref.shape, F32)
    for h in range(n_dh):
        v_h = jnp.concatenate([ref[0, pl.ds(h, page_size, stride=n_dh), :].astype(BF16) for ref in cdv],
                              axis=0)
        pv = jnp.where(row_e // 2 == h, _dot(p, v_h), pv)
    accd_ref[...] = alpha * accd_ref[...] + pv

    lf = jnp.concatenate([ref[0] for ref in clf], axis=0) * LOG2E
    incl = _dot(lf, later, HIGHEST)
    carry = carry_ref[...]
    decay = []
    for p_idx in range(pps):
        rows = slice(p_idx * n_maps, (p_idx + 1) * n_maps)
        decay.append(incl[rows] - lf[rows] + carry)
        carry = carry + incl[rows, 0:1]
    carry_ref[...] = carry
    s_f = _dot(qf, keys_side_by_side(cfk)) + jnp.concatenate(decay, axis=1)
    alpha, p = softmax_update(s_f, mf_ref, lf_ref)
    acc = alpha[:, :, None] * accf_ref[...]
    for p_idx in range(pps):
        p_page = p[:, p_idx * page_size:(p_idx + 1) * page_size]
        acc = acc + p_page[:, None, :] * cfv[p_idx][0].reshape(n_maps, head_dim, page_size)
    accf_ref[...] = acc
    self_ref[...] = alpha * self_ref[...]

    @pl.when(grp == pl.num_programs(1) - 1)
    def _():
        lam = _lambda_value(lam_ref, lambda_init)
        sign = jnp.where(row_e[:, 0:1] % 2 == 0, 1.0, -lam)
        a = accd_ref[...] / ld_ref[...] * sign
        pick = (lax.broadcasted_iota(jnp.int32, (n_dh, n_maps), 1) // 2
                == lax.broadcasted_iota(jnp.int32, (n_dh, n_maps), 0)).astype(F32)
        o_d = _dot(pick, a, HIGHEST)
        o_d = _rms(o_d, subg_ref[...]) * (1.0 - lambda_init)
        od_ref[pl.ds(sub, 1), :] = jnp.concatenate([o_d[h:h + 1] for h in range(n_dh)], axis=1)

        def spread(col):
            return jnp.sum(jnp.where(own, col, 0.0), axis=0, keepdims=True)
        past = jnp.sum(accf_ref[...].reshape(width, page_size).T, axis=0, keepdims=True)
        of_ref[pl.ds(sub, 1), :] = (past + spread(self_ref[...]) * new_row(vf_ref)) / spread(lf_ref[...])


def _decode_attention(page_table, qd, kd, vd, qf, kf, vf, logf_new, rbt, lam_p, subg,
                      cdk, cdv, cfk, cfv, clf, *, head_dim, lambda_init):
    r, n_pages = page_table.shape
    _, width, page_size = cdk.shape
    n_f = clf.shape[1]
    n_maps = width // head_dim
    n_dh = n_maps // 2
    e = 2 * head_dim
    pps = PAGES_PER_STEP
    assert n_pages % pps == 0 and n_maps == n_f
    assert page_size >= MAX_DISTANCE
    assert r % SUBLANES == 0
    row = pl.BlockSpec((SUBLANES, width), lambda i, g, pt: (i // SUBLANES, 0))
    head_rows = pl.BlockSpec((SUBLANES * n_dh, e), lambda i, g, pt: (i // SUBLANES, 0))

    def paged(p, rows):
        def index(i, g, pt):
            return (pt[i, n_pages - 1 - (g * pps + p)], 0, 0)
        return pl.BlockSpec((1, rows, cdk.shape[2]), index)

    in_specs = [row, row, head_rows, row, row, row, _resident(logf_new.shape),
                _resident(rbt.shape), _resident(lam_p.shape), _resident(subg.shape)]
    operands = [qd, kd, vd, qf, kf, vf, logf_new, rbt, lam_p, subg]
    for p in range(pps):
        in_specs += [paged(p, width), paged(p, cdv.shape[1]), paged(p, width), paged(p, width), paged(p, n_f)]
        operands += [cdk, cdv, cfk, cfv, clf]
    return pl.pallas_call(
        functools.partial(_decode_kernel, n_pages=n_pages, page_size=page_size,
                          head_dim=head_dim, lambda_init=lambda_init),
        out_shape=[jax.ShapeDtypeStruct((r, width), F32)] * 2,
        grid_spec=pltpu.PrefetchScalarGridSpec(
            num_scalar_prefetch=1,
            grid=(r, n_pages // pps),
            in_specs=in_specs,
            out_specs=[row, row],
            scratch_shapes=[pltpu.VMEM((n_maps, 1), F32), pltpu.VMEM((n_maps, 1), F32),
                            pltpu.VMEM((n_maps, e), F32),
                            pltpu.VMEM((n_maps, 1), F32), pltpu.VMEM((n_maps, 1), F32),
                            pltpu.VMEM((n_maps, head_dim, page_size), F32),
                            pltpu.VMEM((n_maps, 1), F32),
                            pltpu.VMEM((n_f, 1), F32), pltpu.VMEM((n_maps, page_size), F32)]),
        compiler_params=_params(("arbitrary", "arbitrary")),
        name="decode_attention",
    )(page_table, *operands)


def _merge_kernel(x_ref, od_ref, of_ref, g1_ref, wg_ref, gb_ref, wa_ref, wb_ref, wo_ref, o_ref):
    x = x_ref[...]
    d = x.shape[-1]
    h = _rms(x, g1_ref[...])
    ya = _dot(od_ref[...].astype(F32), wa_ref[...])
    yb = _dot(of_ref[...].astype(F32), wb_ref[...])
    gate_a = jax.nn.sigmoid(_dot(h, wg_ref[:, :d]) + gb_ref[:, :d])
    gate_b = jax.nn.sigmoid(_dot(h, wg_ref[:, d:]) + gb_ref[:, d:])
    merged = gate_a * ya + gate_b * yb
    o_ref[...] = x + _dot(merged, wo_ref[...])


def _merge(x, od, of, g1, wg, gb, wa, wb, wo):
    m, d = x.shape
    tm = min(ROW_TILE, m)
    assert m % tm == 0
    row = lambda c: pl.BlockSpec((tm, c), lambda i: (i, 0))
    return pl.pallas_call(
        _merge_kernel,
        out_shape=jax.ShapeDtypeStruct((m, d), F32),
        grid=(m // tm,),
        in_specs=[row(d), row(od.shape[1]), row(of.shape[1])]
                 + [_resident(a.shape) for a in (g1, wg, gb, wa, wb, wo)],
        out_specs=row(d),
        compiler_params=_params(("arbitrary",)),
        name="merge",
    )(x, od, of, g1, wg, gb, wa, wb, wo)


def _ffn_kernel(x_ref, g2_ref, wgu_ref, wdn_ref, o_ref):
    x = x_ref[...]
    d_ff = wdn_ref.shape[0]
    h = _rms(x, g2_ref[...])
    gate = _dot(h, wgu_ref[:, :d_ff])
    up = _dot(h, wgu_ref[:, d_ff:])
    act = gate * jax.nn.sigmoid(gate) * up
    o_ref[...] = x + _dot(act, wdn_ref[...])


def _ffn(x, g2, wgu, wdn):
    m, d = x.shape
    tm = min(FFN_ROW_TILE, m)
    assert m % tm == 0
    row = pl.BlockSpec((tm, d), lambda i: (i, 0))
    return pl.pallas_call(
        _ffn_kernel,
        out_shape=jax.ShapeDtypeStruct((m, d), F32),
        grid=(m // tm,),
        in_specs=[row, _resident(g2.shape), _resident(wgu.shape), _resident(wdn.shape)],
        out_specs=row,
        compiler_params=_params(("arbitrary",)),
        name="ffn",
    )(x, g2, wgu, wdn)


def kernel(x_prompt, x_sample, cache_diff_k, cache_diff_v, cache_fox_k, cache_fox_v, cache_fox_logf,
           page_table, rel_bias, norm1_g, w_in, diff_q_g, diff_k_g, fox_q_g, fox_k_g, diff_lambda,
           fox_f_b, gate_b, diff_subln_g, w_branch_a, w_branch_b, w_out, norm2_g, w_gate_up, w_down):
    depth = w_in.shape[0]
    batch, seq, d = x_prompt.shape
    dec_batch, dec_seq, _ = x_sample.shape
    assert dec_seq == 1
    _, n_phys, page_size, n_dh, _, head_dim = cache_diff_k.shape
    n_f = cache_fox_k.shape[3]
    e = 2 * head_dim
    width = n_dh * e
    assert n_f * head_dim == width

    rb_flat = rel_bias.reshape(-1)
    rbt = jnp.repeat(rel_bias.T, 2, axis=0)

    yp, ys = x_prompt, x_sample.reshape(1, dec_batch, d)
    rows_p, rows_s = [], []
    for l in range(depth):
        lambda_init = 0.8 - 0.6 * math.exp(-0.3 * l)
        w_t = w_in[l].T
        wg = w_in[l][:, 6 * width + n_f:]
        fb = fox_f_b[l].reshape(n_f, 1)
        gains = jnp.stack([jnp.tile(g[l], width // head_dim)
                           for g in (diff_q_g, diff_k_g, fox_q_g, fox_k_g)]).reshape(4, width, 1)
        g1 = norm1_g[l].reshape(1, d)
        g2 = norm2_g[l].reshape(1, d)
        gb = gate_b[l].reshape(1, 2 * d)
        subg = diff_subln_g[l].reshape(1, e)
        wa, wb, wo, wgu, wdn = w_branch_a[l], w_branch_b[l], w_out[l], w_gate_up[l], w_down[l]
        lam_p = diff_lambda[l]

        def tail(x, od, of):
            return _ffn(_merge(x, od, of, g1, wg, gb, wa, wb, wo), g2, wgu, wdn)

        def cache_rows(kd_t, vd, kf_t, vf_t, logf_t):
            b, _, s = kd_t.shape
            return (jnp.transpose(kd_t.reshape(b, n_dh, 2, head_dim, s), (0, 4, 1, 2, 3)),
                    vd.reshape(b, s, n_dh, e),
                    jnp.transpose(kf_t.reshape(b, n_f, head_dim, s), (0, 3, 1, 2)),
                    jnp.transpose(vf_t.reshape(b, n_f, head_dim, s), (0, 3, 1, 2)),
                    jnp.transpose(logf_t, (0, 2, 1)))

        qd_t, kd_t, vd, vd_t, qf_t, kf_t, vf_t, logf_t = _proj(
            yp, g1, w_t, gains, fb, width=width, head_dim=head_dim, with_rows=False)
        od = _diff_attention(rb_flat, qd_t, kd_t, vd_t, lam_p, subg,
                             n_heads=n_dh, head_dim=head_dim, lambda_init=lambda_init)
        of = _fox_attention(qf_t, kf_t, vf_t, logf_t, head_dim=head_dim)
        yp = tail(yp.reshape(batch * seq, d), od.reshape(batch * seq, width),
                  of.reshape(batch * seq, width)).reshape(batch, seq, d)
        rows_p.append(cache_rows(kd_t, vd, kf_t, vf_t, logf_t))

        (qd_t, kd_t, vd, _, qf_t, kf_t, vf_t, logf_t, qd_r, kd_r, qf_r, kf_r, vf_r) = _proj(
            ys, g1, w_t, gains, fb, width=width, head_dim=head_dim, with_rows=True)
        as_row = lambda a: a.reshape(dec_batch, width)
        cdk = jnp.transpose(cache_diff_k[l], (0, 2, 3, 4, 1)).reshape(n_phys, width, page_size)
        cdv = cache_diff_v[l].reshape(n_phys, page_size * n_dh, e)
        cfk = jnp.transpose(cache_fox_k[l], (0, 2, 3, 1)).reshape(n_phys, width, page_size)
        cfv = jnp.transpose(cache_fox_v[l], (0, 2, 3, 1)).reshape(n_phys, width, page_size)
        clf = jnp.transpose(cache_fox_logf[l], (0, 2, 1))
        od, of = _decode_attention(
            page_table, as_row(qd_r), as_row(kd_r), vd.reshape(dec_batch * n_dh, e),
            as_row(qf_r), as_row(kf_r), as_row(vf_r), logf_t, rbt, lam_p, subg,
            cdk, cdv, cfk, cfv, clf, head_dim=head_dim, lambda_init=lambda_init)
        ys = tail(ys.reshape(dec_batch, d), od, of).reshape(1, dec_batch, d)
        rows_s.append(tuple(jnp.swapaxes(a, 0, 1) for a in cache_rows(kd_t, vd, kf_t, vf_t, logf_t)))

    stack = lambda rows, i: jnp.stack([r[i] for r in rows], axis=0)
    return (yp, ys.reshape(dec_batch, 1, d),
            *(stack(rows_p, i) for i in range(5)), *(stack(rows_s, i) for i in range(5)))
```

```python
import functools
import math
from typing import Any, NamedTuple

import jax
import jax.numpy as jnp
from jax import lax
from jax.experimental import pallas as pl
from jax.experimental.pallas import tpu as pltpu

F32 = jnp.float32
BF16 = jnp.bfloat16

N_BUCKETS = 32
MAX_DISTANCE = 128
EPS = 1e-6
NEG_INF = -1e30
LOG2E = math.log2(math.e)

LANES = 128
SUBLANES = 8
BF16_ROWS = 16
VMEM_LIMIT = 56 * 1024 * 1024
DECODE_VMEM_LIMIT = 60 * 1024 * 1024

ROW_TILE = 512
FFN_ROW_TILE = 256
ATTN_TQ = 512
ATTN_TK = 256
PAGES_PER_STEP = 16
DECAY_PARTS = 3

NT_DIMS = (((1,), (1,)), ((), ()))
HIGHEST = lax.Precision.HIGHEST


def _resident(shape):
    zeros = (0,) * len(shape)
    return pl.BlockSpec(shape, lambda *_: zeros, pipeline_mode=pl.Buffered(1))


def _params(semantics, vmem=VMEM_LIMIT):
    return pltpu.CompilerParams(dimension_semantics=semantics, vmem_limit_bytes=vmem)


def _rms(x, g):
    return x * lax.rsqrt(jnp.mean(x * x, axis=-1, keepdims=True) + EPS) * g


def _dot(a, b, precision=None):
    return jnp.dot(a, b, preferred_element_type=F32, precision=precision)


def _dot_nt(a, b):
    return lax.dot_general(a, b, NT_DIMS, preferred_element_type=F32)


def _log_sigmoid(x):
    return jnp.minimum(x, 0.0) - jnp.log1p(jnp.exp(-jnp.abs(x)))


def _rel_bucket(n):
    max_exact = N_BUCKETS // 2
    nf = jnp.maximum(n, 1).astype(F32)
    large = max_exact + (jnp.log(nf / max_exact) / math.log(MAX_DISTANCE / max_exact)
                         * (N_BUCKETS - max_exact)).astype(jnp.int32)
    return jnp.where(n < max_exact, n, jnp.minimum(large, N_BUCKETS - 1))


def _lambda_value(lam_ref, lambda_init):
    lp = lam_ref[...]
    a = jnp.sum(lp[0:1] * lp[1:2], axis=-1, keepdims=True)
    b = jnp.sum(lp[2:3] * lp[3:4], axis=-1, keepdims=True)
    return jnp.exp(a) - jnp.exp(b) + lambda_init


def _proj_kernel(x_ref, g1_ref, wt_ref, gains_ref, fb_ref,
                 qd_ref, kd_ref, vd_ref, vdt_ref, qf_ref, kf_ref, vf_ref, logf_ref, *row_refs,
                 head_dim, q_scale):
    h = _rms(x_ref[0], g1_ref[...])
    width = kd_ref.shape[1]
    n_heads = width // head_dim
    n_f = logf_ref.shape[1]
    section = lambda i: _dot_nt(wt_ref[i * width:(i + 1) * width, :], h)

    def head_norm_t(z, i):
        z3 = z.reshape(n_heads, head_dim, z.shape[-1])
        ss = jnp.sum(z3 * z3, axis=1, keepdims=True)
        gain = gains_ref[i].reshape(n_heads, head_dim, 1)
        return (z3 * lax.rsqrt(ss * (1.0 / head_dim) + EPS) * gain).reshape(z.shape)

    qd = head_norm_t(section(0), 0) * q_scale
    kd = head_norm_t(section(1), 1)
    qf = head_norm_t(section(3), 2) * q_scale
    kf = head_norm_t(section(4), 3)
    vd_t = section(2)
    vd_rows = vd_t.T
    vf = section(5)
    qd_ref[0] = qd.astype(qd_ref.dtype)
    kd_ref[0] = kd
    e = vd_ref.shape[-1]
    for hd in range(width // e):
        vd_ref[0, pl.ds(hd, vd_rows.shape[0], stride=width // e), :] = vd_rows[:, hd * e:(hd + 1) * e]
    vdt_ref[0] = vd_t.astype(vdt_ref.dtype)
    qf_ref[0] = qf.astype(qf_ref.dtype)
    kf_ref[0] = kf
    vf_ref[0] = vf
    logf_ref[0] = _log_sigmoid(_dot_nt(wt_ref[6 * width:6 * width + n_f, :], h) + fb_ref[...])
    if row_refs:
        for ref, val in zip(row_refs, (qd, kd, qf, kf, vf)):
            ref[0] = val.T


def _proj(x, g1, wt, gains, fb, *, width, head_dim, with_rows):
    b, s, d = x.shape
    n_f = fb.shape[0]
    e = 2 * head_dim
    tm = min(ROW_TILE, s)
    assert s % tm == 0 and tm % LANES == 0 and width % e == 0
    t_blk = lambda r: pl.BlockSpec((1, r, tm), lambda bi, si: (bi, 0, si))
    r_blk = lambda c: pl.BlockSpec((1, tm, c), lambda bi, si: (bi, si, 0))
    t_shape = lambda dt: jax.ShapeDtypeStruct((b, width, s), dt)
    r_shape = jax.ShapeDtypeStruct((b, s, width), F32)
    heads_per_token = width // e
    out_shape = [t_shape(BF16), t_shape(F32), jax.ShapeDtypeStruct((b, s * heads_per_token, e), F32),
                 t_shape(BF16), t_shape(BF16), t_shape(F32), t_shape(F32),
                 jax.ShapeDtypeStruct((b, n_f, s), F32)]
    out_specs = [t_blk(width), t_blk(width),
                 pl.BlockSpec((1, tm * heads_per_token, e), lambda bi, si: (bi, si, 0)),
                 t_blk(width), t_blk(width), t_blk(width), t_blk(width), t_blk(n_f)]
    if with_rows:
        out_shape += [r_shape] * 5
        out_specs += [r_blk(width)] * 5
    return pl.pallas_call(
        functools.partial(_proj_kernel, head_dim=head_dim, q_scale=head_dim ** -0.5 * LOG2E),
        out_shape=out_shape,
        grid=(b, s // tm),
        in_specs=[r_blk(d)] + [_resident(a.shape) for a in (g1, wt, gains, fb)],
        out_specs=out_specs,
        compiler_params=_params(("arbitrary", "arbitrary")),
        name="proj",
    )(x, g1, wt, gains, fb)


def _flash_step(scores, values, state):
    new_state = []
    for s_t, v_t, (m_prev, acc_prev) in zip(scores, values, state):
        m_new = jnp.maximum(m_prev, jnp.max(s_t, axis=0, keepdims=True))
        alpha = jnp.exp2(m_prev - m_new)
        p_t = jnp.exp2(s_t - m_new).astype(BF16)
        new_state.append((m_new, alpha * acc_prev + _dot(v_t, p_t)))
    return tuple(new_state)


def _flash_init(n_slots, rows, tq):
    return tuple((jnp.full((1, tq), NEG_INF, F32), jnp.zeros((rows, tq), F32)) for _ in range(n_slots))


def _ones_rows(seq):
    return jnp.where(lax.broadcasted_iota(jnp.int32, (BF16_ROWS, seq), 0) == 0, 1.0, 0.0).astype(BF16)


def _key_minus_query(tk, tq):
    return lax.broadcasted_iota(jnp.int32, (tk, tq), 0) - lax.broadcasted_iota(jnp.int32, (tk, tq), 1)


class _ScoreRing:
    def __init__(self, s_ref, qk):
        self.s_ref, self.qk = s_ref, qk

    def fill(self, j, slot):
        for idx, s in enumerate(self.qk(j)):
            self.s_ref[slot, idx] = s

    def scores(self, slot):
        return [self.s_ref[slot, idx] for idx in range(self.s_ref.shape[1])]

    def far_sweep(self, softmax, state, n_pairs):
        self.fill(0, 0)

        def body(jj, st):
            j = 2 * jj
            self.fill(j + 1, 1)
            st = softmax(self.scores(0), j, st)
            self.fill(j + 2, 0)
            return softmax(self.scores(1), j + 1, st)
        return lax.fori_loop(0, n_pairs, body, state)

    def tail(self, softmax, state, first, kinds, more_follows):
        assert not more_follows or len(kinds) % 2 == 0
        for n, kind in enumerate(kinds):
            slot = n % 2
            if n + 1 < len(kinds) or more_follows:
                self.fill(first + n + 1, 1 - slot)
            state = softmax(self.scores(slot), first + n, state, kind)
        return state


def _diff_attn_kernel(rb_ref, q_ref, k_ref, v_ref, lam_ref, subg_ref, o_ref,
                      kb_ref, vb_ref, bias_ref, s_ref, *, n_heads, head_dim, lambda_init):
    tq, tk = ATTN_TQ, ATTN_TK
    ratio = tq // tk
    head = pl.program_id(0)
    seq = q_ref.shape[2]
    e = 2 * head_dim
    kmq = _key_minus_query(tk, tq)
    near = [tk] + [-d * tk for d in range(ratio)]

    @pl.when(pl.program_id(1) == 0)
    def _():
        far = rb_ref[(N_BUCKETS - 1) * n_heads + head]
        for idx, delta in enumerate(near):
            bucket = _rel_bucket(jnp.maximum(delta - kmq, 0))
            val = jnp.zeros((tk, tq), F32)
            for b in range(N_BUCKETS):
                val = jnp.where(bucket == b, rb_ref[b * n_heads + head], val)
            bias_ref[idx] = (val - far) * LOG2E

    k_rows = k_ref[0].T
    first = lax.broadcasted_iota(jnp.int32, (1, e), 1) < head_dim
    kb_ref[0] = jnp.where(first, k_rows, 0.0).astype(BF16)
    kb_ref[1] = jnp.where(first, 0.0, k_rows).astype(BF16)
    vb_ref[0:e, :] = v_ref[0]
    vb_ref[e:, :] = _ones_rows(seq)
    lam = _lambda_value(lam_ref, lambda_init)

    def q_block(i, carry):
        qs = pl.multiple_of(i * tq, tq)
        q_t = q_ref[0, :, pl.ds(qs, tq)]

        def qk(j):
            ks = pl.multiple_of(j * tk, tk)
            return [_dot(kb_ref[mp, pl.ds(ks, tk), :], q_t) for mp in range(2)]

        def softmax(scores, j, state, near_idx=None):
            if near_idx is not None:
                scores = [s + bias_ref[near_idx] for s in scores]
                if near[near_idx] <= 0:
                    visible = kmq <= near[near_idx]
                    scores = [jnp.where(visible, s, NEG_INF) for s in scores]
            v_t = vb_ref[:, pl.ds(pl.multiple_of(j * tk, tk), tk)]
            return _flash_step(scores, [v_t, v_t], state)

        ring = _ScoreRing(s_ref, qk)
        first_near = ratio * (i - 1)
        state = ring.far_sweep(softmax, _flash_init(2, vb_ref.shape[0], tq),
                               jnp.maximum(first_near, 0) // 2)
        before = [None] * (ratio - 1) + [0]
        state = lax.cond(i >= 1, lambda st: ring.tail(softmax, st, first_near, before, True),
                         lambda st: st, state)
        state = ring.tail(softmax, state, ratio * i, [1 + d for d in range(ratio)], False)
        (_, acc0), (_, acc1) = state
        o_t = acc0[0:e] / acc0[e:e + 1] - lam * (acc1[0:e] / acc1[e:e + 1])
        o = _rms(o_t.T, subg_ref[...]) * (1.0 - lambda_init)
        o_ref[0, pl.ds(qs, tq), :] = o.astype(o_ref.dtype)
        return carry

    lax.fori_loop(0, seq // tq, q_block, 0)


def _diff_attention(rb_flat, q_t, k_t, v, lam_p, subg, *, n_heads, head_dim, lambda_init):
    b, w, s = q_t.shape
    e = 2 * head_dim
    tq, tk = ATTN_TQ, ATTN_TK
    assert s % tq == 0 and tq % tk == 0 and tk % LANES == 0 and tk >= MAX_DISTANCE and e == LANES
    t_blk = pl.BlockSpec((1, e, s), lambda h, bi, *_: (bi, h, 0))
    r_blk = pl.BlockSpec((1, s, e), lambda h, bi, *_: (bi, 0, h))
    return pl.pallas_call(
        functools.partial(_diff_attn_kernel, n_heads=n_heads, head_dim=head_dim, lambda_init=lambda_init),
        out_shape=jax.ShapeDtypeStruct((b, s, w), BF16),
        grid_spec=pltpu.PrefetchScalarGridSpec(
            num_scalar_prefetch=1,
            grid=(n_heads, b),
            in_specs=[t_blk, t_blk, t_blk, _resident(lam_p.shape), _resident(subg.shape)],
            out_specs=r_blk,
            scratch_shapes=[pltpu.VMEM((2, s, e), BF16), pltpu.VMEM((e + BF16_ROWS, s), BF16),
                            pltpu.VMEM((tq // tk + 1, tk, tq), F32),
                            pltpu.VMEM((2, 2, tk, tq), F32)]),
        compiler_params=_params(("arbitrary", "arbitrary")),
        name="diff_attention",
    )(rb_flat, q_t, k_t, v, lam_p, subg)


def _fox_attn_kernel(q_ref, k_ref, v_ref, logf_ref, o_ref, kb_ref, vb_ref, c_ref, s_ref, *, head_dim):
    tq, tk = ATTN_TQ, ATTN_TK
    ratio = tq // tk
    pair = pl.program_id(1)
    seq = q_ref.shape[2]
    e = 2 * head_dim
    kmq = _key_minus_query(tk, tq)

    @pl.when(pair == 0)
    def _():
        upper = (lax.broadcasted_iota(jnp.int32, (tk, tk), 0)
                 <= lax.broadcasted_iota(jnp.int32, (tk, tk), 1)).astype(F32)
        carry = jnp.zeros((logf_ref.shape[1], 1), F32)
        for blk in range(seq // tk):
            c = _dot(logf_ref[0, :, blk * tk:(blk + 1) * tk], upper, HIGHEST) + carry
            c_ref[:, blk * tk:(blk + 1) * tk] = c
            carry = c[:, tk - 1:tk]

    k_rows = k_ref[0].T
    lane = lax.broadcasted_iota(jnp.int32, (1, e), 1)
    for hh in range(2):
        spare = (1 - hh) * head_dim
        rest = jnp.broadcast_to(c_ref[pl.ds(2 * pair + hh, 1), :] * LOG2E, (LANES, seq)).T
        k_aug = jnp.where(lane // head_dim == hh, k_rows, 0.0)
        for part in range(DECAY_PARTS):
            piece = rest.astype(BF16).astype(F32)
            k_aug = jnp.where(lane == spare + part, piece, k_aug)
            rest = rest - piece
        kb_ref[hh] = k_aug.astype(BF16)
        vb_ref[hh, 0:head_dim, :] = v_ref[0, hh * head_dim:(hh + 1) * head_dim, :].astype(BF16)
        vb_ref[hh, head_dim:, :] = _ones_rows(seq)

    def q_block(i, carry):
        qs = pl.multiple_of(i * tq, tq)
        q_t = q_ref[0, :, pl.ds(qs, tq)]
        row = lax.broadcasted_iota(jnp.int32, (e, 1), 0)
        q_aug = []
        for hh in range(2):
            spare = (1 - hh) * head_dim
            minus_one = jnp.logical_and(row >= spare, row < spare + DECAY_PARTS)
            q_aug.append(jnp.where(row // head_dim == hh, q_t,
                                   jnp.where(minus_one, -1.0, 0.0).astype(BF16)))

        def qk(j):
            ks = pl.multiple_of(j * tk, tk)
            return [_dot(kb_ref[hh, pl.ds(ks, tk), :], q_aug[hh]) for hh in range(2)]

        def softmax(scores, j, state, diagonal=None):
            ks = pl.multiple_of(j * tk, tk)
            if diagonal is not None:
                visible = kmq <= -diagonal * tk
                scores = [jnp.where(visible, s, NEG_INF) for s in scores]
            values = [vb_ref[hh, :, pl.ds(ks, tk)] for hh in range(2)]
            return _flash_step(scores, values, state)

        ring = _ScoreRing(s_ref, qk)
        state = ring.far_sweep(softmax, _flash_init(2, vb_ref.shape[1], tq), ratio * i // 2)
        state = ring.tail(softmax, state, ratio * i, list(range(ratio)), False)
        o_t = jnp.concatenate([acc[0:head_dim] / acc[head_dim:head_dim + 1] for _, acc in state], axis=0)
        o_ref[0, pl.ds(qs, tq), :] = o_t.T.astype(o_ref.dtype)
        return carry

    lax.fori_loop(0, seq // tq, q_block, 0)


def _fox_attention(q_t, k_t, v_t, logf_t, *, head_dim):
    b, w, s = q_t.shape
    e = 2 * head_dim
    tq, tk = ATTN_TQ, ATTN_TK
    n_f = logf_t.shape[1]
    assert s % tq == 0 and tq % tk == 0 and tk % LANES == 0 and e == LANES and w // e * 2 == n_f
    assert DECAY_PARTS <= head_dim
    t_blk = pl.BlockSpec((1, e, s), lambda bi, g: (bi, g, 0))
    return pl.pallas_call(
        functools.partial(_fox_attn_kernel, head_dim=head_dim),
        out_shape=jax.ShapeDtypeStruct((b, s, w), BF16),
        grid=(b, w // e),
        in_specs=[t_blk, t_blk, t_blk, pl.BlockSpec((1, n_f, s), lambda bi, g: (bi, 0, 0))],
        out_specs=pl.BlockSpec((1, s, e), lambda bi, g: (bi, 0, g)),
        scratch_shapes=[pltpu.VMEM((2, s, e), BF16), pltpu.VMEM((2, head_dim + BF16_ROWS, s), BF16),
                        pltpu.VMEM((n_f, s), F32), pltpu.VMEM((2, 2, tk, tq), F32)],
        compiler_params=_params(("arbitrary", "arbitrary")),
        name="fox_attention",
    )(q_t, k_t, v_t, logf_t)


def _decode_kernel(pt_ref, qd_ref, kd_ref, vd_ref, qf_ref, kf_ref, vf_ref, lfn_ref,
                   rbt_ref, lam_ref, subg_ref, *rest,
                   n_pages, page_size, head_dim, lambda_init):
    pps = PAGES_PER_STEP
    pages = [rest[5 * p:5 * p + 5] for p in range(pps)]
    od_ref, of_ref = rest[5 * pps:5 * pps + 2]
    (md_ref, ld_ref, accd_ref, mf_ref, lf_ref, accf_ref, self_ref, carry_ref, bias_ref) = rest[5 * pps + 2:]
    req, grp = pl.program_id(0), pl.program_id(1)
    n_maps = md_ref.shape[0]
    width = qd_ref.shape[-1]
    sub = req % qd_ref.shape[0]
    new_row = lambda ref: ref[pl.ds(sub, 1), :]
    n_dh = n_maps // 2
    e = 2 * head_dim

    map_of_lane = lax.broadcasted_iota(jnp.int32, (n_maps, width), 1) // head_dim
    row = lax.broadcasted_iota(jnp.int32, (n_maps, width), 0)
    own = map_of_lane == row
    row_e = lax.broadcasted_iota(jnp.int32, (n_maps, e), 0)

    @pl.when(jnp.logical_and(pl.program_id(0) == 0, grp == 0))
    def _():
        lane = lax.broadcasted_iota(jnp.int32, (n_maps, page_size), 1)
        bucket = _rel_bucket(page_size - lane)
        val = jnp.zeros((n_maps, page_size), F32)
        for b in range(N_BUCKETS):
            val = jnp.where(bucket == b, rbt_ref[:, b:b + 1], val)
        bias_ref[...] = (val - rbt_ref[:, N_BUCKETS - 1:N_BUCKETS]) * LOG2E

    qd = jnp.where(own, new_row(qd_ref), 0.0).astype(BF16)
    qf = jnp.where(own, new_row(qf_ref), 0.0).astype(BF16)

    @pl.when(grp == 0)
    def _():
        def self_score(q, k_ref):
            k = new_row(k_ref).astype(BF16).astype(F32)
            return jnp.sum(q.astype(F32) * k, axis=-1, keepdims=True)
        bias0 = (rbt_ref[:, 0:1] - rbt_ref[:, N_BUCKETS - 1:N_BUCKETS]) * LOG2E
        md_ref[...] = self_score(qd, kd_ref) + bias0
        mf_ref[...] = self_score(qf, kf_ref)
        ld_ref[...] = jnp.ones(ld_ref.shape, F32)
        lf_ref[...] = jnp.ones(lf_ref.shape, F32)
        v_new = vd_ref[pl.ds(sub * n_dh, n_dh), :]
        acc0 = jnp.zeros(accd_ref.shape, F32)
        for h in range(n_dh):
            acc0 = jnp.where(row_e // 2 == h, v_new[h:h + 1, :], acc0)
        accd_ref[...] = acc0
        accf_ref[...] = jnp.zeros(accf_ref.shape, F32)
        self_ref[...] = jnp.ones(self_ref.shape, F32)
        lane = lax.broadcasted_iota(jnp.int32, lfn_ref.shape[1:], 1)
        log_f_new = jnp.sum(jnp.where(lane == req, lfn_ref[0], 0.0), axis=1, keepdims=True)
        carry_ref[...] = log_f_new * LOG2E

    later = (lax.broadcasted_iota(jnp.int32, (page_size, page_size), 0)
             >= lax.broadcasted_iota(jnp.int32, (page_size, page_size), 1)).astype(F32)

    def softmax_update(s, m_ref, l_ref):
        m_prev = m_ref[...]
        m_new = jnp.maximum(m_prev, jnp.max(s, axis=-1, keepdims=True))
        alpha = jnp.exp2(m_prev - m_new)
        p = jnp.exp2(s - m_new)
        l_ref[...] = alpha * l_ref[...] + jnp.sum(p, axis=-1, keepdims=True)
        m_ref[...] = m_new
        return alpha, p

    cdk, cdv, cfk, cfv, clf = zip(*pages)

    def keys_side_by_side(refs):
        return jnp.concatenate([ref[0].astype(BF16) for ref in refs], axis=1)

    newest = jnp.where(grp == 0, bias_ref[...], 0.0)
    bias = jnp.concatenate([newest, jnp.zeros((n_maps, (pps - 1) * page_size), F32)], axis=1)
    alpha, p = softmax_update(_dot(qd, keys_side_by_side(cdk)) + bias, md_ref, ld_ref)
    p = p.astype(BF16)
    pv = jnp.zeros(accd_ref.shape, F32)
    for h in range(n_dh):
        v_h = jnp.concatenate([ref[0, pl.ds(h, page_size, stride=n_dh), :].astype(BF16) for ref in cdv],
                              axis=0)
        pv = jnp.where(row_e // 2 == h, _dot(p, v_h), pv)
    accd_ref[...] = alpha * accd_ref[...] + pv

    lf = jnp.concatenate([ref[0] for ref in clf], axis=0) * LOG2E
    incl = _dot(lf, later, HIGHEST)
    carry = carry_ref[...]
    decay = []
    for p_idx in range(pps):
        rows = slice(p_idx * n_maps, (p_idx + 1) * n_maps)
        decay.append(incl[rows] - lf[rows] + carry)
        carry = carry + incl[rows, 0:1]
    carry_ref[...] = carry
    s_f = _dot(qf, keys_side_by_side(cfk)) + jnp.concatenate(decay, axis=1)
    alpha, p = softmax_update(s_f, mf_ref, lf_ref)
    acc = alpha[:, :, None] * accf_ref[...]
    for p_idx in range(pps):
        p_page = p[:, p_idx * page_size:(p_idx + 1) * page_size]
        acc = acc + p_page[:, None, :] * cfv[p_idx][0].reshape(n_maps, head_dim, page_size)
    accf_ref[...] = acc
    self_ref[...] = alpha * self_ref[...]

    @pl.when(grp == pl.num_programs(1) - 1)
    def _():
        lam = _lambda_value(lam_ref, lambda_init)
        sign = jnp.where(row_e[:, 0:1] % 2 == 0, 1.0, -lam)
        a = accd_ref[...] / ld_ref[...] * sign
        pick = (lax.broadcasted_iota(jnp.int32, (n_dh, n_maps), 1) // 2
                == lax.broadcasted_iota(jnp.int32, (n_dh, n_maps), 0)).astype(F32)
        o_d = _dot(pick, a, HIGHEST)
        o_d = _rms(o_d, subg_ref[...]) * (1.0 - lambda_init)
        od_ref[pl.ds(sub, 1), :] = jnp.concatenate([o_d[h:h + 1] for h in range(n_dh)], axis=1)

        def spread(col):
            return jnp.sum(jnp.where(own, col, 0.0), axis=0, keepdims=True)
        past = jnp.sum(accf_ref[...].reshape(width, page_size).T, axis=0, keepdims=True)
        of_ref[pl.ds(sub, 1), :] = (past + spread(self_ref[...]) * new_row(vf_ref)) / spread(lf_ref[...])


def _decode_attention(page_table, qd, kd, vd, qf, kf, vf, logf_new, rbt, lam_p, subg,
                      cdk, cdv, cfk, cfv, clf, *, head_dim, lambda_init):
    r, n_pages = page_table.shape
    _, width, page_size = cdk.shape
    n_f = clf.shape[1]
    n_maps = width // head_dim
    n_dh = n_maps // 2
    e = 2 * head_dim
    pps = PAGES_PER_STEP
    assert n_pages % pps == 0 and n_maps == n_f
    assert page_size >= MAX_DISTANCE
    assert r % SUBLANES == 0
    row = pl.BlockSpec((SUBLANES, width), lambda i, g, pt: (i // SUBLANES, 0))
    head_rows = pl.BlockSpec((SUBLANES * n_dh, e), lambda i, g, pt: (i // SUBLANES, 0))

    def paged(p, rows):
        def index(i, g, pt):
            return (pt[i, n_pages - 1 - (g * pps + p)], 0, 0)
        return pl.BlockSpec((1, rows, cdk.shape[2]), index)

    in_specs = [row, row, head_rows, row, row, row, _resident(logf_new.shape),
                _resident(rbt.shape), _resident(lam_p.shape), _resident(subg.shape)]
    operands = [qd, kd, vd, qf, kf, vf, logf_new, rbt, lam_p, subg]
    for p in range(pps):
        in_specs += [paged(p, width), paged(p, cdv.shape[1]), paged(p, width), paged(p, width), paged(p, n_f)]
        operands += [cdk, cdv, cfk, cfv, clf]
    return pl.pallas_call(
        functools.partial(_decode_kernel, n_pages=n_pages, page_size=page_size,
                          head_dim=head_dim, lambda_init=lambda_init),
        out_shape=[jax.ShapeDtypeStruct((r, width), F32)] * 2,
        grid_spec=pltpu.PrefetchScalarGridSpec(
            num_scalar_prefetch=1,
            grid=(r, n_pages // pps),
            in_specs=in_specs,
            out_specs=[row, row],
            scratch_shapes=[pltpu.VMEM((n_maps, 1), F32), pltpu.VMEM((n_maps, 1), F32),
                            pltpu.VMEM((n_maps, e), F32),
                            pltpu.VMEM((n_maps, 1), F32), pltpu.VMEM((n_maps, 1), F32),
                            pltpu.VMEM((n_maps, head_dim, page_size), F32),
                            pltpu.VMEM((n_maps, 1), F32),
                            pltpu.VMEM((n_f, 1), F32), pltpu.VMEM((n_maps, page_size), F32)]),
        compiler_params=_params(("arbitrary", "arbitrary")),
        name="decode_attention",
    )(page_table, *operands)


class _Rider(NamedTuple):
    kernel: Any
    operands: tuple
    in_specs: tuple
    out_shape: Any
    out_spec: Any


def _decode_request_kernel(pt_ref, qd_ref, kd_ref, vd_ref, qf_ref, kf_ref, vf_ref, lfn_ref,
                           rbt_ref, lam_ref, subg_ref, *rest,
                           n_pages, page_size, head_dim, lambda_init, rider_kernel, n_rider_in):
    del pt_ref
    rider_in, rest = rest[:n_rider_in], rest[n_rider_in:]
    pages = [rest[5 * p:5 * p + 5] for p in range(n_pages)]
    od_ref, of_ref, *rider_out = rest[5 * n_pages:-1]
    bias_ref = rest[-1]
    req = pl.program_id(0)
    width = qd_ref.shape[-1]
    n_maps = width // head_dim
    n_dh = n_maps // 2
    e = 2 * head_dim
    sub = req % qd_ref.shape[0]
    new_row = lambda ref: ref[pl.ds(sub, 1), :]

    map_of_lane = lax.broadcasted_iota(jnp.int32, (n_maps, width), 1) // head_dim
    row = lax.broadcasted_iota(jnp.int32, (n_maps, width), 0)
    own = map_of_lane == row
    row_e = lax.broadcasted_iota(jnp.int32, (n_maps, e), 0)
    far_bias = rbt_ref[:, N_BUCKETS - 1:N_BUCKETS]

    @pl.when(req == 0)
    def _():
        lane = lax.broadcasted_iota(jnp.int32, (n_maps, page_size), 1)
        bucket = _rel_bucket(page_size - lane)
        val = jnp.zeros((n_maps, page_size), F32)
        for b in range(N_BUCKETS):
            val = jnp.where(bucket == b, rbt_ref[:, b:b + 1], val)
        bias_ref[...] = (val - far_bias) * LOG2E

    qd = jnp.where(own, new_row(qd_ref), 0.0).astype(BF16)
    qf = jnp.where(own, new_row(qf_ref), 0.0).astype(BF16)

    def self_score(q, k_ref):
        k = new_row(k_ref).astype(BF16).astype(F32)
        return jnp.sum(q.astype(F32) * k, axis=-1, keepdims=True)

    def softmax(s, s_self):
        m = jnp.maximum(jnp.max(s, axis=-1, keepdims=True), s_self)
        p, p_self = jnp.exp2(s - m), jnp.exp2(s_self - m)
        return p, p_self, jnp.sum(p, axis=-1, keepdims=True) + p_self

    cdk, cdv, cfk, cfv, clf = zip(*pages)

    def keys_side_by_side(refs):
        return jnp.concatenate([ref[0].astype(BF16) for ref in refs], axis=1)

    bias = jnp.concatenate([bias_ref[...], jnp.zeros((n_maps, (n_pages - 1) * page_size), F32)], axis=1)
    p, p_self, l = softmax(_dot(qd, keys_side_by_side(cdk)) + bias,
                           self_score(qd, kd_ref) + (rbt_ref[:, 0:1] - far_bias) * LOG2E)
    p = p.astype(BF16)
    v_new = vd_ref[pl.ds(sub * n_dh, n_dh), :]
    acc = jnp.zeros((n_maps, e), F32)
    for h in range(n_dh):
        v_h = jnp.concatenate([ref[0, pl.ds(h, page_size, stride=n_dh), :].astype(BF16) for ref in cdv],
                              axis=0)
        acc = jnp.where(row_e // 2 == h, _dot(p, v_h) + p_self * v_new[h:h + 1, :], acc)
    lam = _lambda_value(lam_ref, lambda_init)
    sign = jnp.where(row_e[:, 0:1] % 2 == 0, 1.0, -lam)
    pick = (lax.broadcasted_iota(jnp.int32, (n_dh, n_maps), 1) // 2
            == lax.broadcasted_iota(jnp.int32, (n_dh, n_maps), 0)).astype(F32)
    o_d = _dot(pick, acc / l * sign, HIGHEST)
    o_d = _rms(o_d, subg_ref[...]) * (1.0 - lambda_init)
    od_ref[pl.ds(sub, 1), :] = jnp.concatenate([o_d[h:h + 1] for h in range(n_dh)], axis=1)

    later = (lax.broadcasted_iota(jnp.int32, (page_size, page_size), 0)
             >= lax.broadcasted_iota(jnp.int32, (page_size, page_size), 1)).astype(F32)
    lane = lax.broadcasted_iota(jnp.int32, lfn_ref.shape[1:], 1)
    carry = jnp.sum(jnp.where(lane == req, lfn_ref[0], 0.0), axis=1, keepdims=True) * LOG2E
    lf = jnp.concatenate([ref[0] for ref in clf], axis=0) * LOG2E
    incl = _dot(lf, later, HIGHEST)
    decay = []
    for p_idx in range(n_pages):
        rows = slice(p_idx * n_maps, (p_idx + 1) * n_maps)
        decay.append(incl[rows] - lf[rows] + carry)
        carry = carry + incl[rows, 0:1]
    p, p_self, l = softmax(_dot(qf, keys_side_by_side(cfk)) + jnp.concatenate(decay, axis=1),
                           self_score(qf, kf_ref))
    acc = jnp.zeros((n_maps, head_dim, page_size), F32)
    for p_idx in range(n_pages):
        p_page = p[:, p_idx * page_size:(p_idx + 1) * page_size]
        acc = acc + p_page[:, None, :] * cfv[p_idx][0].reshape(n_maps, head_dim, page_size)

    def spread(col):
        return jnp.sum(jnp.where(own, col, 0.0), axis=0, keepdims=True)
    past = jnp.sum(acc.reshape(width, page_size).T, axis=0, keepdims=True)
    of_ref[pl.ds(sub, 1), :] = (past + spread(p_self) * new_row(vf_ref)) / spread(l)

    if rider_kernel is not None:
        rider_kernel(*rider_in, *rider_out)


def _decode_requests(page_table, qd, kd, vd, qf, kf, vf, logf_new, rbt, lam_p, subg,
                     cdk, cdv, cfk, cfv, clf, *, head_dim, lambda_init, rider=None):
    r, n_pages = page_table.shape
    _, width, page_size = cdk.shape
    n_f = clf.shape[1]
    n_maps = width // head_dim
    n_dh = n_maps // 2
    e = 2 * head_dim
    assert n_maps == n_f and r % SUBLANES == 0
    assert page_size >= MAX_DISTANCE
    row = pl.BlockSpec((SUBLANES, width), lambda i, pt: (i // SUBLANES, 0))
    head_rows = pl.BlockSpec((SUBLANES * n_dh, e), lambda i, pt: (i // SUBLANES, 0))

    def paged(p, rows):
        return pl.BlockSpec((1, rows, page_size), lambda i, pt: (pt[i, n_pages - 1 - p], 0, 0))

    in_specs = [row, row, head_rows, row, row, row, _resident(logf_new.shape),
                _resident(rbt.shape), _resident(lam_p.shape), _resident(subg.shape)]
    operands = [qd, kd, vd, qf, kf, vf, logf_new, rbt, lam_p, subg]
    out_shape = [jax.ShapeDtypeStruct((r, width), F32)] * 2
    out_specs = [row, row]
    if rider is not None:
        in_specs += list(rider.in_specs)
        operands += list(rider.operands)
        out_shape.append(rider.out_shape)
        out_specs.append(rider.out_spec)
    for p in range(n_pages):
        in_specs += [paged(p, width), paged(p, cdv.shape[1]), paged(p, width), paged(p, width), paged(p, n_f)]
        operands += [cdk, cdv, cfk, cfv, clf]
    return pl.pallas_call(
        functools.partial(_decode_request_kernel, n_pages=n_pages, page_size=page_size,
                          head_dim=head_dim, lambda_init=lambda_init,
                          rider_kernel=rider.kernel if rider else None,
                          n_rider_in=len(rider.operands) if rider else 0),
        out_shape=out_shape,
        grid_spec=pltpu.PrefetchScalarGridSpec(
            num_scalar_prefetch=1,
            grid=(r,),
            in_specs=in_specs,
            out_specs=out_specs,
            scratch_shapes=[pltpu.VMEM((n_maps, page_size), F32)]),
        compiler_params=_params(("arbitrary",), DECODE_VMEM_LIMIT),
        name="decode_attention",
    )(page_table, *operands)


def _merge_kernel(x_ref, od_ref, of_ref, g1_ref, wt_ref, gb_ref, wa_ref, wb_ref, wo_ref, o_ref, *, gate_row):
    x = x_ref[...]
    d = x.shape[-1]
    h = _rms(x, g1_ref[...])
    ya = _dot(od_ref[...].astype(F32), wa_ref[...])
    yb = _dot(of_ref[...].astype(F32), wb_ref[...])
    gate_a = jax.nn.sigmoid(_dot_nt(h, wt_ref[gate_row:gate_row + d, :]) + gb_ref[:, :d])
    gate_b = jax.nn.sigmoid(_dot_nt(h, wt_ref[gate_row + d:gate_row + 2 * d, :]) + gb_ref[:, d:])
    merged = gate_a * ya + gate_b * yb
    o_ref[...] = x + _dot(merged, wo_ref[...])


def _merge(x, od, of, g1, wt, gb, wa, wb, wo):
    m, d = x.shape
    tm = min(ROW_TILE, m)
    gate_row = wt.shape[0] - 2 * d
    assert m % tm == 0 and gate_row % SUBLANES == 0
    row = lambda c: pl.BlockSpec((tm, c), lambda i: (i, 0))
    return pl.pallas_call(
        functools.partial(_merge_kernel, gate_row=gate_row),
        out_shape=jax.ShapeDtypeStruct((m, d), F32),
        grid=(m // tm,),
        in_specs=[row(d), row(od.shape[1]), row(of.shape[1])]
                 + [_resident(a.shape) for a in (g1, wt, gb, wa, wb, wo)],
        out_specs=row(d),
        compiler_params=_params(("arbitrary",)),
        name="merge",
    )(x, od, of, g1, wt, gb, wa, wb, wo)


def _ffn_kernel(x_ref, g2_ref, wgu_ref, wdn_ref, o_ref):
    x = x_ref[...]
    d_ff = wdn_ref.shape[0]
    h = _rms(x, g2_ref[...]).astype(wgu_ref.dtype)
    gate = _dot(h, wgu_ref[:, :d_ff])
    up = _dot(h, wgu_ref[:, d_ff:])
    act = (gate * jax.nn.sigmoid(gate) * up).astype(wdn_ref.dtype)
    o_ref[...] = x + _dot(act, wdn_ref[...])


def _ffn(x, g2, wgu, wdn):
    m, d = x.shape
    tm = min(FFN_ROW_TILE, m)
    assert m % tm == 0
    row = pl.BlockSpec((tm, d), lambda i: (i, 0))
    return pl.pallas_call(
        _ffn_kernel,
        out_shape=jax.ShapeDtypeStruct((m, d), F32),
        grid=(m // tm,),
        in_specs=[row, _resident(g2.shape), _resident(wgu.shape), _resident(wdn.shape)],
        out_specs=row,
        compiler_params=_params(("arbitrary",)),
        name="ffn",
    )(x, g2, wgu, wdn)


def kernel(x_prompt, x_sample, cache_diff_k, cache_diff_v, cache_fox_k, cache_fox_v, cache_fox_logf,
           page_table, rel_bias, norm1_g, w_in, diff_q_g, diff_k_g, fox_q_g, fox_k_g, diff_lambda,
           fox_f_b, gate_b, diff_subln_g, w_branch_a, w_branch_b, w_out, norm2_g, w_gate_up, w_down):
    depth = w_in.shape[0]
    batch, seq, d = x_prompt.shape
    dec_batch, dec_seq, _ = x_sample.shape
    assert dec_seq == 1
    _, n_phys, page_size, n_dh, _, head_dim = cache_diff_k.shape
    n_f = cache_fox_k.shape[3]
    e = 2 * head_dim
    width = n_dh * e
    assert n_f * head_dim == width

    rb_flat = rel_bias.reshape(-1)
    rbt = jnp.repeat(rel_bias.T, 2, axis=0)

    yp, ys = x_prompt, x_sample.reshape(1, dec_batch, d)
    rows_p, rows_s = [], []
    for l in range(depth):
        lambda_init = 0.8 - 0.6 * math.exp(-0.3 * l)
        w_t = w_in[l].T
        fb = fox_f_b[l].reshape(n_f, 1)
        gains = jnp.stack([jnp.tile(g[l], width // head_dim)
                           for g in (diff_q_g, diff_k_g, fox_q_g, fox_k_g)]).reshape(4, width, 1)
        g1 = norm1_g[l].reshape(1, d)
        g2 = norm2_g[l].reshape(1, d)
        gb = gate_b[l].reshape(1, 2 * d)
        subg = diff_subln_g[l].reshape(1, e)
        wa, wb, wo, wgu, wdn = w_branch_a[l], w_branch_b[l], w_out[l], w_gate_up[l], w_down[l]
        lam_p = diff_lambda[l]

        def tail(x, od, of):
            return _ffn(_merge(x, od, of, g1, w_t, gb, wa, wb, wo), g2, wgu, wdn)

        def cache_rows(kd_t, vd, kf_t, vf_t, logf_t):
            b, _, s = kd_t.shape
            return (jnp.transpose(kd_t.reshape(b, n_dh, 2, head_dim, s), (0, 4, 1, 2, 3)),
                    vd.reshape(b, s, n_dh, e),
                    jnp.transpose(kf_t.reshape(b, n_f, head_dim, s), (0, 3, 1, 2)),
                    jnp.transpose(vf_t.reshape(b, n_f, head_dim, s), (0, 3, 1, 2)),
                    jnp.transpose(logf_t, (0, 2, 1)))

        qd_t, kd_t, vd, vd_t, qf_t, kf_t, vf_t, logf_t = _proj(
            yp, g1, w_t, gains, fb, width=width, head_dim=head_dim, with_rows=False)
        od = _diff_attention(rb_flat, qd_t, kd_t, vd_t, lam_p, subg,
                             n_heads=n_dh, head_dim=head_dim, lambda_init=lambda_init)
        of = _fox_attention(qf_t, kf_t, vf_t, logf_t, head_dim=head_dim)
        x1p = _merge(yp.reshape(batch * seq, d), od.reshape(batch * seq, width),
                     of.reshape(batch * seq, width), g1, w_t, gb, wa, wb, wo)
        assert (batch * seq) % dec_batch == 0
        ffn_rows = pl.BlockSpec((batch * seq // dec_batch, d), lambda i, pt: (i, 0))
        wgu_b, wdn_b = wgu.astype(BF16), wdn.astype(BF16)
        rider = _Rider(_ffn_kernel, (x1p, g2, wgu_b, wdn_b),
                       (ffn_rows, _resident(g2.shape), _resident(wgu_b.shape), _resident(wdn_b.shape)),
                       jax.ShapeDtypeStruct(x1p.shape, F32), ffn_rows)
        rows_p.append(cache_rows(kd_t, vd, kf_t, vf_t, logf_t))

        (qd_t, kd_t, vd, _, qf_t, kf_t, vf_t, logf_t, qd_r, kd_r, qf_r, kf_r, vf_r) = _proj(
            ys, g1, w_t, gains, fb, width=width, head_dim=head_dim, with_rows=True)
        as_row = lambda a: a.reshape(dec_batch, width)
        cdk = jnp.transpose(cache_diff_k[l], (0, 2, 3, 4, 1)).reshape(n_phys, width, page_size)
        cdv = cache_diff_v[l].reshape(n_phys, page_size * n_dh, e)
        cfk = jnp.transpose(cache_fox_k[l], (0, 2, 3, 1)).reshape(n_phys, width, page_size)
        cfv = jnp.transpose(cache_fox_v[l], (0, 2, 3, 1)).reshape(n_phys, width, page_size)
        clf = jnp.transpose(cache_fox_logf[l], (0, 2, 1))
        od, of, yp = _decode_requests(
            page_table, as_row(qd_r), as_row(kd_r), vd.reshape(dec_batch * n_dh, e),
            as_row(qf_r), as_row(kf_r), as_row(vf_r), logf_t, rbt, lam_p, subg,
            cdk, cdv, cfk, cfv, clf, head_dim=head_dim, lambda_init=lambda_init, rider=rider)
        yp = yp.reshape(batch, seq, d)
        ys = tail(ys.reshape(dec_batch, d), od, of).reshape(1, dec_batch, d)
        rows_s.append(tuple(jnp.swapaxes(a, 0, 1) for a in cache_rows(kd_t, vd, kf_t, vf_t, logf_t)))

    stack = lambda rows, i: jnp.stack([r[i] for r in rows], axis=0)
    return (yp, ys.reshape(dec_batch, 1, d),
            *(stack(rows_p, i) for i in range(5)), *(stack(rows_s, i) for i in range(5)))
```

```python
import functools
import math
from typing import Any, NamedTuple

import jax
import jax.numpy as jnp
from jax import lax
from jax.experimental import pallas as pl
from jax.experimental.pallas import tpu as pltpu

F32 = jnp.float32
BF16 = jnp.bfloat16

N_BUCKETS = 32
MAX_DISTANCE = 128
EPS = 1e-6
NEG_INF = -1e30
LOG2E = math.log2(math.e)

LANES = 128
SUBLANES = 8
BF16_ROWS = 16
VMEM_LIMIT = 56 * 1024 * 1024
DECODE_VMEM_LIMIT = 60 * 1024 * 1024

ROW_TILE = 512
FFN_ROW_TILE = 256
ATTN_TQ = 512
ATTN_TK = 256
PAGES_PER_STEP = 16
DECAY_PARTS = 3

NT_DIMS = (((1,), (1,)), ((), ()))
HIGHEST = lax.Precision.HIGHEST


def _resident(shape):
    zeros = (0,) * len(shape)
    return pl.BlockSpec(shape, lambda *_: zeros, pipeline_mode=pl.Buffered(1))


def _params(semantics, vmem=VMEM_LIMIT):
    return pltpu.CompilerParams(dimension_semantics=semantics, vmem_limit_bytes=vmem)


def _rms(x, g):
    return x * lax.rsqrt(jnp.mean(x * x, axis=-1, keepdims=True) + EPS) * g


def _dot(a, b, precision=None):
    return jnp.dot(a, b, preferred_element_type=F32, precision=precision)


def _dot_nt(a, b):
    return lax.dot_general(a, b, NT_DIMS, preferred_element_type=F32)


def _log_sigmoid(x):
    return jnp.minimum(x, 0.0) - jnp.log1p(jnp.exp(-jnp.abs(x)))


def _rel_bucket(n):
    max_exact = N_BUCKETS // 2
    nf = jnp.maximum(n, 1).astype(F32)
    large = max_exact + (jnp.log(nf / max_exact) / math.log(MAX_DISTANCE / max_exact)
                         * (N_BUCKETS - max_exact)).astype(jnp.int32)
    return jnp.where(n < max_exact, n, jnp.minimum(large, N_BUCKETS - 1))


def _lambda_value(lam_ref, lambda_init):
    lp = lam_ref[...]
    a = jnp.sum(lp[0:1] * lp[1:2], axis=-1, keepdims=True)
    b = jnp.sum(lp[2:3] * lp[3:4], axis=-1, keepdims=True)
    return jnp.exp(a) - jnp.exp(b) + lambda_init


def _proj_kernel(x_ref, g1_ref, wt_ref, gains_ref, fb_ref,
                 qd_ref, kd_ref, vd_ref, vdt_ref, qf_ref, kf_ref, vf_ref, logf_ref, *row_refs,
                 head_dim, q_scale):
    h = _rms(x_ref[0], g1_ref[...])
    width = kd_ref.shape[1]
    n_heads = width // head_dim
    n_f = logf_ref.shape[1]
    section = lambda i: _dot_nt(wt_ref[i * width:(i + 1) * width, :], h)

    def head_norm_t(z, i):
        z3 = z.reshape(n_heads, head_dim, z.shape[-1])
        ss = jnp.sum(z3 * z3, axis=1, keepdims=True)
        gain = gains_ref[i].reshape(n_heads, head_dim, 1)
        return (z3 * lax.rsqrt(ss * (1.0 / head_dim) + EPS) * gain).reshape(z.shape)

    qd = head_norm_t(section(0), 0) * q_scale
    kd = head_norm_t(section(1), 1)
    qf = head_norm_t(section(3), 2) * q_scale
    kf = head_norm_t(section(4), 3)
    vd_t = section(2)
    vd_rows = vd_t.T
    vf = section(5)
    qd_ref[0] = qd.astype(qd_ref.dtype)
    kd_ref[0] = kd
    e = vd_ref.shape[-1]
    for hd in range(width // e):
        vd_ref[0, pl.ds(hd, vd_rows.shape[0], stride=width // e), :] = vd_rows[:, hd * e:(hd + 1) * e]
    vdt_ref[0] = vd_t.astype(vdt_ref.dtype)
    qf_ref[0] = qf.astype(qf_ref.dtype)
    kf_ref[0] = kf
    vf_ref[0] = vf
    logf_ref[0] = _log_sigmoid(_dot_nt(wt_ref[6 * width:6 * width + n_f, :], h) + fb_ref[...])
    if row_refs:
        for ref, val in zip(row_refs, (qd, kd, qf, kf, vf)):
            ref[0] = val.T


def _proj(x, g1, wt, gains, fb, *, width, head_dim, with_rows):
    b, s, d = x.shape
    n_f = fb.shape[0]
    e = 2 * head_dim
    tm = min(ROW_TILE, s)
    assert s % tm == 0 and tm % LANES == 0 and width % e == 0
    t_blk = lambda r: pl.BlockSpec((1, r, tm), lambda bi, si: (bi, 0, si))
    r_blk = lambda c: pl.BlockSpec((1, tm, c), lambda bi, si: (bi, si, 0))
    t_shape = lambda dt: jax.ShapeDtypeStruct((b, width, s), dt)
    r_shape = jax.ShapeDtypeStruct((b, s, width), F32)
    heads_per_token = width // e
    out_shape = [t_shape(BF16), t_shape(F32), jax.ShapeDtypeStruct((b, s * heads_per_token, e), F32),
                 t_shape(BF16), t_shape(BF16), t_shape(F32), t_shape(F32),
                 jax.ShapeDtypeStruct((b, n_f, s), F32)]
    out_specs = [t_blk(width), t_blk(width),
                 pl.BlockSpec((1, tm * heads_per_token, e), lambda bi, si: (bi, si, 0)),
                 t_blk(width), t_blk(width), t_blk(width), t_blk(width), t_blk(n_f)]
    if with_rows:
        out_shape += [r_shape] * 5
        out_specs += [r_blk(width)] * 5
    return pl.pallas_call(
        functools.partial(_proj_kernel, head_dim=head_dim, q_scale=head_dim ** -0.5 * LOG2E),
        out_shape=out_shape,
        grid=(b, s // tm),
        in_specs=[r_blk(d)] + [_resident(a.shape) for a in (g1, wt, gains, fb)],
        out_specs=out_specs,
        compiler_params=_params(("arbitrary", "arbitrary")),
        name="proj",
    )(x, g1, wt, gains, fb)


def _flash_step(scores, values, state):
    new_state = []
    for s_t, v_t, (m_prev, acc_prev) in zip(scores, values, state):
        m_new = jnp.maximum(m_prev, jnp.max(s_t, axis=0, keepdims=True))
        alpha = jnp.exp2(m_prev - m_new)
        p_t = jnp.exp2(s_t - m_new).astype(BF16)
        new_state.append((m_new, alpha * acc_prev + _dot(v_t, p_t)))
    return tuple(new_state)


def _flash_init(n_slots, rows, tq):
    return tuple((jnp.full((1, tq), NEG_INF, F32), jnp.zeros((rows, tq), F32)) for _ in range(n_slots))


def _ones_rows(seq):
    return jnp.where(lax.broadcasted_iota(jnp.int32, (BF16_ROWS, seq), 0) == 0, 1.0, 0.0).astype(BF16)


def _key_minus_query(tk, tq):
    return lax.broadcasted_iota(jnp.int32, (tk, tq), 0) - lax.broadcasted_iota(jnp.int32, (tk, tq), 1)


class _ScoreRing:
    def __init__(self, s_ref, qk):
        self.s_ref, self.qk = s_ref, qk

    def fill(self, j, slot):
        for idx, s in enumerate(self.qk(j)):
            self.s_ref[slot, idx] = s

    def scores(self, slot):
        return [self.s_ref[slot, idx] for idx in range(self.s_ref.shape[1])]

    def far_sweep(self, softmax, state, n_pairs):
        self.fill(0, 0)

        def body(jj, st):
            j = 2 * jj
            self.fill(j + 1, 1)
            st = softmax(self.scores(0), j, st)
            self.fill(j + 2, 0)
            return softmax(self.scores(1), j + 1, st)
        return lax.fori_loop(0, n_pairs, body, state)

    def tail(self, softmax, state, first, kinds, more_follows):
        assert not more_follows or len(kinds) % 2 == 0
        for n, kind in enumerate(kinds):
            slot = n % 2
            if n + 1 < len(kinds) or more_follows:
                self.fill(first + n + 1, 1 - slot)
            state = softmax(self.scores(slot), first + n, state, kind)
        return state


def _diff_attn_kernel(rb_ref, q_ref, k_ref, v_ref, lam_ref, subg_ref, o_ref,
                      kb_ref, vb_ref, bias_ref, s_ref, *, n_heads, head_dim, lambda_init):
    tq, tk = ATTN_TQ, ATTN_TK
    ratio = tq // tk
    head = pl.program_id(0)
    seq = q_ref.shape[2]
    e = 2 * head_dim
    kmq = _key_minus_query(tk, tq)
    near = [tk] + [-d * tk for d in range(ratio)]

    @pl.when(pl.program_id(1) == 0)
    def _():
        far = rb_ref[(N_BUCKETS - 1) * n_heads + head]
        for idx, delta in enumerate(near):
            bucket = _rel_bucket(jnp.maximum(delta - kmq, 0))
            val = jnp.zeros((tk, tq), F32)
            for b in range(N_BUCKETS):
                val = jnp.where(bucket == b, rb_ref[b * n_heads + head], val)
            bias_ref[idx] = (val - far) * LOG2E

    k_rows = k_ref[0].T
    first = lax.broadcasted_iota(jnp.int32, (1, e), 1) < head_dim
    kb_ref[0] = jnp.where(first, k_rows, 0.0).astype(BF16)
    kb_ref[1] = jnp.where(first, 0.0, k_rows).astype(BF16)
    vb_ref[0:e, :] = v_ref[0]
    vb_ref[e:, :] = _ones_rows(seq)
    lam = _lambda_value(lam_ref, lambda_init)

    def q_block(i, carry):
        qs = pl.multiple_of(i * tq, tq)
        q_t = q_ref[0, :, pl.ds(qs, tq)]

        def qk(j):
            ks = pl.multiple_of(j * tk, tk)
            return [_dot(kb_ref[mp, pl.ds(ks, tk), :], q_t) for mp in range(2)]

        def softmax(scores, j, state, near_idx=None):
            if near_idx is not None:
                scores = [s + bias_ref[near_idx] for s in scores]
                if near[near_idx] <= 0:
                    visible = kmq <= near[near_idx]
                    scores = [jnp.where(visible, s, NEG_INF) for s in scores]
            v_t = vb_ref[:, pl.ds(pl.multiple_of(j * tk, tk), tk)]
            return _flash_step(scores, [v_t, v_t], state)

        ring = _ScoreRing(s_ref, qk)
        first_near = ratio * (i - 1)
        state = ring.far_sweep(softmax, _flash_init(2, vb_ref.shape[0], tq),
                               jnp.maximum(first_near, 0) // 2)
        before = [None] * (ratio - 1) + [0]
        state = lax.cond(i >= 1, lambda st: ring.tail(softmax, st, first_near, before, True),
                         lambda st: st, state)
        state = ring.tail(softmax, state, ratio * i, [1 + d for d in range(ratio)], False)
        (_, acc0), (_, acc1) = state
        o_t = acc0[0:e] / acc0[e:e + 1] - lam * (acc1[0:e] / acc1[e:e + 1])
        o = _rms(o_t.T, subg_ref[...]) * (1.0 - lambda_init)
        o_ref[0, pl.ds(qs, tq), :] = o.astype(o_ref.dtype)
        return carry

    lax.fori_loop(0, seq // tq, q_block, 0)


def _diff_attention(rb_flat, q_t, k_t, v, lam_p, subg, *, n_heads, head_dim, lambda_init):
    b, w, s = q_t.shape
    e = 2 * head_dim
    tq, tk = ATTN_TQ, ATTN_TK
    assert s % tq == 0 and tq % tk == 0 and tk % LANES == 0 and tk >= MAX_DISTANCE and e == LANES
    t_blk = pl.BlockSpec((1, e, s), lambda h, bi, *_: (bi, h, 0))
    r_blk = pl.BlockSpec((1, s, e), lambda h, bi, *_: (bi, 0, h))
    return pl.pallas_call(
        functools.partial(_diff_attn_kernel, n_heads=n_heads, head_dim=head_dim, lambda_init=lambda_init),
        out_shape=jax.ShapeDtypeStruct((b, s, w), BF16),
        grid_spec=pltpu.PrefetchScalarGridSpec(
            num_scalar_prefetch=1,
            grid=(n_heads, b),
            in_specs=[t_blk, t_blk, t_blk, _resident(lam_p.shape), _resident(subg.shape)],
            out_specs=r_blk,
            scratch_shapes=[pltpu.VMEM((2, s, e), BF16), pltpu.VMEM((e + BF16_ROWS, s), BF16),
                            pltpu.VMEM((tq // tk + 1, tk, tq), F32),
                            pltpu.VMEM((2, 2, tk, tq), F32)]),
        compiler_params=_params(("arbitrary", "arbitrary")),
        name="diff_attention",
    )(rb_flat, q_t, k_t, v, lam_p, subg)


def _fox_attn_kernel(q_ref, k_ref, v_ref, logf_ref, o_ref, kb_ref, vb_ref, c_ref, s_ref, *, head_dim):
    tq, tk = ATTN_TQ, ATTN_TK
    ratio = tq // tk
    pair = pl.program_id(1)
    seq = q_ref.shape[2]
    e = 2 * head_dim
    kmq = _key_minus_query(tk, tq)

    @pl.when(pair == 0)
    def _():
        upper = (lax.broadcasted_iota(jnp.int32, (tk, tk), 0)
                 <= lax.broadcasted_iota(jnp.int32, (tk, tk), 1)).astype(F32)
        carry = jnp.zeros((logf_ref.shape[1], 1), F32)
        for blk in range(seq // tk):
            c = _dot(logf_ref[0, :, blk * tk:(blk + 1) * tk], upper, HIGHEST) + carry
            c_ref[:, blk * tk:(blk + 1) * tk] = c
            carry = c[:, tk - 1:tk]

    k_rows = k_ref[0].T
    lane = lax.broadcasted_iota(jnp.int32, (1, e), 1)
    for hh in range(2):
        spare = (1 - hh) * head_dim
        rest = jnp.broadcast_to(c_ref[pl.ds(2 * pair + hh, 1), :] * LOG2E, (LANES, seq)).T
        k_aug = jnp.where(lane // head_dim == hh, k_rows, 0.0)
        for part in range(DECAY_PARTS):
            piece = rest.astype(BF16).astype(F32)
            k_aug = jnp.where(lane == spare + part, piece, k_aug)
            rest = rest - piece
        kb_ref[hh] = k_aug.astype(BF16)
        vb_ref[hh, 0:head_dim, :] = v_ref[0, hh * head_dim:(hh + 1) * head_dim, :].astype(BF16)
        vb_ref[hh, head_dim:, :] = _ones_rows(seq)

    def q_block(i, carry):
        qs = pl.multiple_of(i * tq, tq)
        q_t = q_ref[0, :, pl.ds(qs, tq)]
        row = lax.broadcasted_iota(jnp.int32, (e, 1), 0)
        q_aug = []
        for hh in range(2):
            spare = (1 - hh) * head_dim
            minus_one = jnp.logical_and(row >= spare, row < spare + DECAY_PARTS)
            q_aug.append(jnp.where(row // head_dim == hh, q_t,
                                   jnp.where(minus_one, -1.0, 0.0).astype(BF16)))

        def qk(j):
            ks = pl.multiple_of(j * tk, tk)
            return [_dot(kb_ref[hh, pl.ds(ks, tk), :], q_aug[hh]) for hh in range(2)]

        def softmax(scores, j, state, diagonal=None):
            ks = pl.multiple_of(j * tk, tk)
            if diagonal is not None:
                visible = kmq <= -diagonal * tk
                scores = [jnp.where(visible, s, NEG_INF) for s in scores]
            values = [vb_ref[hh, :, pl.ds(ks, tk)] for hh in range(2)]
            return _flash_step(scores, values, state)

        ring = _ScoreRing(s_ref, qk)
        state = ring.far_sweep(softmax, _flash_init(2, vb_ref.shape[1], tq), ratio * i // 2)
        state = ring.tail(softmax, state, ratio * i, list(range(ratio)), False)
        o_t = jnp.concatenate([acc[0:head_dim] / acc[head_dim:head_dim + 1] for _, acc in state], axis=0)
        o_ref[0, pl.ds(qs, tq), :] = o_t.T.astype(o_ref.dtype)
        return carry

    lax.fori_loop(0, seq // tq, q_block, 0)


def _fox_attention(q_t, k_t, v_t, logf_t, *, head_dim):
    b, w, s = q_t.shape
    e = 2 * head_dim
    tq, tk = ATTN_TQ, ATTN_TK
    n_f = logf_t.shape[1]
    assert s % tq == 0 and tq % tk == 0 and tk % LANES == 0 and e == LANES and w // e * 2 == n_f
    assert DECAY_PARTS <= head_dim
    t_blk = pl.BlockSpec((1, e, s), lambda bi, g: (bi, g, 0))
    return pl.pallas_call(
        functools.partial(_fox_attn_kernel, head_dim=head_dim),
        out_shape=jax.ShapeDtypeStruct((b, s, w), BF16),
        grid=(b, w // e),
        in_specs=[t_blk, t_blk, t_blk, pl.BlockSpec((1, n_f, s), lambda bi, g: (bi, 0, 0))],
        out_specs=pl.BlockSpec((1, s, e), lambda bi, g: (bi, 0, g)),
        scratch_shapes=[pltpu.VMEM((2, s, e), BF16), pltpu.VMEM((2, head_dim + BF16_ROWS, s), BF16),
                        pltpu.VMEM((n_f, s), F32), pltpu.VMEM((2, 2, tk, tq), F32)],
        compiler_params=_params(("arbitrary", "arbitrary")),
        name="fox_attention",
    )(q_t, k_t, v_t, logf_t)


def _decode_kernel(pt_ref, qd_ref, kd_ref, vd_ref, qf_ref, kf_ref, vf_ref, lfn_ref,
                   rbt_ref, lam_ref, subg_ref, *rest,
                   n_pages, page_size, head_dim, lambda_init):
    pps = PAGES_PER_STEP
    pages = [rest[5 * p:5 * p + 5] for p in range(pps)]
    od_ref, of_ref = rest[5 * pps:5 * pps + 2]
    (md_ref, ld_ref, accd_ref, mf_ref, lf_ref, accf_ref, self_ref, carry_ref, bias_ref) = rest[5 * pps + 2:]
    req, grp = pl.program_id(0), pl.program_id(1)
    n_maps = md_ref.shape[0]
    width = qd_ref.shape[-1]
    sub = req % qd_ref.shape[0]
    new_row = lambda ref: ref[pl.ds(sub, 1), :]
    n_dh = n_maps // 2
    e = 2 * head_dim

    map_of_lane = lax.broadcasted_iota(jnp.int32, (n_maps, width), 1) // head_dim
    row = lax.broadcasted_iota(jnp.int32, (n_maps, width), 0)
    own = map_of_lane == row
    row_e = lax.broadcasted_iota(jnp.int32, (n_maps, e), 0)

    @pl.when(jnp.logical_and(pl.program_id(0) == 0, grp == 0))
    def _():
        lane = lax.broadcasted_iota(jnp.int32, (n_maps, page_size), 1)
        bucket = _rel_bucket(page_size - lane)
        val = jnp.zeros((n_maps, page_size), F32)
        for b in range(N_BUCKETS):
            val = jnp.where(bucket == b, rbt_ref[:, b:b + 1], val)
        bias_ref[...] = (val - rbt_ref[:, N_BUCKETS - 1:N_BUCKETS]) * LOG2E

    qd = jnp.where(own, new_row(qd_ref), 0.0).astype(BF16)
    qf = jnp.where(own, new_row(qf_ref), 0.0).astype(BF16)

    @pl.when(grp == 0)
    def _():
        def self_score(q, k_ref):
            k = new_row(k_ref).astype(BF16).astype(F32)
            return jnp.sum(q.astype(F32) * k, axis=-1, keepdims=True)
        bias0 = (rbt_ref[:, 0:1] - rbt_ref[:, N_BUCKETS - 1:N_BUCKETS]) * LOG2E
        md_ref[...] = self_score(qd, kd_ref) + bias0
        mf_ref[...] = self_score(qf, kf_ref)
        ld_ref[...] = jnp.ones(ld_ref.shape, F32)
        lf_ref[...] = jnp.ones(lf_ref.shape, F32)
        v_new = vd_ref[pl.ds(sub * n_dh, n_dh), :]
        acc0 = jnp.zeros(accd_ref.shape, F32)
        for h in range(n_dh):
            acc0 = jnp.where(row_e // 2 == h, v_new[h:h + 1, :], acc0)
        accd_ref[...] = acc0
        accf_ref[...] = jnp.zeros(accf_ref.shape, F32)
        self_ref[...] = jnp.ones(self_ref.shape, F32)
        lane = lax.broadcasted_iota(jnp.int32, lfn_ref.shape[1:], 1)
        log_f_new = jnp.sum(jnp.where(lane == req, lfn_ref[0], 0.0), axis=1, keepdims=True)
        carry_ref[...] = log_f_new * LOG2E

    later = (lax.broadcasted_iota(jnp.int32, (page_size, page_size), 0)
             >= lax.broadcasted_iota(jnp.int32, (page_size, page_size), 1)).astype(F32)

    def softmax_update(s, m_ref, l_ref):
        m_prev = m_ref[...]
        m_new = jnp.maximum(m_prev, jnp.max(s, axis=-1, keepdims=True))
        alpha = jnp.exp2(m_prev - m_new)
        p = jnp.exp2(s - m_new)
        l_ref[...] = alpha * l_ref[...] + jnp.sum(p, axis=-1, keepdims=True)
        m_ref[...] = m_new
        return alpha, p

    cdk, cdv, cfk, cfv, clf = zip(*pages)

    def keys_side_by_side(refs):
        return jnp.concatenate([ref[0].astype(BF16) for ref in refs], axis=1)

    newest = jnp.where(grp == 0, bias_ref[...], 0.0)
    bias = jnp.concatenate([newest, jnp.zeros((n_maps, (pps - 1) * page_size), F32)], axis=1)
    alpha, p = softmax_update(_dot(qd, keys_side_by_side(cdk)) + bias, md_ref, ld_ref)
    p = p.astype(BF16)
    pv = jnp.zeros(accd_ref.shape, F32)
    for h in range(n_dh):
        v_h = jnp.concatenate([ref[0, pl.ds(h, page_size, stride=n_dh), :].astype(BF16) for ref in cdv],
                              axis=0)
        pv = jnp.where(row_e // 2 == h, _dot(p, v_h), pv)
    accd_ref[...] = alpha * accd_ref[...] + pv

    lf = jnp.concatenate([ref[0] for ref in clf], axis=0) * LOG2E
    incl = _dot(lf, later, HIGHEST)
    carry = carry_ref[...]
    decay = []
    for p_idx in range(pps):
        rows = slice(p_idx * n_maps, (p_idx + 1) * n_maps)
        decay.append(incl[rows] - lf[rows] + carry)
        carry = carry + incl[rows, 0:1]
    carry_ref[...] = carry
    s_f = _dot(qf, keys_side_by_side(cfk)) + jnp.concatenate(decay, axis=1)
    alpha, p = softmax_update(s_f, mf_ref, lf_ref)
    acc = alpha[:, :, None] * accf_ref[...]
    for p_idx in range(pps):
        p_page = p[:, p_idx * page_size:(p_idx + 1) * page_size]
        acc = acc + p_page[:, None, :] * cfv[p_idx][0].reshape(n_maps, head_dim, page_size)
    accf_ref[...] = acc
    self_ref[...] = alpha * self_ref[...]

    @pl.when(grp == pl.num_programs(1) - 1)
    def _():
        lam = _lambda_value(lam_ref, lambda_init)
        sign = jnp.where(row_e[:, 0:1] % 2 == 0, 1.0, -lam)
        a = accd_ref[...] / ld_ref[...] * sign
        pick = (lax.broadcasted_iota(jnp.int32, (n_dh, n_maps), 1) // 2
                == lax.broadcasted_iota(jnp.int32, (n_dh, n_maps), 0)).astype(F32)
        o_d = _dot(pick, a, HIGHEST)
        o_d = _rms(o_d, subg_ref[...]) * (1.0 - lambda_init)
        od_ref[pl.ds(sub, 1), :] = jnp.concatenate([o_d[h:h + 1] for h in range(n_dh)], axis=1)

        def spread(col):
            return jnp.sum(jnp.where(own, col, 0.0), axis=0, keepdims=True)
        past = jnp.sum(accf_ref[...].reshape(width, page_size).T, axis=0, keepdims=True)
        of_ref[pl.ds(sub, 1), :] = (past + spread(self_ref[...]) * new_row(vf_ref)) / spread(lf_ref[...])


def _decode_attention(page_table, qd, kd, vd, qf, kf, vf, logf_new, rbt, lam_p, subg,
                      cdk, cdv, cfk, cfv, clf, *, head_dim, lambda_init):
    r, n_pages = page_table.shape
    _, width, page_size = cdk.shape
    n_f = clf.shape[1]
    n_maps = width // head_dim
    n_dh = n_maps // 2
    e = 2 * head_dim
    pps = PAGES_PER_STEP
    assert n_pages % pps == 0 and n_maps == n_f
    assert page_size >= MAX_DISTANCE
    assert r % SUBLANES == 0
    row = pl.BlockSpec((SUBLANES, width), lambda i, g, pt: (i // SUBLANES, 0))
    head_rows = pl.BlockSpec((SUBLANES * n_dh, e), lambda i, g, pt: (i // SUBLANES, 0))

    def paged(p, rows):
        def index(i, g, pt):
            return (pt[i, n_pages - 1 - (g * pps + p)], 0, 0)
        return pl.BlockSpec((1, rows, cdk.shape[2]), index)

    in_specs = [row, row, head_rows, row, row, row, _resident(logf_new.shape),
                _resident(rbt.shape), _resident(lam_p.shape), _resident(subg.shape)]
    operands = [qd, kd, vd, qf, kf, vf, logf_new, rbt, lam_p, subg]
    for p in range(pps):
        in_specs += [paged(p, width), paged(p, cdv.shape[1]), paged(p, width), paged(p, width), paged(p, n_f)]
        operands += [cdk, cdv, cfk, cfv, clf]
    return pl.pallas_call(
        functools.partial(_decode_kernel, n_pages=n_pages, page_size=page_size,
                          head_dim=head_dim, lambda_init=lambda_init),
        out_shape=[jax.ShapeDtypeStruct((r, width), F32)] * 2,
        grid_spec=pltpu.PrefetchScalarGridSpec(
            num_scalar_prefetch=1,
            grid=(r, n_pages // pps),
            in_specs=in_specs,
            out_specs=[row, row],
            scratch_shapes=[pltpu.VMEM((n_maps, 1), F32), pltpu.VMEM((n_maps, 1), F32),
                            pltpu.VMEM((n_maps, e), F32),
                            pltpu.VMEM((n_maps, 1), F32), pltpu.VMEM((n_maps, 1), F32),
                            pltpu.VMEM((n_maps, head_dim, page_size), F32),
                            pltpu.VMEM((n_maps, 1), F32),
                            pltpu.VMEM((n_f, 1), F32), pltpu.VMEM((n_maps, page_size), F32)]),
        compiler_params=_params(("arbitrary", "arbitrary")),
        name="decode_attention",
    )(page_table, *operands)


class _Rider(NamedTuple):
    first: Any
    second: Any
    operands: tuple
    in_specs: tuple
    out_shape: Any
    out_spec: Any


REQUESTS_PER_STEP = 2


def _decode_step_kernel(pt_ref, qd_ref, kd_ref, vd_ref, qf_ref, kf_ref, vf_ref, lfn_ref,
                        rbt_ref, lam_ref, subg_ref, *rest,
                        n_pages, page_size, head_dim, lambda_init, rider, n_rider_in):
    rider_in, rest = rest[:n_rider_in], rest[n_rider_in:]
    caches, rest = rest[:5], rest[5:]
    n_out = len(rest) - 7
    (od_ref, of_ref, *rider_out), (*bufs, sems, bias_ref) = rest[:n_out], rest[n_out:]
    step = pl.program_id(0)
    last = pl.num_programs(0) - 1
    n_maps = bias_ref.shape[0]

    def copies(req, slot):
        for p in range(n_pages):
            page = pt_ref[req, n_pages - 1 - p]
            for cache, buf in zip(caches, bufs):
                yield pltpu.make_async_copy(cache.at[page], buf.at[slot, p], sems.at[slot])

    def fetch(req, slot):
        for copy in copies(req, slot):
            copy.start()

    @pl.when(step == 0)
    def _():
        fetch(0, 0)
        lane = lax.broadcasted_iota(jnp.int32, (n_maps, page_size), 1)
        bucket = _rel_bucket(page_size - lane)
        val = jnp.zeros((n_maps, page_size), F32)
        for b in range(N_BUCKETS):
            val = jnp.where(bucket == b, rbt_ref[:, b:b + 1], val)
        bias_ref[...] = (val - rbt_ref[:, N_BUCKETS - 1:N_BUCKETS]) * LOG2E

    rider_rows = rider_in[0].shape[0] // REQUESTS_PER_STEP if rider is not None else 0
    for k in range(REQUESTS_PER_STEP):
        req = REQUESTS_PER_STEP * step + k
        if k + 1 < REQUESTS_PER_STEP:
            fetch(req + 1, k + 1)
        else:
            @pl.when(step < last)
            def _():
                fetch(req + 1, 0)
        for copy in copies(req, k):
            copy.wait()
        pages = [[buf.at[k, pl.ds(p, 1)] for buf in bufs] for p in range(n_pages)]
        tile = lambda ref: ref.at[pl.ds(k * rider_rows, rider_rows)]
        _decode_one_request(
            req, pages, qd_ref, kd_ref, vd_ref, qf_ref, kf_ref, vf_ref, lfn_ref, rbt_ref, lam_ref,
            subg_ref, od_ref, of_ref, bias_ref,
            [tile(rider_in[0])] + list(rider_in[1:]), [tile(ref) for ref in rider_out],
            page_size=page_size, head_dim=head_dim, lambda_init=lambda_init, rider=rider)


def _decode_one_request(req, pages, qd_ref, kd_ref, vd_ref, qf_ref, kf_ref, vf_ref, lfn_ref,
                        rbt_ref, lam_ref, subg_ref, od_ref, of_ref, bias_ref, rider_in, rider_out,
                        *, page_size, head_dim, lambda_init, rider):
    n_pages = len(pages)
    width = qd_ref.shape[-1]
    n_maps = width // head_dim
    n_dh = n_maps // 2
    e = 2 * head_dim
    sub = req % qd_ref.shape[0]
    new_row = lambda ref: ref[pl.ds(sub, 1), :]

    map_of_lane = lax.broadcasted_iota(jnp.int32, (n_maps, width), 1) // head_dim
    row = lax.broadcasted_iota(jnp.int32, (n_maps, width), 0)
    own = map_of_lane == row
    row_e = lax.broadcasted_iota(jnp.int32, (n_maps, e), 0)
    far_bias = rbt_ref[:, N_BUCKETS - 1:N_BUCKETS]

    qd = jnp.where(own, new_row(qd_ref), 0.0).astype(BF16)
    qf = jnp.where(own, new_row(qf_ref), 0.0).astype(BF16)

    def self_score(q, k_ref):
        k = new_row(k_ref).astype(BF16).astype(F32)
        return jnp.sum(q.astype(F32) * k, axis=-1, keepdims=True)

    def softmax(s, s_self):
        m = jnp.maximum(jnp.max(s, axis=-1, keepdims=True), s_self)
        p, p_self = jnp.exp2(s - m), jnp.exp2(s_self - m)
        return p, p_self, jnp.sum(p, axis=-1, keepdims=True) + p_self

    cdk, cdv, cfk, cfv, clf = zip(*pages)

    def keys_side_by_side(refs):
        return jnp.concatenate([ref[0].astype(BF16) for ref in refs], axis=1)

    rider_carry = rider.first(*rider_in) if rider is not None else None
    bias = jnp.concatenate([bias_ref[...], jnp.zeros((n_maps, (n_pages - 1) * page_size), F32)], axis=1)
    s_d = _dot(qd, keys_side_by_side(cdk)) + bias
    later = (lax.broadcasted_iota(jnp.int32, (page_size, page_size), 0)
             >= lax.broadcasted_iota(jnp.int32, (page_size, page_size), 1)).astype(F32)
    lane = lax.broadcasted_iota(jnp.int32, lfn_ref.shape[1:], 1)
    carry = jnp.sum(jnp.where(lane == req, lfn_ref[0], 0.0), axis=1, keepdims=True) * LOG2E
    lf = jnp.concatenate([ref[0] for ref in clf], axis=0) * LOG2E
    incl = _dot(lf, later, HIGHEST)
    decay = []
    for p_idx in range(n_pages):
        rows = slice(p_idx * n_maps, (p_idx + 1) * n_maps)
        decay.append(incl[rows] - lf[rows] + carry)
        carry = carry + incl[rows, 0:1]
    s_f = _dot(qf, keys_side_by_side(cfk)) + jnp.concatenate(decay, axis=1)

    if rider is not None:
        rider.second(rider_carry, *rider_in, *rider_out)

    p, p_self, l = softmax(s_d, self_score(qd, kd_ref) + (rbt_ref[:, 0:1] - far_bias) * LOG2E)
    p = p.astype(BF16)
    v_new = vd_ref[pl.ds(sub * n_dh, n_dh), :]
    acc = jnp.zeros((n_maps, e), F32)
    for h in range(n_dh):
        v_h = jnp.concatenate([ref[0, pl.ds(h, page_size, stride=n_dh), :].astype(BF16) for ref in cdv],
                              axis=0)
        acc = jnp.where(row_e // 2 == h, _dot(p, v_h) + p_self * v_new[h:h + 1, :], acc)
    lam = _lambda_value(lam_ref, lambda_init)
    sign = jnp.where(row_e[:, 0:1] % 2 == 0, 1.0, -lam)
    pick = (lax.broadcasted_iota(jnp.int32, (n_dh, n_maps), 1) // 2
            == lax.broadcasted_iota(jnp.int32, (n_dh, n_maps), 0)).astype(F32)
    scaled = acc / l * sign

    p, p_self, l = softmax(s_f, self_score(qf, kf_ref))
    acc = jnp.zeros((n_maps, head_dim, page_size), F32)
    for p_idx in range(n_pages):
        p_page = p[:, p_idx * page_size:(p_idx + 1) * page_size]
        acc = acc + p_page[:, None, :] * cfv[p_idx][0].reshape(n_maps, head_dim, page_size)

    def spread(col):
        return jnp.sum(jnp.where(own, col, 0.0), axis=0, keepdims=True)
    past = jnp.sum(acc.reshape(width, page_size).T, axis=0, keepdims=True)
    of_ref[pl.ds(sub, 1), :] = (past + spread(p_self) * new_row(vf_ref)) / spread(l)

    o_d = _dot(pick, scaled, HIGHEST)
    o_d = _rms(o_d, subg_ref[...]) * (1.0 - lambda_init)
    od_ref[pl.ds(sub, 1), :] = jnp.concatenate([o_d[h:h + 1] for h in range(n_dh)], axis=1)


def _decode_requests(page_table, qd, kd, vd, qf, kf, vf, logf_new, rbt, lam_p, subg,
                     cdk, cdv, cfk, cfv, clf, *, head_dim, lambda_init, rider=None):
    r, n_pages = page_table.shape
    _, width, page_size = cdk.shape
    n_f = clf.shape[1]
    n_maps = width // head_dim
    n_dh = n_maps // 2
    e = 2 * head_dim
    per_step = REQUESTS_PER_STEP
    assert n_maps == n_f and r % SUBLANES == 0 and SUBLANES % per_step == 0
    assert page_size >= MAX_DISTANCE
    steps_per_block = SUBLANES // per_step
    row = pl.BlockSpec((SUBLANES, width), lambda i, pt: (i // steps_per_block, 0))
    head_rows = pl.BlockSpec((SUBLANES * n_dh, e), lambda i, pt: (i // steps_per_block, 0))
    caches = [cdk, cdv, cfk, cfv, clf]

    in_specs = [row, row, head_rows, row, row, row, _resident(logf_new.shape),
                _resident(rbt.shape), _resident(lam_p.shape), _resident(subg.shape)]
    operands = [qd, kd, vd, qf, kf, vf, logf_new, rbt, lam_p, subg]
    out_shape = [jax.ShapeDtypeStruct((r, width), F32)] * 2
    out_specs = [row, row]
    if rider is not None:
        in_specs += list(rider.in_specs)
        operands += list(rider.operands)
        out_shape.append(rider.out_shape)
        out_specs.append(rider.out_spec)
    in_specs += [pl.BlockSpec(memory_space=pl.ANY)] * len(caches)
    operands += caches
    return pl.pallas_call(
        functools.partial(_decode_step_kernel, n_pages=n_pages, page_size=page_size,
                          head_dim=head_dim, lambda_init=lambda_init,
                          rider=rider, n_rider_in=len(rider.operands) if rider else 0),
        out_shape=out_shape,
        grid_spec=pltpu.PrefetchScalarGridSpec(
            num_scalar_prefetch=1,
            grid=(r // per_step,),
            in_specs=in_specs,
            out_specs=out_specs,
            scratch_shapes=[pltpu.VMEM((2, n_pages) + c.shape[1:], c.dtype) for c in caches]
                           + [pltpu.SemaphoreType.DMA((2,)), pltpu.VMEM((n_maps, page_size), F32)]),
        compiler_params=_params(("arbitrary",), DECODE_VMEM_LIMIT),
        name="decode_attention",
    )(page_table, *operands)


def _merge_kernel(x_ref, od_ref, of_ref, g1_ref, wt_ref, gb_ref, wa_ref, wb_ref, wo_ref, o_ref, *, gate_row):
    x = x_ref[...]
    d = x.shape[-1]
    h = _rms(x, g1_ref[...])
    ya = _dot(od_ref[...].astype(F32), wa_ref[...])
    yb = _dot(of_ref[...].astype(F32), wb_ref[...])
    gate_a = jax.nn.sigmoid(_dot_nt(h, wt_ref[gate_row:gate_row + d, :]) + gb_ref[:, :d])
    gate_b = jax.nn.sigmoid(_dot_nt(h, wt_ref[gate_row + d:gate_row + 2 * d, :]) + gb_ref[:, d:])
    merged = gate_a * ya + gate_b * yb
    o_ref[...] = x + _dot(merged, wo_ref[...])


def _merge(x, od, of, g1, wt, gb, wa, wb, wo):
    m, d = x.shape
    tm = min(ROW_TILE, m)
    gate_row = wt.shape[0] - 2 * d
    assert m % tm == 0 and gate_row % SUBLANES == 0
    row = lambda c: pl.BlockSpec((tm, c), lambda i: (i, 0))
    return pl.pallas_call(
        functools.partial(_merge_kernel, gate_row=gate_row),
        out_shape=jax.ShapeDtypeStruct((m, d), F32),
        grid=(m // tm,),
        in_specs=[row(d), row(od.shape[1]), row(of.shape[1])]
                 + [_resident(a.shape) for a in (g1, wt, gb, wa, wb, wo)],
        out_specs=row(d),
        compiler_params=_params(("arbitrary",)),
        name="merge",
    )(x, od, of, g1, wt, gb, wa, wb, wo)


def _ffn_up(x_ref, g2_ref, wgu_ref, wdn_ref):
    d_ff = wdn_ref.shape[0]
    h = _rms(x_ref[...], g2_ref[...]).astype(wgu_ref.dtype)
    gate = _dot(h, wgu_ref[:, :d_ff])
    up = _dot(h, wgu_ref[:, d_ff:])
    return (gate * jax.nn.sigmoid(gate) * up).astype(wdn_ref.dtype)


def _ffn_down(act, x_ref, g2_ref, wgu_ref, wdn_ref, o_ref):
    o_ref[...] = x_ref[...] + _dot(act, wdn_ref[...])


def _ffn_kernel(x_ref, g2_ref, wgu_ref, wdn_ref, o_ref):
    _ffn_down(_ffn_up(x_ref, g2_ref, wgu_ref, wdn_ref), x_ref, g2_ref, wgu_ref, wdn_ref, o_ref)


def _ffn(x, g2, wgu, wdn):
    m, d = x.shape
    tm = min(FFN_ROW_TILE, m)
    assert m % tm == 0
    row = pl.BlockSpec((tm, d), lambda i: (i, 0))
    return pl.pallas_call(
        _ffn_kernel,
        out_shape=jax.ShapeDtypeStruct((m, d), F32),
        grid=(m // tm,),
        in_specs=[row, _resident(g2.shape), _resident(wgu.shape), _resident(wdn.shape)],
        out_specs=row,
        compiler_params=_params(("arbitrary",)),
        name="ffn",
    )(x, g2, wgu, wdn)


def kernel(x_prompt, x_sample, cache_diff_k, cache_diff_v, cache_fox_k, cache_fox_v, cache_fox_logf,
           page_table, rel_bias, norm1_g, w_in, diff_q_g, diff_k_g, fox_q_g, fox_k_g, diff_lambda,
           fox_f_b, gate_b, diff_subln_g, w_branch_a, w_branch_b, w_out, norm2_g, w_gate_up, w_down):
    depth = w_in.shape[0]
    batch, seq, d = x_prompt.shape
    dec_batch, dec_seq, _ = x_sample.shape
    assert dec_seq == 1
    _, n_phys, page_size, n_dh, _, head_dim = cache_diff_k.shape
    n_f = cache_fox_k.shape[3]
    e = 2 * head_dim
    width = n_dh * e
    assert n_f * head_dim == width

    rb_flat = rel_bias.reshape(-1)
    rbt = jnp.repeat(rel_bias.T, 2, axis=0)

    yp, ys = x_prompt, x_sample.reshape(1, dec_batch, d)
    rows_p, rows_s = [], []
    for l in range(depth):
        lambda_init = 0.8 - 0.6 * math.exp(-0.3 * l)
        w_t = w_in[l].T
        fb = fox_f_b[l].reshape(n_f, 1)
        gains = jnp.stack([jnp.tile(g[l], width // head_dim)
                           for g in (diff_q_g, diff_k_g, fox_q_g, fox_k_g)]).reshape(4, width, 1)
        g1 = norm1_g[l].reshape(1, d)
        g2 = norm2_g[l].reshape(1, d)
        gb = gate_b[l].reshape(1, 2 * d)
        subg = diff_subln_g[l].reshape(1, e)
        wa, wb, wo, wgu, wdn = w_branch_a[l], w_branch_b[l], w_out[l], w_gate_up[l], w_down[l]
        lam_p = diff_lambda[l]

        def tail(x, od, of):
            return _ffn(_merge(x, od, of, g1, w_t, gb, wa, wb, wo), g2, wgu, wdn)

        def cache_rows(kd_t, vd, kf_t, vf_t, logf_t):
            b, _, s = kd_t.shape
            return (jnp.transpose(kd_t.reshape(b, n_dh, 2, head_dim, s), (0, 4, 1, 2, 3)),
                    vd.reshape(b, s, n_dh, e),
                    jnp.transpose(kf_t.reshape(b, n_f, head_dim, s), (0, 3, 1, 2)),
                    jnp.transpose(vf_t.reshape(b, n_f, head_dim, s), (0, 3, 1, 2)),
                    jnp.transpose(logf_t, (0, 2, 1)))

        qd_t, kd_t, vd, vd_t, qf_t, kf_t, vf_t, logf_t = _proj(
            yp, g1, w_t, gains, fb, width=width, head_dim=head_dim, with_rows=False)
        od = _diff_attention(rb_flat, qd_t, kd_t, vd_t, lam_p, subg,
                             n_heads=n_dh, head_dim=head_dim, lambda_init=lambda_init)
        of = _fox_attention(qf_t, kf_t, vf_t, logf_t, head_dim=head_dim)
        x1p = _merge(yp.reshape(batch * seq, d), od.reshape(batch * seq, width),
                     of.reshape(batch * seq, width), g1, w_t, gb, wa, wb, wo)
        assert (batch * seq) % dec_batch == 0
        ffn_rows = pl.BlockSpec((REQUESTS_PER_STEP * (batch * seq // dec_batch), d), lambda i, pt: (i, 0))
        wgu_b, wdn_b = wgu.astype(BF16), wdn.astype(BF16)
        rider = _Rider(_ffn_up, _ffn_down, (x1p, g2, wgu_b, wdn_b),
                       (ffn_rows, _resident(g2.shape), _resident(wgu_b.shape), _resident(wdn_b.shape)),
                       jax.ShapeDtypeStruct(x1p.shape, F32), ffn_rows)
        rows_p.append(cache_rows(kd_t, vd, kf_t, vf_t, logf_t))

        (qd_t, kd_t, vd, _, qf_t, kf_t, vf_t, logf_t, qd_r, kd_r, qf_r, kf_r, vf_r) = _proj(
            ys, g1, w_t, gains, fb, width=width, head_dim=head_dim, with_rows=True)
        as_row = lambda a: a.reshape(dec_batch, width)
        cdk = jnp.transpose(cache_diff_k[l], (0, 2, 3, 4, 1)).reshape(n_phys, width, page_size)
        cdv = cache_diff_v[l].reshape(n_phys, page_size * n_dh, e)
        cfk = jnp.transpose(cache_fox_k[l], (0, 2, 3, 1)).reshape(n_phys, width, page_size)
        cfv = jnp.transpose(cache_fox_v[l], (0, 2, 3, 1)).reshape(n_phys, width, page_size)
        clf = jnp.transpose(cache_fox_logf[l], (0, 2, 1))
        od, of, yp = _decode_requests(
            page_table, as_row(qd_r), as_row(kd_r), vd.reshape(dec_batch * n_dh, e),
            as_row(qf_r), as_row(kf_r), as_row(vf_r), logf_t, rbt, lam_p, subg,
            cdk, cdv, cfk, cfv, clf, head_dim=head_dim, lambda_init=lambda_init, rider=rider)
        yp = yp.reshape(batch, seq, d)
        ys = tail(ys.reshape(dec_batch, d), od, of).reshape(1, dec_batch, d)
        rows_s.append(tuple(jnp.swapaxes(a, 0, 1) for a in cache_rows(kd_t, vd, kf_t, vf_t, logf_t)))

    stack = lambda rows, i: jnp.stack([r[i] for r in rows], axis=0)
    return (yp, ys.reshape(dec_batch, 1, d),
            *(stack(rows_p, i) for i in range(5)), *(stack(rows_s, i) for i in range(5)))
```

```python
import functools
import math
from typing import Any, NamedTuple

import jax
import jax.numpy as jnp
from jax import lax
from jax.experimental import pallas as pl
from jax.experimental.pallas import tpu as pltpu

F32 = jnp.float32
BF16 = jnp.bfloat16

N_BUCKETS = 32
MAX_DISTANCE = 128
EPS = 1e-6
NEG_INF = -1e30
LOG2E = math.log2(math.e)

LANES = 128
SUBLANES = 8
BF16_ROWS = 16
VMEM_LIMIT = 56 * 1024 * 1024
DECODE_VMEM_LIMIT = 60 * 1024 * 1024

ROW_TILE = 512
FFN_ROW_TILE = 256
ATTN_TQ = 512
ATTN_TK = 256
PAGES_PER_STEP = 16
DECAY_PARTS = 3

NT_DIMS = (((1,), (1,)), ((), ()))
HIGHEST = lax.Precision.HIGHEST


def _resident(shape):
    zeros = (0,) * len(shape)
    return pl.BlockSpec(shape, lambda *_: zeros, pipeline_mode=pl.Buffered(1))


def _params(semantics, vmem=VMEM_LIMIT):
    return pltpu.CompilerParams(dimension_semantics=semantics, vmem_limit_bytes=vmem)


def _rms(x, g):
    return x * lax.rsqrt(jnp.mean(x * x, axis=-1, keepdims=True) + EPS) * g


def _dot(a, b, precision=None):
    return jnp.dot(a, b, preferred_element_type=F32, precision=precision)


def _dot_nt(a, b):
    return lax.dot_general(a, b, NT_DIMS, preferred_element_type=F32)


def _log_sigmoid(x):
    return jnp.minimum(x, 0.0) - jnp.log1p(jnp.exp(-jnp.abs(x)))


def _rel_bucket(n):
    max_exact = N_BUCKETS // 2
    nf = jnp.maximum(n, 1).astype(F32)
    large = max_exact + (jnp.log(nf / max_exact) / math.log(MAX_DISTANCE / max_exact)
                         * (N_BUCKETS - max_exact)).astype(jnp.int32)
    return jnp.where(n < max_exact, n, jnp.minimum(large, N_BUCKETS - 1))


def _lambda_value(lam_ref, lambda_init):
    lp = lam_ref[...]
    a = jnp.sum(lp[0:1] * lp[1:2], axis=-1, keepdims=True)
    b = jnp.sum(lp[2:3] * lp[3:4], axis=-1, keepdims=True)
    return jnp.exp(a) - jnp.exp(b) + lambda_init


def _proj_kernel(x_ref, g1_ref, wt_ref, gains_ref, fb_ref,
                 qd_ref, kd_ref, vd_ref, vdt_ref, qf_ref, kf_ref, vf_ref, logf_ref, *row_refs,
                 head_dim, q_scale):
    h = _rms(x_ref[0], g1_ref[...])
    width = kd_ref.shape[1]
    n_heads = width // head_dim
    n_f = logf_ref.shape[1]
    section = lambda i: _dot_nt(wt_ref[i * width:(i + 1) * width, :], h)

    def head_norm_t(z, i):
        z3 = z.reshape(n_heads, head_dim, z.shape[-1])
        ss = jnp.sum(z3 * z3, axis=1, keepdims=True)
        gain = gains_ref[i].reshape(n_heads, head_dim, 1)
        return (z3 * lax.rsqrt(ss * (1.0 / head_dim) + EPS) * gain).reshape(z.shape)

    qd = head_norm_t(section(0), 0) * q_scale
    kd = head_norm_t(section(1), 1)
    qf = head_norm_t(section(3), 2) * q_scale
    kf = head_norm_t(section(4), 3)
    vd_t = section(2)
    vd_rows = vd_t.T
    vf = section(5)
    qd_ref[0] = qd.astype(qd_ref.dtype)
    kd_ref[0] = kd
    e = vd_ref.shape[-1]
    for hd in range(width // e):
        vd_ref[0, pl.ds(hd, vd_rows.shape[0], stride=width // e), :] = vd_rows[:, hd * e:(hd + 1) * e]
    vdt_ref[0] = vd_t.astype(vdt_ref.dtype)
    qf_ref[0] = qf.astype(qf_ref.dtype)
    kf_ref[0] = kf
    vf_ref[0] = vf
    logf_ref[0] = _log_sigmoid(_dot_nt(wt_ref[6 * width:6 * width + n_f, :], h) + fb_ref[...])
    if row_refs:
        for ref, val in zip(row_refs, (qd, kd, qf, kf, vf)):
            ref[0] = val.T


def _proj(x, g1, wt, gains, fb, *, width, head_dim, with_rows):
    b, s, d = x.shape
    n_f = fb.shape[0]
    e = 2 * head_dim
    tm = min(ROW_TILE, s)
    assert s % tm == 0 and tm % LANES == 0 and width % e == 0
    t_blk = lambda r: pl.BlockSpec((1, r, tm), lambda bi, si: (bi, 0, si))
    r_blk = lambda c: pl.BlockSpec((1, tm, c), lambda bi, si: (bi, si, 0))
    t_shape = lambda dt: jax.ShapeDtypeStruct((b, width, s), dt)
    r_shape = jax.ShapeDtypeStruct((b, s, width), F32)
    heads_per_token = width // e
    out_shape = [t_shape(BF16), t_shape(F32), jax.ShapeDtypeStruct((b, s * heads_per_token, e), F32),
                 t_shape(BF16), t_shape(BF16), t_shape(F32), t_shape(F32),
                 jax.ShapeDtypeStruct((b, n_f, s), F32)]
    out_specs = [t_blk(width), t_blk(width),
                 pl.BlockSpec((1, tm * heads_per_token, e), lambda bi, si: (bi, si, 0)),
                 t_blk(width), t_blk(width), t_blk(width), t_blk(width), t_blk(n_f)]
    if with_rows:
        out_shape += [r_shape] * 5
        out_specs += [r_blk(width)] * 5
    return pl.pallas_call(
        functools.partial(_proj_kernel, head_dim=head_dim, q_scale=head_dim ** -0.5 * LOG2E),
        out_shape=out_shape,
        grid=(b, s // tm),
        in_specs=[r_blk(d)] + [_resident(a.shape) for a in (g1, wt, gains, fb)],
        out_specs=out_specs,
        compiler_params=_params(("arbitrary", "arbitrary")),
        name="proj",
    )(x, g1, wt, gains, fb)


def _flash_step(scores, values, state):
    new_state = []
    for s_t, v_t, (m_prev, acc_prev) in zip(scores, values, state):
        m_new = jnp.maximum(m_prev, jnp.max(s_t, axis=0, keepdims=True))
        alpha = jnp.exp2(m_prev - m_new)
        p_t = jnp.exp2(s_t - m_new).astype(BF16)
        new_state.append((m_new, alpha * acc_prev + _dot(v_t, p_t)))
    return tuple(new_state)


def _flash_init(n_slots, rows, tq):
    return tuple((jnp.full((1, tq), NEG_INF, F32), jnp.zeros((rows, tq), F32)) for _ in range(n_slots))


def _ones_rows(seq):
    return jnp.where(lax.broadcasted_iota(jnp.int32, (BF16_ROWS, seq), 0) == 0, 1.0, 0.0).astype(BF16)


def _key_minus_query(tk, tq):
    return lax.broadcasted_iota(jnp.int32, (tk, tq), 0) - lax.broadcasted_iota(jnp.int32, (tk, tq), 1)


class _ScoreRing:
    def __init__(self, s_ref, qk):
        self.s_ref, self.qk = s_ref, qk

    def fill(self, j, slot):
        for idx, s in enumerate(self.qk(j)):
            self.s_ref[slot, idx] = s

    def scores(self, slot):
        return [self.s_ref[slot, idx] for idx in range(self.s_ref.shape[1])]

    def far_sweep(self, softmax, state, n_pairs):
        self.fill(0, 0)

        def body(jj, st):
            j = 2 * jj
            self.fill(j + 1, 1)
            st = softmax(self.scores(0), j, st)
            self.fill(j + 2, 0)
            return softmax(self.scores(1), j + 1, st)
        return lax.fori_loop(0, n_pairs, body, state)

    def tail(self, softmax, state, first, kinds, more_follows):
        assert not more_follows or len(kinds) % 2 == 0
        for n, kind in enumerate(kinds):
            slot = n % 2
            if n + 1 < len(kinds) or more_follows:
                self.fill(first + n + 1, 1 - slot)
            state = softmax(self.scores(slot), first + n, state, kind)
        return state


def _diff_attn_kernel(rb_ref, q_ref, k_ref, v_ref, lam_ref, subg_ref, o_ref,
                      kb_ref, vb_ref, bias_ref, s_ref, *, n_heads, head_dim, lambda_init):
    tq, tk = ATTN_TQ, ATTN_TK
    ratio = tq // tk
    head = pl.program_id(0)
    seq = q_ref.shape[2]
    e = 2 * head_dim
    kmq = _key_minus_query(tk, tq)
    near = [tk] + [-d * tk for d in range(ratio)]

    @pl.when(pl.program_id(1) == 0)
    def _():
        far = rb_ref[(N_BUCKETS - 1) * n_heads + head]
        for idx, delta in enumerate(near):
            bucket = _rel_bucket(jnp.maximum(delta - kmq, 0))
            val = jnp.zeros((tk, tq), F32)
            for b in range(N_BUCKETS):
                val = jnp.where(bucket == b, rb_ref[b * n_heads + head], val)
            bias_ref[idx] = (val - far) * LOG2E

    k_rows = k_ref[0].T
    first = lax.broadcasted_iota(jnp.int32, (1, e), 1) < head_dim
    kb_ref[0] = jnp.where(first, k_rows, 0.0).astype(BF16)
    kb_ref[1] = jnp.where(first, 0.0, k_rows).astype(BF16)
    vb_ref[0:e, :] = v_ref[0]
    vb_ref[e:, :] = _ones_rows(seq)
    lam = _lambda_value(lam_ref, lambda_init)

    def q_block(i, carry):
        qs = pl.multiple_of(i * tq, tq)
        q_t = q_ref[0, :, pl.ds(qs, tq)]

        def qk(j):
            ks = pl.multiple_of(j * tk, tk)
            return [_dot(kb_ref[mp, pl.ds(ks, tk), :], q_t) for mp in range(2)]

        def softmax(scores, j, state, near_idx=None):
            if near_idx is not None:
                scores = [s + bias_ref[near_idx] for s in scores]
                if near[near_idx] <= 0:
                    visible = kmq <= near[near_idx]
                    scores = [jnp.where(visible, s, NEG_INF) for s in scores]
            v_t = vb_ref[:, pl.ds(pl.multiple_of(j * tk, tk), tk)]
            return _flash_step(scores, [v_t, v_t], state)

        ring = _ScoreRing(s_ref, qk)
        first_near = ratio * (i - 1)
        state = ring.far_sweep(softmax, _flash_init(2, vb_ref.shape[0], tq),
                               jnp.maximum(first_near, 0) // 2)
        before = [None] * (ratio - 1) + [0]
        state = lax.cond(i >= 1, lambda st: ring.tail(softmax, st, first_near, before, True),
                         lambda st: st, state)
        state = ring.tail(softmax, state, ratio * i, [1 + d for d in range(ratio)], False)
        (_, acc0), (_, acc1) = state
        o_t = acc0[0:e] / acc0[e:e + 1] - lam * (acc1[0:e] / acc1[e:e + 1])
        o = _rms(o_t.T, subg_ref[...]) * (1.0 - lambda_init)
        o_ref[0, pl.ds(qs, tq), :] = o.astype(o_ref.dtype)
        return carry

    lax.fori_loop(0, seq // tq, q_block, 0)


def _diff_attention(rb_flat, q_t, k_t, v, lam_p, subg, *, n_heads, head_dim, lambda_init):
    b, w, s = q_t.shape
    e = 2 * head_dim
    tq, tk = ATTN_TQ, ATTN_TK
    assert s % tq == 0 and tq % tk == 0 and tk % LANES == 0 and tk >= MAX_DISTANCE and e == LANES
    t_blk = pl.BlockSpec((1, e, s), lambda h, bi, *_: (bi, h, 0))
    r_blk = pl.BlockSpec((1, s, e), lambda h, bi, *_: (bi, 0, h))
    return pl.pallas_call(
        functools.partial(_diff_attn_kernel, n_heads=n_heads, head_dim=head_dim, lambda_init=lambda_init),
        out_shape=jax.ShapeDtypeStruct((b, s, w), BF16),
        grid_spec=pltpu.PrefetchScalarGridSpec(
            num_scalar_prefetch=1,
            grid=(n_heads, b),
            in_specs=[t_blk, t_blk, t_blk, _resident(lam_p.shape), _resident(subg.shape)],
            out_specs=r_blk,
            scratch_shapes=[pltpu.VMEM((2, s, e), BF16), pltpu.VMEM((e + BF16_ROWS, s), BF16),
                            pltpu.VMEM((tq // tk + 1, tk, tq), F32),
                            pltpu.VMEM((2, 2, tk, tq), F32)]),
        compiler_params=_params(("arbitrary", "arbitrary")),
        name="diff_attention",
    )(rb_flat, q_t, k_t, v, lam_p, subg)


def _fox_attn_kernel(q_ref, k_ref, v_ref, logf_ref, o_ref, kb_ref, vb_ref, c_ref, s_ref, *, head_dim):
    tq, tk = ATTN_TQ, ATTN_TK
    ratio = tq // tk
    pair = pl.program_id(1)
    seq = q_ref.shape[2]
    e = 2 * head_dim
    kmq = _key_minus_query(tk, tq)

    @pl.when(pair == 0)
    def _():
        upper = (lax.broadcasted_iota(jnp.int32, (tk, tk), 0)
                 <= lax.broadcasted_iota(jnp.int32, (tk, tk), 1)).astype(F32)
        carry = jnp.zeros((logf_ref.shape[1], 1), F32)
        for blk in range(seq // tk):
            c = _dot(logf_ref[0, :, blk * tk:(blk + 1) * tk], upper, HIGHEST) + carry
            c_ref[:, blk * tk:(blk + 1) * tk] = c
            carry = c[:, tk - 1:tk]

    k_rows = k_ref[0].T
    lane = lax.broadcasted_iota(jnp.int32, (1, e), 1)
    for hh in range(2):
        spare = (1 - hh) * head_dim
        rest = jnp.broadcast_to(c_ref[pl.ds(2 * pair + hh, 1), :] * LOG2E, (LANES, seq)).T
        k_aug = jnp.where(lane // head_dim == hh, k_rows, 0.0)
        for part in range(DECAY_PARTS):
            piece = rest.astype(BF16).astype(F32)
            k_aug = jnp.where(lane == spare + part, piece, k_aug)
            rest = rest - piece
        kb_ref[hh] = k_aug.astype(BF16)
        vb_ref[hh, 0:head_dim, :] = v_ref[0, hh * head_dim:(hh + 1) * head_dim, :].astype(BF16)
        vb_ref[hh, head_dim:, :] = _ones_rows(seq)

    def q_block(i, carry):
        qs = pl.multiple_of(i * tq, tq)
        q_t = q_ref[0, :, pl.ds(qs, tq)]
        row = lax.broadcasted_iota(jnp.int32, (e, 1), 0)
        q_aug = []
        for hh in range(2):
            spare = (1 - hh) * head_dim
            minus_one = jnp.logical_and(row >= spare, row < spare + DECAY_PARTS)
            q_aug.append(jnp.where(row // head_dim == hh, q_t,
                                   jnp.where(minus_one, -1.0, 0.0).astype(BF16)))

        def qk(j):
            ks = pl.multiple_of(j * tk, tk)
            return [_dot(kb_ref[hh, pl.ds(ks, tk), :], q_aug[hh]) for hh in range(2)]

        def softmax(scores, j, state, diagonal=None):
            ks = pl.multiple_of(j * tk, tk)
            if diagonal is not None:
                visible = kmq <= -diagonal * tk
                scores = [jnp.where(visible, s, NEG_INF) for s in scores]
            values = [vb_ref[hh, :, pl.ds(ks, tk)] for hh in range(2)]
            return _flash_step(scores, values, state)

        ring = _ScoreRing(s_ref, qk)
        state = ring.far_sweep(softmax, _flash_init(2, vb_ref.shape[1], tq), ratio * i // 2)
        state = ring.tail(softmax, state, ratio * i, list(range(ratio)), False)
        o_t = jnp.concatenate([acc[0:head_dim] / acc[head_dim:head_dim + 1] for _, acc in state], axis=0)
        o_ref[0, pl.ds(qs, tq), :] = o_t.T.astype(o_ref.dtype)
        return carry

    lax.fori_loop(0, seq // tq, q_block, 0)


def _fox_attention(q_t, k_t, v_t, logf_t, *, head_dim):
    b, w, s = q_t.shape
    e = 2 * head_dim
    tq, tk = ATTN_TQ, ATTN_TK
    n_f = logf_t.shape[1]
    assert s % tq == 0 and tq % tk == 0 and tk % LANES == 0 and e == LANES and w // e * 2 == n_f
    assert DECAY_PARTS <= head_dim
    t_blk = pl.BlockSpec((1, e, s), lambda bi, g: (bi, g, 0))
    return pl.pallas_call(
        functools.partial(_fox_attn_kernel, head_dim=head_dim),
        out_shape=jax.ShapeDtypeStruct((b, s, w), BF16),
        grid=(b, w // e),
        in_specs=[t_blk, t_blk, t_blk, pl.BlockSpec((1, n_f, s), lambda bi, g: (bi, 0, 0))],
        out_specs=pl.BlockSpec((1, s, e), lambda bi, g: (bi, 0, g)),
        scratch_shapes=[pltpu.VMEM((2, s, e), BF16), pltpu.VMEM((2, head_dim + BF16_ROWS, s), BF16),
                        pltpu.VMEM((n_f, s), F32), pltpu.VMEM((2, 2, tk, tq), F32)],
        compiler_params=_params(("arbitrary", "arbitrary")),
        name="fox_attention",
    )(q_t, k_t, v_t, logf_t)


def _decode_kernel(pt_ref, qd_ref, kd_ref, vd_ref, qf_ref, kf_ref, vf_ref, lfn_ref,
                   rbt_ref, lam_ref, subg_ref, *rest,
                   n_pages, page_size, head_dim, lambda_init):
    pps = PAGES_PER_STEP
    pages = [rest[5 * p:5 * p + 5] for p in range(pps)]
    od_ref, of_ref = rest[5 * pps:5 * pps + 2]
    (md_ref, ld_ref, accd_ref, mf_ref, lf_ref, accf_ref, self_ref, carry_ref, bias_ref) = rest[5 * pps + 2:]
    req, grp = pl.program_id(0), pl.program_id(1)
    n_maps = md_ref.shape[0]
    width = qd_ref.shape[-1]
    sub = req % qd_ref.shape[0]
    new_row = lambda ref: ref[pl.ds(sub, 1), :]
    n_dh = n_maps // 2
    e = 2 * head_dim

    map_of_lane = lax.broadcasted_iota(jnp.int32, (n_maps, width), 1) // head_dim
    row = lax.broadcasted_iota(jnp.int32, (n_maps, width), 0)
    own = map_of_lane == row
    row_e = lax.broadcasted_iota(jnp.int32, (n_maps, e), 0)

    @pl.when(jnp.logical_and(pl.program_id(0) == 0, grp == 0))
    def _():
        lane = lax.broadcasted_iota(jnp.int32, (n_maps, page_size), 1)
        bucket = _rel_bucket(page_size - lane)
        val = jnp.zeros((n_maps, page_size), F32)
        for b in range(N_BUCKETS):
            val = jnp.where(bucket == b, rbt_ref[:, b:b + 1], val)
        bias_ref[...] = (val - rbt_ref[:, N_BUCKETS - 1:N_BUCKETS]) * LOG2E

    qd = jnp.where(own, new_row(qd_ref), 0.0).astype(BF16)
    qf = jnp.where(own, new_row(qf_ref), 0.0).astype(BF16)

    @pl.when(grp == 0)
    def _():
        def self_score(q, k_ref):
            k = new_row(k_ref).astype(BF16).astype(F32)
            return jnp.sum(q.astype(F32) * k, axis=-1, keepdims=True)
        bias0 = (rbt_ref[:, 0:1] - rbt_ref[:, N_BUCKETS - 1:N_BUCKETS]) * LOG2E
        md_ref[...] = self_score(qd, kd_ref) + bias0
        mf_ref[...] = self_score(qf, kf_ref)
        ld_ref[...] = jnp.ones(ld_ref.shape, F32)
        lf_ref[...] = jnp.ones(lf_ref.shape, F32)
        v_new = vd_ref[pl.ds(sub * n_dh, n_dh), :]
        acc0 = jnp.zeros(accd_ref.shape, F32)
        for h in range(n_dh):
            acc0 = jnp.where(row_e // 2 == h, v_new[h:h + 1, :], acc0)
        accd_ref[...] = acc0
        accf_ref[...] = jnp.zeros(accf_ref.shape, F32)
        self_ref[...] = jnp.ones(self_ref.shape, F32)
        lane = lax.broadcasted_iota(jnp.int32, lfn_ref.shape[1:], 1)
        log_f_new = jnp.sum(jnp.where(lane == req, lfn_ref[0], 0.0), axis=1, keepdims=True)
        carry_ref[...] = log_f_new * LOG2E

    later = (lax.broadcasted_iota(jnp.int32, (page_size, page_size), 0)
             >= lax.broadcasted_iota(jnp.int32, (page_size, page_size), 1)).astype(F32)

    def softmax_update(s, m_ref, l_ref):
        m_prev = m_ref[...]
        m_new = jnp.maximum(m_prev, jnp.max(s, axis=-1, keepdims=True))
        alpha = jnp.exp2(m_prev - m_new)
        p = jnp.exp2(s - m_new)
        l_ref[...] = alpha * l_ref[...] + jnp.sum(p, axis=-1, keepdims=True)
        m_ref[...] = m_new
        return alpha, p

    cdk, cdv, cfk, cfv, clf = zip(*pages)

    def keys_side_by_side(refs):
        return jnp.concatenate([ref[0].astype(BF16) for ref in refs], axis=1)

    newest = jnp.where(grp == 0, bias_ref[...], 0.0)
    bias = jnp.concatenate([newest, jnp.zeros((n_maps, (pps - 1) * page_size), F32)], axis=1)
    alpha, p = softmax_update(_dot(qd, keys_side_by_side(cdk)) + bias, md_ref, ld_ref)
    p = p.astype(BF16)
    pv = jnp.zeros(accd_ref.shape, F32)
    for h in range(n_dh):
        v_h = jnp.concatenate([ref[0, pl.ds(h, page_size, stride=n_dh), :].astype(BF16) for ref in cdv],
                              axis=0)
        pv = jnp.where(row_e // 2 == h, _dot(p, v_h), pv)
    accd_ref[...] = alpha * accd_ref[...] + pv

    lf = jnp.concatenate([ref[0] for ref in clf], axis=0) * LOG2E
    incl = _dot(lf, later, HIGHEST)
    carry = carry_ref[...]
    decay = []
    for p_idx in range(pps):
        rows = slice(p_idx * n_maps, (p_idx + 1) * n_maps)
        decay.append(incl[rows] - lf[rows] + carry)
        carry = carry + incl[rows, 0:1]
    carry_ref[...] = carry
    s_f = _dot(qf, keys_side_by_side(cfk)) + jnp.concatenate(decay, axis=1)
    alpha, p = softmax_update(s_f, mf_ref, lf_ref)
    acc = alpha[:, :, None] * accf_ref[...]
    for p_idx in range(pps):
        p_page = p[:, p_idx * page_size:(p_idx + 1) * page_size]
        acc = acc + p_page[:, None, :] * cfv[p_idx][0].reshape(n_maps, head_dim, page_size)
    accf_ref[...] = acc
    self_ref[...] = alpha * self_ref[...]

    @pl.when(grp == pl.num_programs(1) - 1)
    def _():
        lam = _lambda_value(lam_ref, lambda_init)
        sign = jnp.where(row_e[:, 0:1] % 2 == 0, 1.0, -lam)
        a = accd_ref[...] / ld_ref[...] * sign
        pick = (lax.broadcasted_iota(jnp.int32, (n_dh, n_maps), 1) // 2
                == lax.broadcasted_iota(jnp.int32, (n_dh, n_maps), 0)).astype(F32)
        o_d = _dot(pick, a, HIGHEST)
        o_d = _rms(o_d, subg_ref[...]) * (1.0 - lambda_init)
        od_ref[pl.ds(sub, 1), :] = jnp.concatenate([o_d[h:h + 1] for h in range(n_dh)], axis=1)

        def spread(col):
            return jnp.sum(jnp.where(own, col, 0.0), axis=0, keepdims=True)
        past = jnp.sum(accf_ref[...].reshape(width, page_size).T, axis=0, keepdims=True)
        of_ref[pl.ds(sub, 1), :] = (past + spread(self_ref[...]) * new_row(vf_ref)) / spread(lf_ref[...])


def _decode_attention(page_table, qd, kd, vd, qf, kf, vf, logf_new, rbt, lam_p, subg,
                      cdk, cdv, cfk, cfv, clf, *, head_dim, lambda_init):
    r, n_pages = page_table.shape
    _, width, page_size = cdk.shape
    n_f = clf.shape[1]
    n_maps = width // head_dim
    n_dh = n_maps // 2
    e = 2 * head_dim
    pps = PAGES_PER_STEP
    assert n_pages % pps == 0 and n_maps == n_f
    assert page_size >= MAX_DISTANCE
    assert r % SUBLANES == 0
    row = pl.BlockSpec((SUBLANES, width), lambda i, g, pt: (i // SUBLANES, 0))
    head_rows = pl.BlockSpec((SUBLANES * n_dh, e), lambda i, g, pt: (i // SUBLANES, 0))

    def paged(p, rows):
        def index(i, g, pt):
            return (pt[i, n_pages - 1 - (g * pps + p)], 0, 0)
        return pl.BlockSpec((1, rows, cdk.shape[2]), index)

    in_specs = [row, row, head_rows, row, row, row, _resident(logf_new.shape),
                _resident(rbt.shape), _resident(lam_p.shape), _resident(subg.shape)]
    operands = [qd, kd, vd, qf, kf, vf, logf_new, rbt, lam_p, subg]
    for p in range(pps):
        in_specs += [paged(p, width), paged(p, cdv.shape[1]), paged(p, width), paged(p, width), paged(p, n_f)]
        operands += [cdk, cdv, cfk, cfv, clf]
    return pl.pallas_call(
        functools.partial(_decode_kernel, n_pages=n_pages, page_size=page_size,
                          head_dim=head_dim, lambda_init=lambda_init),
        out_shape=[jax.ShapeDtypeStruct((r, width), F32)] * 2,
        grid_spec=pltpu.PrefetchScalarGridSpec(
            num_scalar_prefetch=1,
            grid=(r, n_pages // pps),
            in_specs=in_specs,
            out_specs=[row, row],
            scratch_shapes=[pltpu.VMEM((n_maps, 1), F32), pltpu.VMEM((n_maps, 1), F32),
                            pltpu.VMEM((n_maps, e), F32),
                            pltpu.VMEM((n_maps, 1), F32), pltpu.VMEM((n_maps, 1), F32),
                            pltpu.VMEM((n_maps, head_dim, page_size), F32),
                            pltpu.VMEM((n_maps, 1), F32),
                            pltpu.VMEM((n_f, 1), F32), pltpu.VMEM((n_maps, page_size), F32)]),
        compiler_params=_params(("arbitrary", "arbitrary")),
        name="decode_attention",
    )(page_table, *operands)


class _Rider(NamedTuple):
    first: Any
    second: Any
    operands: tuple
    in_specs: tuple
    out_shape: Any
    out_spec: Any


REQUESTS_PER_STEP = 2


def _decode_step_kernel(pt_ref, qd_ref, kd_ref, vd_ref, qf_ref, kf_ref, vf_ref, lfn_ref,
                        rbt_ref, lam_ref, subg_ref, *rest,
                        n_pages, page_size, head_dim, lambda_init, rider, n_rider_in):
    rider_in, rest = rest[:n_rider_in], rest[n_rider_in:]
    caches, rest = rest[:5], rest[5:]
    n_out = len(rest) - 7
    (od_ref, of_ref, *rider_out), (*bufs, sems, bias_ref) = rest[:n_out], rest[n_out:]
    step = pl.program_id(0)
    last = pl.num_programs(0) - 1
    n_maps = bias_ref.shape[0]

    def copies(req, slot):
        for p in range(n_pages):
            page = pt_ref[req, n_pages - 1 - p]
            for cache, buf in zip(caches, bufs):
                yield pltpu.make_async_copy(cache.at[page], buf.at[slot, p], sems.at[slot])

    def fetch(req, slot):
        for n, copy in enumerate(copies(req, slot)):
            copy.start(priority=n % 2)

    @pl.when(step == 0)
    def _():
        fetch(0, 0)
        lane = lax.broadcasted_iota(jnp.int32, (n_maps, page_size), 1)
        bucket = _rel_bucket(page_size - lane)
        val = jnp.zeros((n_maps, page_size), F32)
        for b in range(N_BUCKETS):
            val = jnp.where(bucket == b, rbt_ref[:, b:b + 1], val)
        bias_ref[...] = (val - rbt_ref[:, N_BUCKETS - 1:N_BUCKETS]) * LOG2E

    rider_rows = rider_in[0].shape[0] // REQUESTS_PER_STEP if rider is not None else 0
    for k in range(REQUESTS_PER_STEP):
        req = REQUESTS_PER_STEP * step + k
        if k + 1 < REQUESTS_PER_STEP:
            fetch(req + 1, k + 1)
        else:
            @pl.when(step < last)
            def _():
                fetch(req + 1, 0)
        for copy in copies(req, k):
            copy.wait()
        pages = [[buf.at[k, pl.ds(p, 1)] for buf in bufs] for p in range(n_pages)]
        tile = lambda ref: ref.at[pl.ds(k * rider_rows, rider_rows)]
        _decode_one_request(
            req, pages, qd_ref, kd_ref, vd_ref, qf_ref, kf_ref, vf_ref, lfn_ref, rbt_ref, lam_ref,
            subg_ref, od_ref, of_ref, bias_ref,
            [tile(rider_in[0])] + list(rider_in[1:]), [tile(ref) for ref in rider_out],
            page_size=page_size, head_dim=head_dim, lambda_init=lambda_init, rider=rider)


def _decode_one_request(req, pages, qd_ref, kd_ref, vd_ref, qf_ref, kf_ref, vf_ref, lfn_ref,
                        rbt_ref, lam_ref, subg_ref, od_ref, of_ref, bias_ref, rider_in, rider_out,
                        *, page_size, head_dim, lambda_init, rider):
    n_pages = len(pages)
    width = qd_ref.shape[-1]
    n_maps = width // head_dim
    n_dh = n_maps // 2
    e = 2 * head_dim
    sub = req % qd_ref.shape[0]
    new_row = lambda ref: ref[pl.ds(sub, 1), :]

    map_of_lane = lax.broadcasted_iota(jnp.int32, (n_maps, width), 1) // head_dim
    row = lax.broadcasted_iota(jnp.int32, (n_maps, width), 0)
    own = map_of_lane == row
    row_e = lax.broadcasted_iota(jnp.int32, (n_maps, e), 0)
    far_bias = rbt_ref[:, N_BUCKETS - 1:N_BUCKETS]

    qd = jnp.where(own, new_row(qd_ref), 0.0).astype(BF16)
    qf = jnp.where(own, new_row(qf_ref), 0.0).astype(BF16)

    def self_score(q, k_ref):
        k = new_row(k_ref).astype(BF16).astype(F32)
        return jnp.sum(q.astype(F32) * k, axis=-1, keepdims=True)

    def softmax(s, s_self):
        m = jnp.maximum(jnp.max(s, axis=-1, keepdims=True), s_self)
        p, p_self = jnp.exp2(s - m), jnp.exp2(s_self - m)
        return p, p_self, jnp.sum(p, axis=-1, keepdims=True) + p_self

    cdk, cdv, cfk, cfv, clf = zip(*pages)

    def keys_side_by_side(refs):
        return jnp.concatenate([ref[0].astype(BF16) for ref in refs], axis=1)

    rider_carry = rider.first(*rider_in) if rider is not None else None
    bias = jnp.concatenate([bias_ref[...], jnp.zeros((n_maps, (n_pages - 1) * page_size), F32)], axis=1)
    s_d = _dot(qd, keys_side_by_side(cdk)) + bias
    later = (lax.broadcasted_iota(jnp.int32, (page_size, page_size), 0)
             >= lax.broadcasted_iota(jnp.int32, (page_size, page_size), 1)).astype(F32)
    lane = lax.broadcasted_iota(jnp.int32, lfn_ref.shape[1:], 1)
    carry = jnp.sum(jnp.where(lane == req, lfn_ref[0], 0.0), axis=1, keepdims=True) * LOG2E
    lf = jnp.concatenate([ref[0] for ref in clf], axis=0) * LOG2E
    incl = _dot(lf, later, HIGHEST)
    decay = []
    for p_idx in range(n_pages):
        rows = slice(p_idx * n_maps, (p_idx + 1) * n_maps)
        decay.append(incl[rows] - lf[rows] + carry)
        carry = carry + incl[rows, 0:1]
    s_f = _dot(qf, keys_side_by_side(cfk)) + jnp.concatenate(decay, axis=1)

    if rider is not None:
        rider.second(rider_carry, *rider_in, *rider_out)

    p, p_self, l = softmax(s_d, self_score(qd, kd_ref) + (rbt_ref[:, 0:1] - far_bias) * LOG2E)
    p = p.astype(BF16)
    v_new = vd_ref[pl.ds(sub * n_dh, n_dh), :]
    acc = jnp.zeros((n_maps, e), F32)
    for h in range(n_dh):
        v_h = jnp.concatenate([ref[0, pl.ds(h, page_size, stride=n_dh), :].astype(BF16) for ref in cdv],
                              axis=0)
        acc = jnp.where(row_e // 2 == h, _dot(p, v_h) + p_self * v_new[h:h + 1, :], acc)
    lam = _lambda_value(lam_ref, lambda_init)
    sign = jnp.where(row_e[:, 0:1] % 2 == 0, 1.0, -lam)
    pick = (lax.broadcasted_iota(jnp.int32, (n_dh, n_maps), 1) // 2
            == lax.broadcasted_iota(jnp.int32, (n_dh, n_maps), 0)).astype(F32)
    scaled = acc / l * sign

    p, p_self, l = softmax(s_f, self_score(qf, kf_ref))
    acc = jnp.zeros((n_maps, head_dim, page_size), F32)
    for p_idx in range(n_pages):
        p_page = p[:, p_idx * page_size:(p_idx + 1) * page_size]
        acc = acc + p_page[:, None, :] * cfv[p_idx][0].reshape(n_maps, head_dim, page_size)

    def spread(col):
        return jnp.sum(jnp.where(own, col, 0.0), axis=0, keepdims=True)
    past = jnp.sum(acc.reshape(width, page_size).T, axis=0, keepdims=True)
    of_ref[pl.ds(sub, 1), :] = (past + spread(p_self) * new_row(vf_ref)) / spread(l)

    o_d = _dot(pick, scaled, HIGHEST)
    o_d = _rms(o_d, subg_ref[...]) * (1.0 - lambda_init)
    od_ref[pl.ds(sub, 1), :] = jnp.concatenate([o_d[h:h + 1] for h in range(n_dh)], axis=1)


def _decode_requests(page_table, qd, kd, vd, qf, kf, vf, logf_new, rbt, lam_p, subg,
                     cdk, cdv, cfk, cfv, clf, *, head_dim, lambda_init, rider=None):
    r, n_pages = page_table.shape
    _, width, page_size = cdk.shape
    n_f = clf.shape[1]
    n_maps = width // head_dim
    n_dh = n_maps // 2
    e = 2 * head_dim
    per_step = REQUESTS_PER_STEP
    assert n_maps == n_f and r % SUBLANES == 0 and SUBLANES % per_step == 0
    assert page_size >= MAX_DISTANCE
    steps_per_block = SUBLANES // per_step
    row = pl.BlockSpec((SUBLANES, width), lambda i, pt: (i // steps_per_block, 0))
    head_rows = pl.BlockSpec((SUBLANES * n_dh, e), lambda i, pt: (i // steps_per_block, 0))
    caches = [cdk, cdv, cfk, cfv, clf]

    in_specs = [row, row, head_rows, row, row, row, _resident(logf_new.shape),
                _resident(rbt.shape), _resident(lam_p.shape), _resident(subg.shape)]
    operands = [qd, kd, vd, qf, kf, vf, logf_new, rbt, lam_p, subg]
    out_shape = [jax.ShapeDtypeStruct((r, width), F32)] * 2
    out_specs = [row, row]
    if rider is not None:
        in_specs += list(rider.in_specs)
        operands += list(rider.operands)
        out_shape.append(rider.out_shape)
        out_specs.append(rider.out_spec)
    in_specs += [pl.BlockSpec(memory_space=pl.ANY)] * len(caches)
    operands += caches
    return pl.pallas_call(
        functools.partial(_decode_step_kernel, n_pages=n_pages, page_size=page_size,
                          head_dim=head_dim, lambda_init=lambda_init,
                          rider=rider, n_rider_in=len(rider.operands) if rider else 0),
        out_shape=out_shape,
        grid_spec=pltpu.PrefetchScalarGridSpec(
            num_scalar_prefetch=1,
            grid=(r // per_step,),
            in_specs=in_specs,
            out_specs=out_specs,
            scratch_shapes=[pltpu.VMEM((2, n_pages) + c.shape[1:], c.dtype) for c in caches]
                           + [pltpu.SemaphoreType.DMA((2,)), pltpu.VMEM((n_maps, page_size), F32)]),
        compiler_params=_params(("arbitrary",), DECODE_VMEM_LIMIT),
        name="decode_attention",
    )(page_table, *operands)


def _merge_kernel(x_ref, od_ref, of_ref, g1_ref, wt_ref, gb_ref, wa_ref, wb_ref, wo_ref, o_ref, *, gate_row):
    x = x_ref[...]
    d = x.shape[-1]
    h = _rms(x, g1_ref[...])
    ya = _dot(od_ref[...].astype(F32), wa_ref[...])
    yb = _dot(of_ref[...].astype(F32), wb_ref[...])
    gate_a = jax.nn.sigmoid(_dot_nt(h, wt_ref[gate_row:gate_row + d, :]) + gb_ref[:, :d])
    gate_b = jax.nn.sigmoid(_dot_nt(h, wt_ref[gate_row + d:gate_row + 2 * d, :]) + gb_ref[:, d:])
    merged = gate_a * ya + gate_b * yb
    o_ref[...] = x + _dot(merged, wo_ref[...])


def _merge(x, od, of, g1, wt, gb, wa, wb, wo):
    m, d = x.shape
    tm = min(ROW_TILE, m)
    gate_row = wt.shape[0] - 2 * d
    assert m % tm == 0 and gate_row % SUBLANES == 0
    row = lambda c: pl.BlockSpec((tm, c), lambda i: (i, 0))
    return pl.pallas_call(
        functools.partial(_merge_kernel, gate_row=gate_row),
        out_shape=jax.ShapeDtypeStruct((m, d), F32),
        grid=(m // tm,),
        in_specs=[row(d), row(od.shape[1]), row(of.shape[1])]
                 + [_resident(a.shape) for a in (g1, wt, gb, wa, wb, wo)],
        out_specs=row(d),
        compiler_params=_params(("arbitrary",)),
        name="merge",
    )(x, od, of, g1, wt, gb, wa, wb, wo)


def _ffn_up(x_ref, g2_ref, wgu_ref, wdn_ref):
    d_ff = wdn_ref.shape[0]
    h = _rms(x_ref[...], g2_ref[...]).astype(wgu_ref.dtype)
    gate = _dot(h, wgu_ref[:, :d_ff])
    up = _dot(h, wgu_ref[:, d_ff:])
    return (gate * jax.nn.sigmoid(gate) * up).astype(wdn_ref.dtype)


def _ffn_down(act, x_ref, g2_ref, wgu_ref, wdn_ref, o_ref):
    o_ref[...] = x_ref[...] + _dot(act, wdn_ref[...])


def _ffn_kernel(x_ref, g2_ref, wgu_ref, wdn_ref, o_ref):
    _ffn_down(_ffn_up(x_ref, g2_ref, wgu_ref, wdn_ref), x_ref, g2_ref, wgu_ref, wdn_ref, o_ref)


def _ffn(x, g2, wgu, wdn):
    m, d = x.shape
    tm = min(FFN_ROW_TILE, m)
    assert m % tm == 0
    row = pl.BlockSpec((tm, d), lambda i: (i, 0))
    return pl.pallas_call(
        _ffn_kernel,
        out_shape=jax.ShapeDtypeStruct((m, d), F32),
        grid=(m // tm,),
        in_specs=[row, _resident(g2.shape), _resident(wgu.shape), _resident(wdn.shape)],
        out_specs=row,
        compiler_params=_params(("arbitrary",)),
        name="ffn",
    )(x, g2, wgu, wdn)


def kernel(x_prompt, x_sample, cache_diff_k, cache_diff_v, cache_fox_k, cache_fox_v, cache_fox_logf,
           page_table, rel_bias, norm1_g, w_in, diff_q_g, diff_k_g, fox_q_g, fox_k_g, diff_lambda,
           fox_f_b, gate_b, diff_subln_g, w_branch_a, w_branch_b, w_out, norm2_g, w_gate_up, w_down):
    depth = w_in.shape[0]
    batch, seq, d = x_prompt.shape
    dec_batch, dec_seq, _ = x_sample.shape
    assert dec_seq == 1
    _, n_phys, page_size, n_dh, _, head_dim = cache_diff_k.shape
    n_f = cache_fox_k.shape[3]
    e = 2 * head_dim
    width = n_dh * e
    assert n_f * head_dim == width

    rb_flat = rel_bias.reshape(-1)
    rbt = jnp.repeat(rel_bias.T, 2, axis=0)

    yp, ys = x_prompt, x_sample.reshape(1, dec_batch, d)
    rows_p, rows_s = [], []
    for l in range(depth):
        lambda_init = 0.8 - 0.6 * math.exp(-0.3 * l)
        w_t = w_in[l].T
        fb = fox_f_b[l].reshape(n_f, 1)
        gains = jnp.stack([jnp.tile(g[l], width // head_dim)
                           for g in (diff_q_g, diff_k_g, fox_q_g, fox_k_g)]).reshape(4, width, 1)
        g1 = norm1_g[l].reshape(1, d)
        g2 = norm2_g[l].reshape(1, d)
        gb = gate_b[l].reshape(1, 2 * d)
        subg = diff_subln_g[l].reshape(1, e)
        wa, wb, wo, wgu, wdn = w_branch_a[l], w_branch_b[l], w_out[l], w_gate_up[l], w_down[l]
        lam_p = diff_lambda[l]

        def tail(x, od, of):
            return _ffn(_merge(x, od, of, g1, w_t, gb, wa, wb, wo), g2, wgu, wdn)

        def cache_rows(kd_t, vd, kf_t, vf_t, logf_t):
            b, _, s = kd_t.shape
            return (jnp.transpose(kd_t.reshape(b, n_dh, 2, head_dim, s), (0, 4, 1, 2, 3)),
                    vd.reshape(b, s, n_dh, e),
                    jnp.transpose(kf_t.reshape(b, n_f, head_dim, s), (0, 3, 1, 2)),
                    jnp.transpose(vf_t.reshape(b, n_f, head_dim, s), (0, 3, 1, 2)),
                    jnp.transpose(logf_t, (0, 2, 1)))

        qd_t, kd_t, vd, vd_t, qf_t, kf_t, vf_t, logf_t = _proj(
            yp, g1, w_t, gains, fb, width=width, head_dim=head_dim, with_rows=False)
        od = _diff_attention(rb_flat, qd_t, kd_t, vd_t, lam_p, subg,
                             n_heads=n_dh, head_dim=head_dim, lambda_init=lambda_init)
        of = _fox_attention(qf_t, kf_t, vf_t, logf_t, head_dim=head_dim)
        x1p = _merge(yp.reshape(batch * seq, d), od.reshape(batch * seq, width),
                     of.reshape(batch * seq, width), g1, w_t, gb, wa, wb, wo)
        assert (batch * seq) % dec_batch == 0
        ffn_rows = pl.BlockSpec((REQUESTS_PER_STEP * (batch * seq // dec_batch), d), lambda i, pt: (i, 0))
        wgu_b, wdn_b = wgu.astype(BF16), wdn.astype(BF16)
        rider = _Rider(_ffn_up, _ffn_down, (x1p, g2, wgu_b, wdn_b),
                       (ffn_rows, _resident(g2.shape), _resident(wgu_b.shape), _resident(wdn_b.shape)),
                       jax.ShapeDtypeStruct(x1p.shape, F32), ffn_rows)
        rows_p.append(cache_rows(kd_t, vd, kf_t, vf_t, logf_t))

        (qd_t, kd_t, vd, _, qf_t, kf_t, vf_t, logf_t, qd_r, kd_r, qf_r, kf_r, vf_r) = _proj(
            ys, g1, w_t, gains, fb, width=width, head_dim=head_dim, with_rows=True)
        as_row = lambda a: a.reshape(dec_batch, width)
        cdk = jnp.transpose(cache_diff_k[l], (0, 2, 3, 4, 1)).reshape(n_phys, width, page_size)
        cdv = cache_diff_v[l].reshape(n_phys, page_size * n_dh, e)
        cfk = jnp.transpose(cache_fox_k[l], (0, 2, 3, 1)).reshape(n_phys, width, page_size)
        cfv = jnp.transpose(cache_fox_v[l], (0, 2, 3, 1)).reshape(n_phys, width, page_size)
        clf = jnp.transpose(cache_fox_logf[l], (0, 2, 1))
        od, of, yp = _decode_requests(
            page_table, as_row(qd_r), as_row(kd_r), vd.reshape(dec_batch * n_dh, e),
            as_row(qf_r), as_row(kf_r), as_row(vf_r), logf_t, rbt, lam_p, subg,
            cdk, cdv, cfk, cfv, clf, head_dim=head_dim, lambda_init=lambda_init, rider=rider)
        yp = yp.reshape(batch, seq, d)
        ys = tail(ys.reshape(dec_batch, d), od, of).reshape(1, dec_batch, d)
        rows_s.append(tuple(jnp.swapaxes(a, 0, 1) for a in cache_rows(kd_t, vd, kf_t, vf_t, logf_t)))

    stack = lambda rows, i: jnp.stack([r[i] for r in rows], axis=0)
    return (yp, ys.reshape(dec_batch, 1, d),
            *(stack(rows_p, i) for i in range(5)), *(stack(rows_s, i) for i in range(5)))
```

```python
import functools
import math
from typing import Any, NamedTuple

import jax
import jax.numpy as jnp
from jax import lax
from jax.experimental import pallas as pl
from jax.experimental.pallas import tpu as pltpu

F32 = jnp.float32
BF16 = jnp.bfloat16

N_BUCKETS = 32
MAX_DISTANCE = 128
EPS = 1e-6
NEG_INF = -1e30
LOG2E = math.log2(math.e)

LANES = 128
SUBLANES = 8
BF16_ROWS = 16
VMEM_LIMIT = 56 * 1024 * 1024
DECODE_VMEM_LIMIT = 60 * 1024 * 1024

ROW_TILE = 512
FFN_ROW_TILE = 256
ATTN_TQ = 512
ATTN_TK = 256
FOX_PAIRS_PER_STEP = 2
DIFF_HEADS_PER_STEP = 2
DECAY_PARTS = 3

NT_DIMS = (((1,), (1,)), ((), ()))
HIGHEST = lax.Precision.HIGHEST


def _resident(shape):
    zeros = (0,) * len(shape)
    return pl.BlockSpec(shape, lambda *_: zeros, pipeline_mode=pl.Buffered(1))


def _params(semantics, vmem=VMEM_LIMIT):
    return pltpu.CompilerParams(dimension_semantics=semantics, vmem_limit_bytes=vmem)


def _rms(x, g):
    return x * lax.rsqrt(jnp.mean(x * x, axis=-1, keepdims=True) + EPS) * g


def _dot(a, b, precision=None):
    return jnp.dot(a, b, preferred_element_type=F32, precision=precision)


def _dot_nt(a, b):
    return lax.dot_general(a, b, NT_DIMS, preferred_element_type=F32)


def _log_sigmoid(x):
    return jnp.minimum(x, 0.0) - jnp.log1p(jnp.exp(-jnp.abs(x)))


def _rel_bucket(n):
    max_exact = N_BUCKETS // 2
    nf = jnp.maximum(n, 1).astype(F32)
    large = max_exact + (jnp.log(nf / max_exact) / math.log(MAX_DISTANCE / max_exact)
                         * (N_BUCKETS - max_exact)).astype(jnp.int32)
    return jnp.where(n < max_exact, n, jnp.minimum(large, N_BUCKETS - 1))


def _lambda_value(lam_ref, lambda_init):
    lp = lam_ref[...]
    a = jnp.sum(lp[0:1] * lp[1:2], axis=-1, keepdims=True)
    b = jnp.sum(lp[2:3] * lp[3:4], axis=-1, keepdims=True)
    return jnp.exp(a) - jnp.exp(b) + lambda_init


def _proj_kernel(x_ref, g1_ref, wt_ref, gains_ref, fb_ref,
                 qd_ref, kd_ref, vd_ref, vdt_ref, qf_ref, kf_ref, vf_ref, logf_ref, *row_refs,
                 head_dim, q_scale):
    h = _rms(x_ref[0], g1_ref[...])
    width = kd_ref.shape[1]
    n_heads = width // head_dim
    n_f = logf_ref.shape[1]
    section = lambda i: _dot_nt(wt_ref[i * width:(i + 1) * width, :], h)

    def head_norm_t(z, i):
        z3 = z.reshape(n_heads, head_dim, z.shape[-1])
        ss = jnp.sum(z3 * z3, axis=1, keepdims=True)
        gain = gains_ref[i].reshape(n_heads, head_dim, 1)
        return (z3 * lax.rsqrt(ss * (1.0 / head_dim) + EPS) * gain).reshape(z.shape)

    qd = head_norm_t(section(0), 0) * q_scale
    kd = head_norm_t(section(1), 1)
    qf = head_norm_t(section(3), 2) * q_scale
    kf = head_norm_t(section(4), 3)
    vd_t = section(2)
    vd_rows = vd_t.T
    vf = section(5)
    qd_ref[0] = qd.astype(qd_ref.dtype)
    kd_ref[0] = kd
    e = vd_ref.shape[-1]
    for hd in range(width // e):
        vd_ref[0, pl.ds(hd, vd_rows.shape[0], stride=width // e), :] = vd_rows[:, hd * e:(hd + 1) * e]
    vdt_ref[0] = vd_t.astype(vdt_ref.dtype)
    qf_ref[0] = qf.astype(qf_ref.dtype)
    kf_ref[0] = kf
    vf_ref[0] = vf
    logf_ref[0] = _log_sigmoid(_dot_nt(wt_ref[6 * width:6 * width + n_f, :], h) + fb_ref[...])
    if row_refs:
        for ref, val in zip(row_refs, (qd, kd, qf, kf, vf)):
            ref[0] = val.T


def _proj(x, g1, wt, gains, fb, *, width, head_dim, with_rows):
    b, s, d = x.shape
    n_f = fb.shape[0]
    e = 2 * head_dim
    tm = min(ROW_TILE, s)
    assert s % tm == 0 and tm % LANES == 0 and width % e == 0
    t_blk = lambda r: pl.BlockSpec((1, r, tm), lambda bi, si: (bi, 0, si))
    r_blk = lambda c: pl.BlockSpec((1, tm, c), lambda bi, si: (bi, si, 0))
    t_shape = lambda dt: jax.ShapeDtypeStruct((b, width, s), dt)
    r_shape = jax.ShapeDtypeStruct((b, s, width), F32)
    heads_per_token = width // e
    out_shape = [t_shape(BF16), t_shape(F32), jax.ShapeDtypeStruct((b, s * heads_per_token, e), F32),
                 t_shape(BF16), t_shape(BF16), t_shape(F32), t_shape(F32),
                 jax.ShapeDtypeStruct((b, n_f, s), F32)]
    out_specs = [t_blk(width), t_blk(width),
                 pl.BlockSpec((1, tm * heads_per_token, e), lambda bi, si: (bi, si, 0)),
                 t_blk(width), t_blk(width), t_blk(width), t_blk(width), t_blk(n_f)]
    if with_rows:
        out_shape += [r_shape] * 5
        out_specs += [r_blk(width)] * 5
    return pl.pallas_call(
        functools.partial(_proj_kernel, head_dim=head_dim, q_scale=head_dim ** -0.5 * LOG2E),
        out_shape=out_shape,
        grid=(b, s // tm),
        in_specs=[r_blk(d)] + [_resident(a.shape) for a in (g1, wt, gains, fb)],
        out_specs=out_specs,
        compiler_params=_params(("arbitrary", "arbitrary")),
        name="proj",
    )(x, g1, wt, gains, fb)


def _flash_step(scores, values, state):
    new_state = []
    for s_t, v_t, (m_prev, acc_prev) in zip(scores, values, state):
        m_new = jnp.maximum(m_prev, jnp.max(s_t, axis=0, keepdims=True))
        alpha = jnp.exp2(m_prev - m_new)
        p_t = jnp.exp2(s_t - m_new).astype(BF16)
        new_state.append((m_new, alpha * acc_prev + _dot(v_t, p_t)))
    return tuple(new_state)


def _flash_init(n_slots, rows, tq):
    return tuple((jnp.full((1, tq), NEG_INF, F32), jnp.zeros((rows, tq), F32)) for _ in range(n_slots))


def _ones_rows(seq):
    return jnp.where(lax.broadcasted_iota(jnp.int32, (BF16_ROWS, seq), 0) == 0, 1.0, 0.0).astype(BF16)


def _key_minus_query(tk, tq):
    return lax.broadcasted_iota(jnp.int32, (tk, tq), 0) - lax.broadcasted_iota(jnp.int32, (tk, tq), 1)


class _ScoreRing:
    def __init__(self, s_ref, qk):
        self.s_ref, self.qk = s_ref, qk

    def fill(self, j, slot):
        for idx, s in enumerate(self.qk(j)):
            self.s_ref[slot, idx] = s

    def scores(self, slot):
        return [self.s_ref[slot, idx] for idx in range(self.s_ref.shape[1])]

    def far_sweep(self, softmax, state, n_pairs):
        self.fill(0, 0)

        def body(jj, st):
            j = 2 * jj
            self.fill(j + 1, 1)
            st = softmax(self.scores(0), j, st)
            self.fill(j + 2, 0)
            return softmax(self.scores(1), j + 1, st)
        return lax.fori_loop(0, n_pairs, body, state)

    def tail(self, softmax, state, first, kinds, more_follows):
        assert not more_follows or len(kinds) % 2 == 0
        for n, kind in enumerate(kinds):
            slot = n % 2
            if n + 1 < len(kinds) or more_follows:
                self.fill(first + n + 1, 1 - slot)
            state = softmax(self.scores(slot), first + n, state, kind)
        return state


def _diff_attn_kernel(rb_ref, q_ref, k_ref, v_ref, lam_ref, subg_ref, o_ref,
                      kb_ref, vb_ref, bias_ref, s_ref, *, n_heads, head_dim, lambda_init):
    tq, tk = ATTN_TQ, ATTN_TK
    ratio = tq // tk
    n_hd = vb_ref.shape[0]
    head0 = n_hd * pl.program_id(0)
    seq = q_ref.shape[2]
    e = 2 * head_dim
    kmq = _key_minus_query(tk, tq)
    near = [tk] + [-d * tk for d in range(ratio)]

    @pl.when(pl.program_id(1) == 0)
    def _():
        for hd in range(n_hd):
            far = rb_ref[(N_BUCKETS - 1) * n_heads + head0 + hd]
            for idx, delta in enumerate(near):
                bucket = _rel_bucket(jnp.maximum(delta - kmq, 0))
                val = jnp.zeros((tk, tq), F32)
                for b in range(N_BUCKETS):
                    val = jnp.where(bucket == b, rb_ref[b * n_heads + head0 + hd], val)
                bias_ref[hd, idx] = (val - far) * LOG2E

    first = lax.broadcasted_iota(jnp.int32, (1, e), 1) < head_dim
    for hd in range(n_hd):
        k_rows = k_ref[0, hd * e:(hd + 1) * e, :].T
        kb_ref[2 * hd] = jnp.where(first, k_rows, 0.0).astype(BF16)
        kb_ref[2 * hd + 1] = jnp.where(first, 0.0, k_rows).astype(BF16)
        vb_ref[hd, 0:e, :] = v_ref[0, hd * e:(hd + 1) * e, :]
        vb_ref[hd, e:, :] = _ones_rows(seq)
    lam = _lambda_value(lam_ref, lambda_init)

    def q_block(i, carry):
        qs = pl.multiple_of(i * tq, tq)
        q_t = [q_ref[0, hd * e:(hd + 1) * e, pl.ds(qs, tq)] for hd in range(n_hd)]

        def qk(j):
            ks = pl.multiple_of(j * tk, tk)
            return [_dot(kb_ref[mp, pl.ds(ks, tk), :], q_t[mp // 2]) for mp in range(2 * n_hd)]

        def softmax(scores, j, state, near_idx=None):
            if near_idx is not None:
                scores = [s + bias_ref[mp // 2, near_idx] for mp, s in enumerate(scores)]
                if near[near_idx] <= 0:
                    visible = kmq <= near[near_idx]
                    scores = [jnp.where(visible, s, NEG_INF) for s in scores]
            ks = pl.multiple_of(j * tk, tk)
            values = [vb_ref[mp // 2, :, pl.ds(ks, tk)] for mp in range(2 * n_hd)]
            return _flash_step(scores, values, state)

        ring = _ScoreRing(s_ref, qk)
        first_near = ratio * (i - 1)
        state = ring.far_sweep(softmax, _flash_init(2 * n_hd, vb_ref.shape[1], tq),
                               jnp.maximum(first_near, 0) // 2)
        before = [None] * (ratio - 1) + [0]
        state = lax.cond(i >= 1, lambda st: ring.tail(softmax, st, first_near, before, True),
                         lambda st: st, state)
        state = ring.tail(softmax, state, ratio * i, [1 + d for d in range(ratio)], False)
        outs = []
        for hd in range(n_hd):
            (_, acc0), (_, acc1) = state[2 * hd:2 * hd + 2]
            o_t = acc0[0:e] / acc0[e:e + 1] - lam * (acc1[0:e] / acc1[e:e + 1])
            outs.append(_rms(o_t.T, subg_ref[...]) * (1.0 - lambda_init))
        o_ref[0, pl.ds(qs, tq), :] = jnp.concatenate(outs, axis=1).astype(o_ref.dtype)
        return carry

    lax.fori_loop(0, seq // tq, q_block, 0)


def _diff_attention(rb_flat, q_t, k_t, v, lam_p, subg, *, n_heads, head_dim, lambda_init):
    b, w, s = q_t.shape
    e = 2 * head_dim
    tq, tk = ATTN_TQ, ATTN_TK
    assert s % tq == 0 and tq % tk == 0 and tk % LANES == 0 and tk >= MAX_DISTANCE and e == LANES
    n_hd = DIFF_HEADS_PER_STEP
    assert n_heads % n_hd == 0
    t_blk = pl.BlockSpec((1, n_hd * e, s), lambda g, bi, *_: (bi, g, 0))
    r_blk = pl.BlockSpec((1, s, n_hd * e), lambda g, bi, *_: (bi, 0, g))
    return pl.pallas_call(
        functools.partial(_diff_attn_kernel, n_heads=n_heads, head_dim=head_dim, lambda_init=lambda_init),
        out_shape=jax.ShapeDtypeStruct((b, s, w), BF16),
        grid_spec=pltpu.PrefetchScalarGridSpec(
            num_scalar_prefetch=1,
            grid=(n_heads // n_hd, b),
            in_specs=[t_blk, t_blk, t_blk, _resident(lam_p.shape), _resident(subg.shape)],
            out_specs=r_blk,
            scratch_shapes=[pltpu.VMEM((2 * n_hd, s, e), BF16), pltpu.VMEM((n_hd, e + BF16_ROWS, s), BF16),
                            pltpu.VMEM((n_hd, tq // tk + 1, tk, tq), F32),
                            pltpu.VMEM((2, 2 * n_hd, tk, tq), F32)]),
        compiler_params=_params(("arbitrary", "arbitrary")),
        name="diff_attention",
    )(rb_flat, q_t, k_t, v, lam_p, subg)


def _fox_attn_kernel(q_ref, k_ref, v_ref, logf_ref, o_ref, kb_ref, vb_ref, c_ref, s_ref, *, head_dim):
    tq, tk = ATTN_TQ, ATTN_TK
    ratio = tq // tk
    group = pl.program_id(1)
    seq = q_ref.shape[2]
    e = 2 * head_dim
    n_h = kb_ref.shape[0]
    kmq = _key_minus_query(tk, tq)

    @pl.when(group == 0)
    def _():
        upper = (lax.broadcasted_iota(jnp.int32, (tk, tk), 0)
                 <= lax.broadcasted_iota(jnp.int32, (tk, tk), 1)).astype(F32)
        carry = jnp.zeros((logf_ref.shape[1], 1), F32)
        for blk in range(seq // tk):
            c = _dot(logf_ref[0, :, blk * tk:(blk + 1) * tk], upper, HIGHEST) + carry
            c_ref[:, blk * tk:(blk + 1) * tk] = c
            carry = c[:, tk - 1:tk]

    lane = lax.broadcasted_iota(jnp.int32, (1, e), 1)
    for hh in range(n_h):
        pair, half = divmod(hh, 2)
        k_rows = k_ref[0, pair * e:(pair + 1) * e, :].T
        spare = (1 - half) * head_dim
        rest = jnp.broadcast_to(c_ref[pl.ds(n_h * group + hh, 1), :] * LOG2E, (LANES, seq)).T
        k_aug = jnp.where(lane // head_dim == half, k_rows, 0.0)
        for part in range(DECAY_PARTS):
            piece = rest.astype(BF16).astype(F32)
            k_aug = jnp.where(lane == spare + part, piece, k_aug)
            rest = rest - piece
        kb_ref[hh] = k_aug.astype(BF16)
        vb_ref[hh, 0:head_dim, :] = v_ref[0, hh * head_dim:(hh + 1) * head_dim, :].astype(BF16)
        vb_ref[hh, head_dim:, :] = _ones_rows(seq)

    def q_block(i, carry):
        qs = pl.multiple_of(i * tq, tq)
        row = lax.broadcasted_iota(jnp.int32, (e, 1), 0)
        q_aug = []
        for hh in range(n_h):
            pair, half = divmod(hh, 2)
            q_t = q_ref[0, pair * e:(pair + 1) * e, pl.ds(qs, tq)]
            spare = (1 - half) * head_dim
            minus_one = jnp.logical_and(row >= spare, row < spare + DECAY_PARTS)
            q_aug.append(jnp.where(row // head_dim == half, q_t,
                                   jnp.where(minus_one, -1.0, 0.0).astype(BF16)))

        def qk(j):
            ks = pl.multiple_of(j * tk, tk)
            return [_dot(kb_ref[hh, pl.ds(ks, tk), :], q_aug[hh]) for hh in range(n_h)]

        def softmax(scores, j, state, diagonal=None):
            ks = pl.multiple_of(j * tk, tk)
            if diagonal is not None:
                visible = kmq <= -diagonal * tk
                scores = [jnp.where(visible, s, NEG_INF) for s in scores]
            values = [vb_ref[hh, :, pl.ds(ks, tk)] for hh in range(n_h)]
            return _flash_step(scores, values, state)

        ring = _ScoreRing(s_ref, qk)
        state = ring.far_sweep(softmax, _flash_init(n_h, vb_ref.shape[1], tq), ratio * i // 2)
        state = ring.tail(softmax, state, ratio * i, list(range(ratio)), False)
        o_t = jnp.concatenate([acc[0:head_dim] / acc[head_dim:head_dim + 1] for _, acc in state], axis=0)
        o_ref[0, pl.ds(qs, tq), :] = o_t.T.astype(o_ref.dtype)
        return carry

    lax.fori_loop(0, seq // tq, q_block, 0)


def _fox_attention(q_t, k_t, v_t, logf_t, *, head_dim):
    b, w, s = q_t.shape
    e = 2 * head_dim
    tq, tk = ATTN_TQ, ATTN_TK
    n_f = logf_t.shape[1]
    assert s % tq == 0 and tq % tk == 0 and tk % LANES == 0 and e == LANES and w // e * 2 == n_f
    assert DECAY_PARTS <= head_dim
    cols = FOX_PAIRS_PER_STEP * e
    n_h = 2 * FOX_PAIRS_PER_STEP
    assert w % cols == 0
    t_blk = pl.BlockSpec((1, cols, s), lambda bi, g: (bi, g, 0))
    return pl.pallas_call(
        functools.partial(_fox_attn_kernel, head_dim=head_dim),
        out_shape=jax.ShapeDtypeStruct((b, s, w), BF16),
        grid=(b, w // cols),
        in_specs=[t_blk, t_blk, t_blk, pl.BlockSpec((1, n_f, s), lambda bi, g: (bi, 0, 0))],
        out_specs=pl.BlockSpec((1, s, cols), lambda bi, g: (bi, 0, g)),
        scratch_shapes=[pltpu.VMEM((n_h, s, e), BF16), pltpu.VMEM((n_h, head_dim + BF16_ROWS, s), BF16),
                        pltpu.VMEM((n_f, s), F32), pltpu.VMEM((2, n_h, tk, tq), F32)],
        compiler_params=_params(("arbitrary", "arbitrary")),
        name="fox_attention",
    )(q_t, k_t, v_t, logf_t)


class _Rider(NamedTuple):
    first: Any
    second: Any
    operands: tuple
    in_specs: tuple
    out_shape: Any
    out_spec: Any


REQUESTS_PER_STEP = 2


def _decode_step_kernel(pt_ref, qd_ref, kd_ref, vd_ref, qf_ref, kf_ref, vf_ref, lfn_ref,
                        rbt_ref, lam_ref, subg_ref, *rest,
                        n_pages, page_size, head_dim, lambda_init, rider, n_rider_in):
    rider_in, rest = rest[:n_rider_in], rest[n_rider_in:]
    caches, rest = rest[:5], rest[5:]
    n_out = len(rest) - 7
    (od_ref, of_ref, *rider_out), (*bufs, sems, bias_ref) = rest[:n_out], rest[n_out:]
    step = pl.program_id(0)
    last = pl.num_programs(0) - 1
    n_maps = bias_ref.shape[0]

    def copies(req, slot):
        for p in range(n_pages):
            page = pt_ref[req, n_pages - 1 - p]
            for cache, buf in zip(caches, bufs):
                yield pltpu.make_async_copy(cache.at[page], buf.at[slot, p], sems.at[slot])

    def fetch(req, slot):
        for copy in copies(req, slot):
            copy.start()

    @pl.when(step == 0)
    def _():
        fetch(0, 0)
        lane = lax.broadcasted_iota(jnp.int32, (n_maps, page_size), 1)
        bucket = _rel_bucket(page_size - lane)
        val = jnp.zeros((n_maps, page_size), F32)
        for b in range(N_BUCKETS):
            val = jnp.where(bucket == b, rbt_ref[:, b:b + 1], val)
        bias_ref[...] = (val - rbt_ref[:, N_BUCKETS - 1:N_BUCKETS]) * LOG2E

    rider_rows = rider_in[0].shape[0] // REQUESTS_PER_STEP if rider is not None else 0
    for k in range(REQUESTS_PER_STEP):
        req = REQUESTS_PER_STEP * step + k
        if k + 1 < REQUESTS_PER_STEP:
            fetch(req + 1, k + 1)
        else:
            @pl.when(step < last)
            def _():
                fetch(req + 1, 0)
        for copy in copies(req, k):
            copy.wait()
        pages = [[buf.at[k, pl.ds(p, 1)] for buf in bufs] for p in range(n_pages)]
        tile = lambda ref: ref.at[pl.ds(k * rider_rows, rider_rows)]
        _decode_one_request(
            req, pages, qd_ref, kd_ref, vd_ref, qf_ref, kf_ref, vf_ref, lfn_ref, rbt_ref, lam_ref,
            subg_ref, od_ref, of_ref, bias_ref,
            [tile(rider_in[0])] + list(rider_in[1:]), [tile(ref) for ref in rider_out],
            page_size=page_size, head_dim=head_dim, lambda_init=lambda_init, rider=rider)


def _decode_one_request(req, pages, qd_ref, kd_ref, vd_ref, qf_ref, kf_ref, vf_ref, lfn_ref,
                        rbt_ref, lam_ref, subg_ref, od_ref, of_ref, bias_ref, rider_in, rider_out,
                        *, page_size, head_dim, lambda_init, rider):
    n_pages = len(pages)
    width = qd_ref.shape[-1]
    n_maps = width // head_dim
    n_dh = n_maps // 2
    e = 2 * head_dim
    sub = req % qd_ref.shape[0]
    new_row = lambda ref: ref[pl.ds(sub, 1), :]

    map_of_lane = lax.broadcasted_iota(jnp.int32, (n_maps, width), 1) // head_dim
    row = lax.broadcasted_iota(jnp.int32, (n_maps, width), 0)
    own = map_of_lane == row
    row_e = lax.broadcasted_iota(jnp.int32, (n_maps, e), 0)
    far_bias = rbt_ref[:, N_BUCKETS - 1:N_BUCKETS]

    qd = jnp.where(own, new_row(qd_ref), 0.0).astype(BF16)
    qf = jnp.where(own, new_row(qf_ref), 0.0).astype(BF16)

    def self_score(q, k_ref):
        k = new_row(k_ref).astype(BF16).astype(F32)
        return jnp.sum(q.astype(F32) * k, axis=-1, keepdims=True)

    def softmax(s, s_self):
        m = jnp.maximum(jnp.max(s, axis=-1, keepdims=True), s_self)
        p, p_self = jnp.exp2(s - m), jnp.exp2(s_self - m)
        return p, p_self, jnp.sum(p, axis=-1, keepdims=True) + p_self

    cdk, cdv, cfk, cfv, clf = zip(*pages)

    def keys_side_by_side(refs):
        return jnp.concatenate([ref[0].astype(BF16) for ref in refs], axis=1)

    rider_carry = rider.first(*rider_in) if rider is not None else None
    bias = jnp.concatenate([bias_ref[...], jnp.zeros((n_maps, (n_pages - 1) * page_size), F32)], axis=1)
    s_d = _dot(qd, keys_side_by_side(cdk)) + bias
    later = (lax.broadcasted_iota(jnp.int32, (page_size, page_size), 0)
             >= lax.broadcasted_iota(jnp.int32, (page_size, page_size), 1)).astype(F32)
    lane = lax.broadcasted_iota(jnp.int32, lfn_ref.shape[1:], 1)
    carry = jnp.sum(jnp.where(lane == req, lfn_ref[0], 0.0), axis=1, keepdims=True) * LOG2E
    lf = jnp.concatenate([ref[0] for ref in clf], axis=0) * LOG2E
    incl = _dot(lf, later, HIGHEST)
    decay = []
    for p_idx in range(n_pages):
        rows = slice(p_idx * n_maps, (p_idx + 1) * n_maps)
        decay.append(incl[rows] - lf[rows] + carry)
        carry = carry + incl[rows, 0:1]
    s_f = _dot(qf, keys_side_by_side(cfk)) + jnp.concatenate(decay, axis=1)

    if rider is not None:
        rider.second(rider_carry, *rider_in, *rider_out)

    p, p_self, l = softmax(s_d, self_score(qd, kd_ref) + (rbt_ref[:, 0:1] - far_bias) * LOG2E)
    p = p.astype(BF16)
    v_new = vd_ref[pl.ds(sub * n_dh, n_dh), :]
    acc = jnp.zeros((n_maps, e), F32)
    for h in range(n_dh):
        v_h = jnp.concatenate([ref[0, pl.ds(h, page_size, stride=n_dh), :].astype(BF16) for ref in cdv],
                              axis=0)
        acc = jnp.where(row_e // 2 == h, _dot(p, v_h) + p_self * v_new[h:h + 1, :], acc)
    lam = _lambda_value(lam_ref, lambda_init)
    sign = jnp.where(row_e[:, 0:1] % 2 == 0, 1.0, -lam)
    pick = (lax.broadcasted_iota(jnp.int32, (n_dh, n_maps), 1) // 2
            == lax.broadcasted_iota(jnp.int32, (n_dh, n_maps), 0)).astype(F32)
    scaled = acc / l * sign

    p, p_self, l = softmax(s_f, self_score(qf, kf_ref))
    acc = jnp.zeros((n_maps, head_dim, page_size), F32)
    for p_idx in range(n_pages):
        p_page = p[:, p_idx * page_size:(p_idx + 1) * page_size]
        acc = acc + p_page[:, None, :] * cfv[p_idx][0].reshape(n_maps, head_dim, page_size)

    def spread(col):
        return jnp.sum(jnp.where(own, col, 0.0), axis=0, keepdims=True)
    past = jnp.sum(acc.reshape(width, page_size).T, axis=0, keepdims=True)
    of_ref[pl.ds(sub, 1), :] = (past + spread(p_self) * new_row(vf_ref)) / spread(l)

    o_d = _dot(pick, scaled, HIGHEST)
    o_d = _rms(o_d, subg_ref[...]) * (1.0 - lambda_init)
    od_ref[pl.ds(sub, 1), :] = jnp.concatenate([o_d[h:h + 1] for h in range(n_dh)], axis=1)


def _decode_requests(page_table, qd, kd, vd, qf, kf, vf, logf_new, rbt, lam_p, subg,
                     cdk, cdv, cfk, cfv, clf, *, head_dim, lambda_init, rider=None):
    r, n_pages = page_table.shape
    _, width, page_size = cdk.shape
    n_f = clf.shape[1]
    n_maps = width // head_dim
    n_dh = n_maps // 2
    e = 2 * head_dim
    per_step = REQUESTS_PER_STEP
    assert n_maps == n_f and r % SUBLANES == 0 and SUBLANES % per_step == 0
    assert page_size >= MAX_DISTANCE
    steps_per_block = SUBLANES // per_step
    row = pl.BlockSpec((SUBLANES, width), lambda i, pt: (i // steps_per_block, 0))
    head_rows = pl.BlockSpec((SUBLANES * n_dh, e), lambda i, pt: (i // steps_per_block, 0))
    caches = [cdk, cdv, cfk, cfv, clf]

    in_specs = [row, row, head_rows, row, row, row, _resident(logf_new.shape),
                _resident(rbt.shape), _resident(lam_p.shape), _resident(subg.shape)]
    operands = [qd, kd, vd, qf, kf, vf, logf_new, rbt, lam_p, subg]
    out_shape = [jax.ShapeDtypeStruct((r, width), F32)] * 2
    out_specs = [row, row]
    if rider is not None:
        in_specs += list(rider.in_specs)
        operands += list(rider.operands)
        out_shape.append(rider.out_shape)
        out_specs.append(rider.out_spec)
    in_specs += [pl.BlockSpec(memory_space=pl.ANY)] * len(caches)
    operands += caches
    return pl.pallas_call(
        functools.partial(_decode_step_kernel, n_pages=n_pages, page_size=page_size,
                          head_dim=head_dim, lambda_init=lambda_init,
                          rider=rider, n_rider_in=len(rider.operands) if rider else 0),
        out_shape=out_shape,
        grid_spec=pltpu.PrefetchScalarGridSpec(
            num_scalar_prefetch=1,
            grid=(r // per_step,),
            in_specs=in_specs,
            out_specs=out_specs,
            scratch_shapes=[pltpu.VMEM((2, n_pages) + c.shape[1:], c.dtype) for c in caches]
                           + [pltpu.SemaphoreType.DMA((2,)), pltpu.VMEM((n_maps, page_size), F32)]),
        compiler_params=_params(("arbitrary",), DECODE_VMEM_LIMIT),
        name="decode_attention",
    )(page_table, *operands)


def _merge_kernel(x_ref, od_ref, of_ref, g1_ref, wt_ref, gb_ref, wa_ref, wb_ref, wo_ref, o_ref, *, gate_row):
    x = x_ref[...]
    d = x.shape[-1]
    h = _rms(x, g1_ref[...])
    ya = _dot(od_ref[...].astype(F32), wa_ref[...])
    yb = _dot(of_ref[...].astype(F32), wb_ref[...])
    gate_a = jax.nn.sigmoid(_dot_nt(h, wt_ref[gate_row:gate_row + d, :]) + gb_ref[:, :d])
    gate_b = jax.nn.sigmoid(_dot_nt(h, wt_ref[gate_row + d:gate_row + 2 * d, :]) + gb_ref[:, d:])
    merged = gate_a * ya + gate_b * yb
    o_ref[...] = x + _dot(merged, wo_ref[...])


def _merge(x, od, of, g1, wt, gb, wa, wb, wo):
    m, d = x.shape
    tm = min(ROW_TILE, m)
    gate_row = wt.shape[0] - 2 * d
    assert m % tm == 0 and gate_row % SUBLANES == 0
    row = lambda c: pl.BlockSpec((tm, c), lambda i: (i, 0))
    return pl.pallas_call(
        functools.partial(_merge_kernel, gate_row=gate_row),
        out_shape=jax.ShapeDtypeStruct((m, d), F32),
        grid=(m // tm,),
        in_specs=[row(d), row(od.shape[1]), row(of.shape[1])]
                 + [_resident(a.shape) for a in (g1, wt, gb, wa, wb, wo)],
        out_specs=row(d),
        compiler_params=_params(("arbitrary",)),
        name="merge",
    )(x, od, of, g1, wt, gb, wa, wb, wo)


def _ffn_up(x_ref, g2_ref, wgu_ref, wdn_ref):
    d_ff = wdn_ref.shape[0]
    h = _rms(x_ref[...], g2_ref[...]).astype(wgu_ref.dtype)
    gate = _dot(h, wgu_ref[:, :d_ff])
    up = _dot(h, wgu_ref[:, d_ff:])
    return (gate * jax.nn.sigmoid(gate) * up).astype(wdn_ref.dtype)


def _ffn_down(act, x_ref, g2_ref, wgu_ref, wdn_ref, o_ref):
    o_ref[...] = x_ref[...] + _dot(act, wdn_ref[...])


def _ffn_kernel(x_ref, g2_ref, wgu_ref, wdn_ref, o_ref):
    _ffn_down(_ffn_up(x_ref, g2_ref, wgu_ref, wdn_ref), x_ref, g2_ref, wgu_ref, wdn_ref, o_ref)


def _ffn(x, g2, wgu, wdn):
    m, d = x.shape
    tm = min(FFN_ROW_TILE, m)
    assert m % tm == 0
    row = pl.BlockSpec((tm, d), lambda i: (i, 0))
    return pl.pallas_call(
        _ffn_kernel,
        out_shape=jax.ShapeDtypeStruct((m, d), F32),
        grid=(m // tm,),
        in_specs=[row, _resident(g2.shape), _resident(wgu.shape), _resident(wdn.shape)],
        out_specs=row,
        compiler_params=_params(("arbitrary",)),
        name="ffn",
    )(x, g2, wgu, wdn)


def kernel(x_prompt, x_sample, cache_diff_k, cache_diff_v, cache_fox_k, cache_fox_v, cache_fox_logf,
           page_table, rel_bias, norm1_g, w_in, diff_q_g, diff_k_g, fox_q_g, fox_k_g, diff_lambda,
           fox_f_b, gate_b, diff_subln_g, w_branch_a, w_branch_b, w_out, norm2_g, w_gate_up, w_down):
    depth = w_in.shape[0]
    batch, seq, d = x_prompt.shape
    dec_batch, dec_seq, _ = x_sample.shape
    assert dec_seq == 1
    _, n_phys, page_size, n_dh, _, head_dim = cache_diff_k.shape
    n_f = cache_fox_k.shape[3]
    e = 2 * head_dim
    width = n_dh * e
    assert n_f * head_dim == width

    rb_flat = rel_bias.reshape(-1)
    rbt = jnp.repeat(rel_bias.T, 2, axis=0)

    yp, ys = x_prompt, x_sample.reshape(1, dec_batch, d)
    rows_p, rows_s = [], []
    for l in range(depth):
        lambda_init = 0.8 - 0.6 * math.exp(-0.3 * l)
        w_t = w_in[l].T
        fb = fox_f_b[l].reshape(n_f, 1)
        gains = jnp.stack([jnp.tile(g[l], width // head_dim)
                           for g in (diff_q_g, diff_k_g, fox_q_g, fox_k_g)]).reshape(4, width, 1)
        g1 = norm1_g[l].reshape(1, d)
        g2 = norm2_g[l].reshape(1, d)
        gb = gate_b[l].reshape(1, 2 * d)
        subg = diff_subln_g[l].reshape(1, e)
        wa, wb, wo, wgu, wdn = w_branch_a[l], w_branch_b[l], w_out[l], w_gate_up[l], w_down[l]
        lam_p = diff_lambda[l]

        def tail(x, od, of):
            return _ffn(_merge(x, od, of, g1, w_t, gb, wa, wb, wo), g2, wgu, wdn)

        def cache_rows(kd_t, vd, kf_t, vf_t, logf_t):
            b, _, s = kd_t.shape
            return (jnp.transpose(kd_t.reshape(b, n_dh, 2, head_dim, s), (0, 4, 1, 2, 3)),
                    vd.reshape(b, s, n_dh, e),
                    jnp.transpose(kf_t.reshape(b, n_f, head_dim, s), (0, 3, 1, 2)),
                    jnp.transpose(vf_t.reshape(b, n_f, head_dim, s), (0, 3, 1, 2)),
                    jnp.transpose(logf_t, (0, 2, 1)))

        qd_t, kd_t, vd, vd_t, qf_t, kf_t, vf_t, logf_t = _proj(
            yp, g1, w_t, gains, fb, width=width, head_dim=head_dim, with_rows=False)
        od = _diff_attention(rb_flat, qd_t, kd_t, vd_t, lam_p, subg,
                             n_heads=n_dh, head_dim=head_dim, lambda_init=lambda_init)
        of = _fox_attention(qf_t, kf_t, vf_t, logf_t, head_dim=head_dim)
        x1p = _merge(yp.reshape(batch * seq, d), od.reshape(batch * seq, width),
                     of.reshape(batch * seq, width), g1, w_t, gb, wa, wb, wo)
        assert (batch * seq) % dec_batch == 0
        ffn_rows = pl.BlockSpec((REQUESTS_PER_STEP * (batch * seq // dec_batch), d), lambda i, pt: (i, 0))
        wgu_b, wdn_b = wgu.astype(BF16), wdn.astype(BF16)
        rider = _Rider(_ffn_up, _ffn_down, (x1p, g2, wgu_b, wdn_b),
                       (ffn_rows, _resident(g2.shape), _resident(wgu_b.shape), _resident(wdn_b.shape)),
                       jax.ShapeDtypeStruct(x1p.shape, F32), ffn_rows)
        rows_p.append(cache_rows(kd_t, vd, kf_t, vf_t, logf_t))

        (qd_t, kd_t, vd, _, qf_t, kf_t, vf_t, logf_t, qd_r, kd_r, qf_r, kf_r, vf_r) = _proj(
            ys, g1, w_t, gains, fb, width=width, head_dim=head_dim, with_rows=True)
        as_row = lambda a: a.reshape(dec_batch, width)
        cdk = jnp.transpose(cache_diff_k[l], (0, 2, 3, 4, 1)).reshape(n_phys, width, page_size)
        cdv = cache_diff_v[l].reshape(n_phys, page_size * n_dh, e)
        cfk = jnp.transpose(cache_fox_k[l], (0, 2, 3, 1)).reshape(n_phys, width, page_size)
        cfv = jnp.transpose(cache_fox_v[l], (0, 2, 3, 1)).reshape(n_phys, width, page_size)
        clf = jnp.transpose(cache_fox_logf[l], (0, 2, 1))
        od, of, yp = _decode_requests(
            page_table, as_row(qd_r), as_row(kd_r), vd.reshape(dec_batch * n_dh, e),
            as_row(qf_r), as_row(kf_r), as_row(vf_r), logf_t, rbt, lam_p, subg,
            cdk, cdv, cfk, cfv, clf, head_dim=head_dim, lambda_init=lambda_init, rider=rider)
        yp = yp.reshape(batch, seq, d)
        ys = tail(ys.reshape(dec_batch, d), od, of).reshape(1, dec_batch, d)
        rows_s.append(tuple(jnp.swapaxes(a, 0, 1) for a in cache_rows(kd_t, vd, kf_t, vf_t, logf_t)))

    stack = lambda rows, i: jnp.stack([r[i] for r in rows], axis=0)
    return (yp, ys.reshape(dec_batch, 1, d),
            *(stack(rows_p, i) for i in range(5)), *(stack(rows_s, i) for i in range(5)))
```

```python
import functools
import math
from typing import Any, NamedTuple

import jax
import jax.numpy as jnp
from jax import lax
from jax.experimental import pallas as pl
from jax.experimental.pallas import tpu as pltpu

F32 = jnp.float32
BF16 = jnp.bfloat16

N_BUCKETS = 32
MAX_DISTANCE = 128
EPS = 1e-6
NEG_INF = -1e30
LOG2E = math.log2(math.e)

LANES = 128
SUBLANES = 8
BF16_ROWS = 16
VMEM_LIMIT = 56 * 1024 * 1024

ROW_TILE = 512
FFN_ROW_TILE = 256
ATTN_TQ = 512
ATTN_TK = 256
FOX_PAIRS_PER_STEP = 2
DIFF_HEADS_PER_STEP = 2
DECAY_PARTS = 3

NT_DIMS = (((1,), (1,)), ((), ()))
HIGHEST = lax.Precision.HIGHEST


def _resident(shape):
    zeros = (0,) * len(shape)
    return pl.BlockSpec(shape, lambda *_: zeros, pipeline_mode=pl.Buffered(1))


def _params(semantics, vmem=VMEM_LIMIT):
    return pltpu.CompilerParams(dimension_semantics=semantics, vmem_limit_bytes=vmem)


def _rms(x, g):
    return x * lax.rsqrt(jnp.mean(x * x, axis=-1, keepdims=True) + EPS) * g


def _dot(a, b, precision=None):
    return jnp.dot(a, b, preferred_element_type=F32, precision=precision)


def _dot_nt(a, b):
    return lax.dot_general(a, b, NT_DIMS, preferred_element_type=F32)


def _log_sigmoid(x):
    return jnp.minimum(x, 0.0) - jnp.log1p(jnp.exp(-jnp.abs(x)))


def _rel_bucket(n):
    max_exact = N_BUCKETS // 2
    nf = jnp.maximum(n, 1).astype(F32)
    large = max_exact + (jnp.log(nf / max_exact) / math.log(MAX_DISTANCE / max_exact)
                         * (N_BUCKETS - max_exact)).astype(jnp.int32)
    return jnp.where(n < max_exact, n, jnp.minimum(large, N_BUCKETS - 1))


def _lambda_value(lam_ref, lambda_init):
    lp = lam_ref[...]
    a = jnp.sum(lp[0:1] * lp[1:2], axis=-1, keepdims=True)
    b = jnp.sum(lp[2:3] * lp[3:4], axis=-1, keepdims=True)
    return jnp.exp(a) - jnp.exp(b) + lambda_init


def _proj_kernel(x_ref, g1_ref, wt_ref, gains_ref, fb_ref,
                 qd_ref, kd_ref, vd_ref, vdt_ref, qf_ref, kf_ref, vf_ref, logf_ref, *row_refs,
                 head_dim, q_scale):
    h = _rms(x_ref[0], g1_ref[...])
    width = kd_ref.shape[1]
    n_heads = width // head_dim
    n_f = logf_ref.shape[1]
    section = lambda i: _dot_nt(wt_ref[i * width:(i + 1) * width, :], h)

    def head_norm_t(z, i):
        z3 = z.reshape(n_heads, head_dim, z.shape[-1])
        ss = jnp.sum(z3 * z3, axis=1, keepdims=True)
        gain = gains_ref[i].reshape(n_heads, head_dim, 1)
        return (z3 * lax.rsqrt(ss * (1.0 / head_dim) + EPS) * gain).reshape(z.shape)

    qd = head_norm_t(section(0), 0) * q_scale
    kd = head_norm_t(section(1), 1)
    qf = head_norm_t(section(3), 2) * q_scale
    kf = head_norm_t(section(4), 3)
    vd_t = section(2)
    vd_rows = vd_t.T
    vf = section(5)
    qd_ref[0] = qd.astype(qd_ref.dtype)
    kd_ref[0] = kd
    e = vd_ref.shape[-1]
    for hd in range(width // e):
        vd_ref[0, pl.ds(hd, vd_rows.shape[0], stride=width // e), :] = vd_rows[:, hd * e:(hd + 1) * e]
    vdt_ref[0] = vd_t.astype(vdt_ref.dtype)
    qf_ref[0] = qf.astype(qf_ref.dtype)
    kf_ref[0] = kf
    vf_ref[0] = vf
    logf_ref[0] = _log_sigmoid(_dot_nt(wt_ref[6 * width:6 * width + n_f, :], h) + fb_ref[...])
    if row_refs:
        for ref, val in zip(row_refs, (qd, kd, qf, kf, vf)):
            ref[0] = val.T


def _proj(x, g1, wt, gains, fb, *, width, head_dim, with_rows):
    b, s, d = x.shape
    n_f = fb.shape[0]
    e = 2 * head_dim
    tm = min(ROW_TILE, s)
    assert s % tm == 0 and tm % LANES == 0 and width % e == 0
    t_blk = lambda r: pl.BlockSpec((1, r, tm), lambda bi, si: (bi, 0, si))
    r_blk = lambda c: pl.BlockSpec((1, tm, c), lambda bi, si: (bi, si, 0))
    t_shape = lambda dt: jax.ShapeDtypeStruct((b, width, s), dt)
    r_shape = jax.ShapeDtypeStruct((b, s, width), F32)
    heads_per_token = width // e
    out_shape = [t_shape(BF16), t_shape(F32), jax.ShapeDtypeStruct((b, s * heads_per_token, e), F32),
                 t_shape(BF16), t_shape(BF16), t_shape(F32), t_shape(F32),
                 jax.ShapeDtypeStruct((b, n_f, s), F32)]
    out_specs = [t_blk(width), t_blk(width),
                 pl.BlockSpec((1, tm * heads_per_token, e), lambda bi, si: (bi, si, 0)),
                 t_blk(width), t_blk(width), t_blk(width), t_blk(width), t_blk(n_f)]
    if with_rows:
        out_shape += [r_shape] * 5
        out_specs += [r_blk(width)] * 5
    return pl.pallas_call(
        functools.partial(_proj_kernel, head_dim=head_dim, q_scale=head_dim ** -0.5 * LOG2E),
        out_shape=out_shape,
        grid=(b, s // tm),
        in_specs=[r_blk(d)] + [_resident(a.shape) for a in (g1, wt, gains, fb)],
        out_specs=out_specs,
        compiler_params=_params(("arbitrary", "arbitrary")),
        name="proj",
    )(x, g1, wt, gains, fb)


def _flash_step(scores, values, state):
    new_state = []
    for s_t, v_t, (m_prev, acc_prev) in zip(scores, values, state):
        m_new = jnp.maximum(m_prev, jnp.max(s_t, axis=0, keepdims=True))
        alpha = jnp.exp2(m_prev - m_new)
        p_t = jnp.exp2(s_t - m_new).astype(BF16)
        new_state.append((m_new, alpha * acc_prev + _dot(v_t, p_t)))
    return tuple(new_state)


def _flash_init(n_slots, rows, tq):
    return tuple((jnp.full((1, tq), NEG_INF, F32), jnp.zeros((rows, tq), F32)) for _ in range(n_slots))


def _ones_rows(seq):
    return jnp.where(lax.broadcasted_iota(jnp.int32, (BF16_ROWS, seq), 0) == 0, 1.0, 0.0).astype(BF16)


def _key_minus_query(tk, tq):
    return lax.broadcasted_iota(jnp.int32, (tk, tq), 0) - lax.broadcasted_iota(jnp.int32, (tk, tq), 1)


class _ScoreRing:
    def __init__(self, s_ref, qk):
        self.s_ref, self.qk = s_ref, qk

    def fill(self, j, slot):
        for idx, s in enumerate(self.qk(j)):
            self.s_ref[slot, idx] = s

    def scores(self, slot):
        return [self.s_ref[slot, idx] for idx in range(self.s_ref.shape[1])]

    def far_sweep(self, softmax, state, n_pairs):
        self.fill(0, 0)

        def body(jj, st):
            j = 2 * jj
            self.fill(j + 1, 1)
            st = softmax(self.scores(0), j, st)
            self.fill(j + 2, 0)
            return softmax(self.scores(1), j + 1, st)
        return lax.fori_loop(0, n_pairs, body, state)

    def tail(self, softmax, state, first, kinds, more_follows):
        assert not more_follows or len(kinds) % 2 == 0
        for n, kind in enumerate(kinds):
            slot = n % 2
            if n + 1 < len(kinds) or more_follows:
                self.fill(first + n + 1, 1 - slot)
            state = softmax(self.scores(slot), first + n, state, kind)
        return state


def _diff_attn_kernel(rb_ref, q_ref, k_ref, v_ref, lam_ref, subg_ref, o_ref,
                      kb_ref, vb_ref, bias_ref, s_ref, *, n_heads, head_dim, lambda_init):
    tq, tk = ATTN_TQ, ATTN_TK
    ratio = tq // tk
    n_hd = vb_ref.shape[0]
    head0 = n_hd * pl.program_id(0)
    seq = q_ref.shape[2]
    e = 2 * head_dim
    kmq = _key_minus_query(tk, tq)
    near = [tk] + [-d * tk for d in range(ratio)]

    @pl.when(pl.program_id(1) == 0)
    def _():
        for hd in range(n_hd):
            far = rb_ref[(N_BUCKETS - 1) * n_heads + head0 + hd]
            for idx, delta in enumerate(near):
                bucket = _rel_bucket(jnp.maximum(delta - kmq, 0))
                val = jnp.zeros((tk, tq), F32)
                for b in range(N_BUCKETS):
                    val = jnp.where(bucket == b, rb_ref[b * n_heads + head0 + hd], val)
                bias_ref[hd, idx] = (val - far) * LOG2E

    first = lax.broadcasted_iota(jnp.int32, (1, e), 1) < head_dim
    for hd in range(n_hd):
        k_rows = k_ref[0, hd * e:(hd + 1) * e, :].T
        kb_ref[2 * hd] = jnp.where(first, k_rows, 0.0).astype(BF16)
        kb_ref[2 * hd + 1] = jnp.where(first, 0.0, k_rows).astype(BF16)
        vb_ref[hd, 0:e, :] = v_ref[0, hd * e:(hd + 1) * e, :]
        vb_ref[hd, e:, :] = _ones_rows(seq)
    lam = _lambda_value(lam_ref, lambda_init)

    def q_block(i, carry):
        qs = pl.multiple_of(i * tq, tq)
        q_t = [q_ref[0, hd * e:(hd + 1) * e, pl.ds(qs, tq)] for hd in range(n_hd)]

        def qk(j):
            ks = pl.multiple_of(j * tk, tk)
            return [_dot(kb_ref[mp, pl.ds(ks, tk), :], q_t[mp // 2]) for mp in range(2 * n_hd)]

        def softmax(scores, j, state, near_idx=None):
            if near_idx is not None:
                scores = [s + bias_ref[mp // 2, near_idx] for mp, s in enumerate(scores)]
                if near[near_idx] <= 0:
                    visible = kmq <= near[near_idx]
                    scores = [jnp.where(visible, s, NEG_INF) for s in scores]
            ks = pl.multiple_of(j * tk, tk)
            values = [vb_ref[mp // 2, :, pl.ds(ks, tk)] for mp in range(2 * n_hd)]
            return _flash_step(scores, values, state)

        ring = _ScoreRing(s_ref, qk)
        first_near = ratio * (i - 1)
        state = ring.far_sweep(softmax, _flash_init(2 * n_hd, vb_ref.shape[1], tq),
                               jnp.maximum(first_near, 0) // 2)
        before = [None] * (ratio - 1) + [0]
        state = lax.cond(i >= 1, lambda st: ring.tail(softmax, st, first_near, before, True),
                         lambda st: st, state)
        state = ring.tail(softmax, state, ratio * i, [1 + d for d in range(ratio)], False)
        outs = []
        for hd in range(n_hd):
            (_, acc0), (_, acc1) = state[2 * hd:2 * hd + 2]
            o_t = acc0[0:e] / acc0[e:e + 1] - lam * (acc1[0:e] / acc1[e:e + 1])
            outs.append(_rms(o_t.T, subg_ref[...]) * (1.0 - lambda_init))
        o_ref[0, pl.ds(qs, tq), :] = jnp.concatenate(outs, axis=1).astype(o_ref.dtype)
        return carry

    lax.fori_loop(0, seq // tq, q_block, 0)


def _diff_attention(rb_flat, q_t, k_t, v, lam_p, subg, *, n_heads, head_dim, lambda_init):
    b, w, s = q_t.shape
    e = 2 * head_dim
    tq, tk = ATTN_TQ, ATTN_TK
    assert s % tq == 0 and tq % tk == 0 and tk % LANES == 0 and tk >= MAX_DISTANCE and e == LANES
    n_hd = DIFF_HEADS_PER_STEP
    assert n_heads % n_hd == 0
    t_blk = pl.BlockSpec((1, n_hd * e, s), lambda g, bi, *_: (bi, g, 0))
    r_blk = pl.BlockSpec((1, s, n_hd * e), lambda g, bi, *_: (bi, 0, g))
    return pl.pallas_call(
        functools.partial(_diff_attn_kernel, n_heads=n_heads, head_dim=head_dim, lambda_init=lambda_init),
        out_shape=jax.ShapeDtypeStruct((b, s, w), BF16),
        grid_spec=pltpu.PrefetchScalarGridSpec(
            num_scalar_prefetch=1,
            grid=(n_heads // n_hd, b),
            in_specs=[t_blk, t_blk, t_blk, _resident(lam_p.shape), _resident(subg.shape)],
            out_specs=r_blk,
            scratch_shapes=[pltpu.VMEM((2 * n_hd, s, e), BF16), pltpu.VMEM((n_hd, e + BF16_ROWS, s), BF16),
                            pltpu.VMEM((n_hd, tq // tk + 1, tk, tq), F32),
                            pltpu.VMEM((2, 2 * n_hd, tk, tq), F32)]),
        compiler_params=_params(("arbitrary", "arbitrary")),
        name="diff_attention",
    )(rb_flat, q_t, k_t, v, lam_p, subg)


def _fox_attn_kernel(q_ref, k_ref, v_ref, logf_ref, o_ref, kb_ref, vb_ref, c_ref, s_ref, *, head_dim):
    tq, tk = ATTN_TQ, ATTN_TK
    ratio = tq // tk
    group = pl.program_id(1)
    seq = q_ref.shape[2]
    e = 2 * head_dim
    n_h = kb_ref.shape[0]
    kmq = _key_minus_query(tk, tq)

    @pl.when(group == 0)
    def _():
        upper = (lax.broadcasted_iota(jnp.int32, (tk, tk), 0)
                 <= lax.broadcasted_iota(jnp.int32, (tk, tk), 1)).astype(F32)
        carry = jnp.zeros((logf_ref.shape[1], 1), F32)
        for blk in range(seq // tk):
            c = _dot(logf_ref[0, :, blk * tk:(blk + 1) * tk], upper, HIGHEST) + carry
            c_ref[:, blk * tk:(blk + 1) * tk] = c
            carry = c[:, tk - 1:tk]

    lane = lax.broadcasted_iota(jnp.int32, (1, e), 1)
    for hh in range(n_h):
        pair, half = divmod(hh, 2)
        k_rows = k_ref[0, pair * e:(pair + 1) * e, :].T
        spare = (1 - half) * head_dim
        rest = jnp.broadcast_to(c_ref[pl.ds(n_h * group + hh, 1), :] * LOG2E, (LANES, seq)).T
        k_aug = jnp.where(lane // head_dim == half, k_rows, 0.0)
        for part in range(DECAY_PARTS):
            piece = rest.astype(BF16).astype(F32)
            k_aug = jnp.where(lane == spare + part, piece, k_aug)
            rest = rest - piece
        kb_ref[hh] = k_aug.astype(BF16)
        vb_ref[hh, 0:head_dim, :] = v_ref[0, hh * head_dim:(hh + 1) * head_dim, :].astype(BF16)
        vb_ref[hh, head_dim:, :] = _ones_rows(seq)

    def q_block(i, carry):
        qs = pl.multiple_of(i * tq, tq)
        row = lax.broadcasted_iota(jnp.int32, (e, 1), 0)
        q_aug = []
        for hh in range(n_h):
            pair, half = divmod(hh, 2)
            q_t = q_ref[0, pair * e:(pair + 1) * e, pl.ds(qs, tq)]
            spare = (1 - half) * head_dim
            minus_one = jnp.logical_and(row >= spare, row < spare + DECAY_PARTS)
            q_aug.append(jnp.where(row // head_dim == half, q_t,
                                   jnp.where(minus_one, -1.0, 0.0).astype(BF16)))

        def qk(j):
            ks = pl.multiple_of(j * tk, tk)
            return [_dot(kb_ref[hh, pl.ds(ks, tk), :], q_aug[hh]) for hh in range(n_h)]

        def softmax(scores, j, state, diagonal=None):
            ks = pl.multiple_of(j * tk, tk)
            if diagonal is not None:
                visible = kmq <= -diagonal * tk
                scores = [jnp.where(visible, s, NEG_INF) for s in scores]
            values = [vb_ref[hh, :, pl.ds(ks, tk)] for hh in range(n_h)]
            return _flash_step(scores, values, state)

        ring = _ScoreRing(s_ref, qk)
        state = ring.far_sweep(softmax, _flash_init(n_h, vb_ref.shape[1], tq), ratio * i // 2)
        state = ring.tail(softmax, state, ratio * i, list(range(ratio)), False)
        o_t = jnp.concatenate([acc[0:head_dim] / acc[head_dim:head_dim + 1] for _, acc in state], axis=0)
        o_ref[0, pl.ds(qs, tq), :] = o_t.T.astype(o_ref.dtype)
        return carry

    lax.fori_loop(0, seq // tq, q_block, 0)


def _fox_attention(q_t, k_t, v_t, logf_t, *, head_dim):
    b, w, s = q_t.shape
    e = 2 * head_dim
    tq, tk = ATTN_TQ, ATTN_TK
    n_f = logf_t.shape[1]
    assert s % tq == 0 and tq % tk == 0 and tk % LANES == 0 and e == LANES and w // e * 2 == n_f
    assert DECAY_PARTS <= head_dim
    cols = FOX_PAIRS_PER_STEP * e
    n_h = 2 * FOX_PAIRS_PER_STEP
    assert w % cols == 0
    t_blk = pl.BlockSpec((1, cols, s), lambda bi, g: (bi, g, 0))
    return pl.pallas_call(
        functools.partial(_fox_attn_kernel, head_dim=head_dim),
        out_shape=jax.ShapeDtypeStruct((b, s, w), BF16),
        grid=(b, w // cols),
        in_specs=[t_blk, t_blk, t_blk, pl.BlockSpec((1, n_f, s), lambda bi, g: (bi, 0, 0))],
        out_specs=pl.BlockSpec((1, s, cols), lambda bi, g: (bi, 0, g)),
        scratch_shapes=[pltpu.VMEM((n_h, s, e), BF16), pltpu.VMEM((n_h, head_dim + BF16_ROWS, s), BF16),
                        pltpu.VMEM((n_f, s), F32), pltpu.VMEM((2, n_h, tk, tq), F32)],
        compiler_params=_params(("arbitrary", "arbitrary")),
        name="fox_attention",
    )(q_t, k_t, v_t, logf_t)


class _Rider(NamedTuple):
    first: Any
    second: Any
    operands: tuple
    in_specs: tuple
    out_shape: Any
    out_spec: Any


def _decode_step_kernel(pt_ref, qd_ref, kd_ref, vd_ref, qf_ref, kf_ref, vf_ref, lfn_ref,
                        rbt_ref, lam_ref, subg_ref, *rest,
                        n_pages, page_size, head_dim, lambda_init, rider, n_rider_in):
    rider_in, rest = rest[:n_rider_in], rest[n_rider_in:]
    (hbm_dk, hbm_dv, hbm_fk, hbm_fv, hbm_lf), rest = rest[:5], rest[5:]
    n_out = len(rest) - 7
    od_ref, of_ref, *rider_out = rest[:n_out]
    kd_buf, vd_buf, kf_buf, vf_buf, lf_buf, sems, bias_ref = rest[n_out:]
    req = pl.program_id(0)
    last = pl.num_programs(0) - 1
    n_maps = bias_ref.shape[0]
    key_side = ((hbm_dk, kd_buf), (hbm_fk, kf_buf), (hbm_lf, lf_buf)), sems.at[0]
    value_side = ((hbm_dv, vd_buf), (hbm_fv, vf_buf)), sems.at[1]

    def copies(request, side):
        pairs, sem = side
        for p in range(n_pages):
            page = pt_ref[request, n_pages - 1 - p]
            for cache, buf in pairs:
                yield pltpu.make_async_copy(cache.at[page], buf.at[p], sem)

    def start(request, side):
        for copy in copies(request, side):
            copy.start()

    def wait(request, side):
        for copy in copies(request, side):
            copy.wait()

    @pl.when(req == 0)
    def _():
        start(0, key_side)
        lane = lax.broadcasted_iota(jnp.int32, (n_maps, page_size), 1)
        bucket = _rel_bucket(page_size - lane)
        val = jnp.zeros((n_maps, page_size), F32)
        for b in range(N_BUCKETS):
            val = jnp.where(bucket == b, rbt_ref[:, b:b + 1], val)
        bias_ref[...] = (val - rbt_ref[:, N_BUCKETS - 1:N_BUCKETS]) * LOG2E

    start(req, value_side)
    wait(req, key_side)

    def scores_done():
        @pl.when(req < last)
        def _():
            start(req + 1, key_side)

    cdk, cdv, cfk, cfv, clf = ([buf.at[pl.ds(p, 1)] for p in range(n_pages)]
                               for buf in (kd_buf, vd_buf, kf_buf, vf_buf, lf_buf))
    width = qd_ref.shape[-1]
    n_maps = width // head_dim
    n_dh = n_maps // 2
    e = 2 * head_dim
    sub = req % qd_ref.shape[0]
    new_row = lambda ref: ref[pl.ds(sub, 1), :]

    map_of_lane = lax.broadcasted_iota(jnp.int32, (n_maps, width), 1) // head_dim
    row = lax.broadcasted_iota(jnp.int32, (n_maps, width), 0)
    own = map_of_lane == row
    row_e = lax.broadcasted_iota(jnp.int32, (n_maps, e), 0)
    far_bias = rbt_ref[:, N_BUCKETS - 1:N_BUCKETS]

    qd = jnp.where(own, new_row(qd_ref), 0.0).astype(BF16)
    qf = jnp.where(own, new_row(qf_ref), 0.0).astype(BF16)

    def self_score(q, k_ref):
        k = new_row(k_ref).astype(BF16).astype(F32)
        return jnp.sum(q.astype(F32) * k, axis=-1, keepdims=True)

    def softmax(s, s_self):
        m = jnp.maximum(jnp.max(s, axis=-1, keepdims=True), s_self)
        p, p_self = jnp.exp2(s - m), jnp.exp2(s_self - m)
        return p, p_self, jnp.sum(p, axis=-1, keepdims=True) + p_self

    def keys_side_by_side(refs):
        return jnp.concatenate([ref[0].astype(BF16) for ref in refs], axis=1)

    rider_carry = rider.first(*rider_in) if rider is not None else None
    bias = jnp.concatenate([bias_ref[...], jnp.zeros((n_maps, (n_pages - 1) * page_size), F32)], axis=1)
    s_d = _dot(qd, keys_side_by_side(cdk)) + bias
    later = (lax.broadcasted_iota(jnp.int32, (page_size, page_size), 0)
             >= lax.broadcasted_iota(jnp.int32, (page_size, page_size), 1)).astype(F32)
    lane = lax.broadcasted_iota(jnp.int32, lfn_ref.shape[1:], 1)
    carry = jnp.sum(jnp.where(lane == req, lfn_ref[0], 0.0), axis=1, keepdims=True) * LOG2E
    lf = jnp.concatenate([ref[0] for ref in clf], axis=0) * LOG2E
    incl = _dot(lf, later, HIGHEST)
    decay = []
    for p_idx in range(n_pages):
        rows = slice(p_idx * n_maps, (p_idx + 1) * n_maps)
        decay.append(incl[rows] - lf[rows] + carry)
        carry = carry + incl[rows, 0:1]
    s_f = _dot(qf, keys_side_by_side(cfk)) + jnp.concatenate(decay, axis=1)
    scores_done()

    wait(req, value_side)
    if rider is not None:
        rider.second(rider_carry, *rider_in, *rider_out)

    p, p_self, l = softmax(s_d, self_score(qd, kd_ref) + (rbt_ref[:, 0:1] - far_bias) * LOG2E)
    p = p.astype(BF16)
    v_new = vd_ref[pl.ds(sub * n_dh, n_dh), :]
    acc = jnp.zeros((n_maps, e), F32)
    for h in range(n_dh):
        v_h = jnp.concatenate([ref[0, pl.ds(h, page_size, stride=n_dh), :].astype(BF16) for ref in cdv],
                              axis=0)
        acc = jnp.where(row_e // 2 == h, _dot(p, v_h) + p_self * v_new[h:h + 1, :], acc)
    lam = _lambda_value(lam_ref, lambda_init)
    sign = jnp.where(row_e[:, 0:1] % 2 == 0, 1.0, -lam)
    pick = (lax.broadcasted_iota(jnp.int32, (n_dh, n_maps), 1) // 2
            == lax.broadcasted_iota(jnp.int32, (n_dh, n_maps), 0)).astype(F32)
    scaled = acc / l * sign

    p, p_self, l = softmax(s_f, self_score(qf, kf_ref))
    acc = jnp.zeros((n_maps, head_dim, page_size), F32)
    for p_idx in range(n_pages):
        p_page = p[:, p_idx * page_size:(p_idx + 1) * page_size]
        acc = acc + p_page[:, None, :] * cfv[p_idx][0].reshape(n_maps, head_dim, page_size)

    def spread(col):
        return jnp.sum(jnp.where(own, col, 0.0), axis=0, keepdims=True)
    past = jnp.sum(acc.reshape(width, page_size).T, axis=0, keepdims=True)
    of_ref[pl.ds(sub, 1), :] = (past + spread(p_self) * new_row(vf_ref)) / spread(l)

    o_d = _dot(pick, scaled, HIGHEST)
    o_d = _rms(o_d, subg_ref[...]) * (1.0 - lambda_init)
    od_ref[pl.ds(sub, 1), :] = jnp.concatenate([o_d[h:h + 1] for h in range(n_dh)], axis=1)


def _decode_requests(page_table, qd, kd, vd, qf, kf, vf, logf_new, rbt, lam_p, subg,
                     cdk, cdv, cfk, cfv, clf, *, head_dim, lambda_init, rider=None):
    r, n_pages = page_table.shape
    _, width, page_size = cdk.shape
    n_f = clf.shape[1]
    n_maps = width // head_dim
    n_dh = n_maps // 2
    e = 2 * head_dim
    assert n_maps == n_f and r % SUBLANES == 0
    assert page_size >= MAX_DISTANCE
    row = pl.BlockSpec((SUBLANES, width), lambda i, pt: (i // SUBLANES, 0))
    head_rows = pl.BlockSpec((SUBLANES * n_dh, e), lambda i, pt: (i // SUBLANES, 0))
    caches = [cdk, cdv, cfk, cfv, clf]

    in_specs = [row, row, head_rows, row, row, row, _resident(logf_new.shape),
                _resident(rbt.shape), _resident(lam_p.shape), _resident(subg.shape)]
    operands = [qd, kd, vd, qf, kf, vf, logf_new, rbt, lam_p, subg]
    out_shape = [jax.ShapeDtypeStruct((r, width), F32)] * 2
    out_specs = [row, row]
    if rider is not None:
        in_specs += list(rider.in_specs)
        operands += list(rider.operands)
        out_shape.append(rider.out_shape)
        out_specs.append(rider.out_spec)
    in_specs += [pl.BlockSpec(memory_space=pl.ANY)] * len(caches)
    operands += caches
    return pl.pallas_call(
        functools.partial(_decode_step_kernel, n_pages=n_pages, page_size=page_size,
                          head_dim=head_dim, lambda_init=lambda_init,
                          rider=rider, n_rider_in=len(rider.operands) if rider else 0),
        out_shape=out_shape,
        grid_spec=pltpu.PrefetchScalarGridSpec(
            num_scalar_prefetch=1,
            grid=(r,),
            in_specs=in_specs,
            out_specs=out_specs,
            scratch_shapes=[pltpu.VMEM((n_pages,) + c.shape[1:], c.dtype) for c in caches]
                           + [pltpu.SemaphoreType.DMA((2,)), pltpu.VMEM((n_maps, page_size), F32)]),
        compiler_params=_params(("arbitrary",)),
        name="decode_attention",
    )(page_table, *operands)


def _merge_rows(x, od_ref, of_ref, g1_ref, wg_ref, gb_ref, wa_ref, wb_ref, wo_ref):
    d = x.shape[-1]
    dt = wg_ref.dtype
    h = _rms(x, g1_ref[...]).astype(dt)
    ya = _dot(od_ref[...].astype(dt), wa_ref[...])
    yb = _dot(of_ref[...].astype(dt), wb_ref[...])
    gate_a = jax.nn.sigmoid(_dot(h, wg_ref[:, :d]) + gb_ref[:, :d])
    gate_b = jax.nn.sigmoid(_dot(h, wg_ref[:, d:]) + gb_ref[:, d:])
    merged = gate_a * ya + gate_b * yb
    return x + _dot(merged.astype(dt), wo_ref[...])


def _merge_kernel(x_ref, *refs):
    *in_refs, o_ref = refs
    o_ref[...] = _merge_rows(x_ref[...], *in_refs)


def _merge(x, od, of, g1, wg, gb, wa, wb, wo):
    m, d = x.shape
    tm = min(ROW_TILE, m)
    assert m % tm == 0
    row = lambda c: pl.BlockSpec((tm, c), lambda i: (i, 0))
    return pl.pallas_call(
        _merge_kernel,
        out_shape=jax.ShapeDtypeStruct((m, d), F32),
        grid=(m // tm,),
        in_specs=[row(d), row(od.shape[1]), row(of.shape[1])]
                 + [_resident(a.shape) for a in (g1, wg, gb, wa, wb, wo)],
        out_specs=row(d),
        compiler_params=_params(("arbitrary",)),
        name="merge",
    )(x, od, of, g1, wg, gb, wa, wb, wo)


def _ffn_hidden(x, g2_ref, wgu_ref, wdn_ref):
    d_ff = wdn_ref.shape[0]
    h = _rms(x, g2_ref[...]).astype(wgu_ref.dtype)
    gate = _dot(h, wgu_ref[:, :d_ff])
    up = _dot(h, wgu_ref[:, d_ff:])
    return (gate * jax.nn.sigmoid(gate) * up).astype(wdn_ref.dtype)


def _ffn_kernel(x_ref, g2_ref, wgu_ref, wdn_ref, o_ref):
    x = x_ref[...]
    o_ref[...] = x + _dot(_ffn_hidden(x, g2_ref, wgu_ref, wdn_ref), wdn_ref[...])


N_MERGE_REFS = 9


def _tail_first(*refs):
    x_ref, *merge_refs = refs[:N_MERGE_REFS]
    x1 = _merge_rows(x_ref[...], *merge_refs)
    return x1, _ffn_hidden(x1, *refs[N_MERGE_REFS:])


def _tail_second(carry, *refs):
    x1, act = carry
    wdn_ref, o_ref = refs[-2:]
    o_ref[...] = x1 + _dot(act, wdn_ref[...])


def _ffn(x, g2, wgu, wdn):
    m, d = x.shape
    tm = min(FFN_ROW_TILE, m)
    assert m % tm == 0
    row = pl.BlockSpec((tm, d), lambda i: (i, 0))
    return pl.pallas_call(
        _ffn_kernel,
        out_shape=jax.ShapeDtypeStruct((m, d), F32),
        grid=(m // tm,),
        in_specs=[row, _resident(g2.shape), _resident(wgu.shape), _resident(wdn.shape)],
        out_specs=row,
        compiler_params=_params(("arbitrary",)),
        name="ffn",
    )(x, g2, wgu, wdn)


def kernel(x_prompt, x_sample, cache_diff_k, cache_diff_v, cache_fox_k, cache_fox_v, cache_fox_logf,
           page_table, rel_bias, norm1_g, w_in, diff_q_g, diff_k_g, fox_q_g, fox_k_g, diff_lambda,
           fox_f_b, gate_b, diff_subln_g, w_branch_a, w_branch_b, w_out, norm2_g, w_gate_up, w_down):
    depth = w_in.shape[0]
    batch, seq, d = x_prompt.shape
    dec_batch, dec_seq, _ = x_sample.shape
    assert dec_seq == 1
    _, n_phys, page_size, n_dh, _, head_dim = cache_diff_k.shape
    n_f = cache_fox_k.shape[3]
    e = 2 * head_dim
    width = n_dh * e
    assert n_f * head_dim == width

    rb_flat = rel_bias.reshape(-1)
    rbt = jnp.repeat(rel_bias.T, 2, axis=0)

    yp, ys = x_prompt, x_sample.reshape(1, dec_batch, d)
    rows_p, rows_s = [], []
    for l in range(depth):
        lambda_init = 0.8 - 0.6 * math.exp(-0.3 * l)
        w_t = w_in[l].T
        fb = fox_f_b[l].reshape(n_f, 1)
        gains = jnp.stack([jnp.tile(g[l], width // head_dim)
                           for g in (diff_q_g, diff_k_g, fox_q_g, fox_k_g)]).reshape(4, width, 1)
        g1 = norm1_g[l].reshape(1, d)
        g2 = norm2_g[l].reshape(1, d)
        gb = gate_b[l].reshape(1, 2 * d)
        subg = diff_subln_g[l].reshape(1, e)
        wg = w_in[l][:, 6 * width + n_f:]
        merge_w = (g1, wg.astype(BF16), gb, w_branch_a[l].astype(BF16), w_branch_b[l].astype(BF16),
                   w_out[l].astype(BF16))
        ffn_w = (g2, w_gate_up[l].astype(BF16), w_down[l].astype(BF16))
        lam_p = diff_lambda[l]

        def cache_rows(kd_t, vd, kf_t, vf_t, logf_t):
            b, _, s = kd_t.shape
            return (jnp.transpose(kd_t.reshape(b, n_dh, 2, head_dim, s), (0, 4, 1, 2, 3)),
                    vd.reshape(b, s, n_dh, e),
                    jnp.transpose(kf_t.reshape(b, n_f, head_dim, s), (0, 3, 1, 2)),
                    jnp.transpose(vf_t.reshape(b, n_f, head_dim, s), (0, 3, 1, 2)),
                    jnp.transpose(logf_t, (0, 2, 1)))

        qd_t, kd_t, vd, vd_t, qf_t, kf_t, vf_t, logf_t = _proj(
            yp, g1, w_t, gains, fb, width=width, head_dim=head_dim, with_rows=False)
        od = _diff_attention(rb_flat, qd_t, kd_t, vd_t, lam_p, subg,
                             n_heads=n_dh, head_dim=head_dim, lambda_init=lambda_init)
        of = _fox_attention(qf_t, kf_t, vf_t, logf_t, head_dim=head_dim)
        m_p = batch * seq
        assert m_p % dec_batch == 0
        rows = lambda c: pl.BlockSpec((m_p // dec_batch, c), lambda i, pt: (i, 0))
        tiled = (yp.reshape(m_p, d), od.reshape(m_p, width), of.reshape(m_p, width))
        weights = merge_w + ffn_w
        assert len(tiled) + len(merge_w) == N_MERGE_REFS
        rider = _Rider(_tail_first, _tail_second, tiled + weights,
                       tuple(rows(a.shape[1]) for a in tiled) + tuple(_resident(w.shape) for w in weights),
                       jax.ShapeDtypeStruct((m_p, d), F32), rows(d))
        rows_p.append(cache_rows(kd_t, vd, kf_t, vf_t, logf_t))

        (qd_t, kd_t, vd, _, qf_t, kf_t, vf_t, logf_t, qd_r, kd_r, qf_r, kf_r, vf_r) = _proj(
            ys, g1, w_t, gains, fb, width=width, head_dim=head_dim, with_rows=True)
        as_row = lambda a: a.reshape(dec_batch, width)
        cdk = jnp.transpose(cache_diff_k[l], (0, 2, 3, 4, 1)).reshape(n_phys, width, page_size)
        cdv = cache_diff_v[l].reshape(n_phys, page_size * n_dh, e)
        cfk = jnp.transpose(cache_fox_k[l], (0, 2, 3, 1)).reshape(n_phys, width, page_size)
        cfv = jnp.transpose(cache_fox_v[l], (0, 2, 3, 1)).reshape(n_phys, width, page_size)
        clf = jnp.transpose(cache_fox_logf[l], (0, 2, 1))
        od, of, yp = _decode_requests(
            page_table, as_row(qd_r), as_row(kd_r), vd.reshape(dec_batch * n_dh, e),
            as_row(qf_r), as_row(kf_r), as_row(vf_r), logf_t, rbt, lam_p, subg,
            cdk, cdv, cfk, cfv, clf, head_dim=head_dim, lambda_init=lambda_init, rider=rider)
        yp = yp.reshape(batch, seq, d)
        ys = _ffn(_merge(ys.reshape(dec_batch, d), od, of, *merge_w), *ffn_w).reshape(1, dec_batch, d)
        rows_s.append(tuple(jnp.swapaxes(a, 0, 1) for a in cache_rows(kd_t, vd, kf_t, vf_t, logf_t)))

    stack = lambda rows, i: jnp.stack([r[i] for r in rows], axis=0)
    return (yp, ys.reshape(dec_batch, 1, d),
            *(stack(rows_p, i) for i in range(5)), *(stack(rows_s, i) for i in range(5)))
```

```python
import functools
import math
from typing import Any, NamedTuple

import jax
import jax.numpy as jnp
from jax import lax
from jax.experimental import pallas as pl
from jax.experimental.pallas import tpu as pltpu

F32 = jnp.float32
BF16 = jnp.bfloat16

N_BUCKETS = 32
MAX_DISTANCE = 128
EPS = 1e-6
NEG_INF = -1e30
LOG2E = math.log2(math.e)

LANES = 128
SUBLANES = 8
BF16_ROWS = 16
VMEM_LIMIT = 56 * 1024 * 1024

ROW_TILE = 512
FFN_ROW_TILE = 256
ATTN_TQ = 512
ATTN_TK = 256
FOX_PAIRS_PER_STEP = 2
DIFF_HEADS_PER_STEP = 2
DECAY_PARTS = 3

NT_DIMS = (((1,), (1,)), ((), ()))
HIGHEST = lax.Precision.HIGHEST


def _resident(shape):
    zeros = (0,) * len(shape)
    return pl.BlockSpec(shape, lambda *_: zeros, pipeline_mode=pl.Buffered(1))


def _params(semantics, vmem=VMEM_LIMIT):
    return pltpu.CompilerParams(dimension_semantics=semantics, vmem_limit_bytes=vmem)


def _rms(x, g):
    return x * lax.rsqrt(jnp.mean(x * x, axis=-1, keepdims=True) + EPS) * g


def _dot(a, b, precision=None):
    return jnp.dot(a, b, preferred_element_type=F32, precision=precision)


def _dot_nt(a, b):
    return lax.dot_general(a, b, NT_DIMS, preferred_element_type=F32)


def _log_sigmoid(x):
    return jnp.minimum(x, 0.0) - jnp.log1p(jnp.exp(-jnp.abs(x)))


def _rel_bucket(n):
    max_exact = N_BUCKETS // 2
    nf = jnp.maximum(n, 1).astype(F32)
    large = max_exact + (jnp.log(nf / max_exact) / math.log(MAX_DISTANCE / max_exact)
                         * (N_BUCKETS - max_exact)).astype(jnp.int32)
    return jnp.where(n < max_exact, n, jnp.minimum(large, N_BUCKETS - 1))


def _lambda_value(lam_ref, lambda_init):
    lp = lam_ref[...]
    a = jnp.sum(lp[0:1] * lp[1:2], axis=-1, keepdims=True)
    b = jnp.sum(lp[2:3] * lp[3:4], axis=-1, keepdims=True)
    return jnp.exp(a) - jnp.exp(b) + lambda_init


def _proj_kernel(x_ref, g1_ref, wt_ref, gains_ref, fb_ref,
                 qd_ref, kd_ref, vd_ref, vdt_ref, qf_ref, kf_ref, vf_ref, logf_ref, *row_refs,
                 head_dim, q_scale):
    h = _rms(x_ref[0], g1_ref[...])
    width = kd_ref.shape[1]
    n_heads = width // head_dim
    n_f = logf_ref.shape[1]
    section = lambda i: _dot_nt(wt_ref[i * width:(i + 1) * width, :], h)

    def head_norm_t(z, i):
        z3 = z.reshape(n_heads, head_dim, z.shape[-1])
        ss = jnp.sum(z3 * z3, axis=1, keepdims=True)
        gain = gains_ref[i].reshape(n_heads, head_dim, 1)
        return (z3 * lax.rsqrt(ss * (1.0 / head_dim) + EPS) * gain).reshape(z.shape)

    qd = head_norm_t(section(0), 0) * q_scale
    kd = head_norm_t(section(1), 1)
    qf = head_norm_t(section(3), 2) * q_scale
    kf = head_norm_t(section(4), 3)
    vd_t = section(2)
    vd_rows = vd_t.T
    vf = section(5)
    qd_ref[0] = qd.astype(qd_ref.dtype)
    kd_ref[0] = kd
    e = vd_ref.shape[-1]
    for hd in range(width // e):
        vd_ref[0, pl.ds(hd, vd_rows.shape[0], stride=width // e), :] = vd_rows[:, hd * e:(hd + 1) * e]
    vdt_ref[0] = vd_t.astype(vdt_ref.dtype)
    qf_ref[0] = qf.astype(qf_ref.dtype)
    kf_ref[0] = kf
    vf_ref[0] = vf
    logf_ref[0] = _log_sigmoid(_dot_nt(wt_ref[6 * width:6 * width + n_f, :], h) + fb_ref[...])
    if row_refs:
        for ref, val in zip(row_refs, (qd, kd, qf, kf, vf)):
            ref[0] = val.T


def _proj(x, g1, wt, gains, fb, *, width, head_dim, with_rows):
    b, s, d = x.shape
    n_f = fb.shape[0]
    e = 2 * head_dim
    tm = min(ROW_TILE, s)
    assert s % tm == 0 and tm % LANES == 0 and width % e == 0
    t_blk = lambda r: pl.BlockSpec((1, r, tm), lambda bi, si: (bi, 0, si))
    r_blk = lambda c: pl.BlockSpec((1, tm, c), lambda bi, si: (bi, si, 0))
    t_shape = lambda dt: jax.ShapeDtypeStruct((b, width, s), dt)
    r_shape = jax.ShapeDtypeStruct((b, s, width), F32)
    heads_per_token = width // e
    out_shape = [t_shape(BF16), t_shape(F32), jax.ShapeDtypeStruct((b, s * heads_per_token, e), F32),
                 t_shape(BF16), t_shape(BF16), t_shape(F32), t_shape(F32),
                 jax.ShapeDtypeStruct((b, n_f, s), F32)]
    out_specs = [t_blk(width), t_blk(width),
                 pl.BlockSpec((1, tm * heads_per_token, e), lambda bi, si: (bi, si, 0)),
                 t_blk(width), t_blk(width), t_blk(width), t_blk(width), t_blk(n_f)]
    if with_rows:
        out_shape += [r_shape] * 5
        out_specs += [r_blk(width)] * 5
    return pl.pallas_call(
        functools.partial(_proj_kernel, head_dim=head_dim, q_scale=head_dim ** -0.5 * LOG2E),
        out_shape=out_shape,
        grid=(b, s // tm),
        in_specs=[r_blk(d)] + [_resident(a.shape) for a in (g1, wt, gains, fb)],
        out_specs=out_specs,
        compiler_params=_params(("arbitrary", "arbitrary")),
        name="proj",
    )(x, g1, wt, gains, fb)


def _flash_step(scores, values, state):
    new_state = []
    for s_t, v_t, (m_prev, acc_prev) in zip(scores, values, state):
        m_new = jnp.maximum(m_prev, jnp.max(s_t, axis=0, keepdims=True))
        alpha = jnp.exp2(m_prev - m_new)
        p_t = jnp.exp2(s_t - m_new).astype(BF16)
        new_state.append((m_new, alpha * acc_prev + _dot(v_t, p_t)))
    return tuple(new_state)


def _flash_init(n_slots, rows, tq):
    return tuple((jnp.full((1, tq), NEG_INF, F32), jnp.zeros((rows, tq), F32)) for _ in range(n_slots))


def _ones_rows(seq):
    return jnp.where(lax.broadcasted_iota(jnp.int32, (BF16_ROWS, seq), 0) == 0, 1.0, 0.0).astype(BF16)


def _key_minus_query(tk, tq):
    return lax.broadcasted_iota(jnp.int32, (tk, tq), 0) - lax.broadcasted_iota(jnp.int32, (tk, tq), 1)


class _ScoreRing:
    def __init__(self, s_ref, qk):
        self.s_ref, self.qk = s_ref, qk

    def fill(self, j, slot):
        for idx, s in enumerate(self.qk(j)):
            self.s_ref[slot, idx] = s

    def scores(self, slot):
        return [self.s_ref[slot, idx] for idx in range(self.s_ref.shape[1])]

    def far_sweep(self, softmax, state, n_pairs):
        self.fill(0, 0)

        def body(jj, st):
            j = 2 * jj
            self.fill(j + 1, 1)
            st = softmax(self.scores(0), j, st)
            self.fill(j + 2, 0)
            return softmax(self.scores(1), j + 1, st)
        return lax.fori_loop(0, n_pairs, body, state)

    def tail(self, softmax, state, first, kinds, more_follows):
        assert not more_follows or len(kinds) % 2 == 0
        for n, kind in enumerate(kinds):
            slot = n % 2
            if n + 1 < len(kinds) or more_follows:
                self.fill(first + n + 1, 1 - slot)
            state = softmax(self.scores(slot), first + n, state, kind)
        return state


def _diff_attn_kernel(rb_ref, q_ref, k_ref, v_ref, lam_ref, subg_ref, o_ref,
                      kb_ref, vb_ref, bias_ref, s_ref, *, n_heads, head_dim, lambda_init):
    tq, tk = ATTN_TQ, ATTN_TK
    ratio = tq // tk
    n_hd = vb_ref.shape[0]
    head0 = n_hd * pl.program_id(0)
    seq = q_ref.shape[2]
    e = 2 * head_dim
    kmq = _key_minus_query(tk, tq)
    near = [tk] + [-d * tk for d in range(ratio)]

    @pl.when(pl.program_id(1) == 0)
    def _():
        for hd in range(n_hd):
            far = rb_ref[(N_BUCKETS - 1) * n_heads + head0 + hd]
            for idx, delta in enumerate(near):
                bucket = _rel_bucket(jnp.maximum(delta - kmq, 0))
                val = jnp.zeros((tk, tq), F32)
                for b in range(N_BUCKETS):
                    val = jnp.where(bucket == b, rb_ref[b * n_heads + head0 + hd], val)
                bias_ref[hd, idx] = (val - far) * LOG2E

    first = lax.broadcasted_iota(jnp.int32, (1, e), 1) < head_dim
    for hd in range(n_hd):
        k_rows = k_ref[0, hd * e:(hd + 1) * e, :].T
        kb_ref[2 * hd] = jnp.where(first, k_rows, 0.0).astype(BF16)
        kb_ref[2 * hd + 1] = jnp.where(first, 0.0, k_rows).astype(BF16)
        vb_ref[hd, 0:e, :] = v_ref[0, hd * e:(hd + 1) * e, :]
        vb_ref[hd, e:, :] = _ones_rows(seq)
    lam = _lambda_value(lam_ref, lambda_init)

    def q_block(i, carry):
        qs = pl.multiple_of(i * tq, tq)
        q_t = [q_ref[0, hd * e:(hd + 1) * e, pl.ds(qs, tq)] for hd in range(n_hd)]

        def qk(j):
            ks = pl.multiple_of(j * tk, tk)
            return [_dot(kb_ref[mp, pl.ds(ks, tk), :], q_t[mp // 2]) for mp in range(2 * n_hd)]

        def softmax(scores, j, state, near_idx=None):
            if near_idx is not None:
                scores = [s + bias_ref[mp // 2, near_idx] for mp, s in enumerate(scores)]
                if near[near_idx] <= 0:
                    visible = kmq <= near[near_idx]
                    scores = [jnp.where(visible, s, NEG_INF) for s in scores]
            ks = pl.multiple_of(j * tk, tk)
            values = [vb_ref[mp // 2, :, pl.ds(ks, tk)] for mp in range(2 * n_hd)]
            return _flash_step(scores, values, state)

        ring = _ScoreRing(s_ref, qk)
        first_near = ratio * (i - 1)
        state = ring.far_sweep(softmax, _flash_init(2 * n_hd, vb_ref.shape[1], tq),
                               jnp.maximum(first_near, 0) // 2)
        before = [None] * (ratio - 1) + [0]
        state = lax.cond(i >= 1, lambda st: ring.tail(softmax, st, first_near, before, True),
                         lambda st: st, state)
        state = ring.tail(softmax, state, ratio * i, [1 + d for d in range(ratio)], False)
        outs = []
        for hd in range(n_hd):
            (_, acc0), (_, acc1) = state[2 * hd:2 * hd + 2]
            o_t = acc0[0:e] / acc0[e:e + 1] - lam * (acc1[0:e] / acc1[e:e + 1])
            outs.append(_rms(o_t.T, subg_ref[...]) * (1.0 - lambda_init))
        o_ref[0, pl.ds(qs, tq), :] = jnp.concatenate(outs, axis=1).astype(o_ref.dtype)
        return carry

    lax.fori_loop(0, seq // tq, q_block, 0)


def _diff_attention(rb_flat, q_t, k_t, v, lam_p, subg, *, n_heads, head_dim, lambda_init):
    b, w, s = q_t.shape
    e = 2 * head_dim
    tq, tk = ATTN_TQ, ATTN_TK
    assert s % tq == 0 and tq % tk == 0 and tk % LANES == 0 and tk >= MAX_DISTANCE and e == LANES
    n_hd = DIFF_HEADS_PER_STEP
    assert n_heads % n_hd == 0
    t_blk = pl.BlockSpec((1, n_hd * e, s), lambda g, bi, *_: (bi, g, 0))
    r_blk = pl.BlockSpec((1, s, n_hd * e), lambda g, bi, *_: (bi, 0, g))
    return pl.pallas_call(
        functools.partial(_diff_attn_kernel, n_heads=n_heads, head_dim=head_dim, lambda_init=lambda_init),
        out_shape=jax.ShapeDtypeStruct((b, s, w), BF16),
        grid_spec=pltpu.PrefetchScalarGridSpec(
            num_scalar_prefetch=1,
            grid=(n_heads // n_hd, b),
            in_specs=[t_blk, t_blk, t_blk, _resident(lam_p.shape), _resident(subg.shape)],
            out_specs=r_blk,
            scratch_shapes=[pltpu.VMEM((2 * n_hd, s, e), BF16), pltpu.VMEM((n_hd, e + BF16_ROWS, s), BF16),
                            pltpu.VMEM((n_hd, tq // tk + 1, tk, tq), F32),
                            pltpu.VMEM((2, 2 * n_hd, tk, tq), F32)]),
        compiler_params=_params(("arbitrary", "arbitrary")),
        name="diff_attention",
    )(rb_flat, q_t, k_t, v, lam_p, subg)


def _fox_attn_kernel(q_ref, k_ref, v_ref, logf_ref, o_ref, kb_ref, vb_ref, c_ref, s_ref, *, head_dim):
    tq, tk = ATTN_TQ, ATTN_TK
    ratio = tq // tk
    group = pl.program_id(1)
    seq = q_ref.shape[2]
    e = 2 * head_dim
    n_h = kb_ref.shape[0]
    kmq = _key_minus_query(tk, tq)

    @pl.when(group == 0)
    def _():
        upper = (lax.broadcasted_iota(jnp.int32, (tk, tk), 0)
                 <= lax.broadcasted_iota(jnp.int32, (tk, tk), 1)).astype(F32)
        carry = jnp.zeros((logf_ref.shape[1], 1), F32)
        for blk in range(seq // tk):
            c = _dot(logf_ref[0, :, blk * tk:(blk + 1) * tk], upper, HIGHEST) + carry
            c_ref[:, blk * tk:(blk + 1) * tk] = c
            carry = c[:, tk - 1:tk]

    lane = lax.broadcasted_iota(jnp.int32, (1, e), 1)
    for hh in range(n_h):
        pair, half = divmod(hh, 2)
        k_rows = k_ref[0, pair * e:(pair + 1) * e, :].T
        spare = (1 - half) * head_dim
        rest = jnp.broadcast_to(c_ref[pl.ds(n_h * group + hh, 1), :] * LOG2E, (LANES, seq)).T
        k_aug = jnp.where(lane // head_dim == half, k_rows, 0.0)
        for part in range(DECAY_PARTS):
            piece = rest.astype(BF16).astype(F32)
            k_aug = jnp.where(lane == spare + part, piece, k_aug)
            rest = rest - piece
        kb_ref[hh] = k_aug.astype(BF16)
        vb_ref[hh, 0:head_dim, :] = v_ref[0, hh * head_dim:(hh + 1) * head_dim, :].astype(BF16)
        vb_ref[hh, head_dim:, :] = _ones_rows(seq)

    def q_block(i, carry):
        qs = pl.multiple_of(i * tq, tq)
        row = lax.broadcasted_iota(jnp.int32, (e, 1), 0)
        q_aug = []
        for hh in range(n_h):
            pair, half = divmod(hh, 2)
            q_t = q_ref[0, pair * e:(pair + 1) * e, pl.ds(qs, tq)]
            spare = (1 - half) * head_dim
            minus_one = jnp.logical_and(row >= spare, row < spare + DECAY_PARTS)
            q_aug.append(jnp.where(row // head_dim == half, q_t,
                                   jnp.where(minus_one, -1.0, 0.0).astype(BF16)))

        def qk(j):
            ks = pl.multiple_of(j * tk, tk)
            return [_dot(kb_ref[hh, pl.ds(ks, tk), :], q_aug[hh]) for hh in range(n_h)]

        def softmax(scores, j, state, diagonal=None):
            ks = pl.multiple_of(j * tk, tk)
            if diagonal is not None:
                visible = kmq <= -diagonal * tk
                scores = [jnp.where(visible, s, NEG_INF) for s in scores]
            values = [vb_ref[hh, :, pl.ds(ks, tk)] for hh in range(n_h)]
            return _flash_step(scores, values, state)

        ring = _ScoreRing(s_ref, qk)
        state = ring.far_sweep(softmax, _flash_init(n_h, vb_ref.shape[1], tq), ratio * i // 2)
        state = ring.tail(softmax, state, ratio * i, list(range(ratio)), False)
        o_t = jnp.concatenate([acc[0:head_dim] / acc[head_dim:head_dim + 1] for _, acc in state], axis=0)
        o_ref[0, pl.ds(qs, tq), :] = o_t.T.astype(o_ref.dtype)
        return carry

    lax.fori_loop(0, seq // tq, q_block, 0)


def _fox_attention(q_t, k_t, v_t, logf_t, *, head_dim):
    b, w, s = q_t.shape
    e = 2 * head_dim
    tq, tk = ATTN_TQ, ATTN_TK
    n_f = logf_t.shape[1]
    assert s % tq == 0 and tq % tk == 0 and tk % LANES == 0 and e == LANES and w // e * 2 == n_f
    assert DECAY_PARTS <= head_dim
    cols = FOX_PAIRS_PER_STEP * e
    n_h = 2 * FOX_PAIRS_PER_STEP
    assert w % cols == 0
    t_blk = pl.BlockSpec((1, cols, s), lambda bi, g: (bi, g, 0))
    return pl.pallas_call(
        functools.partial(_fox_attn_kernel, head_dim=head_dim),
        out_shape=jax.ShapeDtypeStruct((b, s, w), BF16),
        grid=(b, w // cols),
        in_specs=[t_blk, t_blk, t_blk, pl.BlockSpec((1, n_f, s), lambda bi, g: (bi, 0, 0))],
        out_specs=pl.BlockSpec((1, s, cols), lambda bi, g: (bi, 0, g)),
        scratch_shapes=[pltpu.VMEM((n_h, s, e), BF16), pltpu.VMEM((n_h, head_dim + BF16_ROWS, s), BF16),
                        pltpu.VMEM((n_f, s), F32), pltpu.VMEM((2, n_h, tk, tq), F32)],
        compiler_params=_params(("arbitrary", "arbitrary")),
        name="fox_attention",
    )(q_t, k_t, v_t, logf_t)


class _Rider(NamedTuple):
    first: Any
    second: Any
    operands: tuple
    in_specs: tuple
    out_shape: Any
    out_spec: Any


def _decode_step_kernel(pt_ref, qd_ref, kd_ref, vd_ref, qf_ref, kf_ref, vf_ref, lfn_ref,
                        rbt_ref, lam_ref, subg_ref, *rest,
                        n_pages, page_size, head_dim, lambda_init, rider, n_rider_in):
    rider_in, rest = rest[:n_rider_in], rest[n_rider_in:]
    (hbm_dk, hbm_dv, hbm_fk, hbm_fv, hbm_lf), rest = rest[:5], rest[5:]
    n_out = len(rest) - 7
    od_ref, of_ref, *rider_out = rest[:n_out]
    kd_buf, vd_buf, kf_buf, vf_buf, lf_buf, sems, bias_ref = rest[n_out:]
    req = pl.program_id(0)
    last = pl.num_programs(0) - 1
    n_maps = bias_ref.shape[0]
    key_side = ((hbm_dk, kd_buf), (hbm_fk, kf_buf), (hbm_lf, lf_buf)), sems.at[0]
    value_side = ((hbm_dv, vd_buf), (hbm_fv, vf_buf)), sems.at[1]

    def copies(request, side):
        pairs, sem = side
        for p in range(n_pages):
            page = pt_ref[request, n_pages - 1 - p]
            for cache, buf in pairs:
                yield pltpu.make_async_copy(cache.at[page], buf.at[p], sem)

    def start(request, side):
        for copy in copies(request, side):
            copy.start()

    def wait(request, side):
        for copy in copies(request, side):
            copy.wait()

    @pl.when(req == 0)
    def _():
        start(0, key_side)
        lane = lax.broadcasted_iota(jnp.int32, (n_maps, page_size), 1)
        bucket = _rel_bucket(page_size - lane)
        val = jnp.zeros((n_maps, page_size), F32)
        for b in range(N_BUCKETS):
            val = jnp.where(bucket == b, rbt_ref[:, b:b + 1], val)
        bias_ref[...] = (val - rbt_ref[:, N_BUCKETS - 1:N_BUCKETS]) * LOG2E

    start(req, value_side)
    wait(req, key_side)

    def scores_done():
        @pl.when(req < last)
        def _():
            start(req + 1, key_side)

    cdk, cdv, cfk, cfv, clf = ([buf.at[pl.ds(p, 1)] for p in range(n_pages)]
                               for buf in (kd_buf, vd_buf, kf_buf, vf_buf, lf_buf))
    width = qd_ref.shape[-1]
    n_maps = width // head_dim
    n_dh = n_maps // 2
    e = 2 * head_dim
    sub = req % qd_ref.shape[0]
    new_row = lambda ref: ref[pl.ds(sub, 1), :]

    map_of_lane = lax.broadcasted_iota(jnp.int32, (n_maps, width), 1) // head_dim
    row = lax.broadcasted_iota(jnp.int32, (n_maps, width), 0)
    own = map_of_lane == row
    row_e = lax.broadcasted_iota(jnp.int32, (n_maps, e), 0)
    far_bias = rbt_ref[:, N_BUCKETS - 1:N_BUCKETS]

    qd = jnp.where(own, new_row(qd_ref), 0.0).astype(BF16)
    qf = jnp.where(own, new_row(qf_ref), 0.0).astype(BF16)

    def self_score(q, k_ref):
        k = new_row(k_ref).astype(BF16).astype(F32)
        return jnp.sum(q.astype(F32) * k, axis=-1, keepdims=True)

    def softmax(s, s_self):
        m = jnp.maximum(jnp.max(s, axis=-1, keepdims=True), s_self)
        p, p_self = jnp.exp2(s - m), jnp.exp2(s_self - m)
        return p, p_self, jnp.sum(p, axis=-1, keepdims=True) + p_self

    def keys_side_by_side(refs):
        return jnp.concatenate([ref[0].astype(BF16) for ref in refs], axis=1)

    bias = jnp.concatenate([bias_ref[...], jnp.zeros((n_maps, (n_pages - 1) * page_size), F32)], axis=1)
    s_d = _dot(qd, keys_side_by_side(cdk)) + bias
    later = (lax.broadcasted_iota(jnp.int32, (page_size, page_size), 0)
             >= lax.broadcasted_iota(jnp.int32, (page_size, page_size), 1)).astype(F32)
    lane = lax.broadcasted_iota(jnp.int32, lfn_ref.shape[1:], 1)
    carry = jnp.sum(jnp.where(lane == req, lfn_ref[0], 0.0), axis=1, keepdims=True) * LOG2E
    lf = jnp.concatenate([ref[0] for ref in clf], axis=0) * LOG2E
    incl = _dot(lf, later, HIGHEST)
    decay = []
    for p_idx in range(n_pages):
        rows = slice(p_idx * n_maps, (p_idx + 1) * n_maps)
        decay.append(incl[rows] - lf[rows] + carry)
        carry = carry + incl[rows, 0:1]
    s_f = _dot(qf, keys_side_by_side(cfk)) + jnp.concatenate(decay, axis=1)
    scores_done()
    rider_carry = rider.first(*rider_in) if rider is not None else None

    wait(req, value_side)
    if rider is not None:
        rider.second(rider_carry, *rider_in, *rider_out)

    p, p_self, l = softmax(s_d, self_score(qd, kd_ref) + (rbt_ref[:, 0:1] - far_bias) * LOG2E)
    p = p.astype(BF16)
    v_new = vd_ref[pl.ds(sub * n_dh, n_dh), :]
    acc = jnp.zeros((n_maps, e), F32)
    for h in range(n_dh):
        v_h = jnp.concatenate([ref[0, pl.ds(h, page_size, stride=n_dh), :].astype(BF16) for ref in cdv],
                              axis=0)
        acc = jnp.where(row_e // 2 == h, _dot(p, v_h) + p_self * v_new[h:h + 1, :], acc)
    lam = _lambda_value(lam_ref, lambda_init)
    sign = jnp.where(row_e[:, 0:1] % 2 == 0, 1.0, -lam)
    pick = (lax.broadcasted_iota(jnp.int32, (n_dh, n_maps), 1) // 2
            == lax.broadcasted_iota(jnp.int32, (n_dh, n_maps), 0)).astype(F32)
    scaled = acc / l * sign

    p, p_self, l = softmax(s_f, self_score(qf, kf_ref))
    acc = jnp.zeros((n_maps, head_dim, page_size), F32)
    for p_idx in range(n_pages):
        p_page = p[:, p_idx * page_size:(p_idx + 1) * page_size]
        acc = acc + p_page[:, None, :] * cfv[p_idx][0].reshape(n_maps, head_dim, page_size)

    def spread(col):
        return jnp.sum(jnp.where(own, col, 0.0), axis=0, keepdims=True)
    past = jnp.sum(acc.reshape(width, page_size).T, axis=0, keepdims=True)
    of_ref[pl.ds(sub, 1), :] = (past + spread(p_self) * new_row(vf_ref)) / spread(l)

    o_d = _dot(pick, scaled, HIGHEST)
    o_d = _rms(o_d, subg_ref[...]) * (1.0 - lambda_init)
    od_ref[pl.ds(sub, 1), :] = jnp.concatenate([o_d[h:h + 1] for h in range(n_dh)], axis=1)


def _decode_requests(page_table, qd, kd, vd, qf, kf, vf, logf_new, rbt, lam_p, subg,
                     cdk, cdv, cfk, cfv, clf, *, head_dim, lambda_init, rider=None):
    r, n_pages = page_table.shape
    _, width, page_size = cdk.shape
    n_f = clf.shape[1]
    n_maps = width // head_dim
    n_dh = n_maps // 2
    e = 2 * head_dim
    assert n_maps == n_f and r % SUBLANES == 0
    assert page_size >= MAX_DISTANCE
    row = pl.BlockSpec((SUBLANES, width), lambda i, pt: (i // SUBLANES, 0))
    head_rows = pl.BlockSpec((SUBLANES * n_dh, e), lambda i, pt: (i // SUBLANES, 0))
    caches = [cdk, cdv, cfk, cfv, clf]

    in_specs = [row, row, head_rows, row, row, row, _resident(logf_new.shape),
                _resident(rbt.shape), _resident(lam_p.shape), _resident(subg.shape)]
    operands = [qd, kd, vd, qf, kf, vf, logf_new, rbt, lam_p, subg]
    out_shape = [jax.ShapeDtypeStruct((r, width), F32)] * 2
    out_specs = [row, row]
    if rider is not None:
        in_specs += list(rider.in_specs)
        operands += list(rider.operands)
        out_shape.append(rider.out_shape)
        out_specs.append(rider.out_spec)
    in_specs += [pl.BlockSpec(memory_space=pl.ANY)] * len(caches)
    operands += caches
    return pl.pallas_call(
        functools.partial(_decode_step_kernel, n_pages=n_pages, page_size=page_size,
                          head_dim=head_dim, lambda_init=lambda_init,
                          rider=rider, n_rider_in=len(rider.operands) if rider else 0),
        out_shape=out_shape,
        grid_spec=pltpu.PrefetchScalarGridSpec(
            num_scalar_prefetch=1,
            grid=(r,),
            in_specs=in_specs,
            out_specs=out_specs,
            scratch_shapes=[pltpu.VMEM((n_pages,) + c.shape[1:], c.dtype) for c in caches]
                           + [pltpu.SemaphoreType.DMA((2,)), pltpu.VMEM((n_maps, page_size), F32)]),
        compiler_params=_params(("arbitrary",)),
        name="decode_attention",
    )(page_table, *operands)


def _merge_rows(x, od_ref, of_ref, g1_ref, wg_ref, gb_ref, wa_ref, wb_ref, wo_ref):
    d = x.shape[-1]
    dt = wg_ref.dtype
    h = _rms(x, g1_ref[...]).astype(dt)
    ya = _dot(od_ref[...].astype(dt), wa_ref[...])
    yb = _dot(of_ref[...].astype(dt), wb_ref[...])
    gate_a = jax.nn.sigmoid(_dot(h, wg_ref[:, :d]) + gb_ref[:, :d])
    gate_b = jax.nn.sigmoid(_dot(h, wg_ref[:, d:]) + gb_ref[:, d:])
    merged = gate_a * ya + gate_b * yb
    return x + _dot(merged.astype(dt), wo_ref[...])


def _merge_kernel(x_ref, *refs):
    *in_refs, o_ref = refs
    o_ref[...] = _merge_rows(x_ref[...], *in_refs)


def _merge(x, od, of, g1, wg, gb, wa, wb, wo):
    m, d = x.shape
    tm = min(ROW_TILE, m)
    assert m % tm == 0
    row = lambda c: pl.BlockSpec((tm, c), lambda i: (i, 0))
    return pl.pallas_call(
        _merge_kernel,
        out_shape=jax.ShapeDtypeStruct((m, d), F32),
        grid=(m // tm,),
        in_specs=[row(d), row(od.shape[1]), row(of.shape[1])]
                 + [_resident(a.shape) for a in (g1, wg, gb, wa, wb, wo)],
        out_specs=row(d),
        compiler_params=_params(("arbitrary",)),
        name="merge",
    )(x, od, of, g1, wg, gb, wa, wb, wo)


def _ffn_hidden(x, g2_ref, wgu_ref, wdn_ref):
    d_ff = wdn_ref.shape[0]
    h = _rms(x, g2_ref[...]).astype(wgu_ref.dtype)
    gate = _dot(h, wgu_ref[:, :d_ff])
    up = _dot(h, wgu_ref[:, d_ff:])
    return (gate * jax.nn.sigmoid(gate) * up).astype(wdn_ref.dtype)


def _ffn_kernel(x_ref, g2_ref, wgu_ref, wdn_ref, o_ref):
    x = x_ref[...]
    o_ref[...] = x + _dot(_ffn_hidden(x, g2_ref, wgu_ref, wdn_ref), wdn_ref[...])


N_MERGE_REFS = 9


def _tail_first(*refs):
    x_ref, *merge_refs = refs[:N_MERGE_REFS]
    x1 = _merge_rows(x_ref[...], *merge_refs)
    return x1, _ffn_hidden(x1, *refs[N_MERGE_REFS:])


def _tail_second(carry, *refs):
    x1, act = carry
    wdn_ref, o_ref = refs[-2:]
    o_ref[...] = x1 + _dot(act, wdn_ref[...])


def _ffn(x, g2, wgu, wdn):
    m, d = x.shape
    tm = min(FFN_ROW_TILE, m)
    assert m % tm == 0
    row = pl.BlockSpec((tm, d), lambda i: (i, 0))
    return pl.pallas_call(
        _ffn_kernel,
        out_shape=jax.ShapeDtypeStruct((m, d), F32),
        grid=(m // tm,),
        in_specs=[row, _resident(g2.shape), _resident(wgu.shape), _resident(wdn.shape)],
        out_specs=row,
        compiler_params=_params(("arbitrary",)),
        name="ffn",
    )(x, g2, wgu, wdn)


def kernel(x_prompt, x_sample, cache_diff_k, cache_diff_v, cache_fox_k, cache_fox_v, cache_fox_logf,
           page_table, rel_bias, norm1_g, w_in, diff_q_g, diff_k_g, fox_q_g, fox_k_g, diff_lambda,
           fox_f_b, gate_b, diff_subln_g, w_branch_a, w_branch_b, w_out, norm2_g, w_gate_up, w_down):
    depth = w_in.shape[0]
    batch, seq, d = x_prompt.shape
    dec_batch, dec_seq, _ = x_sample.shape
    assert dec_seq == 1
    _, n_phys, page_size, n_dh, _, head_dim = cache_diff_k.shape
    n_f = cache_fox_k.shape[3]
    e = 2 * head_dim
    width = n_dh * e
    assert n_f * head_dim == width

    rb_flat = rel_bias.reshape(-1)
    rbt = jnp.repeat(rel_bias.T, 2, axis=0)

    yp, ys = x_prompt, x_sample.reshape(1, dec_batch, d)
    rows_p, rows_s = [], []
    for l in range(depth):
        lambda_init = 0.8 - 0.6 * math.exp(-0.3 * l)
        w_t = w_in[l].T
        fb = fox_f_b[l].reshape(n_f, 1)
        gains = jnp.stack([jnp.tile(g[l], width // head_dim)
                           for g in (diff_q_g, diff_k_g, fox_q_g, fox_k_g)]).reshape(4, width, 1)
        g1 = norm1_g[l].reshape(1, d)
        g2 = norm2_g[l].reshape(1, d)
        gb = gate_b[l].reshape(1, 2 * d)
        subg = diff_subln_g[l].reshape(1, e)
        wg = w_in[l][:, 6 * width + n_f:]
        merge_w = (g1, wg.astype(BF16), gb, w_branch_a[l].astype(BF16), w_branch_b[l].astype(BF16),
                   w_out[l].astype(BF16))
        ffn_w = (g2, w_gate_up[l].astype(BF16), w_down[l].astype(BF16))
        lam_p = diff_lambda[l]

        def cache_rows(kd_t, vd, kf_t, vf_t, logf_t):
            b, _, s = kd_t.shape
            return (jnp.transpose(kd_t.reshape(b, n_dh, 2, head_dim, s), (0, 4, 1, 2, 3)),
                    vd.reshape(b, s, n_dh, e),
                    jnp.transpose(kf_t.reshape(b, n_f, head_dim, s), (0, 3, 1, 2)),
                    jnp.transpose(vf_t.reshape(b, n_f, head_dim, s), (0, 3, 1, 2)),
                    jnp.transpose(logf_t, (0, 2, 1)))

        qd_t, kd_t, vd, vd_t, qf_t, kf_t, vf_t, logf_t = _proj(
            yp, g1, w_t, gains, fb, width=width, head_dim=head_dim, with_rows=False)
        od = _diff_attention(rb_flat, qd_t, kd_t, vd_t, lam_p, subg,
                             n_heads=n_dh, head_dim=head_dim, lambda_init=lambda_init)
        of = _fox_attention(qf_t, kf_t, vf_t, logf_t, head_dim=head_dim)
        m_p = batch * seq
        assert m_p % dec_batch == 0
        rows = lambda c: pl.BlockSpec((m_p // dec_batch, c), lambda i, pt: (i, 0))
        tiled = (yp.reshape(m_p, d), od.reshape(m_p, width), of.reshape(m_p, width))
        weights = merge_w + ffn_w
        assert len(tiled) + len(merge_w) == N_MERGE_REFS
        rider = _Rider(_tail_first, _tail_second, tiled + weights,
                       tuple(rows(a.shape[1]) for a in tiled) + tuple(_resident(w.shape) for w in weights),
                       jax.ShapeDtypeStruct((m_p, d), F32), rows(d))
        rows_p.append(cache_rows(kd_t, vd, kf_t, vf_t, logf_t))

        (qd_t, kd_t, vd, _, qf_t, kf_t, vf_t, logf_t, qd_r, kd_r, qf_r, kf_r, vf_r) = _proj(
            ys, g1, w_t, gains, fb, width=width, head_dim=head_dim, with_rows=True)
        as_row = lambda a: a.reshape(dec_batch, width)
        cdk = jnp.transpose(cache_diff_k[l], (0, 2, 3, 4, 1)).reshape(n_phys, width, page_size)
        cdv = cache_diff_v[l].reshape(n_phys, page_size * n_dh, e)
        cfk = jnp.transpose(cache_fox_k[l], (0, 2, 3, 1)).reshape(n_phys, width, page_size)
        cfv = jnp.transpose(cache_fox_v[l], (0, 2, 3, 1)).reshape(n_phys, width, page_size)
        clf = jnp.transpose(cache_fox_logf[l], (0, 2, 1))
        od, of, yp = _decode_requests(
            page_table, as_row(qd_r), as_row(kd_r), vd.reshape(dec_batch * n_dh, e),
            as_row(qf_r), as_row(kf_r), as_row(vf_r), logf_t, rbt, lam_p, subg,
            cdk, cdv, cfk, cfv, clf, head_dim=head_dim, lambda_init=lambda_init, rider=rider)
        yp = yp.reshape(batch, seq, d)
        ys = _ffn(_merge(ys.reshape(dec_batch, d), od, of, *merge_w), *ffn_w).reshape(1, dec_batch, d)
        rows_s.append(tuple(jnp.swapaxes(a, 0, 1) for a in cache_rows(kd_t, vd, kf_t, vf_t, logf_t)))

    stack = lambda rows, i: jnp.stack([r[i] for r in rows], axis=0)
    return (yp, ys.reshape(dec_batch, 1, d),
            *(stack(rows_p, i) for i in range(5)), *(stack(rows_s, i) for i in range(5)))
```

```python
import functools
import math
from typing import Any, NamedTuple

import jax
import jax.numpy as jnp
from jax import lax
from jax.experimental import pallas as pl
from jax.experimental.pallas import tpu as pltpu

F32 = jnp.float32
BF16 = jnp.bfloat16

N_BUCKETS = 32
MAX_DISTANCE = 128
EPS = 1e-6
NEG_INF = -1e30
LOG2E = math.log2(math.e)

LANES = 128
SUBLANES = 8
BF16_ROWS = 16
VMEM_LIMIT = 56 * 1024 * 1024

ROW_TILE = 512
FFN_ROW_TILE = 256
ATTN_TQ = 512
ATTN_TK = 256
FOX_PAIRS_PER_STEP = 2
DIFF_HEADS_PER_STEP = 2
DECAY_PARTS = 3

NT_DIMS = (((1,), (1,)), ((), ()))
HIGHEST = lax.Precision.HIGHEST


def _resident(shape):
    zeros = (0,) * len(shape)
    return pl.BlockSpec(shape, lambda *_: zeros, pipeline_mode=pl.Buffered(1))


def _params(semantics, vmem=VMEM_LIMIT):
    return pltpu.CompilerParams(dimension_semantics=semantics, vmem_limit_bytes=vmem)


def _rms(x, g):
    return x * lax.rsqrt(jnp.mean(x * x, axis=-1, keepdims=True) + EPS) * g


def _dot(a, b, precision=None):
    return jnp.dot(a, b, preferred_element_type=F32, precision=precision)


def _dot_nt(a, b):
    return lax.dot_general(a, b, NT_DIMS, preferred_element_type=F32)


def _log_sigmoid(x):
    return jnp.minimum(x, 0.0) - jnp.log1p(jnp.exp(-jnp.abs(x)))


def _rel_bucket(n):
    max_exact = N_BUCKETS // 2
    nf = jnp.maximum(n, 1).astype(F32)
    large = max_exact + (jnp.log(nf / max_exact) / math.log(MAX_DISTANCE / max_exact)
                         * (N_BUCKETS - max_exact)).astype(jnp.int32)
    return jnp.where(n < max_exact, n, jnp.minimum(large, N_BUCKETS - 1))


def _lambda_value(lam_ref, lambda_init):
    lp = lam_ref[...]
    a = jnp.sum(lp[0:1] * lp[1:2], axis=-1, keepdims=True)
    b = jnp.sum(lp[2:3] * lp[3:4], axis=-1, keepdims=True)
    return jnp.exp(a) - jnp.exp(b) + lambda_init


def _proj_kernel(x_ref, g1_ref, wt_ref, gains_ref, fb_ref,
                 qd_ref, kd_ref, vd_ref, vdt_ref, qf_ref, kf_ref, vf_ref, logf_ref, *row_refs,
                 head_dim, q_scale):
    h = _rms(x_ref[0], g1_ref[...])
    width = kd_ref.shape[1]
    n_heads = width // head_dim
    n_f = logf_ref.shape[1]
    section = lambda i: _dot_nt(wt_ref[i * width:(i + 1) * width, :], h)

    def head_norm_t(z, i):
        z3 = z.reshape(n_heads, head_dim, z.shape[-1])
        ss = jnp.sum(z3 * z3, axis=1, keepdims=True)
        gain = gains_ref[i].reshape(n_heads, head_dim, 1)
        return (z3 * lax.rsqrt(ss * (1.0 / head_dim) + EPS) * gain).reshape(z.shape)

    qd = head_norm_t(section(0), 0) * q_scale
    kd = head_norm_t(section(1), 1)
    qf = head_norm_t(section(3), 2) * q_scale
    kf = head_norm_t(section(4), 3)
    vd_t = section(2)
    vd_rows = vd_t.T
    vf = section(5)
    qd_ref[0] = qd.astype(qd_ref.dtype)
    kd_ref[0] = kd
    e = vd_ref.shape[-1]
    for hd in range(width // e):
        vd_ref[0, pl.ds(hd, vd_rows.shape[0], stride=width // e), :] = vd_rows[:, hd * e:(hd + 1) * e]
    vdt_ref[0] = vd_t.astype(vdt_ref.dtype)
    qf_ref[0] = qf.astype(qf_ref.dtype)
    kf_ref[0] = kf
    vf_ref[0] = vf
    logf_ref[0] = _log_sigmoid(_dot_nt(wt_ref[6 * width:6 * width + n_f, :], h) + fb_ref[...])
    if row_refs:
        for ref, val in zip(row_refs, (qd, kd, qf, kf, vf)):
            ref[0] = val.T


def _proj(x, g1, wt, gains, fb, *, width, head_dim, with_rows):
    b, s, d = x.shape
    n_f = fb.shape[0]
    e = 2 * head_dim
    tm = min(ROW_TILE, s)
    assert s % tm == 0 and tm % LANES == 0 and width % e == 0
    t_blk = lambda r: pl.BlockSpec((1, r, tm), lambda bi, si: (bi, 0, si))
    r_blk = lambda c: pl.BlockSpec((1, tm, c), lambda bi, si: (bi, si, 0))
    t_shape = lambda dt: jax.ShapeDtypeStruct((b, width, s), dt)
    r_shape = jax.ShapeDtypeStruct((b, s, width), F32)
    heads_per_token = width // e
    out_shape = [t_shape(BF16), t_shape(F32), jax.ShapeDtypeStruct((b, s * heads_per_token, e), F32),
                 t_shape(BF16), t_shape(BF16), t_shape(F32), t_shape(F32),
                 jax.ShapeDtypeStruct((b, n_f, s), F32)]
    out_specs = [t_blk(width), t_blk(width),
                 pl.BlockSpec((1, tm * heads_per_token, e), lambda bi, si: (bi, si, 0)),
                 t_blk(width), t_blk(width), t_blk(width), t_blk(width), t_blk(n_f)]
    if with_rows:
        out_shape += [r_shape] * 5
        out_specs += [r_blk(width)] * 5
    return pl.pallas_call(
        functools.partial(_proj_kernel, head_dim=head_dim, q_scale=head_dim ** -0.5 * LOG2E),
        out_shape=out_shape,
        grid=(b, s // tm),
        in_specs=[r_blk(d)] + [_resident(a.shape) for a in (g1, wt, gains, fb)],
        out_specs=out_specs,
        compiler_params=_params(("arbitrary", "arbitrary")),
        name="proj",
    )(x, g1, wt, gains, fb)


def _flash_step(scores, values, state):
    new_state = []
    for s_t, v_t, (m_prev, acc_prev) in zip(scores, values, state):
        m_new = jnp.maximum(m_prev, jnp.max(s_t, axis=0, keepdims=True))
        alpha = jnp.exp2(m_prev - m_new)
        p_t = jnp.exp2(s_t - m_new).astype(BF16)
        new_state.append((m_new, alpha * acc_prev + _dot(v_t, p_t)))
    return tuple(new_state)


def _flash_init(n_slots, rows, tq):
    return tuple((jnp.full((1, tq), NEG_INF, F32), jnp.zeros((rows, tq), F32)) for _ in range(n_slots))


def _ones_rows(seq):
    return jnp.where(lax.broadcasted_iota(jnp.int32, (BF16_ROWS, seq), 0) == 0, 1.0, 0.0).astype(BF16)


def _key_minus_query(tk, tq):
    return lax.broadcasted_iota(jnp.int32, (tk, tq), 0) - lax.broadcasted_iota(jnp.int32, (tk, tq), 1)


class _ScoreRing:
    def __init__(self, s_ref, qk):
        self.s_ref, self.qk = s_ref, qk

    def fill(self, j, slot):
        for idx, s in enumerate(self.qk(j)):
            self.s_ref[slot, idx] = s

    def scores(self, slot):
        return [self.s_ref[slot, idx] for idx in range(self.s_ref.shape[1])]

    def far_sweep(self, softmax, state, n_pairs):
        self.fill(0, 0)

        def body(jj, st):
            j = 2 * jj
            self.fill(j + 1, 1)
            st = softmax(self.scores(0), j, st)
            self.fill(j + 2, 0)
            return softmax(self.scores(1), j + 1, st)
        return lax.fori_loop(0, n_pairs, body, state)

    def tail(self, softmax, state, first, kinds, more_follows):
        assert not more_follows or len(kinds) % 2 == 0
        for n, kind in enumerate(kinds):
            slot = n % 2
            if n + 1 < len(kinds) or more_follows:
                self.fill(first + n + 1, 1 - slot)
            state = softmax(self.scores(slot), first + n, state, kind)
        return state


def _diff_attn_kernel(rb_ref, q_ref, k_ref, v_ref, lam_ref, subg_ref, o_ref,
                      kb_ref, vb_ref, bias_ref, s_ref, *, n_heads, head_dim, lambda_init):
    tq, tk = ATTN_TQ, ATTN_TK
    ratio = tq // tk
    n_hd = vb_ref.shape[0]
    head0 = n_hd * pl.program_id(0)
    seq = q_ref.shape[2]
    e = 2 * head_dim
    kmq = _key_minus_query(tk, tq)
    near = [tk] + [-d * tk for d in range(ratio)]

    @pl.when(pl.program_id(1) == 0)
    def _():
        for hd in range(n_hd):
            far = rb_ref[(N_BUCKETS - 1) * n_heads + head0 + hd]
            for idx, delta in enumerate(near):
                bucket = _rel_bucket(jnp.maximum(delta - kmq, 0))
                val = jnp.zeros((tk, tq), F32)
                for b in range(N_BUCKETS):
                    val = jnp.where(bucket == b, rb_ref[b * n_heads + head0 + hd], val)
                bias_ref[hd, idx] = (val - far) * LOG2E

    first = lax.broadcasted_iota(jnp.int32, (1, e), 1) < head_dim
    for hd in range(n_hd):
        k_rows = k_ref[0, hd * e:(hd + 1) * e, :].T
        kb_ref[2 * hd] = jnp.where(first, k_rows, 0.0).astype(BF16)
        kb_ref[2 * hd + 1] = jnp.where(first, 0.0, k_rows).astype(BF16)
        vb_ref[hd, 0:e, :] = v_ref[0, hd * e:(hd + 1) * e, :]
        vb_ref[hd, e:, :] = _ones_rows(seq)
    lam = _lambda_value(lam_ref, lambda_init)

    def q_block(i, carry):
        qs = pl.multiple_of(i * tq, tq)
        q_t = [q_ref[0, hd * e:(hd + 1) * e, pl.ds(qs, tq)] for hd in range(n_hd)]

        def qk(j):
            ks = pl.multiple_of(j * tk, tk)
            return [_dot(kb_ref[mp, pl.ds(ks, tk), :], q_t[mp // 2]) for mp in range(2 * n_hd)]

        def softmax(scores, j, state, near_idx=None):
            if near_idx is not None:
                scores = [s + bias_ref[mp // 2, near_idx] for mp, s in enumerate(scores)]
                if near[near_idx] <= 0:
                    visible = kmq <= near[near_idx]
                    scores = [jnp.where(visible, s, NEG_INF) for s in scores]
            ks = pl.multiple_of(j * tk, tk)
            values = [vb_ref[mp // 2, :, pl.ds(ks, tk)] for mp in range(2 * n_hd)]
            return _flash_step(scores, values, state)

        ring = _ScoreRing(s_ref, qk)
        first_near = ratio * (i - 1)
        state = ring.far_sweep(softmax, _flash_init(2 * n_hd, vb_ref.shape[1], tq),
                               jnp.maximum(first_near, 0) // 2)
        before = [None] * (ratio - 1) + [0]
        state = lax.cond(i >= 1, lambda st: ring.tail(softmax, st, first_near, before, True),
                         lambda st: st, state)
        state = ring.tail(softmax, state, ratio * i, [1 + d for d in range(ratio)], False)
        outs = []
        for hd in range(n_hd):
            (_, acc0), (_, acc1) = state[2 * hd:2 * hd + 2]
            o_t = acc0[0:e] / acc0[e:e + 1] - lam * (acc1[0:e] / acc1[e:e + 1])
            outs.append(_rms(o_t.T, subg_ref[...]) * (1.0 - lambda_init))
        o_ref[0, pl.ds(qs, tq), :] = jnp.concatenate(outs, axis=1).astype(o_ref.dtype)
        return carry

    lax.fori_loop(0, seq // tq, q_block, 0)


def _diff_attention(rb_flat, q_t, k_t, v, lam_p, subg, *, n_heads, head_dim, lambda_init):
    b, w, s = q_t.shape
    e = 2 * head_dim
    tq, tk = ATTN_TQ, ATTN_TK
    assert s % tq == 0 and tq % tk == 0 and tk % LANES == 0 and tk >= MAX_DISTANCE and e == LANES
    n_hd = DIFF_HEADS_PER_STEP
    assert n_heads % n_hd == 0
    t_blk = pl.BlockSpec((1, n_hd * e, s), lambda g, bi, *_: (bi, g, 0))
    r_blk = pl.BlockSpec((1, s, n_hd * e), lambda g, bi, *_: (bi, 0, g))
    return pl.pallas_call(
        functools.partial(_diff_attn_kernel, n_heads=n_heads, head_dim=head_dim, lambda_init=lambda_init),
        out_shape=jax.ShapeDtypeStruct((b, s, w), BF16),
        grid_spec=pltpu.PrefetchScalarGridSpec(
            num_scalar_prefetch=1,
            grid=(n_heads // n_hd, b),
            in_specs=[t_blk, t_blk, t_blk, _resident(lam_p.shape), _resident(subg.shape)],
            out_specs=r_blk,
            scratch_shapes=[pltpu.VMEM((2 * n_hd, s, e), BF16), pltpu.VMEM((n_hd, e + BF16_ROWS, s), BF16),
                            pltpu.VMEM((n_hd, tq // tk + 1, tk, tq), F32),
                            pltpu.VMEM((2, 2 * n_hd, tk, tq), F32)]),
        compiler_params=_params(("arbitrary", "arbitrary")),
        name="diff_attention",
    )(rb_flat, q_t, k_t, v, lam_p, subg)


def _fox_attn_kernel(q_ref, k_ref, v_ref, logf_ref, o_ref, kb_ref, vb_ref, c_ref, s_ref, *, head_dim):
    tq, tk = ATTN_TQ, ATTN_TK
    ratio = tq // tk
    group = pl.program_id(1)
    seq = q_ref.shape[2]
    e = 2 * head_dim
    n_h = kb_ref.shape[0]
    kmq = _key_minus_query(tk, tq)

    @pl.when(group == 0)
    def _():
        upper = (lax.broadcasted_iota(jnp.int32, (tk, tk), 0)
                 <= lax.broadcasted_iota(jnp.int32, (tk, tk), 1)).astype(F32)
        carry = jnp.zeros((logf_ref.shape[1], 1), F32)
        for blk in range(seq // tk):
            c = _dot(logf_ref[0, :, blk * tk:(blk + 1) * tk], upper, HIGHEST) + carry
            c_ref[:, blk * tk:(blk + 1) * tk] = c
            carry = c[:, tk - 1:tk]

    lane = lax.broadcasted_iota(jnp.int32, (1, e), 1)
    for hh in range(n_h):
        pair, half = divmod(hh, 2)
        k_rows = k_ref[0, pair * e:(pair + 1) * e, :].T
        spare = (1 - half) * head_dim
        rest = jnp.broadcast_to(c_ref[pl.ds(n_h * group + hh, 1), :] * LOG2E, (LANES, seq)).T
        k_aug = jnp.where(lane // head_dim == half, k_rows, 0.0)
        for part in range(DECAY_PARTS):
            piece = rest.astype(BF16).astype(F32)
            k_aug = jnp.where(lane == spare + part, piece, k_aug)
            rest = rest - piece
        kb_ref[hh] = k_aug.astype(BF16)
        vb_ref[hh, 0:head_dim, :] = v_ref[0, hh * head_dim:(hh + 1) * head_dim, :].astype(BF16)
        vb_ref[hh, head_dim:, :] = _ones_rows(seq)

    def q_block(i, carry):
        qs = pl.multiple_of(i * tq, tq)
        row = lax.broadcasted_iota(jnp.int32, (e, 1), 0)
        q_aug = []
        for hh in range(n_h):
            pair, half = divmod(hh, 2)
            q_t = q_ref[0, pair * e:(pair + 1) * e, pl.ds(qs, tq)]
            spare = (1 - half) * head_dim
            minus_one = jnp.logical_and(row >= spare, row < spare + DECAY_PARTS)
            q_aug.append(jnp.where(row // head_dim == half, q_t,
                                   jnp.where(minus_one, -1.0, 0.0).astype(BF16)))

        def qk(j):
            ks = pl.multiple_of(j * tk, tk)
            return [_dot(kb_ref[hh, pl.ds(ks, tk), :], q_aug[hh]) for hh in range(n_h)]

        def softmax(scores, j, state, diagonal=None):
            ks = pl.multiple_of(j * tk, tk)
            if diagonal is not None:
                visible = kmq <= -diagonal * tk
                scores = [jnp.where(visible, s, NEG_INF) for s in scores]
            values = [vb_ref[hh, :, pl.ds(ks, tk)] for hh in range(n_h)]
            return _flash_step(scores, values, state)

        ring = _ScoreRing(s_ref, qk)
        state = ring.far_sweep(softmax, _flash_init(n_h, vb_ref.shape[1], tq), ratio * i // 2)
        state = ring.tail(softmax, state, ratio * i, list(range(ratio)), False)
        o_t = jnp.concatenate([acc[0:head_dim] / acc[head_dim:head_dim + 1] for _, acc in state], axis=0)
        o_ref[0, pl.ds(qs, tq), :] = o_t.T.astype(o_ref.dtype)
        return carry

    lax.fori_loop(0, seq // tq, q_block, 0)


def _fox_attention(q_t, k_t, v_t, logf_t, *, head_dim):
    b, w, s = q_t.shape
    e = 2 * head_dim
    tq, tk = ATTN_TQ, ATTN_TK
    n_f = logf_t.shape[1]
    assert s % tq == 0 and tq % tk == 0 and tk % LANES == 0 and e == LANES and w // e * 2 == n_f
    assert DECAY_PARTS <= head_dim
    cols = FOX_PAIRS_PER_STEP * e
    n_h = 2 * FOX_PAIRS_PER_STEP
    assert w % cols == 0
    t_blk = pl.BlockSpec((1, cols, s), lambda bi, g: (bi, g, 0))
    return pl.pallas_call(
        functools.partial(_fox_attn_kernel, head_dim=head_dim),
        out_shape=jax.ShapeDtypeStruct((b, s, w), BF16),
        grid=(b, w // cols),
        in_specs=[t_blk, t_blk, t_blk, pl.BlockSpec((1, n_f, s), lambda bi, g: (bi, 0, 0))],
        out_specs=pl.BlockSpec((1, s, cols), lambda bi, g: (bi, 0, g)),
        scratch_shapes=[pltpu.VMEM((n_h, s, e), BF16), pltpu.VMEM((n_h, head_dim + BF16_ROWS, s), BF16),
                        pltpu.VMEM((n_f, s), F32), pltpu.VMEM((2, n_h, tk, tq), F32)],
        compiler_params=_params(("arbitrary", "arbitrary")),
        name="fox_attention",
    )(q_t, k_t, v_t, logf_t)


class _Rider(NamedTuple):
    first: Any
    second: Any
    operands: tuple
    in_specs: tuple
    out_shape: Any
    out_spec: Any


def _decode_step_kernel(pt_ref, qd_ref, kd_ref, vd_ref, qf_ref, kf_ref, vf_ref, lfn_ref,
                        rbt_ref, lam_ref, subg_ref, *rest,
                        n_pages, page_size, head_dim, lambda_init, rider, n_rider_in):
    rider_in, rest = rest[:n_rider_in], rest[n_rider_in:]
    (hbm_dk, hbm_dv, hbm_fk, hbm_fv, hbm_lf), rest = rest[:5], rest[5:]
    n_out = len(rest) - 7
    od_ref, of_ref, *rider_out = rest[:n_out]
    kd_buf, vd_buf, kf_buf, vf_buf, lf_buf, sems, bias_ref = rest[n_out:]
    req = pl.program_id(0)
    last = pl.num_programs(0) - 1
    n_maps = bias_ref.shape[0]
    key_side = ((hbm_dk, kd_buf), (hbm_fk, kf_buf), (hbm_lf, lf_buf)), sems.at[0]
    value_side = ((hbm_dv, vd_buf), (hbm_fv, vf_buf)), sems.at[1]

    def copies(request, side):
        pairs, sem = side
        for p in range(n_pages):
            page = pt_ref[request, n_pages - 1 - p]
            for cache, buf in pairs:
                yield pltpu.make_async_copy(cache.at[page], buf.at[p], sem)

    def start(request, side):
        for copy in copies(request, side):
            copy.start()

    def wait(request, side):
        for copy in copies(request, side):
            copy.wait()

    @pl.when(req == 0)
    def _():
        start(0, key_side)
        lane = lax.broadcasted_iota(jnp.int32, (n_maps, page_size), 1)
        bucket = _rel_bucket(page_size - lane)
        val = jnp.zeros((n_maps, page_size), F32)
        for b in range(N_BUCKETS):
            val = jnp.where(bucket == b, rbt_ref[:, b:b + 1], val)
        bias_ref[...] = (val - rbt_ref[:, N_BUCKETS - 1:N_BUCKETS]) * LOG2E

    start(req, value_side)
    wait(req, key_side)

    cdk, cdv, cfk, cfv, clf = ([buf.at[pl.ds(p, 1)] for p in range(n_pages)]
                               for buf in (kd_buf, vd_buf, kf_buf, vf_buf, lf_buf))
    width = qd_ref.shape[-1]
    n_maps = width // head_dim
    n_dh = n_maps // 2
    e = 2 * head_dim
    sub = req % qd_ref.shape[0]
    new_row = lambda ref: ref[pl.ds(sub, 1), :]

    map_of_lane = lax.broadcasted_iota(jnp.int32, (n_maps, width), 1) // head_dim
    row = lax.broadcasted_iota(jnp.int32, (n_maps, width), 0)
    own = map_of_lane == row
    row_e = lax.broadcasted_iota(jnp.int32, (n_maps, e), 0)
    far_bias = rbt_ref[:, N_BUCKETS - 1:N_BUCKETS]

    qd = jnp.where(own, new_row(qd_ref), 0.0).astype(BF16)
    qf = jnp.where(own, new_row(qf_ref), 0.0).astype(BF16)

    def self_score(q, k_ref):
        k = new_row(k_ref).astype(BF16).astype(F32)
        return jnp.sum(q.astype(F32) * k, axis=-1, keepdims=True)

    def softmax(s, s_self):
        m = jnp.maximum(jnp.max(s, axis=-1, keepdims=True), s_self)
        p, p_self = jnp.exp2(s - m), jnp.exp2(s_self - m)
        return p, p_self, jnp.sum(p, axis=-1, keepdims=True) + p_self

    def keys_side_by_side(refs):
        return jnp.concatenate([ref[0].astype(BF16) for ref in refs], axis=1)

    bias = jnp.concatenate([bias_ref[...], jnp.zeros((n_maps, (n_pages - 1) * page_size), F32)], axis=1)
    s_d = _dot(qd, keys_side_by_side(cdk)) + bias
    later = (lax.broadcasted_iota(jnp.int32, (page_size, page_size), 0)
             >= lax.broadcasted_iota(jnp.int32, (page_size, page_size), 1)).astype(F32)
    lane = lax.broadcasted_iota(jnp.int32, lfn_ref.shape[1:], 1)
    carry = jnp.sum(jnp.where(lane == req, lfn_ref[0], 0.0), axis=1, keepdims=True) * LOG2E
    lf = jnp.concatenate([ref[0] for ref in clf], axis=0) * LOG2E
    incl = _dot(lf, later, HIGHEST)
    decay = []
    for p_idx in range(n_pages):
        rows = slice(p_idx * n_maps, (p_idx + 1) * n_maps)
        decay.append(incl[rows] - lf[rows] + carry)
        carry = carry + incl[rows, 0:1]
    s_f = _dot(qf, keys_side_by_side(cfk)) + jnp.concatenate(decay, axis=1)
    rider_carry = rider.first(*rider_in) if rider is not None else None
    start(jnp.minimum(req + 1, last), key_side)

    wait(req, value_side)
    if rider is not None:
        rider.second(rider_carry, *rider_in, *rider_out)

    p, p_self, l = softmax(s_d, self_score(qd, kd_ref) + (rbt_ref[:, 0:1] - far_bias) * LOG2E)
    p = p.astype(BF16)
    v_new = vd_ref[pl.ds(sub * n_dh, n_dh), :]
    acc = jnp.zeros((n_maps, e), F32)
    for h in range(n_dh):
        v_h = jnp.concatenate([ref[0, pl.ds(h, page_size, stride=n_dh), :].astype(BF16) for ref in cdv],
                              axis=0)
        acc = jnp.where(row_e // 2 == h, _dot(p, v_h) + p_self * v_new[h:h + 1, :], acc)
    lam = _lambda_value(lam_ref, lambda_init)
    sign = jnp.where(row_e[:, 0:1] % 2 == 0, 1.0, -lam)
    pick = (lax.broadcasted_iota(jnp.int32, (n_dh, n_maps), 1) // 2
            == lax.broadcasted_iota(jnp.int32, (n_dh, n_maps), 0)).astype(F32)
    scaled = acc / l * sign

    p, p_self, l = softmax(s_f, self_score(qf, kf_ref))
    acc = jnp.zeros((n_maps, head_dim, page_size), F32)
    for p_idx in range(n_pages):
        p_page = p[:, p_idx * page_size:(p_idx + 1) * page_size]
        acc = acc + p_page[:, None, :] * cfv[p_idx][0].reshape(n_maps, head_dim, page_size)

    def spread(col):
        return jnp.sum(jnp.where(own, col, 0.0), axis=0, keepdims=True)
    past = jnp.sum(acc.reshape(width, page_size).T, axis=0, keepdims=True)
    of_ref[pl.ds(sub, 1), :] = (past + spread(p_self) * new_row(vf_ref)) / spread(l)

    o_d = _dot(pick, scaled, HIGHEST)
    o_d = _rms(o_d, subg_ref[...]) * (1.0 - lambda_init)
    od_ref[pl.ds(sub, 1), :] = jnp.concatenate([o_d[h:h + 1] for h in range(n_dh)], axis=1)

    @pl.when(req == last)
    def _():
        wait(last, key_side)


def _decode_requests(page_table, qd, kd, vd, qf, kf, vf, logf_new, rbt, lam_p, subg,
                     cdk, cdv, cfk, cfv, clf, *, head_dim, lambda_init, rider=None):
    r, n_pages = page_table.shape
    _, width, page_size = cdk.shape
    n_f = clf.shape[1]
    n_maps = width // head_dim
    n_dh = n_maps // 2
    e = 2 * head_dim
    assert n_maps == n_f and r % SUBLANES == 0
    assert page_size >= MAX_DISTANCE
    row = pl.BlockSpec((SUBLANES, width), lambda i, pt: (i // SUBLANES, 0))
    head_rows = pl.BlockSpec((SUBLANES * n_dh, e), lambda i, pt: (i // SUBLANES, 0))
    caches = [cdk, cdv, cfk, cfv, clf]

    in_specs = [row, row, head_rows, row, row, row, _resident(logf_new.shape),
                _resident(rbt.shape), _resident(lam_p.shape), _resident(subg.shape)]
    operands = [qd, kd, vd, qf, kf, vf, logf_new, rbt, lam_p, subg]
    out_shape = [jax.ShapeDtypeStruct((r, width), F32)] * 2
    out_specs = [row, row]
    if rider is not None:
        in_specs += list(rider.in_specs)
        operands += list(rider.operands)
        out_shape.append(rider.out_shape)
        out_specs.append(rider.out_spec)
    in_specs += [pl.BlockSpec(memory_space=pl.ANY)] * len(caches)
    operands += caches
    return pl.pallas_call(
        functools.partial(_decode_step_kernel, n_pages=n_pages, page_size=page_size,
                          head_dim=head_dim, lambda_init=lambda_init,
                          rider=rider, n_rider_in=len(rider.operands) if rider else 0),
        out_shape=out_shape,
        grid_spec=pltpu.PrefetchScalarGridSpec(
            num_scalar_prefetch=1,
            grid=(r,),
            in_specs=in_specs,
            out_specs=out_specs,
            scratch_shapes=[pltpu.VMEM((n_pages,) + c.shape[1:], c.dtype) for c in caches]
                           + [pltpu.SemaphoreType.DMA((2,)), pltpu.VMEM((n_maps, page_size), F32)]),
        compiler_params=_params(("arbitrary",)),
        name="decode_attention",
    )(page_table, *operands)


def _merge_rows(x, od_ref, of_ref, g1_ref, wg_ref, gb_ref, wa_ref, wb_ref, wo_ref):
    d = x.shape[-1]
    dt = wg_ref.dtype
    h = _rms(x, g1_ref[...]).astype(dt)
    ya = _dot(od_ref[...].astype(dt), wa_ref[...])
    yb = _dot(of_ref[...].astype(dt), wb_ref[...])
    gate_a = jax.nn.sigmoid(_dot(h, wg_ref[:, :d]) + gb_ref[:, :d])
    gate_b = jax.nn.sigmoid(_dot(h, wg_ref[:, d:]) + gb_ref[:, d:])
    merged = gate_a * ya + gate_b * yb
    return x + _dot(merged.astype(dt), wo_ref[...])


def _merge_kernel(x_ref, *refs):
    *in_refs, o_ref = refs
    o_ref[...] = _merge_rows(x_ref[...], *in_refs)


def _merge(x, od, of, g1, wg, gb, wa, wb, wo):
    m, d = x.shape
    tm = min(ROW_TILE, m)
    assert m % tm == 0
    row = lambda c: pl.BlockSpec((tm, c), lambda i: (i, 0))
    return pl.pallas_call(
        _merge_kernel,
        out_shape=jax.ShapeDtypeStruct((m, d), F32),
        grid=(m // tm,),
        in_specs=[row(d), row(od.shape[1]), row(of.shape[1])]
                 + [_resident(a.shape) for a in (g1, wg, gb, wa, wb, wo)],
        out_specs=row(d),
        compiler_params=_params(("arbitrary",)),
        name="merge",
    )(x, od, of, g1, wg, gb, wa, wb, wo)


def _ffn_hidden(x, g2_ref, wgu_ref, wdn_ref):
    d_ff = wdn_ref.shape[0]
    h = _rms(x, g2_ref[...]).astype(wgu_ref.dtype)
    gate = _dot(h, wgu_ref[:, :d_ff])
    up = _dot(h, wgu_ref[:, d_ff:])
    return (gate * jax.nn.sigmoid(gate) * up).astype(wdn_ref.dtype)


def _ffn_kernel(x_ref, g2_ref, wgu_ref, wdn_ref, o_ref):
    x = x_ref[...]
    o_ref[...] = x + _dot(_ffn_hidden(x, g2_ref, wgu_ref, wdn_ref), wdn_ref[...])


N_MERGE_REFS = 9


def _tail_first(*refs):
    x_ref, *merge_refs = refs[:N_MERGE_REFS]
    return _merge_rows(x_ref[...], *merge_refs)


def _tail_second(x1, *refs):
    *ffn_refs, o_ref = refs[N_MERGE_REFS:]
    o_ref[...] = x1 + _dot(_ffn_hidden(x1, *ffn_refs), ffn_refs[-1][...])


def _ffn(x, g2, wgu, wdn):
    m, d = x.shape
    tm = min(FFN_ROW_TILE, m)
    assert m % tm == 0
    row = pl.BlockSpec((tm, d), lambda i: (i, 0))
    return pl.pallas_call(
        _ffn_kernel,
        out_shape=jax.ShapeDtypeStruct((m, d), F32),
        grid=(m // tm,),
        in_specs=[row, _resident(g2.shape), _resident(wgu.shape), _resident(wdn.shape)],
        out_specs=row,
        compiler_params=_params(("arbitrary",)),
        name="ffn",
    )(x, g2, wgu, wdn)


def kernel(x_prompt, x_sample, cache_diff_k, cache_diff_v, cache_fox_k, cache_fox_v, cache_fox_logf,
           page_table, rel_bias, norm1_g, w_in, diff_q_g, diff_k_g, fox_q_g, fox_k_g, diff_lambda,
           fox_f_b, gate_b, diff_subln_g, w_branch_a, w_branch_b, w_out, norm2_g, w_gate_up, w_down):
    depth = w_in.shape[0]
    batch, seq, d = x_prompt.shape
    dec_batch, dec_seq, _ = x_sample.shape
    assert dec_seq == 1
    _, n_phys, page_size, n_dh, _, head_dim = cache_diff_k.shape
    n_f = cache_fox_k.shape[3]
    e = 2 * head_dim
    width = n_dh * e
    assert n_f * head_dim == width

    rb_flat = rel_bias.reshape(-1)
    rbt = jnp.repeat(rel_bias.T, 2, axis=0)

    yp, ys = x_prompt, x_sample.reshape(1, dec_batch, d)
    rows_p, rows_s = [], []
    for l in range(depth):
        lambda_init = 0.8 - 0.6 * math.exp(-0.3 * l)
        w_t = w_in[l].T
        fb = fox_f_b[l].reshape(n_f, 1)
        gains = jnp.stack([jnp.tile(g[l], width // head_dim)
                           for g in (diff_q_g, diff_k_g, fox_q_g, fox_k_g)]).reshape(4, width, 1)
        g1 = norm1_g[l].reshape(1, d)
        g2 = norm2_g[l].reshape(1, d)
        gb = gate_b[l].reshape(1, 2 * d)
        subg = diff_subln_g[l].reshape(1, e)
        wg = w_in[l][:, 6 * width + n_f:]
        merge_w = (g1, wg.astype(BF16), gb, w_branch_a[l].astype(BF16), w_branch_b[l].astype(BF16),
                   w_out[l].astype(BF16))
        ffn_w = (g2, w_gate_up[l].astype(BF16), w_down[l].astype(BF16))
        lam_p = diff_lambda[l]

        def cache_rows(kd_t, vd, kf_t, vf_t, logf_t):
            b, _, s = kd_t.shape
            return (jnp.transpose(kd_t.reshape(b, n_dh, 2, head_dim, s), (0, 4, 1, 2, 3)),
                    vd.reshape(b, s, n_dh, e),
                    jnp.transpose(kf_t.reshape(b, n_f, head_dim, s), (0, 3, 1, 2)),
                    jnp.transpose(vf_t.reshape(b, n_f, head_dim, s), (0, 3, 1, 2)),
                    jnp.transpose(logf_t, (0, 2, 1)))

        qd_t, kd_t, vd, vd_t, qf_t, kf_t, vf_t, logf_t = _proj(
            yp, g1, w_t, gains, fb, width=width, head_dim=head_dim, with_rows=False)
        od = _diff_attention(rb_flat, qd_t, kd_t, vd_t, lam_p, subg,
                             n_heads=n_dh, head_dim=head_dim, lambda_init=lambda_init)
        of = _fox_attention(qf_t, kf_t, vf_t, logf_t, head_dim=head_dim)
        m_p = batch * seq
        assert m_p % dec_batch == 0
        rows = lambda c: pl.BlockSpec((m_p // dec_batch, c), lambda i, pt: (i, 0))
        tiled = (yp.reshape(m_p, d), od.reshape(m_p, width), of.reshape(m_p, width))
        weights = merge_w + ffn_w
        assert len(tiled) + len(merge_w) == N_MERGE_REFS
        rider = _Rider(_tail_first, _tail_second, tiled + weights,
                       tuple(rows(a.shape[1]) for a in tiled) + tuple(_resident(w.shape) for w in weights),
                       jax.ShapeDtypeStruct((m_p, d), F32), rows(d))
        rows_p.append(cache_rows(kd_t, vd, kf_t, vf_t, logf_t))

        (qd_t, kd_t, vd, _, qf_t, kf_t, vf_t, logf_t, qd_r, kd_r, qf_r, kf_r, vf_r) = _proj(
            ys, g1, w_t, gains, fb, width=width, head_dim=head_dim, with_rows=True)
        as_row = lambda a: a.reshape(dec_batch, width)
        cdk = jnp.transpose(cache_diff_k[l], (0, 2, 3, 4, 1)).reshape(n_phys, width, page_size)
        cdv = cache_diff_v[l].reshape(n_phys, page_size * n_dh, e)
        cfk = jnp.transpose(cache_fox_k[l], (0, 2, 3, 1)).reshape(n_phys, width, page_size)
        cfv = jnp.transpose(cache_fox_v[l], (0, 2, 3, 1)).reshape(n_phys, width, page_size)
        clf = jnp.transpose(cache_fox_logf[l], (0, 2, 1))
        od, of, yp = _decode_requests(
            page_table, as_row(qd_r), as_row(kd_r), vd.reshape(dec_batch * n_dh, e),
            as_row(qf_r), as_row(kf_r), as_row(vf_r), logf_t, rbt, lam_p, subg,
            cdk, cdv, cfk, cfv, clf, head_dim=head_dim, lambda_init=lambda_init, rider=rider)
        yp = yp.reshape(batch, seq, d)
        ys = _ffn(_merge(ys.reshape(dec_batch, d), od, of, *merge_w), *ffn_w).reshape(1, dec_batch, d)
        rows_s.append(tuple(jnp.swapaxes(a, 0, 1) for a in cache_rows(kd_t, vd, kf_t, vf_t, logf_t)))

    stack = lambda rows, i: jnp.stack([r[i] for r in rows], axis=0)
    return (yp, ys.reshape(dec_batch, 1, d),
            *(stack(rows_p, i) for i in range(5)), *(stack(rows_s, i) for i in range(5)))
```

```python
import functools
import math
from typing import Any, NamedTuple

import jax
import jax.numpy as jnp
from jax import lax
from jax.experimental import pallas as pl
from jax.experimental.pallas import tpu as pltpu

F32 = jnp.float32
BF16 = jnp.bfloat16

N_BUCKETS = 32
MAX_DISTANCE = 128
EPS = 1e-6
NEG_INF = -1e30
LOG2E = math.log2(math.e)

LANES = 128
SUBLANES = 8
BF16_ROWS = 16
VMEM_LIMIT = 56 * 1024 * 1024
DECODE_VMEM_LIMIT = 60 * 1024 * 1024

ROW_TILE = 512
FFN_ROW_TILE = 256
ATTN_TQ = 512
ATTN_TK = 256
FOX_PAIRS_PER_STEP = 2
DIFF_HEADS_PER_STEP = 2
DECAY_PARTS = 3

NT_DIMS = (((1,), (1,)), ((), ()))
HIGHEST = lax.Precision.HIGHEST


def _resident(shape):
    zeros = (0,) * len(shape)
    return pl.BlockSpec(shape, lambda *_: zeros, pipeline_mode=pl.Buffered(1))


def _params(semantics, vmem=VMEM_LIMIT):
    return pltpu.CompilerParams(dimension_semantics=semantics, vmem_limit_bytes=vmem)


def _rms(x, g):
    return x * lax.rsqrt(jnp.mean(x * x, axis=-1, keepdims=True) + EPS) * g


def _dot(a, b, precision=None):
    return jnp.dot(a, b, preferred_element_type=F32, precision=precision)


def _dot_nt(a, b):
    return lax.dot_general(a, b, NT_DIMS, preferred_element_type=F32)


def _log_sigmoid(x):
    return jnp.minimum(x, 0.0) - jnp.log1p(jnp.exp(-jnp.abs(x)))


def _rel_bucket(n):
    max_exact = N_BUCKETS // 2
    nf = jnp.maximum(n, 1).astype(F32)
    large = max_exact + (jnp.log(nf / max_exact) / math.log(MAX_DISTANCE / max_exact)
                         * (N_BUCKETS - max_exact)).astype(jnp.int32)
    return jnp.where(n < max_exact, n, jnp.minimum(large, N_BUCKETS - 1))


def _lambda_value(lam_ref, lambda_init):
    lp = lam_ref[...]
    a = jnp.sum(lp[0:1] * lp[1:2], axis=-1, keepdims=True)
    b = jnp.sum(lp[2:3] * lp[3:4], axis=-1, keepdims=True)
    return jnp.exp(a) - jnp.exp(b) + lambda_init


def _proj_kernel(x_ref, g1_ref, wt_ref, gains_ref, fb_ref,
                 qd_ref, kd_ref, vd_ref, vdt_ref, qf_ref, kf_ref, vf_ref, logf_ref, *row_refs,
                 head_dim, q_scale):
    h = _rms(x_ref[0], g1_ref[...])
    width = kd_ref.shape[1]
    n_heads = width // head_dim
    n_f = logf_ref.shape[1]
    section = lambda i: _dot_nt(wt_ref[i * width:(i + 1) * width, :], h)

    def head_norm_t(z, i):
        z3 = z.reshape(n_heads, head_dim, z.shape[-1])
        ss = jnp.sum(z3 * z3, axis=1, keepdims=True)
        gain = gains_ref[i].reshape(n_heads, head_dim, 1)
        return (z3 * lax.rsqrt(ss * (1.0 / head_dim) + EPS) * gain).reshape(z.shape)

    qd = head_norm_t(section(0), 0) * q_scale
    kd = head_norm_t(section(1), 1)
    qf = head_norm_t(section(3), 2) * q_scale
    kf = head_norm_t(section(4), 3)
    vd_t = section(2)
    vd_rows = vd_t.T
    vf = section(5)
    qd_ref[0] = qd.astype(qd_ref.dtype)
    kd_ref[0] = kd
    e = vd_ref.shape[-1]
    for hd in range(width // e):
        vd_ref[0, pl.ds(hd, vd_rows.shape[0], stride=width // e), :] = vd_rows[:, hd * e:(hd + 1) * e]
    vdt_ref[0] = vd_t.astype(vdt_ref.dtype)
    qf_ref[0] = qf.astype(qf_ref.dtype)
    kf_ref[0] = kf
    vf_ref[0] = vf
    logf_ref[0] = _log_sigmoid(_dot_nt(wt_ref[6 * width:6 * width + n_f, :], h) + fb_ref[...])
    if row_refs:
        for ref, val in zip(row_refs, (qd, kd, qf, kf, vf)):
            ref[0] = val.T


def _proj(x, g1, wt, gains, fb, *, width, head_dim, with_rows):
    b, s, d = x.shape
    n_f = fb.shape[0]
    e = 2 * head_dim
    tm = min(ROW_TILE, s)
    assert s % tm == 0 and tm % LANES == 0 and width % e == 0
    t_blk = lambda r: pl.BlockSpec((1, r, tm), lambda bi, si: (bi, 0, si))
    r_blk = lambda c: pl.BlockSpec((1, tm, c), lambda bi, si: (bi, si, 0))
    t_shape = lambda dt: jax.ShapeDtypeStruct((b, width, s), dt)
    r_shape = jax.ShapeDtypeStruct((b, s, width), F32)
    heads_per_token = width // e
    out_shape = [t_shape(BF16), t_shape(F32), jax.ShapeDtypeStruct((b, s * heads_per_token, e), F32),
                 t_shape(BF16), t_shape(BF16), t_shape(F32), t_shape(F32),
                 jax.ShapeDtypeStruct((b, n_f, s), F32)]
    out_specs = [t_blk(width), t_blk(width),
                 pl.BlockSpec((1, tm * heads_per_token, e), lambda bi, si: (bi, si, 0)),
                 t_blk(width), t_blk(width), t_blk(width), t_blk(width), t_blk(n_f)]
    if with_rows:
        out_shape += [r_shape] * 5
        out_specs += [r_blk(width)] * 5
    return pl.pallas_call(
        functools.partial(_proj_kernel, head_dim=head_dim, q_scale=head_dim ** -0.5 * LOG2E),
        out_shape=out_shape,
        grid=(b, s // tm),
        in_specs=[r_blk(d)] + [_resident(a.shape) for a in (g1, wt, gains, fb)],
        out_specs=out_specs,
        compiler_params=_params(("arbitrary", "arbitrary")),
        name="proj",
    )(x, g1, wt, gains, fb)


def _flash_step(scores, values, state):
    new_state = []
    for s_t, v_t, (m_prev, acc_prev) in zip(scores, values, state):
        m_new = jnp.maximum(m_prev, jnp.max(s_t, axis=0, keepdims=True))
        alpha = jnp.exp2(m_prev - m_new)
        p_t = jnp.exp2(s_t - m_new).astype(BF16)
        new_state.append((m_new, alpha * acc_prev + _dot(v_t, p_t)))
    return tuple(new_state)


def _flash_init(n_slots, rows, tq):
    return tuple((jnp.full((1, tq), NEG_INF, F32), jnp.zeros((rows, tq), F32)) for _ in range(n_slots))


def _ones_rows(seq):
    return jnp.where(lax.broadcasted_iota(jnp.int32, (BF16_ROWS, seq), 0) == 0, 1.0, 0.0).astype(BF16)


def _key_minus_query(tk, tq):
    return lax.broadcasted_iota(jnp.int32, (tk, tq), 0) - lax.broadcasted_iota(jnp.int32, (tk, tq), 1)


class _ScoreRing:
    def __init__(self, s_ref, qk):
        self.s_ref, self.qk = s_ref, qk

    def fill(self, j, slot):
        for idx, s in enumerate(self.qk(j)):
            self.s_ref[slot, idx] = s

    def scores(self, slot):
        return [self.s_ref[slot, idx] for idx in range(self.s_ref.shape[1])]

    def far_sweep(self, softmax, state, n_pairs):
        self.fill(0, 0)

        def body(jj, st):
            j = 2 * jj
            self.fill(j + 1, 1)
            st = softmax(self.scores(0), j, st)
            self.fill(j + 2, 0)
            return softmax(self.scores(1), j + 1, st)
        return lax.fori_loop(0, n_pairs, body, state)

    def tail(self, softmax, state, first, kinds, more_follows):
        assert not more_follows or len(kinds) % 2 == 0
        for n, kind in enumerate(kinds):
            slot = n % 2
            if n + 1 < len(kinds) or more_follows:
                self.fill(first + n + 1, 1 - slot)
            state = softmax(self.scores(slot), first + n, state, kind)
        return state


def _diff_attn_kernel(rb_ref, q_ref, k_ref, v_ref, lam_ref, subg_ref, o_ref,
                      kb_ref, vb_ref, bias_ref, s_ref, *, n_heads, head_dim, lambda_init):
    tq, tk = ATTN_TQ, ATTN_TK
    ratio = tq // tk
    n_hd = vb_ref.shape[0]
    head0 = n_hd * pl.program_id(0)
    seq = q_ref.shape[2]
    e = 2 * head_dim
    kmq = _key_minus_query(tk, tq)
    near = [tk] + [-d * tk for d in range(ratio)]

    @pl.when(pl.program_id(1) == 0)
    def _():
        for hd in range(n_hd):
            far = rb_ref[(N_BUCKETS - 1) * n_heads + head0 + hd]
            for idx, delta in enumerate(near):
                bucket = _rel_bucket(jnp.maximum(delta - kmq, 0))
                val = jnp.zeros((tk, tq), F32)
                for b in range(N_BUCKETS):
                    val = jnp.where(bucket == b, rb_ref[b * n_heads + head0 + hd], val)
                bias_ref[hd, idx] = (val - far) * LOG2E

    first = lax.broadcasted_iota(jnp.int32, (1, e), 1) < head_dim
    for hd in range(n_hd):
        k_rows = k_ref[0, hd * e:(hd + 1) * e, :].T
        kb_ref[2 * hd] = jnp.where(first, k_rows, 0.0).astype(BF16)
        kb_ref[2 * hd + 1] = jnp.where(first, 0.0, k_rows).astype(BF16)
        vb_ref[hd, 0:e, :] = v_ref[0, hd * e:(hd + 1) * e, :]
        vb_ref[hd, e:, :] = _ones_rows(seq)
    lam = _lambda_value(lam_ref, lambda_init)

    def q_block(i, carry):
        qs = pl.multiple_of(i * tq, tq)
        q_t = [q_ref[0, hd * e:(hd + 1) * e, pl.ds(qs, tq)] for hd in range(n_hd)]

        def qk(j):
            ks = pl.multiple_of(j * tk, tk)
            return [_dot(kb_ref[mp, pl.ds(ks, tk), :], q_t[mp // 2]) for mp in range(2 * n_hd)]

        def softmax(scores, j, state, near_idx=None):
            if near_idx is not None:
                scores = [s + bias_ref[mp // 2, near_idx] for mp, s in enumerate(scores)]
                if near[near_idx] <= 0:
                    visible = kmq <= near[near_idx]
                    scores = [jnp.where(visible, s, NEG_INF) for s in scores]
            ks = pl.multiple_of(j * tk, tk)
            values = [vb_ref[mp // 2, :, pl.ds(ks, tk)] for mp in range(2 * n_hd)]
            return _flash_step(scores, values, state)

        ring = _ScoreRing(s_ref, qk)
        first_near = ratio * (i - 1)
        state = ring.far_sweep(softmax, _flash_init(2 * n_hd, vb_ref.shape[1], tq),
                               jnp.maximum(first_near, 0) // 2)
        before = [None] * (ratio - 1) + [0]
        state = lax.cond(i >= 1, lambda st: ring.tail(softmax, st, first_near, before, True),
                         lambda st: st, state)
        state = ring.tail(softmax, state, ratio * i, [1 + d for d in range(ratio)], False)
        outs = []
        for hd in range(n_hd):
            (_, acc0), (_, acc1) = state[2 * hd:2 * hd + 2]
            o_t = acc0[0:e] / acc0[e:e + 1] - lam * (acc1[0:e] / acc1[e:e + 1])
            outs.append(_rms(o_t.T, subg_ref[...]) * (1.0 - lambda_init))
        o_ref[0, pl.ds(qs, tq), :] = jnp.concatenate(outs, axis=1).astype(o_ref.dtype)
        return carry

    lax.fori_loop(0, seq // tq, q_block, 0)


def _diff_attention(rb_flat, q_t, k_t, v, lam_p, subg, *, n_heads, head_dim, lambda_init):
    b, w, s = q_t.shape
    e = 2 * head_dim
    tq, tk = ATTN_TQ, ATTN_TK
    assert s % tq == 0 and tq % tk == 0 and tk % LANES == 0 and tk >= MAX_DISTANCE and e == LANES
    n_hd = DIFF_HEADS_PER_STEP
    assert n_heads % n_hd == 0
    t_blk = pl.BlockSpec((1, n_hd * e, s), lambda g, bi, *_: (bi, g, 0))
    r_blk = pl.BlockSpec((1, s, n_hd * e), lambda g, bi, *_: (bi, 0, g))
    return pl.pallas_call(
        functools.partial(_diff_attn_kernel, n_heads=n_heads, head_dim=head_dim, lambda_init=lambda_init),
        out_shape=jax.ShapeDtypeStruct((b, s, w), BF16),
        grid_spec=pltpu.PrefetchScalarGridSpec(
            num_scalar_prefetch=1,
            grid=(n_heads // n_hd, b),
            in_specs=[t_blk, t_blk, t_blk, _resident(lam_p.shape), _resident(subg.shape)],
            out_specs=r_blk,
            scratch_shapes=[pltpu.VMEM((2 * n_hd, s, e), BF16), pltpu.VMEM((n_hd, e + BF16_ROWS, s), BF16),
                            pltpu.VMEM((n_hd, tq // tk + 1, tk, tq), F32),
                            pltpu.VMEM((2, 2 * n_hd, tk, tq), F32)]),
        compiler_params=_params(("arbitrary", "arbitrary")),
        name="diff_attention",
    )(rb_flat, q_t, k_t, v, lam_p, subg)


def _fox_attn_kernel(q_ref, k_ref, v_ref, logf_ref, o_ref, kb_ref, vb_ref, c_ref, s_ref, *, head_dim):
    tq, tk = ATTN_TQ, ATTN_TK
    ratio = tq // tk
    group = pl.program_id(1)
    seq = q_ref.shape[2]
    e = 2 * head_dim
    n_h = kb_ref.shape[0]
    kmq = _key_minus_query(tk, tq)

    @pl.when(group == 0)
    def _():
        upper = (lax.broadcasted_iota(jnp.int32, (tk, tk), 0)
                 <= lax.broadcasted_iota(jnp.int32, (tk, tk), 1)).astype(F32)
        carry = jnp.zeros((logf_ref.shape[1], 1), F32)
        for blk in range(seq // tk):
            c = _dot(logf_ref[0, :, blk * tk:(blk + 1) * tk], upper, HIGHEST) + carry
            c_ref[:, blk * tk:(blk + 1) * tk] = c
            carry = c[:, tk - 1:tk]

    lane = lax.broadcasted_iota(jnp.int32, (1, e), 1)
    for hh in range(n_h):
        pair, half = divmod(hh, 2)
        k_rows = k_ref[0, pair * e:(pair + 1) * e, :].T
        spare = (1 - half) * head_dim
        rest = jnp.broadcast_to(c_ref[pl.ds(n_h * group + hh, 1), :] * LOG2E, (LANES, seq)).T
        k_aug = jnp.where(lane // head_dim == half, k_rows, 0.0)
        for part in range(DECAY_PARTS):
            piece = rest.astype(BF16).astype(F32)
            k_aug = jnp.where(lane == spare + part, piece, k_aug)
            rest = rest - piece
        kb_ref[hh] = k_aug.astype(BF16)
        vb_ref[hh, 0:head_dim, :] = v_ref[0, hh * head_dim:(hh + 1) * head_dim, :].astype(BF16)
        vb_ref[hh, head_dim:, :] = _ones_rows(seq)

    def q_block(i, carry):
        qs = pl.multiple_of(i * tq, tq)
        row = lax.broadcasted_iota(jnp.int32, (e, 1), 0)
        q_aug = []
        for hh in range(n_h):
            pair, half = divmod(hh, 2)
            q_t = q_ref[0, pair * e:(pair + 1) * e, pl.ds(qs, tq)]
            spare = (1 - half) * head_dim
            minus_one = jnp.logical_and(row >= spare, row < spare + DECAY_PARTS)
            q_aug.append(jnp.where(row // head_dim == half, q_t,
                                   jnp.where(minus_one, -1.0, 0.0).astype(BF16)))

        def qk(j):
            ks = pl.multiple_of(j * tk, tk)
            return [_dot(kb_ref[hh, pl.ds(ks, tk), :], q_aug[hh]) for hh in range(n_h)]

        def softmax(scores, j, state, diagonal=None):
            ks = pl.multiple_of(j * tk, tk)
            if diagonal is not None:
                visible = kmq <= -diagonal * tk
                scores = [jnp.where(visible, s, NEG_INF) for s in scores]
            values = [vb_ref[hh, :, pl.ds(ks, tk)] for hh in range(n_h)]
            return _flash_step(scores, values, state)

        ring = _ScoreRing(s_ref, qk)
        state = ring.far_sweep(softmax, _flash_init(n_h, vb_ref.shape[1], tq), ratio * i // 2)
        state = ring.tail(softmax, state, ratio * i, list(range(ratio)), False)
        o_t = jnp.concatenate([acc[0:head_dim] / acc[head_dim:head_dim + 1] for _, acc in state], axis=0)
        o_ref[0, pl.ds(qs, tq), :] = o_t.T.astype(o_ref.dtype)
        return carry

    lax.fori_loop(0, seq // tq, q_block, 0)


def _fox_attention(q_t, k_t, v_t, logf_t, *, head_dim):
    b, w, s = q_t.shape
    e = 2 * head_dim
    tq, tk = ATTN_TQ, ATTN_TK
    n_f = logf_t.shape[1]
    assert s % tq == 0 and tq % tk == 0 and tk % LANES == 0 and e == LANES and w // e * 2 == n_f
    assert DECAY_PARTS <= head_dim
    cols = FOX_PAIRS_PER_STEP * e
    n_h = 2 * FOX_PAIRS_PER_STEP
    assert w % cols == 0
    t_blk = pl.BlockSpec((1, cols, s), lambda bi, g: (bi, g, 0))
    return pl.pallas_call(
        functools.partial(_fox_attn_kernel, head_dim=head_dim),
        out_shape=jax.ShapeDtypeStruct((b, s, w), BF16),
        grid=(b, w // cols),
        in_specs=[t_blk, t_blk, t_blk, pl.BlockSpec((1, n_f, s), lambda bi, g: (bi, 0, 0))],
        out_specs=pl.BlockSpec((1, s, cols), lambda bi, g: (bi, 0, g)),
        scratch_shapes=[pltpu.VMEM((n_h, s, e), BF16), pltpu.VMEM((n_h, head_dim + BF16_ROWS, s), BF16),
                        pltpu.VMEM((n_f, s), F32), pltpu.VMEM((2, n_h, tk, tq), F32)],
        compiler_params=_params(("arbitrary", "arbitrary")),
        name="fox_attention",
    )(q_t, k_t, v_t, logf_t)


class _Rider(NamedTuple):
    first: Any
    second: Any
    n_tiled: int
    operands: tuple
    in_specs: tuple
    out_shape: Any
    out_spec: Any


REQUESTS_PER_STEP = 2


def _decode_step_kernel(pt_ref, qd_ref, kd_ref, vd_ref, qf_ref, kf_ref, vf_ref, lfn_ref,
                        rbt_ref, lam_ref, subg_ref, *rest,
                        n_pages, page_size, head_dim, lambda_init, rider, n_rider_in):
    rider_in, rest = rest[:n_rider_in], rest[n_rider_in:]
    (hbm_dk, hbm_dv, hbm_fk, hbm_fv, hbm_lf), rest = rest[:5], rest[5:]
    n_out = len(rest) - 7
    od_ref, of_ref, *rider_out = rest[:n_out]
    kd_buf, vd_buf, kf_buf, vf_buf, lf_buf, sems, bias_ref = rest[n_out:]
    step = pl.program_id(0)
    last = pl.num_programs(0) - 1
    n_maps = bias_ref.shape[0]
    key_side = ((hbm_dk, kd_buf), (hbm_fk, kf_buf), (hbm_lf, lf_buf)), sems.at[0]
    value_sides = [(((hbm_dv, vd_buf.at[slot]), (hbm_fv, vf_buf.at[slot])), sems.at[1 + slot])
                   for slot in range(REQUESTS_PER_STEP)]

    def copies(request, side):
        pairs, sem = side
        for p in range(n_pages):
            page = pt_ref[request, n_pages - 1 - p]
            for cache, buf in pairs:
                yield pltpu.make_async_copy(cache.at[page], buf.at[p], sem)

    def start(request, side):
        for copy in copies(request, side):
            copy.start()

    def wait(request, side):
        for copy in copies(request, side):
            copy.wait()

    @pl.when(step == 0)
    def _():
        start(0, key_side)
        start(0, value_sides[0])
        lane = lax.broadcasted_iota(jnp.int32, (n_maps, page_size), 1)
        bucket = _rel_bucket(page_size - lane)
        val = jnp.zeros((n_maps, page_size), F32)
        for b in range(N_BUCKETS):
            val = jnp.where(bucket == b, rbt_ref[:, b:b + 1], val)
        bias_ref[...] = (val - rbt_ref[:, N_BUCKETS - 1:N_BUCKETS]) * LOG2E

    rows = rider_out[0].shape[0] // REQUESTS_PER_STEP if rider is not None else 0
    for k in range(REQUESTS_PER_STEP):
        req = REQUESTS_PER_STEP * step + k
        wait(req, key_side)

        def scores_done(req=req, k=k):
            def fetch_next():
                start(req + 1, key_side)
                start(req + 1, value_sides[(k + 1) % REQUESTS_PER_STEP])
            if k + 1 < REQUESTS_PER_STEP:
                fetch_next()
            else:
                pl.when(step < last)(fetch_next)
            wait(req, value_sides[k])

        pages = [[buf.at[pl.ds(p, 1)] for p in range(n_pages)]
                 for buf in (kd_buf, vd_buf.at[k], kf_buf, vf_buf.at[k], lf_buf)]
        tile = lambda ref: ref.at[pl.ds(k * rows, rows)]
        n_tiled = rider.n_tiled if rider is not None else 0
        _decode_one_request(
            req, pages, scores_done, qd_ref, kd_ref, vd_ref, qf_ref, kf_ref, vf_ref, lfn_ref, rbt_ref,
            lam_ref, subg_ref, od_ref, of_ref, bias_ref,
            [tile(ref) for ref in rider_in[:n_tiled]] + list(rider_in[n_tiled:]),
            [tile(ref) for ref in rider_out],
            page_size=page_size, head_dim=head_dim, lambda_init=lambda_init, rider=rider)


def _decode_one_request(req, pages, scores_done, qd_ref, kd_ref, vd_ref, qf_ref, kf_ref, vf_ref,
                        lfn_ref, rbt_ref, lam_ref, subg_ref, od_ref, of_ref, bias_ref, rider_in,
                        rider_out, *, page_size, head_dim, lambda_init, rider):
    cdk, cdv, cfk, cfv, clf = pages
    n_pages = len(cdk)
    width = qd_ref.shape[-1]
    n_maps = width // head_dim
    n_dh = n_maps // 2
    e = 2 * head_dim
    sub = req % qd_ref.shape[0]
    new_row = lambda ref: ref[pl.ds(sub, 1), :]

    map_of_lane = lax.broadcasted_iota(jnp.int32, (n_maps, width), 1) // head_dim
    row = lax.broadcasted_iota(jnp.int32, (n_maps, width), 0)
    own = map_of_lane == row
    row_e = lax.broadcasted_iota(jnp.int32, (n_maps, e), 0)
    far_bias = rbt_ref[:, N_BUCKETS - 1:N_BUCKETS]

    qd = jnp.where(own, new_row(qd_ref), 0.0).astype(BF16)
    qf = jnp.where(own, new_row(qf_ref), 0.0).astype(BF16)

    def self_score(q, k_ref):
        k = new_row(k_ref).astype(BF16).astype(F32)
        return jnp.sum(q.astype(F32) * k, axis=-1, keepdims=True)

    def softmax(s, s_self):
        m = jnp.maximum(jnp.max(s, axis=-1, keepdims=True), s_self)
        p, p_self = jnp.exp2(s - m), jnp.exp2(s_self - m)
        return p, p_self, jnp.sum(p, axis=-1, keepdims=True) + p_self

    def keys_side_by_side(refs):
        return jnp.concatenate([ref[0].astype(BF16) for ref in refs], axis=1)

    bias = jnp.concatenate([bias_ref[...], jnp.zeros((n_maps, (n_pages - 1) * page_size), F32)], axis=1)
    s_d = _dot(qd, keys_side_by_side(cdk)) + bias
    later = (lax.broadcasted_iota(jnp.int32, (page_size, page_size), 0)
             >= lax.broadcasted_iota(jnp.int32, (page_size, page_size), 1)).astype(F32)
    lane = lax.broadcasted_iota(jnp.int32, lfn_ref.shape[1:], 1)
    carry = jnp.sum(jnp.where(lane == req, lfn_ref[0], 0.0), axis=1, keepdims=True) * LOG2E
    lf = jnp.concatenate([ref[0] for ref in clf], axis=0) * LOG2E
    incl = _dot(lf, later, HIGHEST)
    decay = []
    for p_idx in range(n_pages):
        rows = slice(p_idx * n_maps, (p_idx + 1) * n_maps)
        decay.append(incl[rows] - lf[rows] + carry)
        carry = carry + incl[rows, 0:1]
    s_f = _dot(qf, keys_side_by_side(cfk)) + jnp.concatenate(decay, axis=1)
    scores_done()
    if rider is not None:
        rider.second(rider.first(*rider_in), *rider_in, *rider_out)

    p, p_self, l = softmax(s_d, self_score(qd, kd_ref) + (rbt_ref[:, 0:1] - far_bias) * LOG2E)
    p = p.astype(BF16)
    v_new = vd_ref[pl.ds(sub * n_dh, n_dh), :]
    acc = jnp.zeros((n_maps, e), F32)
    for h in range(n_dh):
        v_h = jnp.concatenate([ref[0, pl.ds(h, page_size, stride=n_dh), :].astype(BF16) for ref in cdv],
                              axis=0)
        acc = jnp.where(row_e // 2 == h, _dot(p, v_h) + p_self * v_new[h:h + 1, :], acc)
    lam = _lambda_value(lam_ref, lambda_init)
    sign = jnp.where(row_e[:, 0:1] % 2 == 0, 1.0, -lam)
    pick = (lax.broadcasted_iota(jnp.int32, (n_dh, n_maps), 1) // 2
            == lax.broadcasted_iota(jnp.int32, (n_dh, n_maps), 0)).astype(F32)
    scaled = acc / l * sign

    p, p_self, l = softmax(s_f, self_score(qf, kf_ref))
    acc = jnp.zeros((n_maps, head_dim, page_size), F32)
    for p_idx in range(n_pages):
        p_page = p[:, p_idx * page_size:(p_idx + 1) * page_size]
        acc = acc + p_page[:, None, :] * cfv[p_idx][0].reshape(n_maps, head_dim, page_size)

    def spread(col):
        return jnp.sum(jnp.where(own, col, 0.0), axis=0, keepdims=True)
    past = jnp.sum(acc.reshape(width, page_size).T, axis=0, keepdims=True)
    of_ref[pl.ds(sub, 1), :] = (past + spread(p_self) * new_row(vf_ref)) / spread(l)

    o_d = _dot(pick, scaled, HIGHEST)
    o_d = _rms(o_d, subg_ref[...]) * (1.0 - lambda_init)
    od_ref[pl.ds(sub, 1), :] = jnp.concatenate([o_d[h:h + 1] for h in range(n_dh)], axis=1)


def _decode_requests(page_table, qd, kd, vd, qf, kf, vf, logf_new, rbt, lam_p, subg,
                     cdk, cdv, cfk, cfv, clf, *, head_dim, lambda_init, rider=None):
    r, n_pages = page_table.shape
    _, width, page_size = cdk.shape
    n_f = clf.shape[1]
    n_maps = width // head_dim
    n_dh = n_maps // 2
    e = 2 * head_dim
    per_step = REQUESTS_PER_STEP
    assert n_maps == n_f and r % SUBLANES == 0 and SUBLANES % per_step == 0
    assert page_size >= MAX_DISTANCE
    steps_per_block = SUBLANES // per_step
    row = pl.BlockSpec((SUBLANES, width), lambda i, pt: (i // steps_per_block, 0))
    head_rows = pl.BlockSpec((SUBLANES * n_dh, e), lambda i, pt: (i // steps_per_block, 0))
    caches = [cdk, cdv, cfk, cfv, clf]
    value_side = (cdv, cfv)

    in_specs = [row, row, head_rows, row, row, row, _resident(logf_new.shape),
                _resident(rbt.shape), _resident(lam_p.shape), _resident(subg.shape)]
    operands = [qd, kd, vd, qf, kf, vf, logf_new, rbt, lam_p, subg]
    out_shape = [jax.ShapeDtypeStruct((r, width), F32)] * 2
    out_specs = [row, row]
    if rider is not None:
        in_specs += list(rider.in_specs)
        operands += list(rider.operands)
        out_shape.append(rider.out_shape)
        out_specs.append(rider.out_spec)
    in_specs += [pl.BlockSpec(memory_space=pl.ANY)] * len(caches)
    operands += caches
    return pl.pallas_call(
        functools.partial(_decode_step_kernel, n_pages=n_pages, page_size=page_size,
                          head_dim=head_dim, lambda_init=lambda_init,
                          rider=rider, n_rider_in=len(rider.operands) if rider else 0),
        out_shape=out_shape,
        grid_spec=pltpu.PrefetchScalarGridSpec(
            num_scalar_prefetch=1,
            grid=(r // per_step,),
            in_specs=in_specs,
            out_specs=out_specs,
            scratch_shapes=[pltpu.VMEM(((per_step,) if any(c is v for v in value_side) else ())
                                       + (n_pages,) + c.shape[1:], c.dtype) for c in caches]
                           + [pltpu.SemaphoreType.DMA((1 + per_step,)),
                              pltpu.VMEM((n_maps, page_size), F32)]),
        compiler_params=_params(("arbitrary",), DECODE_VMEM_LIMIT),
        name="decode_attention",
    )(page_table, *operands)


def _merge_rows(x, od_ref, of_ref, g1_ref, wg_ref, gb_ref, wa_ref, wb_ref, wo_ref):
    d = x.shape[-1]
    dt = wg_ref.dtype
    h = _rms(x, g1_ref[...]).astype(dt)
    ya = _dot(od_ref[...].astype(dt), wa_ref[...])
    yb = _dot(of_ref[...].astype(dt), wb_ref[...])
    gate_a = jax.nn.sigmoid(_dot(h, wg_ref[:, :d]) + gb_ref[:, :d])
    gate_b = jax.nn.sigmoid(_dot(h, wg_ref[:, d:]) + gb_ref[:, d:])
    merged = gate_a * ya + gate_b * yb
    return x + _dot(merged.astype(dt), wo_ref[...])


def _merge_kernel(x_ref, *refs):
    *in_refs, o_ref = refs
    o_ref[...] = _merge_rows(x_ref[...], *in_refs)


def _merge(x, od, of, g1, wg, gb, wa, wb, wo):
    m, d = x.shape
    tm = min(ROW_TILE, m)
    assert m % tm == 0
    row = lambda c: pl.BlockSpec((tm, c), lambda i: (i, 0))
    return pl.pallas_call(
        _merge_kernel,
        out_shape=jax.ShapeDtypeStruct((m, d), F32),
        grid=(m // tm,),
        in_specs=[row(d), row(od.shape[1]), row(of.shape[1])]
                 + [_resident(a.shape) for a in (g1, wg, gb, wa, wb, wo)],
        out_specs=row(d),
        compiler_params=_params(("arbitrary",)),
        name="merge",
    )(x, od, of, g1, wg, gb, wa, wb, wo)


def _ffn_hidden(x, g2_ref, wgu_ref, wdn_ref):
    d_ff = wdn_ref.shape[0]
    h = _rms(x, g2_ref[...]).astype(wgu_ref.dtype)
    gate = _dot(h, wgu_ref[:, :d_ff])
    up = _dot(h, wgu_ref[:, d_ff:])
    return (gate * jax.nn.sigmoid(gate) * up).astype(wdn_ref.dtype)


def _ffn_kernel(x_ref, g2_ref, wgu_ref, wdn_ref, o_ref):
    x = x_ref[...]
    o_ref[...] = x + _dot(_ffn_hidden(x, g2_ref, wgu_ref, wdn_ref), wdn_ref[...])


N_MERGE_REFS = 9


def _tail_first(*refs):
    x_ref, *merge_refs = refs[:N_MERGE_REFS]
    return _merge_rows(x_ref[...], *merge_refs)


def _tail_second(x1, *refs):
    *ffn_refs, o_ref = refs[N_MERGE_REFS:]
    o_ref[...] = x1 + _dot(_ffn_hidden(x1, *ffn_refs), ffn_refs[-1][...])


def _ffn(x, g2, wgu, wdn):
    m, d = x.shape
    tm = min(FFN_ROW_TILE, m)
    assert m % tm == 0
    row = pl.BlockSpec((tm, d), lambda i: (i, 0))
    return pl.pallas_call(
        _ffn_kernel,
        out_shape=jax.ShapeDtypeStruct((m, d), F32),
        grid=(m // tm,),
        in_specs=[row, _resident(g2.shape), _resident(wgu.shape), _resident(wdn.shape)],
        out_specs=row,
        compiler_params=_params(("arbitrary",)),
        name="ffn",
    )(x, g2, wgu, wdn)


def kernel(x_prompt, x_sample, cache_diff_k, cache_diff_v, cache_fox_k, cache_fox_v, cache_fox_logf,
           page_table, rel_bias, norm1_g, w_in, diff_q_g, diff_k_g, fox_q_g, fox_k_g, diff_lambda,
           fox_f_b, gate_b, diff_subln_g, w_branch_a, w_branch_b, w_out, norm2_g, w_gate_up, w_down):
    depth = w_in.shape[0]
    batch, seq, d = x_prompt.shape
    dec_batch, dec_seq, _ = x_sample.shape
    assert dec_seq == 1
    _, n_phys, page_size, n_dh, _, head_dim = cache_diff_k.shape
    n_f = cache_fox_k.shape[3]
    e = 2 * head_dim
    width = n_dh * e
    assert n_f * head_dim == width

    rb_flat = rel_bias.reshape(-1)
    rbt = jnp.repeat(rel_bias.T, 2, axis=0)

    yp, ys = x_prompt, x_sample.reshape(1, dec_batch, d)
    rows_p, rows_s = [], []
    for l in range(depth):
        lambda_init = 0.8 - 0.6 * math.exp(-0.3 * l)
        w_t = w_in[l].T
        fb = fox_f_b[l].reshape(n_f, 1)
        gains = jnp.stack([jnp.tile(g[l], width // head_dim)
                           for g in (diff_q_g, diff_k_g, fox_q_g, fox_k_g)]).reshape(4, width, 1)
        g1 = norm1_g[l].reshape(1, d)
        g2 = norm2_g[l].reshape(1, d)
        gb = gate_b[l].reshape(1, 2 * d)
        subg = diff_subln_g[l].reshape(1, e)
        wg = w_in[l][:, 6 * width + n_f:]
        merge_w = (g1, wg.astype(BF16), gb, w_branch_a[l].astype(BF16), w_branch_b[l].astype(BF16),
                   w_out[l].astype(BF16))
        ffn_w = (g2, w_gate_up[l].astype(BF16), w_down[l].astype(BF16))
        lam_p = diff_lambda[l]

        def cache_rows(kd_t, vd, kf_t, vf_t, logf_t):
            b, _, s = kd_t.shape
            return (jnp.transpose(kd_t.reshape(b, n_dh, 2, head_dim, s), (0, 4, 1, 2, 3)),
                    vd.reshape(b, s, n_dh, e),
                    jnp.transpose(kf_t.reshape(b, n_f, head_dim, s), (0, 3, 1, 2)),
                    jnp.transpose(vf_t.reshape(b, n_f, head_dim, s), (0, 3, 1, 2)),
                    jnp.transpose(logf_t, (0, 2, 1)))

        qd_t, kd_t, vd, vd_t, qf_t, kf_t, vf_t, logf_t = _proj(
            yp, g1, w_t, gains, fb, width=width, head_dim=head_dim, with_rows=False)
        od = _diff_attention(rb_flat, qd_t, kd_t, vd_t, lam_p, subg,
                             n_heads=n_dh, head_dim=head_dim, lambda_init=lambda_init)
        of = _fox_attention(qf_t, kf_t, vf_t, logf_t, head_dim=head_dim)
        m_p = batch * seq
        assert m_p % dec_batch == 0
        rows = lambda c: pl.BlockSpec((REQUESTS_PER_STEP * (m_p // dec_batch), c), lambda i, pt: (i, 0))
        tiled = (yp.reshape(m_p, d), od.reshape(m_p, width), of.reshape(m_p, width))
        weights = merge_w + ffn_w
        assert len(tiled) + len(merge_w) == N_MERGE_REFS
        rider = _Rider(_tail_first, _tail_second, len(tiled), tiled + weights,
                       tuple(rows(a.shape[1]) for a in tiled) + tuple(_resident(w.shape) for w in weights),
                       jax.ShapeDtypeStruct((m_p, d), F32), rows(d))
        rows_p.append(cache_rows(kd_t, vd, kf_t, vf_t, logf_t))

        (qd_t, kd_t, vd, _, qf_t, kf_t, vf_t, logf_t, qd_r, kd_r, qf_r, kf_r, vf_r) = _proj(
            ys, g1, w_t, gains, fb, width=width, head_dim=head_dim, with_rows=True)
        as_row = lambda a: a.reshape(dec_batch, width)
        cdk = jnp.transpose(cache_diff_k[l], (0, 2, 3, 4, 1)).reshape(n_phys, width, page_size)
        cdv = cache_diff_v[l].reshape(n_phys, page_size * n_dh, e)
        cfk = jnp.transpose(cache_fox_k[l], (0, 2, 3, 1)).reshape(n_phys, width, page_size)
        cfv = jnp.transpose(cache_fox_v[l], (0, 2, 3, 1)).reshape(n_phys, width, page_size)
        clf = jnp.transpose(cache_fox_logf[l], (0, 2, 1))
        od, of, yp = _decode_requests(
            page_table, as_row(qd_r), as_row(kd_r), vd.reshape(dec_batch * n_dh, e),
            as_row(qf_r), as_row(kf_r), as_row(vf_r), logf_t, rbt, lam_p, subg,
            cdk, cdv, cfk, cfv, clf, head_dim=head_dim, lambda_init=lambda_init, rider=rider)
        yp = yp.reshape(batch, seq, d)
        ys = _ffn(_merge(ys.reshape(dec_batch, d), od, of, *merge_w), *ffn_w).reshape(1, dec_batch, d)
        rows_s.append(tuple(jnp.swapaxes(a, 0, 1) for a in cache_rows(kd_t, vd, kf_t, vf_t, logf_t)))

    stack = lambda rows, i: jnp.stack([r[i] for r in rows], axis=0)
    return (yp, ys.reshape(dec_batch, 1, d),
            *(stack(rows_p, i) for i in range(5)), *(stack(rows_s, i) for i in range(5)))
```

```python
import functools
import math
from typing import Any, NamedTuple

import jax
import jax.numpy as jnp
from jax import lax
from jax.experimental import pallas as pl
from jax.experimental.pallas import tpu as pltpu

F32 = jnp.float32
BF16 = jnp.bfloat16

N_BUCKETS = 32
MAX_DISTANCE = 128
EPS = 1e-6
NEG_INF = -1e30
LOG2E = math.log2(math.e)

LANES = 128
SUBLANES = 8
BF16_ROWS = 16
VMEM_LIMIT = 56 * 1024 * 1024
DECODE_VMEM_LIMIT = 60 * 1024 * 1024

ROW_TILE = 512
ATTN_TQ = 512
ATTN_TK = 256
FOX_PAIRS_PER_STEP = 2
DIFF_HEADS_PER_STEP = 2
DECAY_PARTS = 3

NT_DIMS = (((1,), (1,)), ((), ()))
HIGHEST = lax.Precision.HIGHEST


def _resident(shape):
    zeros = (0,) * len(shape)
    return pl.BlockSpec(shape, lambda *_: zeros, pipeline_mode=pl.Buffered(1))


def _params(semantics, vmem=VMEM_LIMIT):
    return pltpu.CompilerParams(dimension_semantics=semantics, vmem_limit_bytes=vmem)


def _rms(x, g):
    return x * lax.rsqrt(jnp.mean(x * x, axis=-1, keepdims=True) + EPS) * g


def _dot(a, b, precision=None):
    return jnp.dot(a, b, preferred_element_type=F32, precision=precision)


def _dot_nt(a, b):
    return lax.dot_general(a, b, NT_DIMS, preferred_element_type=F32)


def _log_sigmoid(x):
    return jnp.minimum(x, 0.0) - jnp.log1p(jnp.exp(-jnp.abs(x)))


def _rel_bucket(n):
    max_exact = N_BUCKETS // 2
    nf = jnp.maximum(n, 1).astype(F32)
    large = max_exact + (jnp.log(nf / max_exact) / math.log(MAX_DISTANCE / max_exact)
                         * (N_BUCKETS - max_exact)).astype(jnp.int32)
    return jnp.where(n < max_exact, n, jnp.minimum(large, N_BUCKETS - 1))


def _lambda_value(lam_ref, lambda_init):
    lp = lam_ref[...]
    a = jnp.sum(lp[0:1] * lp[1:2], axis=-1, keepdims=True)
    b = jnp.sum(lp[2:3] * lp[3:4], axis=-1, keepdims=True)
    return jnp.exp(a) - jnp.exp(b) + lambda_init


def _proj_kernel(x_ref, g1_ref, wt_ref, gains_ref, fb_ref,
                 qd_ref, kd_ref, vd_ref, vdt_ref, qf_ref, kf_ref, vf_ref, logf_ref, *row_refs,
                 head_dim, q_scale):
    h = _rms(x_ref[0], g1_ref[...])
    width = kd_ref.shape[1]
    n_heads = width // head_dim
    n_f = logf_ref.shape[1]
    section = lambda i: _dot_nt(wt_ref[i * width:(i + 1) * width, :], h)

    def head_norm_t(z, i):
        z3 = z.reshape(n_heads, head_dim, z.shape[-1])
        ss = jnp.sum(z3 * z3, axis=1, keepdims=True)
        gain = gains_ref[i].reshape(n_heads, head_dim, 1)
        return (z3 * lax.rsqrt(ss * (1.0 / head_dim) + EPS) * gain).reshape(z.shape)

    qd = head_norm_t(section(0), 0) * q_scale
    kd = head_norm_t(section(1), 1)
    qf = head_norm_t(section(3), 2) * q_scale
    kf = head_norm_t(section(4), 3)
    vd_t = section(2)
    vd_rows = vd_t.T
    vf = section(5)
    qd_ref[0] = qd.astype(qd_ref.dtype)
    kd_ref[0] = kd
    e = vd_ref.shape[-1]
    for hd in range(width // e):
        vd_ref[0, pl.ds(hd, vd_rows.shape[0], stride=width // e), :] = vd_rows[:, hd * e:(hd + 1) * e]
    vdt_ref[0] = vd_t.astype(vdt_ref.dtype)
    qf_ref[0] = qf.astype(qf_ref.dtype)
    kf_ref[0] = kf
    vf_ref[0] = vf
    logf_ref[0] = _log_sigmoid(_dot_nt(wt_ref[6 * width:6 * width + n_f, :], h) + fb_ref[...])
    if row_refs:
        for ref, val in zip(row_refs, (qd, kd, qf, kf, vf)):
            ref[0] = val.T


def _proj(x, g1, wt, gains, fb, *, width, head_dim, with_rows):
    b, s, d = x.shape
    n_f = fb.shape[0]
    e = 2 * head_dim
    tm = min(ROW_TILE, s)
    assert s % tm == 0 and tm % LANES == 0 and width % e == 0
    t_blk = lambda r: pl.BlockSpec((1, r, tm), lambda bi, si: (bi, 0, si))
    r_blk = lambda c: pl.BlockSpec((1, tm, c), lambda bi, si: (bi, si, 0))
    t_shape = lambda dt: jax.ShapeDtypeStruct((b, width, s), dt)
    r_shape = jax.ShapeDtypeStruct((b, s, width), F32)
    heads_per_token = width // e
    out_shape = [t_shape(BF16), t_shape(F32), jax.ShapeDtypeStruct((b, s * heads_per_token, e), F32),
                 t_shape(BF16), t_shape(BF16), t_shape(F32), t_shape(F32),
                 jax.ShapeDtypeStruct((b, n_f, s), F32)]
    out_specs = [t_blk(width), t_blk(width),
                 pl.BlockSpec((1, tm * heads_per_token, e), lambda bi, si: (bi, si, 0)),
                 t_blk(width), t_blk(width), t_blk(width), t_blk(width), t_blk(n_f)]
    if with_rows:
        out_shape += [r_shape] * 5
        out_specs += [r_blk(width)] * 5
    return pl.pallas_call(
        functools.partial(_proj_kernel, head_dim=head_dim, q_scale=head_dim ** -0.5 * LOG2E),
        out_shape=out_shape,
        grid=(b, s // tm),
        in_specs=[r_blk(d)] + [_resident(a.shape) for a in (g1, wt, gains, fb)],
        out_specs=out_specs,
        compiler_params=_params(("arbitrary", "arbitrary")),
        name="proj",
    )(x, g1, wt, gains, fb)


def _flash_step(scores, values, state):
    new_state = []
    for s_t, v_t, (m_prev, acc_prev) in zip(scores, values, state):
        m_new = jnp.maximum(m_prev, jnp.max(s_t, axis=0, keepdims=True))
        alpha = jnp.exp2(m_prev - m_new)
        p_t = jnp.exp2(s_t - m_new).astype(BF16)
        new_state.append((m_new, alpha * acc_prev + _dot(v_t, p_t)))
    return tuple(new_state)


def _flash_init(n_slots, rows, tq):
    return tuple((jnp.full((1, tq), NEG_INF, F32), jnp.zeros((rows, tq), F32)) for _ in range(n_slots))


def _ones_rows(seq):
    return jnp.where(lax.broadcasted_iota(jnp.int32, (BF16_ROWS, seq), 0) == 0, 1.0, 0.0).astype(BF16)


def _key_minus_query(tk, tq):
    return lax.broadcasted_iota(jnp.int32, (tk, tq), 0) - lax.broadcasted_iota(jnp.int32, (tk, tq), 1)


class _ScoreRing:
    def __init__(self, s_ref, qk):
        self.s_ref, self.qk = s_ref, qk

    def fill(self, j, slot):
        for idx, s in enumerate(self.qk(j)):
            self.s_ref[slot, idx] = s

    def scores(self, slot):
        return [self.s_ref[slot, idx] for idx in range(self.s_ref.shape[1])]

    def far_sweep(self, softmax, state, n_pairs):
        self.fill(0, 0)

        def body(jj, st):
            j = 2 * jj
            self.fill(j + 1, 1)
            st = softmax(self.scores(0), j, st)
            self.fill(j + 2, 0)
            return softmax(self.scores(1), j + 1, st)
        return lax.fori_loop(0, n_pairs, body, state)

    def tail(self, softmax, state, first, kinds, more_follows):
        assert not more_follows or len(kinds) % 2 == 0
        for n, kind in enumerate(kinds):
            slot = n % 2
            if n + 1 < len(kinds) or more_follows:
                self.fill(first + n + 1, 1 - slot)
            state = softmax(self.scores(slot), first + n, state, kind)
        return state


def _diff_attn_kernel(rb_ref, q_ref, k_ref, v_ref, lam_ref, subg_ref, o_ref,
                      kb_ref, vb_ref, bias_ref, s_ref, *, n_heads, head_dim, lambda_init):
    tq, tk = ATTN_TQ, ATTN_TK
    ratio = tq // tk
    n_hd = vb_ref.shape[0]
    head0 = n_hd * pl.program_id(0)
    seq = q_ref.shape[2]
    e = 2 * head_dim
    kmq = _key_minus_query(tk, tq)
    near = [tk] + [-d * tk for d in range(ratio)]

    @pl.when(pl.program_id(1) == 0)
    def _():
        for hd in range(n_hd):
            far = rb_ref[(N_BUCKETS - 1) * n_heads + head0 + hd]
            for idx, delta in enumerate(near):
                bucket = _rel_bucket(jnp.maximum(delta - kmq, 0))
                val = jnp.zeros((tk, tq), F32)
                for b in range(N_BUCKETS):
                    val = jnp.where(bucket == b, rb_ref[b * n_heads + head0 + hd], val)
                bias_ref[hd, idx] = (val - far) * LOG2E

    first = lax.broadcasted_iota(jnp.int32, (1, e), 1) < head_dim
    for hd in range(n_hd):
        k_rows = k_ref[0, hd * e:(hd + 1) * e, :].T
        kb_ref[2 * hd] = jnp.where(first, k_rows, 0.0).astype(BF16)
        kb_ref[2 * hd + 1] = jnp.where(first, 0.0, k_rows).astype(BF16)
        vb_ref[hd, 0:e, :] = v_ref[0, hd * e:(hd + 1) * e, :]
        vb_ref[hd, e:, :] = _ones_rows(seq)
    lam = _lambda_value(lam_ref, lambda_init)

    def q_block(i, carry):
        qs = pl.multiple_of(i * tq, tq)
        q_t = [q_ref[0, hd * e:(hd + 1) * e, pl.ds(qs, tq)] for hd in range(n_hd)]

        def qk(j):
            ks = pl.multiple_of(j * tk, tk)
            return [_dot(kb_ref[mp, pl.ds(ks, tk), :], q_t[mp // 2]) for mp in range(2 * n_hd)]

        def softmax(scores, j, state, near_idx=None):
            if near_idx is not None:
                scores = [s + bias_ref[mp // 2, near_idx] for mp, s in enumerate(scores)]
                if near[near_idx] <= 0:
                    visible = kmq <= near[near_idx]
                    scores = [jnp.where(visible, s, NEG_INF) for s in scores]
            ks = pl.multiple_of(j * tk, tk)
            values = [vb_ref[mp // 2, :, pl.ds(ks, tk)] for mp in range(2 * n_hd)]
            return _flash_step(scores, values, state)

        ring = _ScoreRing(s_ref, qk)
        first_near = ratio * (i - 1)
        state = ring.far_sweep(softmax, _flash_init(2 * n_hd, vb_ref.shape[1], tq),
                               jnp.maximum(first_near, 0) // 2)
        before = [None] * (ratio - 1) + [0]
        state = lax.cond(i >= 1, lambda st: ring.tail(softmax, st, first_near, before, True),
                         lambda st: st, state)
        state = ring.tail(softmax, state, ratio * i, [1 + d for d in range(ratio)], False)
        outs = []
        for hd in range(n_hd):
            (_, acc0), (_, acc1) = state[2 * hd:2 * hd + 2]
            o_t = acc0[0:e] / acc0[e:e + 1] - lam * (acc1[0:e] / acc1[e:e + 1])
            outs.append(_rms(o_t.T, subg_ref[...]) * (1.0 - lambda_init))
        o_ref[0, pl.ds(qs, tq), :] = jnp.concatenate(outs, axis=1).astype(o_ref.dtype)
        return carry

    lax.fori_loop(0, seq // tq, q_block, 0)


def _diff_attention(rb_flat, q_t, k_t, v, lam_p, subg, *, n_heads, head_dim, lambda_init):
    b, w, s = q_t.shape
    e = 2 * head_dim
    tq, tk = ATTN_TQ, ATTN_TK
    assert s % tq == 0 and tq % tk == 0 and tk % LANES == 0 and tk >= MAX_DISTANCE and e == LANES
    n_hd = DIFF_HEADS_PER_STEP
    assert n_heads % n_hd == 0
    t_blk = pl.BlockSpec((1, n_hd * e, s), lambda g, bi, *_: (bi, g, 0))
    r_blk = pl.BlockSpec((1, s, n_hd * e), lambda g, bi, *_: (bi, 0, g))
    return pl.pallas_call(
        functools.partial(_diff_attn_kernel, n_heads=n_heads, head_dim=head_dim, lambda_init=lambda_init),
        out_shape=jax.ShapeDtypeStruct((b, s, w), BF16),
        grid_spec=pltpu.PrefetchScalarGridSpec(
            num_scalar_prefetch=1,
            grid=(n_heads // n_hd, b),
            in_specs=[t_blk, t_blk, t_blk, _resident(lam_p.shape), _resident(subg.shape)],
            out_specs=r_blk,
            scratch_shapes=[pltpu.VMEM((2 * n_hd, s, e), BF16), pltpu.VMEM((n_hd, e + BF16_ROWS, s), BF16),
                            pltpu.VMEM((n_hd, tq // tk + 1, tk, tq), F32),
                            pltpu.VMEM((2, 2 * n_hd, tk, tq), F32)]),
        compiler_params=_params(("arbitrary", "arbitrary")),
        name="diff_attention",
    )(rb_flat, q_t, k_t, v, lam_p, subg)


def _fox_attn_kernel(q_ref, k_ref, v_ref, logf_ref, o_ref, kb_ref, vb_ref, c_ref, s_ref, *, head_dim):
    tq, tk = ATTN_TQ, ATTN_TK
    ratio = tq // tk
    group = pl.program_id(1)
    seq = q_ref.shape[2]
    e = 2 * head_dim
    n_h = kb_ref.shape[0]
    kmq = _key_minus_query(tk, tq)

    @pl.when(group == 0)
    def _():
        upper = (lax.broadcasted_iota(jnp.int32, (tk, tk), 0)
                 <= lax.broadcasted_iota(jnp.int32, (tk, tk), 1)).astype(F32)
        carry = jnp.zeros((logf_ref.shape[1], 1), F32)
        for blk in range(seq // tk):
            c = _dot(logf_ref[0, :, blk * tk:(blk + 1) * tk], upper, HIGHEST) + carry
            c_ref[:, blk * tk:(blk + 1) * tk] = c
            carry = c[:, tk - 1:tk]

    lane = lax.broadcasted_iota(jnp.int32, (1, e), 1)
    for hh in range(n_h):
        pair, half = divmod(hh, 2)
        k_rows = k_ref[0, pair * e:(pair + 1) * e, :].T
        spare = (1 - half) * head_dim
        rest = jnp.broadcast_to(c_ref[pl.ds(n_h * group + hh, 1), :] * LOG2E, (LANES, seq)).T
        k_aug = jnp.where(lane // head_dim == half, k_rows, 0.0)
        for part in range(DECAY_PARTS):
            piece = rest.astype(BF16).astype(F32)
            k_aug = jnp.where(lane == spare + part, piece, k_aug)
            rest = rest - piece
        kb_ref[hh] = k_aug.astype(BF16)
        vb_ref[hh, 0:head_dim, :] = v_ref[0, hh * head_dim:(hh + 1) * head_dim, :].astype(BF16)
        vb_ref[hh, head_dim:, :] = _ones_rows(seq)

    def q_block(i, carry):
        qs = pl.multiple_of(i * tq, tq)
        row = lax.broadcasted_iota(jnp.int32, (e, 1), 0)
        q_aug = []
        for hh in range(n_h):
            pair, half = divmod(hh, 2)
            q_t = q_ref[0, pair * e:(pair + 1) * e, pl.ds(qs, tq)]
            spare = (1 - half) * head_dim
            minus_one = jnp.logical_and(row >= spare, row < spare + DECAY_PARTS)
            q_aug.append(jnp.where(row // head_dim == half, q_t,
                                   jnp.where(minus_one, -1.0, 0.0).astype(BF16)))

        def qk(j):
            ks = pl.multiple_of(j * tk, tk)
            return [_dot(kb_ref[hh, pl.ds(ks, tk), :], q_aug[hh]) for hh in range(n_h)]

        def softmax(scores, j, state, diagonal=None):
            ks = pl.multiple_of(j * tk, tk)
            if diagonal is not None:
                visible = kmq <= -diagonal * tk
                scores = [jnp.where(visible, s, NEG_INF) for s in scores]
            values = [vb_ref[hh, :, pl.ds(ks, tk)] for hh in range(n_h)]
            return _flash_step(scores, values, state)

        ring = _ScoreRing(s_ref, qk)
        state = ring.far_sweep(softmax, _flash_init(n_h, vb_ref.shape[1], tq), ratio * i // 2)
        state = ring.tail(softmax, state, ratio * i, list(range(ratio)), False)
        o_t = jnp.concatenate([acc[0:head_dim] / acc[head_dim:head_dim + 1] for _, acc in state], axis=0)
        o_ref[0, pl.ds(qs, tq), :] = o_t.T.astype(o_ref.dtype)
        return carry

    lax.fori_loop(0, seq // tq, q_block, 0)


def _fox_attention(q_t, k_t, v_t, logf_t, *, head_dim):
    b, w, s = q_t.shape
    e = 2 * head_dim
    tq, tk = ATTN_TQ, ATTN_TK
    n_f = logf_t.shape[1]
    assert s % tq == 0 and tq % tk == 0 and tk % LANES == 0 and e == LANES and w // e * 2 == n_f
    assert DECAY_PARTS <= head_dim
    cols = FOX_PAIRS_PER_STEP * e
    n_h = 2 * FOX_PAIRS_PER_STEP
    assert w % cols == 0
    t_blk = pl.BlockSpec((1, cols, s), lambda bi, g: (bi, g, 0))
    return pl.pallas_call(
        functools.partial(_fox_attn_kernel, head_dim=head_dim),
        out_shape=jax.ShapeDtypeStruct((b, s, w), BF16),
        grid=(b, w // cols),
        in_specs=[t_blk, t_blk, t_blk, pl.BlockSpec((1, n_f, s), lambda bi, g: (bi, 0, 0))],
        out_specs=pl.BlockSpec((1, s, cols), lambda bi, g: (bi, 0, g)),
        scratch_shapes=[pltpu.VMEM((n_h, s, e), BF16), pltpu.VMEM((n_h, head_dim + BF16_ROWS, s), BF16),
                        pltpu.VMEM((n_f, s), F32), pltpu.VMEM((2, n_h, tk, tq), F32)],
        compiler_params=_params(("arbitrary", "arbitrary")),
        name="fox_attention",
    )(q_t, k_t, v_t, logf_t)


class _Rider(NamedTuple):
    first: Any
    second: Any
    n_tiled: int
    operands: tuple
    in_specs: tuple
    out_shape: Any
    out_spec: Any


REQUESTS_PER_STEP = 2


def _decode_step_kernel(pt_ref, qd_ref, kd_ref, vd_ref, qf_ref, kf_ref, vf_ref, lfn_ref,
                        rbt_ref, lam_ref, subg_ref, *rest,
                        n_pages, page_size, head_dim, lambda_init, rider, n_rider_in):
    rider_in, rest = rest[:n_rider_in], rest[n_rider_in:]
    (hbm_dk, hbm_dv, hbm_fk, hbm_fv, hbm_lf), rest = rest[:5], rest[5:]
    n_out = len(rest) - 7
    od_ref, of_ref, *rider_out = rest[:n_out]
    kd_buf, vd_buf, kf_buf, vf_buf, lf_buf, sems, bias_ref = rest[n_out:]
    step = pl.program_id(0)
    last = pl.num_programs(0) - 1
    n_maps = bias_ref.shape[0]
    key_side = ((hbm_dk, kd_buf), (hbm_fk, kf_buf), (hbm_lf, lf_buf)), sems.at[0]
    value_sides = [(((hbm_dv, vd_buf.at[slot]), (hbm_fv, vf_buf.at[slot])), sems.at[1 + slot])
                   for slot in range(REQUESTS_PER_STEP)]

    def copies(request, side):
        pairs, sem = side
        for p in range(n_pages):
            page = pt_ref[request, n_pages - 1 - p]
            for cache, buf in pairs:
                yield pltpu.make_async_copy(cache.at[page], buf.at[p], sem)

    def start(request, side):
        for copy in copies(request, side):
            copy.start()

    def wait(request, side):
        for copy in copies(request, side):
            copy.wait()

    @pl.when(step == 0)
    def _():
        start(0, key_side)
        start(0, value_sides[0])
        lane = lax.broadcasted_iota(jnp.int32, (n_maps, page_size), 1)
        bucket = _rel_bucket(page_size - lane)
        val = jnp.zeros((n_maps, page_size), F32)
        for b in range(N_BUCKETS):
            val = jnp.where(bucket == b, rbt_ref[:, b:b + 1], val)
        bias_ref[...] = (val - rbt_ref[:, N_BUCKETS - 1:N_BUCKETS]) * LOG2E

    rows = rider_out[0].shape[0] // REQUESTS_PER_STEP if rider is not None else 0
    for k in range(REQUESTS_PER_STEP):
        req = REQUESTS_PER_STEP * step + k
        wait(req, key_side)

        def scores_done(req=req, k=k):
            def fetch_next():
                start(req + 1, key_side)
                start(req + 1, value_sides[(k + 1) % REQUESTS_PER_STEP])
            if k + 1 < REQUESTS_PER_STEP:
                fetch_next()
            else:
                pl.when(step < last)(fetch_next)
            wait(req, value_sides[k])

        pages = [[buf.at[pl.ds(p, 1)] for p in range(n_pages)]
                 for buf in (kd_buf, vd_buf.at[k], kf_buf, vf_buf.at[k], lf_buf)]
        tile = lambda ref: ref.at[pl.ds(k * rows, rows)]
        n_tiled = rider.n_tiled if rider is not None else 0
        _decode_one_request(
            req, pages, scores_done, qd_ref, kd_ref, vd_ref, qf_ref, kf_ref, vf_ref, lfn_ref, rbt_ref,
            lam_ref, subg_ref, od_ref, of_ref, bias_ref,
            [tile(ref) for ref in rider_in[:n_tiled]] + list(rider_in[n_tiled:]),
            [tile(ref) for ref in rider_out],
            page_size=page_size, head_dim=head_dim, lambda_init=lambda_init, rider=rider)


def _decode_one_request(req, pages, scores_done, qd_ref, kd_ref, vd_ref, qf_ref, kf_ref, vf_ref,
                        lfn_ref, rbt_ref, lam_ref, subg_ref, od_ref, of_ref, bias_ref, rider_in,
                        rider_out, *, page_size, head_dim, lambda_init, rider):
    cdk, cdv, cfk, cfv, clf = pages
    n_pages = len(cdk)
    width = qd_ref.shape[-1]
    n_maps = width // head_dim
    n_dh = n_maps // 2
    e = 2 * head_dim
    sub = req % qd_ref.shape[0]
    new_row = lambda ref: ref[pl.ds(sub, 1), :]

    map_of_lane = lax.broadcasted_iota(jnp.int32, (n_maps, width), 1) // head_dim
    row = lax.broadcasted_iota(jnp.int32, (n_maps, width), 0)
    own = map_of_lane == row
    row_e = lax.broadcasted_iota(jnp.int32, (n_maps, e), 0)
    far_bias = rbt_ref[:, N_BUCKETS - 1:N_BUCKETS]

    qd = jnp.where(own, new_row(qd_ref), 0.0).astype(BF16)
    qf = jnp.where(own, new_row(qf_ref), 0.0).astype(BF16)

    def self_score(q, k_ref):
        k = new_row(k_ref).astype(BF16).astype(F32)
        return jnp.sum(q.astype(F32) * k, axis=-1, keepdims=True)

    def softmax(s, s_self):
        m = jnp.maximum(jnp.max(s, axis=-1, keepdims=True), s_self)
        p, p_self = jnp.exp2(s - m), jnp.exp2(s_self - m)
        return p, p_self, jnp.sum(p, axis=-1, keepdims=True) + p_self

    def keys_side_by_side(refs):
        return jnp.concatenate([ref[0].astype(BF16) for ref in refs], axis=1)

    bias = jnp.concatenate([bias_ref[...], jnp.zeros((n_maps, (n_pages - 1) * page_size), F32)], axis=1)
    s_d = _dot(qd, keys_side_by_side(cdk)) + bias
    later = (lax.broadcasted_iota(jnp.int32, (page_size, page_size), 0)
             >= lax.broadcasted_iota(jnp.int32, (page_size, page_size), 1)).astype(F32)
    lane = lax.broadcasted_iota(jnp.int32, lfn_ref.shape[1:], 1)
    carry = jnp.sum(jnp.where(lane == req, lfn_ref[0], 0.0), axis=1, keepdims=True) * LOG2E
    lf = jnp.concatenate([ref[0] for ref in clf], axis=0) * LOG2E
    incl = _dot(lf, later, HIGHEST)
    decay = []
    for p_idx in range(n_pages):
        rows = slice(p_idx * n_maps, (p_idx + 1) * n_maps)
        decay.append(incl[rows] - lf[rows] + carry)
        carry = carry + incl[rows, 0:1]
    s_f = _dot(qf, keys_side_by_side(cfk)) + jnp.concatenate(decay, axis=1)
    scores_done()
    if rider is not None:
        rider.second(rider.first(*rider_in), *rider_in, *rider_out)

    p, p_self, l = softmax(s_d, self_score(qd, kd_ref) + (rbt_ref[:, 0:1] - far_bias) * LOG2E)
    p = p.astype(BF16)
    v_new = vd_ref[pl.ds(sub * n_dh, n_dh), :]
    acc = jnp.zeros((n_maps, e), F32)
    for h in range(n_dh):
        v_h = jnp.concatenate([ref[0, pl.ds(h, page_size, stride=n_dh), :].astype(BF16) for ref in cdv],
                              axis=0)
        acc = jnp.where(row_e // 2 == h, _dot(p, v_h) + p_self * v_new[h:h + 1, :], acc)
    lam = _lambda_value(lam_ref, lambda_init)
    sign = jnp.where(row_e[:, 0:1] % 2 == 0, 1.0, -lam)
    pick = (lax.broadcasted_iota(jnp.int32, (n_dh, n_maps), 1) // 2
            == lax.broadcasted_iota(jnp.int32, (n_dh, n_maps), 0)).astype(F32)
    scaled = acc / l * sign

    p, p_self, l = softmax(s_f, self_score(qf, kf_ref))
    acc = jnp.zeros((n_maps, head_dim, page_size), F32)
    for p_idx in range(n_pages):
        p_page = p[:, p_idx * page_size:(p_idx + 1) * page_size]
        acc = acc + p_page[:, None, :] * cfv[p_idx][0].reshape(n_maps, head_dim, page_size)

    def spread(col):
        return jnp.sum(jnp.where(own, col, 0.0), axis=0, keepdims=True)
    past = jnp.sum(acc.reshape(width, page_size).T, axis=0, keepdims=True)
    of_ref[pl.ds(sub, 1), :] = (past + spread(p_self) * new_row(vf_ref)) / spread(l)

    o_d = _dot(pick, scaled, HIGHEST)
    o_d = _rms(o_d, subg_ref[...]) * (1.0 - lambda_init)
    od_ref[pl.ds(sub, 1), :] = jnp.concatenate([o_d[h:h + 1] for h in range(n_dh)], axis=1)


def _decode_requests(page_table, qd, kd, vd, qf, kf, vf, logf_new, rbt, lam_p, subg,
                     cdk, cdv, cfk, cfv, clf, *, head_dim, lambda_init, rider=None):
    r, n_pages = page_table.shape
    _, width, page_size = cdk.shape
    n_f = clf.shape[1]
    n_maps = width // head_dim
    n_dh = n_maps // 2
    e = 2 * head_dim
    per_step = REQUESTS_PER_STEP
    assert n_maps == n_f and r % SUBLANES == 0 and SUBLANES % per_step == 0
    assert page_size >= MAX_DISTANCE
    steps_per_block = SUBLANES // per_step
    row = pl.BlockSpec((SUBLANES, width), lambda i, pt: (i // steps_per_block, 0))
    head_rows = pl.BlockSpec((SUBLANES * n_dh, e), lambda i, pt: (i // steps_per_block, 0))
    caches = [cdk, cdv, cfk, cfv, clf]
    value_side = (cdv, cfv)

    in_specs = [row, row, head_rows, row, row, row, _resident(logf_new.shape),
                _resident(rbt.shape), _resident(lam_p.shape), _resident(subg.shape)]
    operands = [qd, kd, vd, qf, kf, vf, logf_new, rbt, lam_p, subg]
    out_shape = [jax.ShapeDtypeStruct((r, width), F32)] * 2
    out_specs = [row, row]
    if rider is not None:
        in_specs += list(rider.in_specs)
        operands += list(rider.operands)
        out_shape.append(rider.out_shape)
        out_specs.append(rider.out_spec)
    in_specs += [pl.BlockSpec(memory_space=pl.ANY)] * len(caches)
    operands += caches
    return pl.pallas_call(
        functools.partial(_decode_step_kernel, n_pages=n_pages, page_size=page_size,
                          head_dim=head_dim, lambda_init=lambda_init,
                          rider=rider, n_rider_in=len(rider.operands) if rider else 0),
        out_shape=out_shape,
        grid_spec=pltpu.PrefetchScalarGridSpec(
            num_scalar_prefetch=1,
            grid=(r // per_step,),
            in_specs=in_specs,
            out_specs=out_specs,
            scratch_shapes=[pltpu.VMEM(((per_step,) if any(c is v for v in value_side) else ())
                                       + (n_pages,) + c.shape[1:], c.dtype) for c in caches]
                           + [pltpu.SemaphoreType.DMA((1 + per_step,)),
                              pltpu.VMEM((n_maps, page_size), F32)]),
        compiler_params=_params(("arbitrary",), DECODE_VMEM_LIMIT),
        name="decode_attention",
    )(page_table, *operands)


def _merge_rows(x, od_ref, of_ref, g1_ref, wg_ref, gb_ref, wa_ref, wb_ref, wo_ref):
    d = x.shape[-1]
    dt = wg_ref.dtype
    h = _rms(x, g1_ref[...]).astype(dt)
    ya = _dot(od_ref[...].astype(dt), wa_ref[...])
    yb = _dot(of_ref[...].astype(dt), wb_ref[...])
    gate_a = jax.nn.sigmoid(_dot(h, wg_ref[:, :d]) + gb_ref[:, :d])
    gate_b = jax.nn.sigmoid(_dot(h, wg_ref[:, d:]) + gb_ref[:, d:])
    merged = gate_a * ya + gate_b * yb
    return x + _dot(merged.astype(dt), wo_ref[...])


def _ffn_hidden(x, g2_ref, wgu_ref, wdn_ref):
    d_ff = wdn_ref.shape[0]
    h = _rms(x, g2_ref[...]).astype(wgu_ref.dtype)
    gate = _dot(h, wgu_ref[:, :d_ff])
    up = _dot(h, wgu_ref[:, d_ff:])
    return (gate * jax.nn.sigmoid(gate) * up).astype(wdn_ref.dtype)


N_MERGE_REFS = 9


def _tail_first(*refs):
    x_ref, *merge_refs = refs[:N_MERGE_REFS]
    return _merge_rows(x_ref[...], *merge_refs)


def _tail_second(x1, *refs):
    *ffn_refs, o_ref = refs[N_MERGE_REFS:]
    o_ref[...] = x1 + _dot(_ffn_hidden(x1, *ffn_refs), ffn_refs[-1][...])


def _tail_kernel(*refs):
    in_refs = refs[:-1]
    _tail_second(_tail_first(*in_refs), *refs)


def _tail(x, od, of, weights):
    m, d = x.shape
    tm = min(ROW_TILE, m)
    assert m % tm == 0
    row = lambda c: pl.BlockSpec((tm, c), lambda i: (i, 0))
    return pl.pallas_call(
        _tail_kernel,
        out_shape=jax.ShapeDtypeStruct((m, d), F32),
        grid=(m // tm,),
        in_specs=[row(d), row(od.shape[1]), row(of.shape[1])] + [_resident(w.shape) for w in weights],
        out_specs=row(d),
        compiler_params=_params(("arbitrary",)),
        name="tail",
    )(x, od, of, *weights)


def kernel(x_prompt, x_sample, cache_diff_k, cache_diff_v, cache_fox_k, cache_fox_v, cache_fox_logf,
           page_table, rel_bias, norm1_g, w_in, diff_q_g, diff_k_g, fox_q_g, fox_k_g, diff_lambda,
           fox_f_b, gate_b, diff_subln_g, w_branch_a, w_branch_b, w_out, norm2_g, w_gate_up, w_down):
    depth = w_in.shape[0]
    batch, seq, d = x_prompt.shape
    dec_batch, dec_seq, _ = x_sample.shape
    assert dec_seq == 1
    _, n_phys, page_size, n_dh, _, head_dim = cache_diff_k.shape
    n_f = cache_fox_k.shape[3]
    e = 2 * head_dim
    width = n_dh * e
    assert n_f * head_dim == width

    rb_flat = rel_bias.reshape(-1)
    rbt = jnp.repeat(rel_bias.T, 2, axis=0)

    yp, ys = x_prompt, x_sample.reshape(1, dec_batch, d)
    rows_p, rows_s = [], []
    for l in range(depth):
        lambda_init = 0.8 - 0.6 * math.exp(-0.3 * l)
        w_t = w_in[l].T
        fb = fox_f_b[l].reshape(n_f, 1)
        gains = jnp.stack([jnp.tile(g[l], width // head_dim)
                           for g in (diff_q_g, diff_k_g, fox_q_g, fox_k_g)]).reshape(4, width, 1)
        g1 = norm1_g[l].reshape(1, d)
        g2 = norm2_g[l].reshape(1, d)
        gb = gate_b[l].reshape(1, 2 * d)
        subg = diff_subln_g[l].reshape(1, e)
        wg = w_t[6 * width + n_f:].T
        merge_w = (g1, wg.astype(BF16), gb, w_branch_a[l].astype(BF16), w_branch_b[l].astype(BF16),
                   w_out[l].astype(BF16))
        ffn_w = (g2, w_gate_up[l].astype(BF16), w_down[l].astype(BF16))
        lam_p = diff_lambda[l]

        def cache_rows(kd_t, vd, kf_t, vf_t, logf_t):
            b, _, s = kd_t.shape
            return (jnp.transpose(kd_t.reshape(b, n_dh, 2, head_dim, s), (0, 4, 1, 2, 3)),
                    vd.reshape(b, s, n_dh, e),
                    jnp.transpose(kf_t.reshape(b, n_f, head_dim, s), (0, 3, 1, 2)),
                    jnp.transpose(vf_t.reshape(b, n_f, head_dim, s), (0, 3, 1, 2)),
                    jnp.transpose(logf_t, (0, 2, 1)))

        qd_t, kd_t, vd, vd_t, qf_t, kf_t, vf_t, logf_t = _proj(
            yp, g1, w_t, gains, fb, width=width, head_dim=head_dim, with_rows=False)
        od = _diff_attention(rb_flat, qd_t, kd_t, vd_t, lam_p, subg,
                             n_heads=n_dh, head_dim=head_dim, lambda_init=lambda_init)
        of = _fox_attention(qf_t, kf_t, vf_t, logf_t, head_dim=head_dim)
        m_p = batch * seq
        assert m_p % dec_batch == 0
        rows = lambda c: pl.BlockSpec((REQUESTS_PER_STEP * (m_p // dec_batch), c), lambda i, pt: (i, 0))
        tiled = (yp.reshape(m_p, d), od.reshape(m_p, width), of.reshape(m_p, width))
        weights = merge_w + ffn_w
        assert len(tiled) + len(merge_w) == N_MERGE_REFS
        rider = _Rider(_tail_first, _tail_second, len(tiled), tiled + weights,
                       tuple(rows(a.shape[1]) for a in tiled) + tuple(_resident(w.shape) for w in weights),
                       jax.ShapeDtypeStruct((m_p, d), F32), rows(d))
        rows_p.append(cache_rows(kd_t, vd, kf_t, vf_t, logf_t))

        (qd_t, kd_t, vd, _, qf_t, kf_t, vf_t, logf_t, qd_r, kd_r, qf_r, kf_r, vf_r) = _proj(
            ys, g1, w_t, gains, fb, width=width, head_dim=head_dim, with_rows=True)
        as_row = lambda a: a.reshape(dec_batch, width)
        cdk = jnp.transpose(cache_diff_k[l], (0, 2, 3, 4, 1)).reshape(n_phys, width, page_size)
        cdv = cache_diff_v[l].reshape(n_phys, page_size * n_dh, e)
        cfk = jnp.transpose(cache_fox_k[l], (0, 2, 3, 1)).reshape(n_phys, width, page_size)
        cfv = jnp.transpose(cache_fox_v[l], (0, 2, 3, 1)).reshape(n_phys, width, page_size)
        clf = jnp.transpose(cache_fox_logf[l], (0, 2, 1))
        od, of, yp = _decode_requests(
            page_table, as_row(qd_r), as_row(kd_r), vd.reshape(dec_batch * n_dh, e),
            as_row(qf_r), as_row(kf_r), as_row(vf_r), logf_t, rbt, lam_p, subg,
            cdk, cdv, cfk, cfv, clf, head_dim=head_dim, lambda_init=lambda_init, rider=rider)
        yp = yp.reshape(batch, seq, d)
        ys = _tail(ys.reshape(dec_batch, d), od, of, weights).reshape(1, dec_batch, d)
        rows_s.append(tuple(jnp.swapaxes(a, 0, 1) for a in cache_rows(kd_t, vd, kf_t, vf_t, logf_t)))

    stack = lambda rows, i: jnp.stack([r[i] for r in rows], axis=0)
    return (yp, ys.reshape(dec_batch, 1, d),
            *(stack(rows_p, i) for i in range(5)), *(stack(rows_s, i) for i in range(5)))
```

```python
import functools
import math
from typing import Any, NamedTuple

import jax
import jax.numpy as jnp
from jax import lax
from jax.experimental import pallas as pl
from jax.experimental.pallas import tpu as pltpu

F32 = jnp.float32
BF16 = jnp.bfloat16

N_BUCKETS = 32
MAX_DISTANCE = 128
EPS = 1e-6
NEG_INF = -1e30
LOG2E = math.log2(math.e)

LANES = 128
SUBLANES = 8
BF16_ROWS = 16
VMEM_LIMIT = 56 * 1024 * 1024
DECODE_VMEM_LIMIT = 60 * 1024 * 1024

ROW_TILE = 512
ATTN_TQ = 512
ATTN_TK = 256
FOX_PAIRS_PER_STEP = 2
DIFF_HEADS_PER_STEP = 2
DECAY_PARTS = 3

NT_DIMS = (((1,), (1,)), ((), ()))
HIGHEST = lax.Precision.HIGHEST


def _resident(shape):
    zeros = (0,) * len(shape)
    return pl.BlockSpec(shape, lambda *_: zeros, pipeline_mode=pl.Buffered(1))


def _params(semantics, vmem=VMEM_LIMIT):
    return pltpu.CompilerParams(dimension_semantics=semantics, vmem_limit_bytes=vmem)


def _rms(x, g):
    return x * lax.rsqrt(jnp.mean(x * x, axis=-1, keepdims=True) + EPS) * g


def _dot(a, b, precision=None):
    return jnp.dot(a, b, preferred_element_type=F32, precision=precision)


def _dot_nt(a, b):
    return lax.dot_general(a, b, NT_DIMS, preferred_element_type=F32)


def _log_sigmoid(x):
    return jnp.minimum(x, 0.0) - jnp.log1p(jnp.exp(-jnp.abs(x)))


def _rel_bucket(n):
    max_exact = N_BUCKETS // 2
    nf = jnp.maximum(n, 1).astype(F32)
    large = max_exact + (jnp.log(nf / max_exact) / math.log(MAX_DISTANCE / max_exact)
                         * (N_BUCKETS - max_exact)).astype(jnp.int32)
    return jnp.where(n < max_exact, n, jnp.minimum(large, N_BUCKETS - 1))


def _lambda_value(lam_ref, lambda_init):
    lp = lam_ref[...]
    a = jnp.sum(lp[0:1] * lp[1:2], axis=-1, keepdims=True)
    b = jnp.sum(lp[2:3] * lp[3:4], axis=-1, keepdims=True)
    return jnp.exp(a) - jnp.exp(b) + lambda_init


def _proj_kernel(x_ref, g1_ref, wt_ref, gains_ref, fb_ref,
                 qd_ref, kd_ref, vd_ref, vdt_ref, qf_ref, kf_ref, vf_ref, logf_ref, *row_refs,
                 head_dim, q_scale):
    h = _rms(x_ref[0], g1_ref[...])
    width = kd_ref.shape[1]
    n_heads = width // head_dim
    n_f = logf_ref.shape[1]
    section = lambda i: _dot_nt(wt_ref[i * width:(i + 1) * width, :], h)

    def head_norm_t(z, i):
        z3 = z.reshape(n_heads, head_dim, z.shape[-1])
        ss = jnp.sum(z3 * z3, axis=1, keepdims=True)
        gain = gains_ref[i].reshape(n_heads, head_dim, 1)
        return (z3 * lax.rsqrt(ss * (1.0 / head_dim) + EPS) * gain).reshape(z.shape)

    qd = head_norm_t(section(0), 0) * q_scale
    kd = head_norm_t(section(1), 1)
    qf = head_norm_t(section(3), 2) * q_scale
    kf = head_norm_t(section(4), 3)
    vd_t = section(2)
    vd_rows = vd_t.T
    vf = section(5)
    qd_ref[0] = qd.astype(qd_ref.dtype)
    kd_ref[0] = kd
    e = vd_ref.shape[-1]
    for hd in range(width // e):
        vd_ref[0, pl.ds(hd, vd_rows.shape[0], stride=width // e), :] = vd_rows[:, hd * e:(hd + 1) * e]
    vdt_ref[0] = vd_t.astype(vdt_ref.dtype)
    qf_ref[0] = qf.astype(qf_ref.dtype)
    kf_ref[0] = kf
    vf_ref[0] = vf
    logf_ref[0] = _log_sigmoid(_dot_nt(wt_ref[6 * width:6 * width + n_f, :], h) + fb_ref[...])
    if row_refs:
        for ref, val in zip(row_refs, (qd, kd, qf, kf, vf)):
            ref[0] = val.T


def _proj(x, g1, wt, gains, fb, *, width, head_dim, with_rows):
    b, s, d = x.shape
    n_f = fb.shape[0]
    e = 2 * head_dim
    tm = min(ROW_TILE, s)
    assert s % tm == 0 and tm % LANES == 0 and width % e == 0
    t_blk = lambda r: pl.BlockSpec((1, r, tm), lambda bi, si: (bi, 0, si))
    r_blk = lambda c: pl.BlockSpec((1, tm, c), lambda bi, si: (bi, si, 0))
    t_shape = lambda dt: jax.ShapeDtypeStruct((b, width, s), dt)
    r_shape = jax.ShapeDtypeStruct((b, s, width), F32)
    heads_per_token = width // e
    out_shape = [t_shape(BF16), t_shape(F32), jax.ShapeDtypeStruct((b, s * heads_per_token, e), F32),
                 t_shape(BF16), t_shape(BF16), t_shape(F32), t_shape(F32),
                 jax.ShapeDtypeStruct((b, n_f, s), F32)]
    out_specs = [t_blk(width), t_blk(width),
                 pl.BlockSpec((1, tm * heads_per_token, e), lambda bi, si: (bi, si, 0)),
                 t_blk(width), t_blk(width), t_blk(width), t_blk(width), t_blk(n_f)]
    if with_rows:
        out_shape += [r_shape] * 5
        out_specs += [r_blk(width)] * 5
    return pl.pallas_call(
        functools.partial(_proj_kernel, head_dim=head_dim, q_scale=head_dim ** -0.5 * LOG2E),
        out_shape=out_shape,
        grid=(b, s // tm),
        in_specs=[r_blk(d)] + [_resident(a.shape) for a in (g1, wt, gains, fb)],
        out_specs=out_specs,
        compiler_params=_params(("arbitrary", "arbitrary")),
        name="proj",
    )(x, g1, wt, gains, fb)


def _flash_step(scores, values, state):
    new_state = []
    for s_t, v_t, (m_prev, acc_prev) in zip(scores, values, state):
        m_new = jnp.maximum(m_prev, jnp.max(s_t, axis=0, keepdims=True))
        alpha = jnp.exp2(m_prev - m_new)
        p_t = jnp.exp2(s_t - m_new).astype(BF16)
        new_state.append((m_new, alpha * acc_prev + _dot(v_t, p_t)))
    return tuple(new_state)


def _flash_init(n_slots, rows, tq):
    return tuple((jnp.full((1, tq), NEG_INF, F32), jnp.zeros((rows, tq), F32)) for _ in range(n_slots))


def _ones_rows(seq):
    return jnp.where(lax.broadcasted_iota(jnp.int32, (BF16_ROWS, seq), 0) == 0, 1.0, 0.0).astype(BF16)


def _key_minus_query(tk, tq):
    return lax.broadcasted_iota(jnp.int32, (tk, tq), 0) - lax.broadcasted_iota(jnp.int32, (tk, tq), 1)


class _ScoreRing:
    def __init__(self, s_ref, qk):
        self.s_ref, self.qk = s_ref, qk

    def fill(self, j, slot):
        for idx, s in enumerate(self.qk(j)):
            self.s_ref[slot, idx] = s

    def scores(self, slot):
        return [self.s_ref[slot, idx] for idx in range(self.s_ref.shape[1])]

    def far_sweep(self, softmax, state, n_pairs):
        self.fill(0, 0)

        def body(jj, st):
            j = 2 * jj
            self.fill(j + 1, 1)
            st = softmax(self.scores(0), j, st)
            self.fill(j + 2, 0)
            return softmax(self.scores(1), j + 1, st)
        return lax.fori_loop(0, n_pairs, body, state)

    def tail(self, softmax, state, first, kinds, more_follows):
        assert not more_follows or len(kinds) % 2 == 0
        for n, kind in enumerate(kinds):
            slot = n % 2
            if n + 1 < len(kinds) or more_follows:
                self.fill(first + n + 1, 1 - slot)
            state = softmax(self.scores(slot), first + n, state, kind)
        return state


def _diff_attn_kernel(rb_ref, q_ref, k_ref, v_ref, lam_ref, subg_ref, o_ref,
                      kb_ref, vb_ref, bias_ref, s_ref, *, n_heads, head_dim, lambda_init):
    tq, tk = ATTN_TQ, ATTN_TK
    ratio = tq // tk
    n_hd = vb_ref.shape[0]
    head0 = n_hd * pl.program_id(0)
    seq = q_ref.shape[2]
    e = 2 * head_dim
    kmq = _key_minus_query(tk, tq)
    near = [tk] + [-d * tk for d in range(ratio)]

    @pl.when(pl.program_id(1) == 0)
    def _():
        for hd in range(n_hd):
            far = rb_ref[(N_BUCKETS - 1) * n_heads + head0 + hd]
            for idx, delta in enumerate(near):
                bucket = _rel_bucket(jnp.maximum(delta - kmq, 0))
                val = jnp.zeros((tk, tq), F32)
                for b in range(N_BUCKETS):
                    val = jnp.where(bucket == b, rb_ref[b * n_heads + head0 + hd], val)
                bias_ref[hd, idx] = (val - far) * LOG2E

    first = lax.broadcasted_iota(jnp.int32, (1, e), 1) < head_dim
    for hd in range(n_hd):
        k_rows = k_ref[0, hd * e:(hd + 1) * e, :].T
        kb_ref[2 * hd] = jnp.where(first, k_rows, 0.0).astype(BF16)
        kb_ref[2 * hd + 1] = jnp.where(first, 0.0, k_rows).astype(BF16)
        vb_ref[hd, 0:e, :] = v_ref[0, hd * e:(hd + 1) * e, :]
        vb_ref[hd, e:, :] = _ones_rows(seq)
    lam = _lambda_value(lam_ref, lambda_init)

    def q_block(i, carry):
        qs = i * tq
        q_t = [q_ref[0, hd * e:(hd + 1) * e, pl.ds(qs, tq)] for hd in range(n_hd)]

        def qk(j):
            ks = j * tk
            return [_dot(kb_ref[mp, pl.ds(ks, tk), :], q_t[mp // 2]) for mp in range(2 * n_hd)]

        def softmax(scores, j, state, near_idx=None):
            if near_idx is not None:
                scores = [s + bias_ref[mp // 2, near_idx] for mp, s in enumerate(scores)]
                if near[near_idx] <= 0:
                    visible = kmq <= near[near_idx]
                    scores = [jnp.where(visible, s, NEG_INF) for s in scores]
            ks = j * tk
            values = [vb_ref[mp // 2, :, pl.ds(ks, tk)] for mp in range(2 * n_hd)]
            return _flash_step(scores, values, state)

        ring = _ScoreRing(s_ref, qk)
        kinds = ([None] * (ratio * i - 1) + [0] if i >= 1 else []) + [1 + d for d in range(ratio)]
        ring.fill(0, 0)
        state = ring.tail(softmax, _flash_init(2 * n_hd, vb_ref.shape[1], tq), 0, kinds, False)
        outs = []
        for hd in range(n_hd):
            (_, acc0), (_, acc1) = state[2 * hd:2 * hd + 2]
            o_t = acc0[0:e] / acc0[e:e + 1] - lam * (acc1[0:e] / acc1[e:e + 1])
            outs.append(_rms(o_t.T, subg_ref[...]) * (1.0 - lambda_init))
        o_ref[0, pl.ds(qs, tq), :] = jnp.concatenate(outs, axis=1).astype(o_ref.dtype)
        return carry

    for i in range(seq // tq):
        q_block(i, 0)


def _diff_attention(rb_flat, q_t, k_t, v, lam_p, subg, *, n_heads, head_dim, lambda_init):
    b, w, s = q_t.shape
    e = 2 * head_dim
    tq, tk = ATTN_TQ, ATTN_TK
    assert s % tq == 0 and tq % tk == 0 and tk % LANES == 0 and tk >= MAX_DISTANCE and e == LANES
    n_hd = DIFF_HEADS_PER_STEP
    assert n_heads % n_hd == 0
    t_blk = pl.BlockSpec((1, n_hd * e, s), lambda g, bi, *_: (bi, g, 0))
    r_blk = pl.BlockSpec((1, s, n_hd * e), lambda g, bi, *_: (bi, 0, g))
    return pl.pallas_call(
        functools.partial(_diff_attn_kernel, n_heads=n_heads, head_dim=head_dim, lambda_init=lambda_init),
        out_shape=jax.ShapeDtypeStruct((b, s, w), BF16),
        grid_spec=pltpu.PrefetchScalarGridSpec(
            num_scalar_prefetch=1,
            grid=(n_heads // n_hd, b),
            in_specs=[t_blk, t_blk, t_blk, _resident(lam_p.shape), _resident(subg.shape)],
            out_specs=r_blk,
            scratch_shapes=[pltpu.VMEM((2 * n_hd, s, e), BF16), pltpu.VMEM((n_hd, e + BF16_ROWS, s), BF16),
                            pltpu.VMEM((n_hd, tq // tk + 1, tk, tq), F32),
                            pltpu.VMEM((2, 2 * n_hd, tk, tq), F32)]),
        compiler_params=_params(("arbitrary", "arbitrary")),
        name="diff_attention",
    )(rb_flat, q_t, k_t, v, lam_p, subg)


def _fox_attn_kernel(q_ref, k_ref, v_ref, logf_ref, o_ref, kb_ref, vb_ref, c_ref, s_ref, *, head_dim):
    tq, tk = ATTN_TQ, ATTN_TK
    ratio = tq // tk
    group = pl.program_id(1)
    seq = q_ref.shape[2]
    e = 2 * head_dim
    n_h = kb_ref.shape[0]
    kmq = _key_minus_query(tk, tq)

    @pl.when(group == 0)
    def _():
        upper = (lax.broadcasted_iota(jnp.int32, (tk, tk), 0)
                 <= lax.broadcasted_iota(jnp.int32, (tk, tk), 1)).astype(F32)
        carry = jnp.zeros((logf_ref.shape[1], 1), F32)
        for blk in range(seq // tk):
            c = _dot(logf_ref[0, :, blk * tk:(blk + 1) * tk], upper, HIGHEST) + carry
            c_ref[:, blk * tk:(blk + 1) * tk] = c
            carry = c[:, tk - 1:tk]

    lane = lax.broadcasted_iota(jnp.int32, (1, e), 1)
    for hh in range(n_h):
        pair, half = divmod(hh, 2)
        k_rows = k_ref[0, pair * e:(pair + 1) * e, :].T
        spare = (1 - half) * head_dim
        rest = jnp.broadcast_to(c_ref[pl.ds(n_h * group + hh, 1), :] * LOG2E, (LANES, seq)).T
        k_aug = jnp.where(lane // head_dim == half, k_rows, 0.0)
        for part in range(DECAY_PARTS):
            piece = rest.astype(BF16).astype(F32)
            k_aug = jnp.where(lane == spare + part, piece, k_aug)
            rest = rest - piece
        kb_ref[hh] = k_aug.astype(BF16)
        vb_ref[hh, 0:head_dim, :] = v_ref[0, hh * head_dim:(hh + 1) * head_dim, :].astype(BF16)
        vb_ref[hh, head_dim:, :] = _ones_rows(seq)

    def q_block(i, carry):
        qs = i * tq
        row = lax.broadcasted_iota(jnp.int32, (e, 1), 0)
        q_aug = []
        for hh in range(n_h):
            pair, half = divmod(hh, 2)
            q_t = q_ref[0, pair * e:(pair + 1) * e, pl.ds(qs, tq)]
            spare = (1 - half) * head_dim
            minus_one = jnp.logical_and(row >= spare, row < spare + DECAY_PARTS)
            q_aug.append(jnp.where(row // head_dim == half, q_t,
                                   jnp.where(minus_one, -1.0, 0.0).astype(BF16)))

        def qk(j):
            ks = j * tk
            return [_dot(kb_ref[hh, pl.ds(ks, tk), :], q_aug[hh]) for hh in range(n_h)]

        def softmax(scores, j, state, diagonal=None):
            ks = j * tk
            if diagonal is not None:
                visible = kmq <= -diagonal * tk
                scores = [jnp.where(visible, s, NEG_INF) for s in scores]
            values = [vb_ref[hh, :, pl.ds(ks, tk)] for hh in range(n_h)]
            return _flash_step(scores, values, state)

        ring = _ScoreRing(s_ref, qk)
        kinds = [None] * (ratio * i) + list(range(ratio))
        ring.fill(0, 0)
        state = ring.tail(softmax, _flash_init(n_h, vb_ref.shape[1], tq), 0, kinds, False)
        o_t = jnp.concatenate([acc[0:head_dim] / acc[head_dim:head_dim + 1] for _, acc in state], axis=0)
        o_ref[0, pl.ds(qs, tq), :] = o_t.T.astype(o_ref.dtype)
        return carry

    for i in range(seq // tq):
        q_block(i, 0)


def _fox_attention(q_t, k_t, v_t, logf_t, *, head_dim):
    b, w, s = q_t.shape
    e = 2 * head_dim
    tq, tk = ATTN_TQ, ATTN_TK
    n_f = logf_t.shape[1]
    assert s % tq == 0 and tq % tk == 0 and tk % LANES == 0 and e == LANES and w // e * 2 == n_f
    assert DECAY_PARTS <= head_dim
    cols = FOX_PAIRS_PER_STEP * e
    n_h = 2 * FOX_PAIRS_PER_STEP
    assert w % cols == 0
    t_blk = pl.BlockSpec((1, cols, s), lambda bi, g: (bi, g, 0))
    return pl.pallas_call(
        functools.partial(_fox_attn_kernel, head_dim=head_dim),
        out_shape=jax.ShapeDtypeStruct((b, s, w), BF16),
        grid=(b, w // cols),
        in_specs=[t_blk, t_blk, t_blk, pl.BlockSpec((1, n_f, s), lambda bi, g: (bi, 0, 0))],
        out_specs=pl.BlockSpec((1, s, cols), lambda bi, g: (bi, 0, g)),
        scratch_shapes=[pltpu.VMEM((n_h, s, e), BF16), pltpu.VMEM((n_h, head_dim + BF16_ROWS, s), BF16),
                        pltpu.VMEM((n_f, s), F32), pltpu.VMEM((2, n_h, tk, tq), F32)],
        compiler_params=_params(("arbitrary", "arbitrary")),
        name="fox_attention",
    )(q_t, k_t, v_t, logf_t)


class _Rider(NamedTuple):
    first: Any
    second: Any
    n_tiled: int
    operands: tuple
    in_specs: tuple
    out_shape: Any
    out_spec: Any


REQUESTS_PER_STEP = 2


def _decode_step_kernel(pt_ref, qd_ref, kd_ref, vd_ref, qf_ref, kf_ref, vf_ref, lfn_ref,
                        rbt_ref, lam_ref, subg_ref, *rest,
                        n_pages, page_size, head_dim, lambda_init, rider, n_rider_in):
    rider_in, rest = rest[:n_rider_in], rest[n_rider_in:]
    (hbm_dk, hbm_dv, hbm_fk, hbm_fv, hbm_lf), rest = rest[:5], rest[5:]
    n_out = len(rest) - 7
    od_ref, of_ref, *rider_out = rest[:n_out]
    kd_buf, vd_buf, kf_buf, vf_buf, lf_buf, sems, bias_ref = rest[n_out:]
    step = pl.program_id(0)
    last = pl.num_programs(0) - 1
    n_maps = bias_ref.shape[0]
    key_side = ((hbm_dk, kd_buf), (hbm_fk, kf_buf), (hbm_lf, lf_buf)), sems.at[0]
    value_sides = [(((hbm_dv, vd_buf.at[slot]), (hbm_fv, vf_buf.at[slot])), sems.at[1 + slot])
                   for slot in range(REQUESTS_PER_STEP)]

    def copies(request, side):
        pairs, sem = side
        for p in range(n_pages):
            page = pt_ref[request, n_pages - 1 - p]
            for cache, buf in pairs:
                yield pltpu.make_async_copy(cache.at[page], buf.at[p], sem)

    def start(request, side):
        for copy in copies(request, side):
            copy.start()

    def wait(request, side):
        for copy in copies(request, side):
            copy.wait()

    @pl.when(step == 0)
    def _():
        start(0, key_side)
        start(0, value_sides[0])
        lane = lax.broadcasted_iota(jnp.int32, (n_maps, page_size), 1)
        bucket = _rel_bucket(page_size - lane)
        val = jnp.zeros((n_maps, page_size), F32)
        for b in range(N_BUCKETS):
            val = jnp.where(bucket == b, rbt_ref[:, b:b + 1], val)
        bias_ref[...] = (val - rbt_ref[:, N_BUCKETS - 1:N_BUCKETS]) * LOG2E

    rows = rider_out[0].shape[0] // REQUESTS_PER_STEP if rider is not None else 0
    for k in range(REQUESTS_PER_STEP):
        req = REQUESTS_PER_STEP * step + k
        wait(req, key_side)

        def scores_done(req=req, k=k):
            def fetch_next():
                start(req + 1, key_side)
                start(req + 1, value_sides[(k + 1) % REQUESTS_PER_STEP])
            if k + 1 < REQUESTS_PER_STEP:
                fetch_next()
            else:
                pl.when(step < last)(fetch_next)
            wait(req, value_sides[k])

        pages = [[buf.at[pl.ds(p, 1)] for p in range(n_pages)]
                 for buf in (kd_buf, vd_buf.at[k], kf_buf, vf_buf.at[k], lf_buf)]
        tile = lambda ref: ref.at[pl.ds(k * rows, rows)]
        n_tiled = rider.n_tiled if rider is not None else 0
        _decode_one_request(
            req, pages, scores_done, qd_ref, kd_ref, vd_ref, qf_ref, kf_ref, vf_ref, lfn_ref, rbt_ref,
            lam_ref, subg_ref, od_ref, of_ref, bias_ref,
            [tile(ref) for ref in rider_in[:n_tiled]] + list(rider_in[n_tiled:]),
            [tile(ref) for ref in rider_out],
            page_size=page_size, head_dim=head_dim, lambda_init=lambda_init, rider=rider)


def _decode_one_request(req, pages, scores_done, qd_ref, kd_ref, vd_ref, qf_ref, kf_ref, vf_ref,
                        lfn_ref, rbt_ref, lam_ref, subg_ref, od_ref, of_ref, bias_ref, rider_in,
                        rider_out, *, page_size, head_dim, lambda_init, rider):
    cdk, cdv, cfk, cfv, clf = pages
    n_pages = len(cdk)
    width = qd_ref.shape[-1]
    n_maps = width // head_dim
    n_dh = n_maps // 2
    e = 2 * head_dim
    sub = req % qd_ref.shape[0]
    new_row = lambda ref: ref[pl.ds(sub, 1), :]

    map_of_lane = lax.broadcasted_iota(jnp.int32, (n_maps, width), 1) // head_dim
    row = lax.broadcasted_iota(jnp.int32, (n_maps, width), 0)
    own = map_of_lane == row
    row_e = lax.broadcasted_iota(jnp.int32, (n_maps, e), 0)
    far_bias = rbt_ref[:, N_BUCKETS - 1:N_BUCKETS]

    qd = jnp.where(own, new_row(qd_ref), 0.0).astype(BF16)
    qf = jnp.where(own, new_row(qf_ref), 0.0).astype(BF16)

    def self_score(q, k_ref):
        k = new_row(k_ref).astype(BF16).astype(F32)
        return jnp.sum(q.astype(F32) * k, axis=-1, keepdims=True)

    def softmax(s, s_self):
        m = jnp.maximum(jnp.max(s, axis=-1, keepdims=True), s_self)
        p, p_self = jnp.exp2(s - m), jnp.exp2(s_self - m)
        return p, p_self, jnp.sum(p, axis=-1, keepdims=True) + p_self

    def keys_side_by_side(refs):
        return jnp.concatenate([ref[0].astype(BF16) for ref in refs], axis=1)

    bias = jnp.concatenate([bias_ref[...], jnp.zeros((n_maps, (n_pages - 1) * page_size), F32)], axis=1)
    s_d = _dot(qd, keys_side_by_side(cdk)) + bias
    later = (lax.broadcasted_iota(jnp.int32, (page_size, page_size), 0)
             >= lax.broadcasted_iota(jnp.int32, (page_size, page_size), 1)).astype(F32)
    lane = lax.broadcasted_iota(jnp.int32, lfn_ref.shape[1:], 1)
    carry = jnp.sum(jnp.where(lane == req, lfn_ref[0], 0.0), axis=1, keepdims=True) * LOG2E
    lf = jnp.concatenate([ref[0] for ref in clf], axis=0) * LOG2E
    incl = _dot(lf, later, HIGHEST)
    decay = []
    for p_idx in range(n_pages):
        rows = slice(p_idx * n_maps, (p_idx + 1) * n_maps)
        decay.append(incl[rows] - lf[rows] + carry)
        carry = carry + incl[rows, 0:1]
    s_f = _dot(qf, keys_side_by_side(cfk)) + jnp.concatenate(decay, axis=1)
    scores_done()
    if rider is not None:
        rider.second(rider.first(*rider_in), *rider_in, *rider_out)

    p, p_self, l = softmax(s_d, self_score(qd, kd_ref) + (rbt_ref[:, 0:1] - far_bias) * LOG2E)
    p = p.astype(BF16)
    v_new = vd_ref[pl.ds(sub * n_dh, n_dh), :]
    acc = jnp.zeros((n_maps, e), F32)
    for h in range(n_dh):
        v_h = jnp.concatenate([ref[0, pl.ds(h, page_size, stride=n_dh), :].astype(BF16) for ref in cdv],
                              axis=0)
        acc = jnp.where(row_e // 2 == h, _dot(p, v_h) + p_self * v_new[h:h + 1, :], acc)
    lam = _lambda_value(lam_ref, lambda_init)
    sign = jnp.where(row_e[:, 0:1] % 2 == 0, 1.0, -lam)
    pick = (lax.broadcasted_iota(jnp.int32, (n_dh, n_maps), 1) // 2
            == lax.broadcasted_iota(jnp.int32, (n_dh, n_maps), 0)).astype(F32)
    scaled = acc / l * sign

    p, p_self, l = softmax(s_f, self_score(qf, kf_ref))
    acc = jnp.zeros((n_maps, head_dim, page_size), F32)
    for p_idx in range(n_pages):
        p_page = p[:, p_idx * page_size:(p_idx + 1) * page_size]
        acc = acc + p_page[:, None, :] * cfv[p_idx][0].reshape(n_maps, head_dim, page_size)

    def spread(col):
        return jnp.sum(jnp.where(own, col, 0.0), axis=0, keepdims=True)
    past = jnp.sum(acc.reshape(width, page_size).T, axis=0, keepdims=True)
    of_ref[pl.ds(sub, 1), :] = (past + spread(p_self) * new_row(vf_ref)) / spread(l)

    o_d = _dot(pick, scaled, HIGHEST)
    o_d = _rms(o_d, subg_ref[...]) * (1.0 - lambda_init)
    od_ref[pl.ds(sub, 1), :] = jnp.concatenate([o_d[h:h + 1] for h in range(n_dh)], axis=1)


def _decode_requests(page_table, qd, kd, vd, qf, kf, vf, logf_new, rbt, lam_p, subg,
                     cdk, cdv, cfk, cfv, clf, *, head_dim, lambda_init, rider=None):
    r, n_pages = page_table.shape
    _, width, page_size = cdk.shape
    n_f = clf.shape[1]
    n_maps = width // head_dim
    n_dh = n_maps // 2
    e = 2 * head_dim
    per_step = REQUESTS_PER_STEP
    assert n_maps == n_f and r % SUBLANES == 0 and SUBLANES % per_step == 0
    assert page_size >= MAX_DISTANCE
    steps_per_block = SUBLANES // per_step
    row = pl.BlockSpec((SUBLANES, width), lambda i, pt: (i // steps_per_block, 0))
    head_rows = pl.BlockSpec((SUBLANES * n_dh, e), lambda i, pt: (i // steps_per_block, 0))
    caches = [cdk, cdv, cfk, cfv, clf]
    value_side = (cdv, cfv)

    in_specs = [row, row, head_rows, row, row, row, _resident(logf_new.shape),
                _resident(rbt.shape), _resident(lam_p.shape), _resident(subg.shape)]
    operands = [qd, kd, vd, qf, kf, vf, logf_new, rbt, lam_p, subg]
    out_shape = [jax.ShapeDtypeStruct((r, width), F32)] * 2
    out_specs = [row, row]
    if rider is not None:
        in_specs += list(rider.in_specs)
        operands += list(rider.operands)
        out_shape.append(rider.out_shape)
        out_specs.append(rider.out_spec)
    in_specs += [pl.BlockSpec(memory_space=pl.ANY)] * len(caches)
    operands += caches
    return pl.pallas_call(
        functools.partial(_decode_step_kernel, n_pages=n_pages, page_size=page_size,
                          head_dim=head_dim, lambda_init=lambda_init,
                          rider=rider, n_rider_in=len(rider.operands) if rider else 0),
        out_shape=out_shape,
        grid_spec=pltpu.PrefetchScalarGridSpec(
            num_scalar_prefetch=1,
            grid=(r // per_step,),
            in_specs=in_specs,
            out_specs=out_specs,
            scratch_shapes=[pltpu.VMEM(((per_step,) if any(c is v for v in value_side) else ())
                                       + (n_pages,) + c.shape[1:], c.dtype) for c in caches]
                           + [pltpu.SemaphoreType.DMA((1 + per_step,)),
                              pltpu.VMEM((n_maps, page_size), F32)]),
        compiler_params=_params(("arbitrary",), DECODE_VMEM_LIMIT),
        name="decode_attention",
    )(page_table, *operands)


def _merge_rows(x, od_ref, of_ref, g1_ref, wg_ref, gb_ref, wa_ref, wb_ref, wo_ref):
    d = x.shape[-1]
    dt = wg_ref.dtype
    h = _rms(x, g1_ref[...]).astype(dt)
    ya = _dot(od_ref[...].astype(dt), wa_ref[...])
    yb = _dot(of_ref[...].astype(dt), wb_ref[...])
    gate_a = jax.nn.sigmoid(_dot(h, wg_ref[:, :d]) + gb_ref[:, :d])
    gate_b = jax.nn.sigmoid(_dot(h, wg_ref[:, d:]) + gb_ref[:, d:])
    merged = gate_a * ya + gate_b * yb
    return x + _dot(merged.astype(dt), wo_ref[...])


def _ffn_hidden(x, g2_ref, wgu_ref, wdn_ref):
    d_ff = wdn_ref.shape[0]
    h = _rms(x, g2_ref[...]).astype(wgu_ref.dtype)
    gate = _dot(h, wgu_ref[:, :d_ff])
    up = _dot(h, wgu_ref[:, d_ff:])
    return (gate * jax.nn.sigmoid(gate) * up).astype(wdn_ref.dtype)


N_MERGE_REFS = 9


def _tail_first(*refs):
    x_ref, *merge_refs = refs[:N_MERGE_REFS]
    return _merge_rows(x_ref[...], *merge_refs)


def _tail_second(x1, *refs):
    *ffn_refs, o_ref = refs[N_MERGE_REFS:]
    o_ref[...] = x1 + _dot(_ffn_hidden(x1, *ffn_refs), ffn_refs[-1][...])


def _tail_kernel(*refs):
    in_refs = refs[:-1]
    _tail_second(_tail_first(*in_refs), *refs)


def _tail(x, od, of, weights):
    m, d = x.shape
    tm = min(ROW_TILE, m)
    assert m % tm == 0
    row = lambda c: pl.BlockSpec((tm, c), lambda i: (i, 0))
    return pl.pallas_call(
        _tail_kernel,
        out_shape=jax.ShapeDtypeStruct((m, d), F32),
        grid=(m // tm,),
        in_specs=[row(d), row(od.shape[1]), row(of.shape[1])] + [_resident(w.shape) for w in weights],
        out_specs=row(d),
        compiler_params=_params(("arbitrary",)),
        name="tail",
    )(x, od, of, *weights)


def kernel(x_prompt, x_sample, cache_diff_k, cache_diff_v, cache_fox_k, cache_fox_v, cache_fox_logf,
           page_table, rel_bias, norm1_g, w_in, diff_q_g, diff_k_g, fox_q_g, fox_k_g, diff_lambda,
           fox_f_b, gate_b, diff_subln_g, w_branch_a, w_branch_b, w_out, norm2_g, w_gate_up, w_down):
    depth = w_in.shape[0]
    batch, seq, d = x_prompt.shape
    dec_batch, dec_seq, _ = x_sample.shape
    assert dec_seq == 1
    _, n_phys, page_size, n_dh, _, head_dim = cache_diff_k.shape
    n_f = cache_fox_k.shape[3]
    e = 2 * head_dim
    width = n_dh * e
    assert n_f * head_dim == width

    rb_flat = rel_bias.reshape(-1)
    rbt = jnp.repeat(rel_bias.T, 2, axis=0)

    yp, ys = x_prompt, x_sample.reshape(1, dec_batch, d)
    rows_p, rows_s = [], []
    for l in range(depth):
        lambda_init = 0.8 - 0.6 * math.exp(-0.3 * l)
        w_t = w_in[l].T
        fb = fox_f_b[l].reshape(n_f, 1)
        gains = jnp.stack([jnp.tile(g[l], width // head_dim)
                           for g in (diff_q_g, diff_k_g, fox_q_g, fox_k_g)]).reshape(4, width, 1)
        g1 = norm1_g[l].reshape(1, d)
        g2 = norm2_g[l].reshape(1, d)
        gb = gate_b[l].reshape(1, 2 * d)
        subg = diff_subln_g[l].reshape(1, e)
        wg = w_t[6 * width + n_f:].T
        merge_w = (g1, wg.astype(BF16), gb, w_branch_a[l].astype(BF16), w_branch_b[l].astype(BF16),
                   w_out[l].astype(BF16))
        ffn_w = (g2, w_gate_up[l].astype(BF16), w_down[l].astype(BF16))
        lam_p = diff_lambda[l]

        def cache_rows(kd_t, vd, kf_t, vf_t, logf_t):
            b, _, s = kd_t.shape
            return (jnp.transpose(kd_t.reshape(b, n_dh, 2, head_dim, s), (0, 4, 1, 2, 3)),
                    vd.reshape(b, s, n_dh, e),
                    jnp.transpose(kf_t.reshape(b, n_f, head_dim, s), (0, 3, 1, 2)),
                    jnp.transpose(vf_t.reshape(b, n_f, head_dim, s), (0, 3, 1, 2)),
                    jnp.transpose(logf_t, (0, 2, 1)))

        qd_t, kd_t, vd, vd_t, qf_t, kf_t, vf_t, logf_t = _proj(
            yp, g1, w_t, gains, fb, width=width, head_dim=head_dim, with_rows=False)
        od = _diff_attention(rb_flat, qd_t, kd_t, vd_t, lam_p, subg,
                             n_heads=n_dh, head_dim=head_dim, lambda_init=lambda_init)
        of = _fox_attention(qf_t, kf_t, vf_t, logf_t, head_dim=head_dim)
        m_p = batch * seq
        assert m_p % dec_batch == 0
        rows = lambda c: pl.BlockSpec((REQUESTS_PER_STEP * (m_p // dec_batch), c), lambda i, pt: (i, 0))
        tiled = (yp.reshape(m_p, d), od.reshape(m_p, width), of.reshape(m_p, width))
        weights = merge_w + ffn_w
        assert len(tiled) + len(merge_w) == N_MERGE_REFS
        rider = _Rider(_tail_first, _tail_second, len(tiled), tiled + weights,
                       tuple(rows(a.shape[1]) for a in tiled) + tuple(_resident(w.shape) for w in weights),
                       jax.ShapeDtypeStruct((m_p, d), F32), rows(d))
        rows_p.append(cache_rows(kd_t, vd, kf_t, vf_t, logf_t))

        (qd_t, kd_t, vd, _, qf_t, kf_t, vf_t, logf_t, qd_r, kd_r, qf_r, kf_r, vf_r) = _proj(
            ys, g1, w_t, gains, fb, width=width, head_dim=head_dim, with_rows=True)
        as_row = lambda a: a.reshape(dec_batch, width)
        cdk = jnp.transpose(cache_diff_k[l], (0, 2, 3, 4, 1)).reshape(n_phys, width, page_size)
        cdv = cache_diff_v[l].reshape(n_phys, page_size * n_dh, e)
        cfk = jnp.transpose(cache_fox_k[l], (0, 2, 3, 1)).reshape(n_phys, width, page_size)
        cfv = jnp.transpose(cache_fox_v[l], (0, 2, 3, 1)).reshape(n_phys, width, page_size)
        clf = jnp.transpose(cache_fox_logf[l], (0, 2, 1))
        od, of, yp = _decode_requests(
            page_table, as_row(qd_r), as_row(kd_r), vd.reshape(dec_batch * n_dh, e),
            as_row(qf_r), as_row(kf_r), as_row(vf_r), logf_t, rbt, lam_p, subg,
            cdk, cdv, cfk, cfv, clf, head_dim=head_dim, lambda_init=lambda_init, rider=rider)
        yp = yp.reshape(batch, seq, d)
        ys = _tail(ys.reshape(dec_batch, d), od, of, weights).reshape(1, dec_batch, d)
        rows_s.append(tuple(jnp.swapaxes(a, 0, 1) for a in cache_rows(kd_t, vd, kf_t, vf_t, logf_t)))

    stack = lambda rows, i: jnp.stack([r[i] for r in rows], axis=0)
    return (yp, ys.reshape(dec_batch, 1, d),
            *(stack(rows_p, i) for i in range(5)), *(stack(rows_s, i) for i in range(5)))
```

```python
import functools
import math
from typing import Any, NamedTuple

import jax
import jax.numpy as jnp
from jax import lax
from jax.experimental import pallas as pl
from jax.experimental.pallas import tpu as pltpu

F32 = jnp.float32
BF16 = jnp.bfloat16

N_BUCKETS = 32
MAX_DISTANCE = 128
EPS = 1e-6
NEG_INF = -1e30
LOG2E = math.log2(math.e)

LANES = 128
SUBLANES = 8
BF16_ROWS = 16
VMEM_LIMIT = 56 * 1024 * 1024
DECODE_VMEM_LIMIT = 60 * 1024 * 1024

ROW_TILE = 512
ATTN_TQ = 512
ATTN_TK = 256
FOX_PAIRS_PER_STEP = 2
DIFF_HEADS_PER_STEP = 2
DECAY_PARTS = 3

NT_DIMS = (((1,), (1,)), ((), ()))
HIGHEST = lax.Precision.HIGHEST


def _resident(shape):
    zeros = (0,) * len(shape)
    return pl.BlockSpec(shape, lambda *_: zeros, pipeline_mode=pl.Buffered(1))


def _params(semantics, vmem=VMEM_LIMIT):
    return pltpu.CompilerParams(dimension_semantics=semantics, vmem_limit_bytes=vmem)


def _rms(x, g):
    return x * lax.rsqrt(jnp.mean(x * x, axis=-1, keepdims=True) + EPS) * g


def _dot(a, b, precision=None):
    return jnp.dot(a, b, preferred_element_type=F32, precision=precision)


def _dot_nt(a, b):
    return lax.dot_general(a, b, NT_DIMS, preferred_element_type=F32)


def _log_sigmoid(x):
    return jnp.minimum(x, 0.0) - jnp.log1p(jnp.exp(-jnp.abs(x)))


def _rel_bucket(n):
    max_exact = N_BUCKETS // 2
    nf = jnp.maximum(n, 1).astype(F32)
    large = max_exact + (jnp.log(nf / max_exact) / math.log(MAX_DISTANCE / max_exact)
                         * (N_BUCKETS - max_exact)).astype(jnp.int32)
    return jnp.where(n < max_exact, n, jnp.minimum(large, N_BUCKETS - 1))


def _lambda_value(lam_ref, lambda_init):
    lp = lam_ref[...]
    a = jnp.sum(lp[0:1] * lp[1:2], axis=-1, keepdims=True)
    b = jnp.sum(lp[2:3] * lp[3:4], axis=-1, keepdims=True)
    return jnp.exp(a) - jnp.exp(b) + lambda_init


def _proj_kernel(x_ref, g1_ref, wt_ref, gains_ref, fb_ref,
                 qd_ref, kd_ref, vd_ref, vdt_ref, qf_ref, kf_ref, vf_ref, logf_ref, *row_refs,
                 head_dim, q_scale):
    h = _rms(x_ref[0], g1_ref[...])
    width = kd_ref.shape[1]
    n_heads = width // head_dim
    n_f = logf_ref.shape[1]
    section = lambda i: _dot_nt(wt_ref[i * width:(i + 1) * width, :], h)

    def head_norm_t(z, i):
        z3 = z.reshape(n_heads, head_dim, z.shape[-1])
        ss = jnp.sum(z3 * z3, axis=1, keepdims=True)
        gain = gains_ref[i].reshape(n_heads, head_dim, 1)
        return (z3 * lax.rsqrt(ss * (1.0 / head_dim) + EPS) * gain).reshape(z.shape)

    qd = head_norm_t(section(0), 0) * q_scale
    kd = head_norm_t(section(1), 1)
    qf = head_norm_t(section(3), 2) * q_scale
    kf = head_norm_t(section(4), 3)
    vd_t = section(2)
    vd_rows = vd_t.T
    vf = section(5)
    qd_ref[0] = qd.astype(qd_ref.dtype)
    kd_ref[0] = kd
    e = vd_ref.shape[-1]
    for hd in range(width // e):
        vd_ref[0, pl.ds(hd, vd_rows.shape[0], stride=width // e), :] = vd_rows[:, hd * e:(hd + 1) * e]
    vdt_ref[0] = vd_t.astype(vdt_ref.dtype)
    qf_ref[0] = qf.astype(qf_ref.dtype)
    kf_ref[0] = kf
    vf_ref[0] = vf
    logf_ref[0] = _log_sigmoid(_dot_nt(wt_ref[6 * width:6 * width + n_f, :], h) + fb_ref[...])
    if row_refs:
        for ref, val in zip(row_refs, (qd, kd, qf, kf, vf)):
            ref[0] = val.T


def _proj(x, g1, wt, gains, fb, *, width, head_dim, with_rows):
    b, s, d = x.shape
    n_f = fb.shape[0]
    e = 2 * head_dim
    tm = min(ROW_TILE, s)
    assert s % tm == 0 and tm % LANES == 0 and width % e == 0
    t_blk = lambda r: pl.BlockSpec((1, r, tm), lambda bi, si: (bi, 0, si))
    r_blk = lambda c: pl.BlockSpec((1, tm, c), lambda bi, si: (bi, si, 0))
    t_shape = lambda dt: jax.ShapeDtypeStruct((b, width, s), dt)
    r_shape = jax.ShapeDtypeStruct((b, s, width), F32)
    heads_per_token = width // e
    out_shape = [t_shape(BF16), t_shape(F32), jax.ShapeDtypeStruct((b, s * heads_per_token, e), F32),
                 t_shape(BF16), t_shape(BF16), t_shape(F32), t_shape(F32),
                 jax.ShapeDtypeStruct((b, n_f, s), F32)]
    out_specs = [t_blk(width), t_blk(width),
                 pl.BlockSpec((1, tm * heads_per_token, e), lambda bi, si: (bi, si, 0)),
                 t_blk(width), t_blk(width), t_blk(width), t_blk(width), t_blk(n_f)]
    if with_rows:
        out_shape += [r_shape] * 5
        out_specs += [r_blk(width)] * 5
    return pl.pallas_call(
        functools.partial(_proj_kernel, head_dim=head_dim, q_scale=head_dim ** -0.5 * LOG2E),
        out_shape=out_shape,
        grid=(b, s // tm),
        in_specs=[r_blk(d)] + [_resident(a.shape) for a in (g1, wt, gains, fb)],
        out_specs=out_specs,
        compiler_params=_params(("arbitrary", "arbitrary")),
        name="proj",
    )(x, g1, wt, gains, fb)


def _flash_step(scores, values, state):
    new_state = []
    for s_t, v_t, (m_prev, acc_prev) in zip(scores, values, state):
        m_new = jnp.maximum(m_prev, jnp.max(s_t, axis=0, keepdims=True))
        alpha = jnp.exp2(m_prev - m_new)
        p_t = jnp.exp2(s_t - m_new).astype(BF16)
        new_state.append((m_new, alpha * acc_prev + _dot(v_t, p_t)))
    return tuple(new_state)


def _flash_init(n_slots, rows, tq):
    return tuple((jnp.full((1, tq), NEG_INF, F32), jnp.zeros((rows, tq), F32)) for _ in range(n_slots))


def _ones_rows(seq):
    return jnp.where(lax.broadcasted_iota(jnp.int32, (BF16_ROWS, seq), 0) == 0, 1.0, 0.0).astype(BF16)


def _key_minus_query(tk, tq):
    return lax.broadcasted_iota(jnp.int32, (tk, tq), 0) - lax.broadcasted_iota(jnp.int32, (tk, tq), 1)


class _ScoreRing:
    def __init__(self, s_ref, qk):
        self.s_ref, self.qk = s_ref, qk

    def fill(self, j, slot):
        for idx, s in enumerate(self.qk(j)):
            self.s_ref[slot, idx] = s

    def scores(self, slot):
        return [self.s_ref[slot, idx] for idx in range(self.s_ref.shape[1])]

    def sweep(self, softmax, state, kinds):
        self.fill(0, 0)
        for j, kind in enumerate(kinds):
            slot = j % 2
            if j + 1 < len(kinds):
                self.fill(j + 1, 1 - slot)
            state = softmax(self.scores(slot), j, state, kind)
        return state


def _diff_attn_kernel(rb_ref, q_ref, k_ref, v_ref, lam_ref, subg_ref, o_ref,
                      kb_ref, vb_ref, bias_ref, s_ref, *, n_heads, head_dim, lambda_init):
    tq, tk = ATTN_TQ, ATTN_TK
    ratio = tq // tk
    n_hd = vb_ref.shape[0]
    head0 = n_hd * pl.program_id(0)
    seq = q_ref.shape[2]
    e = 2 * head_dim
    kmq = _key_minus_query(tk, tq)
    near = [tk] + [-d * tk for d in range(ratio)]

    @pl.when(pl.program_id(1) == 0)
    def _():
        for hd in range(n_hd):
            far = rb_ref[(N_BUCKETS - 1) * n_heads + head0 + hd]
            for idx, delta in enumerate(near):
                bucket = _rel_bucket(jnp.maximum(delta - kmq, 0))
                val = jnp.zeros((tk, tq), F32)
                for b in range(N_BUCKETS):
                    val = jnp.where(bucket == b, rb_ref[b * n_heads + head0 + hd], val)
                bias_ref[hd, idx] = (val - far) * LOG2E

    first = lax.broadcasted_iota(jnp.int32, (1, e), 1) < head_dim
    for hd in range(n_hd):
        k_rows = k_ref[0, hd * e:(hd + 1) * e, :].T
        kb_ref[2 * hd] = jnp.where(first, k_rows, 0.0).astype(BF16)
        kb_ref[2 * hd + 1] = jnp.where(first, 0.0, k_rows).astype(BF16)
        vb_ref[hd, 0:e, :] = v_ref[0, hd * e:(hd + 1) * e, :]
        vb_ref[hd, e:, :] = _ones_rows(seq)
    lam = _lambda_value(lam_ref, lambda_init)

    def q_block(i, carry):
        qs = i * tq
        q_t = [q_ref[0, hd * e:(hd + 1) * e, pl.ds(qs, tq)] for hd in range(n_hd)]

        def qk(j):
            ks = j * tk
            return [_dot(kb_ref[mp, pl.ds(ks, tk), :], q_t[mp // 2]) for mp in range(2 * n_hd)]

        def softmax(scores, j, state, near_idx=None):
            if near_idx is not None:
                scores = [s + bias_ref[mp // 2, near_idx] for mp, s in enumerate(scores)]
                if near[near_idx] <= 0:
                    visible = kmq <= near[near_idx]
                    scores = [jnp.where(visible, s, NEG_INF) for s in scores]
            ks = j * tk
            values = [vb_ref[mp // 2, :, pl.ds(ks, tk)] for mp in range(2 * n_hd)]
            return _flash_step(scores, values, state)

        ring = _ScoreRing(s_ref, qk)
        kinds = ([None] * (ratio * i - 1) + [0] if i >= 1 else []) + [1 + d for d in range(ratio)]
        state = ring.sweep(softmax, _flash_init(2 * n_hd, vb_ref.shape[1], tq), kinds)
        outs = []
        for hd in range(n_hd):
            (_, acc0), (_, acc1) = state[2 * hd:2 * hd + 2]
            o_t = acc0[0:e] / acc0[e:e + 1] - lam * (acc1[0:e] / acc1[e:e + 1])
            outs.append(_rms(o_t.T, subg_ref[...]) * (1.0 - lambda_init))
        o_ref[0, pl.ds(qs, tq), :] = jnp.concatenate(outs, axis=1).astype(o_ref.dtype)
        return carry

    for i in range(seq // tq):
        q_block(i, 0)


def _diff_attention(rb_flat, q_t, k_t, v, lam_p, subg, *, n_heads, head_dim, lambda_init):
    b, w, s = q_t.shape
    e = 2 * head_dim
    tq, tk = ATTN_TQ, ATTN_TK
    assert s % tq == 0 and tq % tk == 0 and tk % LANES == 0 and tk >= MAX_DISTANCE and e == LANES
    n_hd = DIFF_HEADS_PER_STEP
    assert n_heads % n_hd == 0
    t_blk = pl.BlockSpec((1, n_hd * e, s), lambda g, bi, *_: (bi, g, 0))
    r_blk = pl.BlockSpec((1, s, n_hd * e), lambda g, bi, *_: (bi, 0, g))
    return pl.pallas_call(
        functools.partial(_diff_attn_kernel, n_heads=n_heads, head_dim=head_dim, lambda_init=lambda_init),
        out_shape=jax.ShapeDtypeStruct((b, s, w), BF16),
        grid_spec=pltpu.PrefetchScalarGridSpec(
            num_scalar_prefetch=1,
            grid=(n_heads // n_hd, b),
            in_specs=[t_blk, t_blk, t_blk, _resident(lam_p.shape), _resident(subg.shape)],
            out_specs=r_blk,
            scratch_shapes=[pltpu.VMEM((2 * n_hd, s, e), BF16), pltpu.VMEM((n_hd, e + BF16_ROWS, s), BF16),
                            pltpu.VMEM((n_hd, tq // tk + 1, tk, tq), F32),
                            pltpu.VMEM((2, 2 * n_hd, tk, tq), F32)]),
        compiler_params=_params(("arbitrary", "arbitrary")),
        name="diff_attention",
    )(rb_flat, q_t, k_t, v, lam_p, subg)


def _fox_attn_kernel(q_ref, k_ref, v_ref, logf_ref, o_ref, kb_ref, vb_ref, c_ref, s_ref, *, head_dim):
    tq, tk = ATTN_TQ, ATTN_TK
    ratio = tq // tk
    group = pl.program_id(1)
    seq = q_ref.shape[2]
    e = 2 * head_dim
    n_h = kb_ref.shape[0]
    kmq = _key_minus_query(tk, tq)

    @pl.when(group == 0)
    def _():
        upper = (lax.broadcasted_iota(jnp.int32, (tk, tk), 0)
                 <= lax.broadcasted_iota(jnp.int32, (tk, tk), 1)).astype(F32)
        carry = jnp.zeros((logf_ref.shape[1], 1), F32)
        for blk in range(seq // tk):
            c = _dot(logf_ref[0, :, blk * tk:(blk + 1) * tk], upper, HIGHEST) + carry
            c_ref[:, blk * tk:(blk + 1) * tk] = c
            carry = c[:, tk - 1:tk]

    lane = lax.broadcasted_iota(jnp.int32, (1, e), 1)
    for hh in range(n_h):
        pair, half = divmod(hh, 2)
        k_rows = k_ref[0, pair * e:(pair + 1) * e, :].T
        spare = (1 - half) * head_dim
        rest = jnp.broadcast_to(c_ref[pl.ds(n_h * group + hh, 1), :] * LOG2E, (LANES, seq)).T
        k_aug = jnp.where(lane // head_dim == half, k_rows, 0.0)
        for part in range(DECAY_PARTS):
            piece = rest.astype(BF16).astype(F32)
            k_aug = jnp.where(lane == spare + part, piece, k_aug)
            rest = rest - piece
        kb_ref[hh] = k_aug.astype(BF16)
        vb_ref[hh, 0:head_dim, :] = v_ref[0, hh * head_dim:(hh + 1) * head_dim, :].astype(BF16)
        vb_ref[hh, head_dim:, :] = _ones_rows(seq)

    def q_block(i, carry):
        qs = i * tq
        row = lax.broadcasted_iota(jnp.int32, (e, 1), 0)
        q_aug = []
        for hh in range(n_h):
            pair, half = divmod(hh, 2)
            q_t = q_ref[0, pair * e:(pair + 1) * e, pl.ds(qs, tq)]
            spare = (1 - half) * head_dim
            minus_one = jnp.logical_and(row >= spare, row < spare + DECAY_PARTS)
            q_aug.append(jnp.where(row // head_dim == half, q_t,
                                   jnp.where(minus_one, -1.0, 0.0).astype(BF16)))

        def qk(j):
            ks = j * tk
            return [_dot(kb_ref[hh, pl.ds(ks, tk), :], q_aug[hh]) for hh in range(n_h)]

        def softmax(scores, j, state, diagonal=None):
            ks = j * tk
            if diagonal is not None:
                visible = kmq <= -diagonal * tk
                scores = [jnp.where(visible, s, NEG_INF) for s in scores]
            values = [vb_ref[hh, :, pl.ds(ks, tk)] for hh in range(n_h)]
            return _flash_step(scores, values, state)

        ring = _ScoreRing(s_ref, qk)
        kinds = [None] * (ratio * i) + list(range(ratio))
        state = ring.sweep(softmax, _flash_init(n_h, vb_ref.shape[1], tq), kinds)
        o_t = jnp.concatenate([acc[0:head_dim] / acc[head_dim:head_dim + 1] for _, acc in state], axis=0)
        o_ref[0, pl.ds(qs, tq), :] = o_t.T.astype(o_ref.dtype)
        return carry

    for i in range(seq // tq):
        q_block(i, 0)


def _fox_attention(q_t, k_t, v_t, logf_t, *, head_dim):
    b, w, s = q_t.shape
    e = 2 * head_dim
    tq, tk = ATTN_TQ, ATTN_TK
    n_f = logf_t.shape[1]
    assert s % tq == 0 and tq % tk == 0 and tk % LANES == 0 and e == LANES and w // e * 2 == n_f
    assert DECAY_PARTS <= head_dim
    cols = FOX_PAIRS_PER_STEP * e
    n_h = 2 * FOX_PAIRS_PER_STEP
    assert w % cols == 0
    t_blk = pl.BlockSpec((1, cols, s), lambda bi, g: (bi, g, 0))
    return pl.pallas_call(
        functools.partial(_fox_attn_kernel, head_dim=head_dim),
        out_shape=jax.ShapeDtypeStruct((b, s, w), BF16),
        grid=(b, w // cols),
        in_specs=[t_blk, t_blk, t_blk, pl.BlockSpec((1, n_f, s), lambda bi, g: (bi, 0, 0))],
        out_specs=pl.BlockSpec((1, s, cols), lambda bi, g: (bi, 0, g)),
        scratch_shapes=[pltpu.VMEM((n_h, s, e), BF16), pltpu.VMEM((n_h, head_dim + BF16_ROWS, s), BF16),
                        pltpu.VMEM((n_f, s), F32), pltpu.VMEM((2, n_h, tk, tq), F32)],
        compiler_params=_params(("arbitrary", "arbitrary")),
        name="fox_attention",
    )(q_t, k_t, v_t, logf_t)


class _Rider(NamedTuple):
    first: Any
    second: Any
    n_tiled: int
    operands: tuple
    in_specs: tuple
    out_shape: Any
    out_spec: Any


REQUESTS_PER_STEP = 2


def _decode_step_kernel(pt_ref, qd_ref, kd_ref, vd_ref, qf_ref, kf_ref, vf_ref, lfn_ref,
                        rbt_ref, lam_ref, subg_ref, *rest,
                        n_pages, page_size, head_dim, lambda_init, rider, n_rider_in):
    rider_in, rest = rest[:n_rider_in], rest[n_rider_in:]
    (hbm_dk, hbm_dv, hbm_fk, hbm_fv, hbm_lf), rest = rest[:5], rest[5:]
    n_out = len(rest) - 7
    od_ref, of_ref, *rider_out = rest[:n_out]
    kd_buf, vd_buf, kf_buf, vf_buf, lf_buf, sems, bias_ref = rest[n_out:]
    step = pl.program_id(0)
    last = pl.num_programs(0) - 1
    n_maps = bias_ref.shape[0]
    key_side = ((hbm_dk, kd_buf), (hbm_fk, kf_buf), (hbm_lf, lf_buf)), sems.at[0]
    value_sides = [(((hbm_dv, vd_buf.at[slot]), (hbm_fv, vf_buf.at[slot])), sems.at[1 + slot])
                   for slot in range(REQUESTS_PER_STEP)]

    def copies(request, side):
        pairs, sem = side
        for p in range(n_pages):
            page = pt_ref[request, n_pages - 1 - p]
            for cache, buf in pairs:
                yield pltpu.make_async_copy(cache.at[page], buf.at[p], sem)

    def start(request, side):
        for copy in copies(request, side):
            copy.start()

    def wait(request, side):
        for copy in copies(request, side):
            copy.wait()

    @pl.when(step == 0)
    def _():
        start(0, key_side)
        start(0, value_sides[0])
        lane = lax.broadcasted_iota(jnp.int32, (n_maps, page_size), 1)
        bucket = _rel_bucket(page_size - lane)
        val = jnp.zeros((n_maps, page_size), F32)
        for b in range(N_BUCKETS):
            val = jnp.where(bucket == b, rbt_ref[:, b:b + 1], val)
        bias_ref[...] = (val - rbt_ref[:, N_BUCKETS - 1:N_BUCKETS]) * LOG2E

    rows = rider_out[0].shape[0] // REQUESTS_PER_STEP if rider is not None else 0
    for k in range(REQUESTS_PER_STEP):
        req = REQUESTS_PER_STEP * step + k
        wait(req, key_side)

        def scores_done(req=req, k=k):
            def fetch_next():
                start(req + 1, key_side)
                start(req + 1, value_sides[(k + 1) % REQUESTS_PER_STEP])
            if k + 1 < REQUESTS_PER_STEP:
                fetch_next()
            else:
                pl.when(step < last)(fetch_next)
            wait(req, value_sides[k])

        pages = [[buf.at[pl.ds(p, 1)] for p in range(n_pages)]
                 for buf in (kd_buf, vd_buf.at[k], kf_buf, vf_buf.at[k], lf_buf)]
        tile = lambda ref: ref.at[pl.ds(k * rows, rows)]
        n_tiled = rider.n_tiled if rider is not None else 0
        _decode_one_request(
            req, pages, scores_done, qd_ref, kd_ref, vd_ref, qf_ref, kf_ref, vf_ref, lfn_ref, rbt_ref,
            lam_ref, subg_ref, od_ref, of_ref, bias_ref,
            [tile(ref) for ref in rider_in[:n_tiled]] + list(rider_in[n_tiled:]),
            [tile(ref) for ref in rider_out],
            page_size=page_size, head_dim=head_dim, lambda_init=lambda_init, rider=rider)


def _decode_one_request(req, pages, scores_done, qd_ref, kd_ref, vd_ref, qf_ref, kf_ref, vf_ref,
                        lfn_ref, rbt_ref, lam_ref, subg_ref, od_ref, of_ref, bias_ref, rider_in,
                        rider_out, *, page_size, head_dim, lambda_init, rider):
    cdk, cdv, cfk, cfv, clf = pages
    n_pages = len(cdk)
    width = qd_ref.shape[-1]
    n_maps = width // head_dim
    n_dh = n_maps // 2
    e = 2 * head_dim
    sub = req % qd_ref.shape[0]
    new_row = lambda ref: ref[pl.ds(sub, 1), :]

    map_of_lane = lax.broadcasted_iota(jnp.int32, (n_maps, width), 1) // head_dim
    row = lax.broadcasted_iota(jnp.int32, (n_maps, width), 0)
    own = map_of_lane == row
    row_e = lax.broadcasted_iota(jnp.int32, (n_maps, e), 0)
    far_bias = rbt_ref[:, N_BUCKETS - 1:N_BUCKETS]

    qd = jnp.where(own, new_row(qd_ref), 0.0).astype(BF16)
    qf = jnp.where(own, new_row(qf_ref), 0.0).astype(BF16)

    def self_score(q, k_ref):
        k = new_row(k_ref).astype(BF16).astype(F32)
        return jnp.sum(q.astype(F32) * k, axis=-1, keepdims=True)

    def softmax(s, s_self):
        m = jnp.maximum(jnp.max(s, axis=-1, keepdims=True), s_self)
        p, p_self = jnp.exp2(s - m), jnp.exp2(s_self - m)
        return p, p_self, jnp.sum(p, axis=-1, keepdims=True) + p_self

    def keys_side_by_side(refs):
        return jnp.concatenate([ref[0].astype(BF16) for ref in refs], axis=1)

    bias = jnp.concatenate([bias_ref[...], jnp.zeros((n_maps, (n_pages - 1) * page_size), F32)], axis=1)
    s_d = _dot(qd, keys_side_by_side(cdk)) + bias
    later = (lax.broadcasted_iota(jnp.int32, (page_size, page_size), 0)
             >= lax.broadcasted_iota(jnp.int32, (page_size, page_size), 1)).astype(F32)
    lane = lax.broadcasted_iota(jnp.int32, lfn_ref.shape[1:], 1)
    carry = jnp.sum(jnp.where(lane == req, lfn_ref[0], 0.0), axis=1, keepdims=True) * LOG2E
    lf = jnp.concatenate([ref[0] for ref in clf], axis=0) * LOG2E
    incl = _dot(lf, later, HIGHEST)
    decay = []
    for p_idx in range(n_pages):
        rows = slice(p_idx * n_maps, (p_idx + 1) * n_maps)
        decay.append(incl[rows] - lf[rows] + carry)
        carry = carry + incl[rows, 0:1]
    s_f = _dot(qf, keys_side_by_side(cfk)) + jnp.concatenate(decay, axis=1)
    scores_done()
    if rider is not None:
        rider.second(rider.first(*rider_in), *rider_in, *rider_out)

    p, p_self, l = softmax(s_d, self_score(qd, kd_ref) + (rbt_ref[:, 0:1] - far_bias) * LOG2E)
    p = p.astype(BF16)
    v_new = vd_ref[pl.ds(sub * n_dh, n_dh), :]
    acc = jnp.zeros((n_maps, e), F32)
    for h in range(n_dh):
        v_h = jnp.concatenate([ref[0, pl.ds(h, page_size, stride=n_dh), :].astype(BF16) for ref in cdv],
                              axis=0)
        acc = jnp.where(row_e // 2 == h, _dot(p, v_h) + p_self * v_new[h:h + 1, :], acc)
    lam = _lambda_value(lam_ref, lambda_init)
    sign = jnp.where(row_e[:, 0:1] % 2 == 0, 1.0, -lam)
    pick = (lax.broadcasted_iota(jnp.int32, (n_dh, n_maps), 1) // 2
            == lax.broadcasted_iota(jnp.int32, (n_dh, n_maps), 0)).astype(F32)
    scaled = acc / l * sign

    p, p_self, l = softmax(s_f, self_score(qf, kf_ref))
    acc = jnp.zeros((n_maps, head_dim, page_size), F32)
    for p_idx in range(n_pages):
        p_page = p[:, p_idx * page_size:(p_idx + 1) * page_size]
        acc = acc + p_page[:, None, :] * cfv[p_idx][0].reshape(n_maps, head_dim, page_size)

    def spread(col):
        return jnp.sum(jnp.where(own, col, 0.0), axis=0, keepdims=True)
    past = jnp.sum(acc.reshape(width, page_size).T, axis=0, keepdims=True)
    of_ref[pl.ds(sub, 1), :] = (past + spread(p_self) * new_row(vf_ref)) / spread(l)

    o_d = _dot(pick, scaled, HIGHEST)
    o_d = _rms(o_d, subg_ref[...]) * (1.0 - lambda_init)
    od_ref[pl.ds(sub, 1), :] = jnp.concatenate([o_d[h:h + 1] for h in range(n_dh)], axis=1)


def _decode_requests(page_table, qd, kd, vd, qf, kf, vf, logf_new, rbt, lam_p, subg,
                     cdk, cdv, cfk, cfv, clf, *, head_dim, lambda_init, rider=None):
    r, n_pages = page_table.shape
    _, width, page_size = cdk.shape
    n_f = clf.shape[1]
    n_maps = width // head_dim
    n_dh = n_maps // 2
    e = 2 * head_dim
    per_step = REQUESTS_PER_STEP
    assert n_maps == n_f and r % SUBLANES == 0 and SUBLANES % per_step == 0
    assert page_size >= MAX_DISTANCE
    steps_per_block = SUBLANES // per_step
    row = pl.BlockSpec((SUBLANES, width), lambda i, pt: (i // steps_per_block, 0))
    head_rows = pl.BlockSpec((SUBLANES * n_dh, e), lambda i, pt: (i // steps_per_block, 0))
    caches = [cdk, cdv, cfk, cfv, clf]
    value_side = (cdv, cfv)

    in_specs = [row, row, head_rows, row, row, row, _resident(logf_new.shape),
                _resident(rbt.shape), _resident(lam_p.shape), _resident(subg.shape)]
    operands = [qd, kd, vd, qf, kf, vf, logf_new, rbt, lam_p, subg]
    out_shape = [jax.ShapeDtypeStruct((r, width), F32)] * 2
    out_specs = [row, row]
    if rider is not None:
        in_specs += list(rider.in_specs)
        operands += list(rider.operands)
        out_shape.append(rider.out_shape)
        out_specs.append(rider.out_spec)
    in_specs += [pl.BlockSpec(memory_space=pl.ANY)] * len(caches)
    operands += caches
    return pl.pallas_call(
        functools.partial(_decode_step_kernel, n_pages=n_pages, page_size=page_size,
                          head_dim=head_dim, lambda_init=lambda_init,
                          rider=rider, n_rider_in=len(rider.operands) if rider else 0),
        out_shape=out_shape,
        grid_spec=pltpu.PrefetchScalarGridSpec(
            num_scalar_prefetch=1,
            grid=(r // per_step,),
            in_specs=in_specs,
            out_specs=out_specs,
            scratch_shapes=[pltpu.VMEM(((per_step,) if any(c is v for v in value_side) else ())
                                       + (n_pages,) + c.shape[1:], c.dtype) for c in caches]
                           + [pltpu.SemaphoreType.DMA((1 + per_step,)),
                              pltpu.VMEM((n_maps, page_size), F32)]),
        compiler_params=_params(("arbitrary",), DECODE_VMEM_LIMIT),
        name="decode_attention",
    )(page_table, *operands)


def _merge_rows(x, od_ref, of_ref, g1_ref, wg_ref, gb_ref, wa_ref, wb_ref, wo_ref):
    d = x.shape[-1]
    dt = wg_ref.dtype
    h = _rms(x, g1_ref[...]).astype(dt)
    ya = _dot(od_ref[...].astype(dt), wa_ref[...])
    yb = _dot(of_ref[...].astype(dt), wb_ref[...])
    gate_a = jax.nn.sigmoid(_dot(h, wg_ref[:, :d]) + gb_ref[:, :d])
    gate_b = jax.nn.sigmoid(_dot(h, wg_ref[:, d:]) + gb_ref[:, d:])
    merged = gate_a * ya + gate_b * yb
    return x + _dot(merged.astype(dt), wo_ref[...])


def _ffn_hidden(x, g2_ref, wgu_ref, wdn_ref):
    d_ff = wdn_ref.shape[0]
    h = _rms(x, g2_ref[...]).astype(wgu_ref.dtype)
    gate = _dot(h, wgu_ref[:, :d_ff])
    up = _dot(h, wgu_ref[:, d_ff:])
    return (gate * jax.nn.sigmoid(gate) * up).astype(wdn_ref.dtype)


N_MERGE_REFS = 9


def _tail_first(*refs):
    x_ref, *merge_refs = refs[:N_MERGE_REFS]
    return _merge_rows(x_ref[...], *merge_refs)


def _tail_second(x1, *refs):
    *ffn_refs, o_ref = refs[N_MERGE_REFS:]
    o_ref[...] = x1 + _dot(_ffn_hidden(x1, *ffn_refs), ffn_refs[-1][...])


def _tail_kernel(*refs):
    in_refs = refs[:-1]
    _tail_second(_tail_first(*in_refs), *refs)


def _tail(x, od, of, weights):
    m, d = x.shape
    tm = min(ROW_TILE, m)
    assert m % tm == 0
    row = lambda c: pl.BlockSpec((tm, c), lambda i: (i, 0))
    return pl.pallas_call(
        _tail_kernel,
        out_shape=jax.ShapeDtypeStruct((m, d), F32),
        grid=(m // tm,),
        in_specs=[row(d), row(od.shape[1]), row(of.shape[1])] + [_resident(w.shape) for w in weights],
        out_specs=row(d),
        compiler_params=_params(("arbitrary",)),
        name="tail",
    )(x, od, of, *weights)


def kernel(x_prompt, x_sample, cache_diff_k, cache_diff_v, cache_fox_k, cache_fox_v, cache_fox_logf,
           page_table, rel_bias, norm1_g, w_in, diff_q_g, diff_k_g, fox_q_g, fox_k_g, diff_lambda,
           fox_f_b, gate_b, diff_subln_g, w_branch_a, w_branch_b, w_out, norm2_g, w_gate_up, w_down):
    depth = w_in.shape[0]
    batch, seq, d = x_prompt.shape
    dec_batch, dec_seq, _ = x_sample.shape
    assert dec_seq == 1
    _, n_phys, page_size, n_dh, _, head_dim = cache_diff_k.shape
    n_f = cache_fox_k.shape[3]
    e = 2 * head_dim
    width = n_dh * e
    assert n_f * head_dim == width

    rb_flat = rel_bias.reshape(-1)
    rbt = jnp.repeat(rel_bias.T, 2, axis=0)

    yp, ys = x_prompt, x_sample.reshape(1, dec_batch, d)
    rows_p, rows_s = [], []
    for l in range(depth):
        lambda_init = 0.8 - 0.6 * math.exp(-0.3 * l)
        w_t = w_in[l].T
        fb = fox_f_b[l].reshape(n_f, 1)
        gains = jnp.stack([jnp.tile(g[l], width // head_dim)
                           for g in (diff_q_g, diff_k_g, fox_q_g, fox_k_g)]).reshape(4, width, 1)
        g1 = norm1_g[l].reshape(1, d)
        g2 = norm2_g[l].reshape(1, d)
        gb = gate_b[l].reshape(1, 2 * d)
        subg = diff_subln_g[l].reshape(1, e)
        wg = w_t[6 * width + n_f:].T
        merge_w = (g1, wg.astype(BF16), gb, w_branch_a[l].astype(BF16), w_branch_b[l].astype(BF16),
                   w_out[l].astype(BF16))
        ffn_w = (g2, w_gate_up[l].astype(BF16), w_down[l].astype(BF16))
        lam_p = diff_lambda[l]

        def cache_rows(kd_t, vd, kf_t, vf_t, logf_t):
            b, _, s = kd_t.shape
            return (jnp.transpose(kd_t.reshape(b, n_dh, 2, head_dim, s), (0, 4, 1, 2, 3)),
                    vd.reshape(b, s, n_dh, e),
                    jnp.transpose(kf_t.reshape(b, n_f, head_dim, s), (0, 3, 1, 2)),
                    jnp.transpose(vf_t.reshape(b, n_f, head_dim, s), (0, 3, 1, 2)),
                    jnp.transpose(logf_t, (0, 2, 1)))

        qd_t, kd_t, vd, vd_t, qf_t, kf_t, vf_t, logf_t = _proj(
            yp, g1, w_t, gains, fb, width=width, head_dim=head_dim, with_rows=False)
        od = _diff_attention(rb_flat, qd_t, kd_t, vd_t, lam_p, subg,
                             n_heads=n_dh, head_dim=head_dim, lambda_init=lambda_init)
        of = _fox_attention(qf_t, kf_t, vf_t, logf_t, head_dim=head_dim)
        m_p = batch * seq
        assert m_p % dec_batch == 0
        rows = lambda c: pl.BlockSpec((REQUESTS_PER_STEP * (m_p // dec_batch), c), lambda i, pt: (i, 0))
        tiled = (yp.reshape(m_p, d), od.reshape(m_p, width), of.reshape(m_p, width))
        weights = merge_w + ffn_w
        assert len(tiled) + len(merge_w) == N_MERGE_REFS
        rider = _Rider(_tail_first, _tail_second, len(tiled), tiled + weights,
                       tuple(rows(a.shape[1]) for a in tiled) + tuple(_resident(w.shape) for w in weights),
                       jax.ShapeDtypeStruct((m_p, d), F32), rows(d))
        rows_p.append(cache_rows(kd_t, vd, kf_t, vf_t, logf_t))

        (qd_t, kd_t, vd, _, qf_t, kf_t, vf_t, logf_t, qd_r, kd_r, qf_r, kf_r, vf_r) = _proj(
            ys, g1, w_t, gains, fb, width=width, head_dim=head_dim, with_rows=True)
        as_row = lambda a: a.reshape(dec_batch, width)
        cdk = jnp.transpose(cache_diff_k[l], (0, 2, 3, 4, 1)).reshape(n_phys, width, page_size)
        cdv = cache_diff_v[l].reshape(n_phys, page_size * n_dh, e)
        cfk = jnp.transpose(cache_fox_k[l], (0, 2, 3, 1)).reshape(n_phys, width, page_size)
        cfv = jnp.transpose(cache_fox_v[l], (0, 2, 3, 1)).reshape(n_phys, width, page_size)
        clf = jnp.transpose(cache_fox_logf[l], (0, 2, 1))
        od, of, yp = _decode_requests(
            page_table, as_row(qd_r), as_row(kd_r), vd.reshape(dec_batch * n_dh, e),
            as_row(qf_r), as_row(kf_r), as_row(vf_r), logf_t, rbt, lam_p, subg,
            cdk, cdv, cfk, cfv, clf, head_dim=head_dim, lambda_init=lambda_init, rider=rider)
        yp = yp.reshape(batch, seq, d)
        ys = _tail(ys.reshape(dec_batch, d), od, of, weights).reshape(1, dec_batch, d)
        rows_s.append(tuple(jnp.swapaxes(a, 0, 1) for a in cache_rows(kd_t, vd, kf_t, vf_t, logf_t)))

    stack = lambda rows, i: jnp.stack([r[i] for r in rows], axis=0)
    return (yp, ys.reshape(dec_batch, 1, d),
            *(stack(rows_p, i) for i in range(5)), *(stack(rows_s, i) for i in range(5)))
```

```python
import functools
import math
from typing import Any, NamedTuple

import jax
import jax.numpy as jnp
from jax import lax
from jax.experimental import pallas as pl
from jax.experimental.pallas import tpu as pltpu

F32 = jnp.float32
BF16 = jnp.bfloat16

N_BUCKETS = 32
MAX_DISTANCE = 128
EPS = 1e-6
NEG_INF = -1e30
LOG2E = math.log2(math.e)

LANES = 128
SUBLANES = 8
BF16_ROWS = 16
VMEM_LIMIT = 56 * 1024 * 1024
DECODE_VMEM_LIMIT = 60 * 1024 * 1024

ROW_TILE = 512
ATTN_TQ = 256
ATTN_TK = 256
FOX_PAIRS_PER_STEP = 2
DIFF_HEADS_PER_STEP = 2
DECAY_PARTS = 3

NT_DIMS = (((1,), (1,)), ((), ()))
HIGHEST = lax.Precision.HIGHEST


def _resident(shape):
    zeros = (0,) * len(shape)
    return pl.BlockSpec(shape, lambda *_: zeros, pipeline_mode=pl.Buffered(1))


def _params(semantics, vmem=VMEM_LIMIT):
    return pltpu.CompilerParams(dimension_semantics=semantics, vmem_limit_bytes=vmem)


def _rms(x, g):
    return x * lax.rsqrt(jnp.mean(x * x, axis=-1, keepdims=True) + EPS) * g


def _dot(a, b, precision=None):
    return jnp.dot(a, b, preferred_element_type=F32, precision=precision)


def _dot_nt(a, b):
    return lax.dot_general(a, b, NT_DIMS, preferred_element_type=F32)


def _log_sigmoid(x):
    return jnp.minimum(x, 0.0) - jnp.log1p(jnp.exp(-jnp.abs(x)))


def _rel_bucket(n):
    max_exact = N_BUCKETS // 2
    nf = jnp.maximum(n, 1).astype(F32)
    large = max_exact + (jnp.log(nf / max_exact) / math.log(MAX_DISTANCE / max_exact)
                         * (N_BUCKETS - max_exact)).astype(jnp.int32)
    return jnp.where(n < max_exact, n, jnp.minimum(large, N_BUCKETS - 1))


def _lambda_value(lam_ref, lambda_init):
    lp = lam_ref[...]
    a = jnp.sum(lp[0:1] * lp[1:2], axis=-1, keepdims=True)
    b = jnp.sum(lp[2:3] * lp[3:4], axis=-1, keepdims=True)
    return jnp.exp(a) - jnp.exp(b) + lambda_init


def _proj_kernel(x_ref, g1_ref, wt_ref, gains_ref, fb_ref,
                 qd_ref, kd_ref, vd_ref, vdt_ref, qf_ref, kf_ref, vf_ref, logf_ref, *row_refs,
                 head_dim, q_scale):
    h = _rms(x_ref[0], g1_ref[...])
    width = kd_ref.shape[1]
    n_heads = width // head_dim
    n_f = logf_ref.shape[1]
    section = lambda i: _dot_nt(wt_ref[i * width:(i + 1) * width, :], h)

    def head_norm_t(z, i):
        z3 = z.reshape(n_heads, head_dim, z.shape[-1])
        ss = jnp.sum(z3 * z3, axis=1, keepdims=True)
        gain = gains_ref[i].reshape(n_heads, head_dim, 1)
        return (z3 * lax.rsqrt(ss * (1.0 / head_dim) + EPS) * gain).reshape(z.shape)

    qd = head_norm_t(section(0), 0) * q_scale
    kd = head_norm_t(section(1), 1)
    qf = head_norm_t(section(3), 2) * q_scale
    kf = head_norm_t(section(4), 3)
    vd_t = section(2)
    vd_rows = vd_t.T
    vf = section(5)
    qd_ref[0] = qd.astype(qd_ref.dtype)
    kd_ref[0] = kd
    e = vd_ref.shape[-1]
    for hd in range(width // e):
        vd_ref[0, pl.ds(hd, vd_rows.shape[0], stride=width // e), :] = vd_rows[:, hd * e:(hd + 1) * e]
    vdt_ref[0] = vd_t.astype(vdt_ref.dtype)
    qf_ref[0] = qf.astype(qf_ref.dtype)
    kf_ref[0] = kf
    vf_ref[0] = vf
    logf_ref[0] = _log_sigmoid(_dot_nt(wt_ref[6 * width:6 * width + n_f, :], h) + fb_ref[...])
    if row_refs:
        for ref, val in zip(row_refs, (qd, kd, qf, kf, vf)):
            ref[0] = val.T


def _proj(x, g1, wt, gains, fb, *, width, head_dim, with_rows):
    b, s, d = x.shape
    n_f = fb.shape[0]
    e = 2 * head_dim
    tm = min(ROW_TILE, s)
    assert s % tm == 0 and tm % LANES == 0 and width % e == 0
    t_blk = lambda r: pl.BlockSpec((1, r, tm), lambda bi, si: (bi, 0, si))
    r_blk = lambda c: pl.BlockSpec((1, tm, c), lambda bi, si: (bi, si, 0))
    t_shape = lambda dt: jax.ShapeDtypeStruct((b, width, s), dt)
    r_shape = jax.ShapeDtypeStruct((b, s, width), F32)
    heads_per_token = width // e
    out_shape = [t_shape(BF16), t_shape(F32), jax.ShapeDtypeStruct((b, s * heads_per_token, e), F32),
                 t_shape(BF16), t_shape(BF16), t_shape(F32), t_shape(F32),
                 jax.ShapeDtypeStruct((b, n_f, s), F32)]
    out_specs = [t_blk(width), t_blk(width),
                 pl.BlockSpec((1, tm * heads_per_token, e), lambda bi, si: (bi, si, 0)),
                 t_blk(width), t_blk(width), t_blk(width), t_blk(width), t_blk(n_f)]
    if with_rows:
        out_shape += [r_shape] * 5
        out_specs += [r_blk(width)] * 5
    return pl.pallas_call(
        functools.partial(_proj_kernel, head_dim=head_dim, q_scale=head_dim ** -0.5 * LOG2E),
        out_shape=out_shape,
        grid=(b, s // tm),
        in_specs=[r_blk(d)] + [_resident(a.shape) for a in (g1, wt, gains, fb)],
        out_specs=out_specs,
        compiler_params=_params(("arbitrary", "arbitrary")),
        name="proj",
    )(x, g1, wt, gains, fb)


def _flash_step(scores, values, state):
    new_state = []
    for s_t, v_t, (m_prev, acc_prev) in zip(scores, values, state):
        m_new = jnp.maximum(m_prev, jnp.max(s_t, axis=0, keepdims=True))
        alpha = jnp.exp2(m_prev - m_new)
        p_t = jnp.exp2(s_t - m_new).astype(BF16)
        new_state.append((m_new, alpha * acc_prev + _dot(v_t, p_t)))
    return tuple(new_state)


def _flash_init(n_slots, rows, tq):
    return tuple((jnp.full((1, tq), NEG_INF, F32), jnp.zeros((rows, tq), F32)) for _ in range(n_slots))


def _ones_rows(seq):
    return jnp.where(lax.broadcasted_iota(jnp.int32, (BF16_ROWS, seq), 0) == 0, 1.0, 0.0).astype(BF16)


def _key_minus_query(tk, tq):
    return lax.broadcasted_iota(jnp.int32, (tk, tq), 0) - lax.broadcasted_iota(jnp.int32, (tk, tq), 1)


class _ScoreRing:
    def __init__(self, s_ref, qk):
        self.s_ref, self.qk = s_ref, qk

    def fill(self, j, slot):
        for idx, s in enumerate(self.qk(j)):
            self.s_ref[slot, idx] = s

    def scores(self, slot):
        return [self.s_ref[slot, idx] for idx in range(self.s_ref.shape[1])]

    def sweep(self, softmax, state, kinds):
        self.fill(0, 0)
        for j, kind in enumerate(kinds):
            slot = j % 2
            if j + 1 < len(kinds):
                self.fill(j + 1, 1 - slot)
            state = softmax(self.scores(slot), j, state, kind)
        return state


def _diff_attn_kernel(rb_ref, q_ref, k_ref, v_ref, lam_ref, subg_ref, o_ref,
                      kb_ref, vb_ref, bias_ref, s_ref, *, n_heads, head_dim, lambda_init):
    tq, tk = ATTN_TQ, ATTN_TK
    ratio = tq // tk
    n_hd = vb_ref.shape[0]
    head0 = n_hd * pl.program_id(0)
    seq = q_ref.shape[2]
    e = 2 * head_dim
    kmq = _key_minus_query(tk, tq)
    near = [tk] + [-d * tk for d in range(ratio)]

    @pl.when(pl.program_id(1) == 0)
    def _():
        for hd in range(n_hd):
            far = rb_ref[(N_BUCKETS - 1) * n_heads + head0 + hd]
            for idx, delta in enumerate(near):
                bucket = _rel_bucket(jnp.maximum(delta - kmq, 0))
                val = jnp.zeros((tk, tq), F32)
                for b in range(N_BUCKETS):
                    val = jnp.where(bucket == b, rb_ref[b * n_heads + head0 + hd], val)
                bias_ref[hd, idx] = (val - far) * LOG2E

    first = lax.broadcasted_iota(jnp.int32, (1, e), 1) < head_dim
    for hd in range(n_hd):
        k_rows = k_ref[0, hd * e:(hd + 1) * e, :].T
        kb_ref[2 * hd] = jnp.where(first, k_rows, 0.0).astype(BF16)
        kb_ref[2 * hd + 1] = jnp.where(first, 0.0, k_rows).astype(BF16)
        vb_ref[hd, 0:e, :] = v_ref[0, hd * e:(hd + 1) * e, :]
        vb_ref[hd, e:, :] = _ones_rows(seq)
    lam = _lambda_value(lam_ref, lambda_init)

    def q_block(i, carry):
        qs = i * tq
        q_t = [q_ref[0, hd * e:(hd + 1) * e, pl.ds(qs, tq)] for hd in range(n_hd)]

        def qk(j):
            ks = j * tk
            return [_dot(kb_ref[mp, pl.ds(ks, tk), :], q_t[mp // 2]) for mp in range(2 * n_hd)]

        def softmax(scores, j, state, near_idx=None):
            if near_idx is not None:
                scores = [s + bias_ref[mp // 2, near_idx] for mp, s in enumerate(scores)]
                if near[near_idx] <= 0:
                    visible = kmq <= near[near_idx]
                    scores = [jnp.where(visible, s, NEG_INF) for s in scores]
            ks = j * tk
            values = [vb_ref[mp // 2, :, pl.ds(ks, tk)] for mp in range(2 * n_hd)]
            return _flash_step(scores, values, state)

        ring = _ScoreRing(s_ref, qk)
        kinds = ([None] * (ratio * i - 1) + [0] if i >= 1 else []) + [1 + d for d in range(ratio)]
        state = ring.sweep(softmax, _flash_init(2 * n_hd, vb_ref.shape[1], tq), kinds)
        outs = []
        for hd in range(n_hd):
            (_, acc0), (_, acc1) = state[2 * hd:2 * hd + 2]
            o_t = acc0[0:e] / acc0[e:e + 1] - lam * (acc1[0:e] / acc1[e:e + 1])
            outs.append(_rms(o_t.T, subg_ref[...]) * (1.0 - lambda_init))
        o_ref[0, pl.ds(qs, tq), :] = jnp.concatenate(outs, axis=1).astype(o_ref.dtype)
        return carry

    for i in range(seq // tq):
        q_block(i, 0)


def _diff_attention(rb_flat, q_t, k_t, v, lam_p, subg, *, n_heads, head_dim, lambda_init):
    b, w, s = q_t.shape
    e = 2 * head_dim
    tq, tk = ATTN_TQ, ATTN_TK
    assert s % tq == 0 and tq % tk == 0 and tk % LANES == 0 and tk >= MAX_DISTANCE and e == LANES
    n_hd = DIFF_HEADS_PER_STEP
    assert n_heads % n_hd == 0
    t_blk = pl.BlockSpec((1, n_hd * e, s), lambda g, bi, *_: (bi, g, 0))
    r_blk = pl.BlockSpec((1, s, n_hd * e), lambda g, bi, *_: (bi, 0, g))
    return pl.pallas_call(
        functools.partial(_diff_attn_kernel, n_heads=n_heads, head_dim=head_dim, lambda_init=lambda_init),
        out_shape=jax.ShapeDtypeStruct((b, s, w), BF16),
        grid_spec=pltpu.PrefetchScalarGridSpec(
            num_scalar_prefetch=1,
            grid=(n_heads // n_hd, b),
            in_specs=[t_blk, t_blk, t_blk, _resident(lam_p.shape), _resident(subg.shape)],
            out_specs=r_blk,
            scratch_shapes=[pltpu.VMEM((2 * n_hd, s, e), BF16), pltpu.VMEM((n_hd, e + BF16_ROWS, s), BF16),
                            pltpu.VMEM((n_hd, tq // tk + 1, tk, tq), F32),
                            pltpu.VMEM((2, 2 * n_hd, tk, tq), F32)]),
        compiler_params=_params(("arbitrary", "arbitrary")),
        name="diff_attention",
    )(rb_flat, q_t, k_t, v, lam_p, subg)


def _fox_attn_kernel(q_ref, k_ref, v_ref, logf_ref, o_ref, kb_ref, vb_ref, c_ref, s_ref, *, head_dim):
    tq, tk = ATTN_TQ, ATTN_TK
    ratio = tq // tk
    group = pl.program_id(1)
    seq = q_ref.shape[2]
    e = 2 * head_dim
    n_h = kb_ref.shape[0]
    kmq = _key_minus_query(tk, tq)

    @pl.when(group == 0)
    def _():
        upper = (lax.broadcasted_iota(jnp.int32, (tk, tk), 0)
                 <= lax.broadcasted_iota(jnp.int32, (tk, tk), 1)).astype(F32)
        carry = jnp.zeros((logf_ref.shape[1], 1), F32)
        for blk in range(seq // tk):
            c = _dot(logf_ref[0, :, blk * tk:(blk + 1) * tk], upper, HIGHEST) + carry
            c_ref[:, blk * tk:(blk + 1) * tk] = c
            carry = c[:, tk - 1:tk]

    lane = lax.broadcasted_iota(jnp.int32, (1, e), 1)
    for hh in range(n_h):
        pair, half = divmod(hh, 2)
        k_rows = k_ref[0, pair * e:(pair + 1) * e, :].T
        spare = (1 - half) * head_dim
        rest = jnp.broadcast_to(c_ref[pl.ds(n_h * group + hh, 1), :] * LOG2E, (LANES, seq)).T
        k_aug = jnp.where(lane // head_dim == half, k_rows, 0.0)
        for part in range(DECAY_PARTS):
            piece = rest.astype(BF16).astype(F32)
            k_aug = jnp.where(lane == spare + part, piece, k_aug)
            rest = rest - piece
        kb_ref[hh] = k_aug.astype(BF16)
        vb_ref[hh, 0:head_dim, :] = v_ref[0, hh * head_dim:(hh + 1) * head_dim, :].astype(BF16)
        vb_ref[hh, head_dim:, :] = _ones_rows(seq)

    def q_block(i, carry):
        qs = i * tq
        row = lax.broadcasted_iota(jnp.int32, (e, 1), 0)
        q_aug = []
        for hh in range(n_h):
            pair, half = divmod(hh, 2)
            q_t = q_ref[0, pair * e:(pair + 1) * e, pl.ds(qs, tq)]
            spare = (1 - half) * head_dim
            minus_one = jnp.logical_and(row >= spare, row < spare + DECAY_PARTS)
            q_aug.append(jnp.where(row // head_dim == half, q_t,
                                   jnp.where(minus_one, -1.0, 0.0).astype(BF16)))

        def qk(j):
            ks = j * tk
            return [_dot(kb_ref[hh, pl.ds(ks, tk), :], q_aug[hh]) for hh in range(n_h)]

        def softmax(scores, j, state, diagonal=None):
            ks = j * tk
            if diagonal is not None:
                visible = kmq <= -diagonal * tk
                scores = [jnp.where(visible, s, NEG_INF) for s in scores]
            values = [vb_ref[hh, :, pl.ds(ks, tk)] for hh in range(n_h)]
            return _flash_step(scores, values, state)

        ring = _ScoreRing(s_ref, qk)
        kinds = [None] * (ratio * i) + list(range(ratio))
        state = ring.sweep(softmax, _flash_init(n_h, vb_ref.shape[1], tq), kinds)
        o_t = jnp.concatenate([acc[0:head_dim] / acc[head_dim:head_dim + 1] for _, acc in state], axis=0)
        o_ref[0, pl.ds(qs, tq), :] = o_t.T.astype(o_ref.dtype)
        return carry

    for i in range(seq // tq):
        q_block(i, 0)


def _fox_attention(q_t, k_t, v_t, logf_t, *, head_dim):
    b, w, s = q_t.shape
    e = 2 * head_dim
    tq, tk = ATTN_TQ, ATTN_TK
    n_f = logf_t.shape[1]
    assert s % tq == 0 and tq % tk == 0 and tk % LANES == 0 and e == LANES and w // e * 2 == n_f
    assert DECAY_PARTS <= head_dim
    cols = FOX_PAIRS_PER_STEP * e
    n_h = 2 * FOX_PAIRS_PER_STEP
    assert w % cols == 0
    t_blk = pl.BlockSpec((1, cols, s), lambda bi, g: (bi, g, 0))
    return pl.pallas_call(
        functools.partial(_fox_attn_kernel, head_dim=head_dim),
        out_shape=jax.ShapeDtypeStruct((b, s, w), BF16),
        grid=(b, w // cols),
        in_specs=[t_blk, t_blk, t_blk, pl.BlockSpec((1, n_f, s), lambda bi, g: (bi, 0, 0))],
        out_specs=pl.BlockSpec((1, s, cols), lambda bi, g: (bi, 0, g)),
        scratch_shapes=[pltpu.VMEM((n_h, s, e), BF16), pltpu.VMEM((n_h, head_dim + BF16_ROWS, s), BF16),
                        pltpu.VMEM((n_f, s), F32), pltpu.VMEM((2, n_h, tk, tq), F32)],
        compiler_params=_params(("arbitrary", "arbitrary")),
        name="fox_attention",
    )(q_t, k_t, v_t, logf_t)


class _Rider(NamedTuple):
    first: Any
    second: Any
    n_tiled: int
    operands: tuple
    in_specs: tuple
    out_shape: Any
    out_spec: Any


REQUESTS_PER_STEP = 2


def _decode_step_kernel(pt_ref, qd_ref, kd_ref, vd_ref, qf_ref, kf_ref, vf_ref, lfn_ref,
                        rbt_ref, lam_ref, subg_ref, *rest,
                        n_pages, page_size, head_dim, lambda_init, rider, n_rider_in):
    rider_in, rest = rest[:n_rider_in], rest[n_rider_in:]
    (hbm_dk, hbm_dv, hbm_fk, hbm_fv, hbm_lf), rest = rest[:5], rest[5:]
    n_out = len(rest) - 7
    od_ref, of_ref, *rider_out = rest[:n_out]
    kd_buf, vd_buf, kf_buf, vf_buf, lf_buf, sems, bias_ref = rest[n_out:]
    step = pl.program_id(0)
    last = pl.num_programs(0) - 1
    n_maps = bias_ref.shape[0]
    key_side = ((hbm_dk, kd_buf), (hbm_fk, kf_buf), (hbm_lf, lf_buf)), sems.at[0]
    value_sides = [(((hbm_dv, vd_buf.at[slot]), (hbm_fv, vf_buf.at[slot])), sems.at[1 + slot])
                   for slot in range(REQUESTS_PER_STEP)]

    def copies(request, side):
        pairs, sem = side
        for p in range(n_pages):
            page = pt_ref[request, n_pages - 1 - p]
            for cache, buf in pairs:
                yield pltpu.make_async_copy(cache.at[page], buf.at[p], sem)

    def start(request, side):
        for copy in copies(request, side):
            copy.start()

    def wait(request, side):
        for copy in copies(request, side):
            copy.wait()

    @pl.when(step == 0)
    def _():
        start(0, key_side)
        start(0, value_sides[0])
        lane = lax.broadcasted_iota(jnp.int32, (n_maps, page_size), 1)
        bucket = _rel_bucket(page_size - lane)
        val = jnp.zeros((n_maps, page_size), F32)
        for b in range(N_BUCKETS):
            val = jnp.where(bucket == b, rbt_ref[:, b:b + 1], val)
        bias_ref[...] = (val - rbt_ref[:, N_BUCKETS - 1:N_BUCKETS]) * LOG2E

    rows = rider_out[0].shape[0] // REQUESTS_PER_STEP if rider is not None else 0
    for k in range(REQUESTS_PER_STEP):
        req = REQUESTS_PER_STEP * step + k
        wait(req, key_side)

        def scores_done(req=req, k=k):
            def fetch_next():
                start(req + 1, key_side)
                start(req + 1, value_sides[(k + 1) % REQUESTS_PER_STEP])
            if k + 1 < REQUESTS_PER_STEP:
                fetch_next()
            else:
                pl.when(step < last)(fetch_next)
            wait(req, value_sides[k])

        pages = [[buf.at[pl.ds(p, 1)] for p in range(n_pages)]
                 for buf in (kd_buf, vd_buf.at[k], kf_buf, vf_buf.at[k], lf_buf)]
        tile = lambda ref: ref.at[pl.ds(k * rows, rows)]
        n_tiled = rider.n_tiled if rider is not None else 0
        _decode_one_request(
            req, pages, scores_done, qd_ref, kd_ref, vd_ref, qf_ref, kf_ref, vf_ref, lfn_ref, rbt_ref,
            lam_ref, subg_ref, od_ref, of_ref, bias_ref,
            [tile(ref) for ref in rider_in[:n_tiled]] + list(rider_in[n_tiled:]),
            [tile(ref) for ref in rider_out],
            page_size=page_size, head_dim=head_dim, lambda_init=lambda_init, rider=rider)


def _decode_one_request(req, pages, scores_done, qd_ref, kd_ref, vd_ref, qf_ref, kf_ref, vf_ref,
                        lfn_ref, rbt_ref, lam_ref, subg_ref, od_ref, of_ref, bias_ref, rider_in,
                        rider_out, *, page_size, head_dim, lambda_init, rider):
    cdk, cdv, cfk, cfv, clf = pages
    n_pages = len(cdk)
    width = qd_ref.shape[-1]
    n_maps = width // head_dim
    n_dh = n_maps // 2
    e = 2 * head_dim
    sub = req % qd_ref.shape[0]
    new_row = lambda ref: ref[pl.ds(sub, 1), :]

    map_of_lane = lax.broadcasted_iota(jnp.int32, (n_maps, width), 1) // head_dim
    row = lax.broadcasted_iota(jnp.int32, (n_maps, width), 0)
    own = map_of_lane == row
    row_e = lax.broadcasted_iota(jnp.int32, (n_maps, e), 0)
    far_bias = rbt_ref[:, N_BUCKETS - 1:N_BUCKETS]

    qd = jnp.where(own, new_row(qd_ref), 0.0).astype(BF16)
    qf = jnp.where(own, new_row(qf_ref), 0.0).astype(BF16)

    def self_score(q, k_ref):
        k = new_row(k_ref).astype(BF16).astype(F32)
        return jnp.sum(q.astype(F32) * k, axis=-1, keepdims=True)

    def softmax(s, s_self):
        m = jnp.maximum(jnp.max(s, axis=-1, keepdims=True), s_self)
        p, p_self = jnp.exp2(s - m), jnp.exp2(s_self - m)
        return p, p_self, jnp.sum(p, axis=-1, keepdims=True) + p_self

    def keys_side_by_side(refs):
        return jnp.concatenate([ref[0].astype(BF16) for ref in refs], axis=1)

    bias = jnp.concatenate([bias_ref[...], jnp.zeros((n_maps, (n_pages - 1) * page_size), F32)], axis=1)
    s_d = _dot(qd, keys_side_by_side(cdk)) + bias
    later = (lax.broadcasted_iota(jnp.int32, (page_size, page_size), 0)
             >= lax.broadcasted_iota(jnp.int32, (page_size, page_size), 1)).astype(F32)
    lane = lax.broadcasted_iota(jnp.int32, lfn_ref.shape[1:], 1)
    carry = jnp.sum(jnp.where(lane == req, lfn_ref[0], 0.0), axis=1, keepdims=True) * LOG2E
    lf = jnp.concatenate([ref[0] for ref in clf], axis=0) * LOG2E
    incl = _dot(lf, later, HIGHEST)
    decay = []
    for p_idx in range(n_pages):
        rows = slice(p_idx * n_maps, (p_idx + 1) * n_maps)
        decay.append(incl[rows] - lf[rows] + carry)
        carry = carry + incl[rows, 0:1]
    s_f = _dot(qf, keys_side_by_side(cfk)) + jnp.concatenate(decay, axis=1)
    scores_done()
    if rider is not None:
        rider.second(rider.first(*rider_in), *rider_in, *rider_out)

    p, p_self, l = softmax(s_d, self_score(qd, kd_ref) + (rbt_ref[:, 0:1] - far_bias) * LOG2E)
    p = p.astype(BF16)
    v_new = vd_ref[pl.ds(sub * n_dh, n_dh), :]
    acc = jnp.zeros((n_maps, e), F32)
    for h in range(n_dh):
        v_h = jnp.concatenate([ref[0, pl.ds(h, page_size, stride=n_dh), :].astype(BF16) for ref in cdv],
                              axis=0)
        acc = jnp.where(row_e // 2 == h, _dot(p, v_h) + p_self * v_new[h:h + 1, :], acc)
    lam = _lambda_value(lam_ref, lambda_init)
    sign = jnp.where(row_e[:, 0:1] % 2 == 0, 1.0, -lam)
    pick = (lax.broadcasted_iota(jnp.int32, (n_dh, n_maps), 1) // 2
            == lax.broadcasted_iota(jnp.int32, (n_dh, n_maps), 0)).astype(F32)
    scaled = acc / l * sign

    p, p_self, l = softmax(s_f, self_score(qf, kf_ref))
    acc = jnp.zeros((n_maps, head_dim, page_size), F32)
    for p_idx in range(n_pages):
        p_page = p[:, p_idx * page_size:(p_idx + 1) * page_size]
        acc = acc + p_page[:, None, :] * cfv[p_idx][0].reshape(n_maps, head_dim, page_size)

    def spread(col):
        return jnp.sum(jnp.where(own, col, 0.0), axis=0, keepdims=True)
    past = jnp.sum(acc.reshape(width, page_size).T, axis=0, keepdims=True)
    of_ref[pl.ds(sub, 1), :] = (past + spread(p_self) * new_row(vf_ref)) / spread(l)

    o_d = _dot(pick, scaled, HIGHEST)
    o_d = _rms(o_d, subg_ref[...]) * (1.0 - lambda_init)
    od_ref[pl.ds(sub, 1), :] = jnp.concatenate([o_d[h:h + 1] for h in range(n_dh)], axis=1)


def _decode_requests(page_table, qd, kd, vd, qf, kf, vf, logf_new, rbt, lam_p, subg,
                     cdk, cdv, cfk, cfv, clf, *, head_dim, lambda_init, rider=None):
    r, n_pages = page_table.shape
    _, width, page_size = cdk.shape
    n_f = clf.shape[1]
    n_maps = width // head_dim
    n_dh = n_maps // 2
    e = 2 * head_dim
    per_step = REQUESTS_PER_STEP
    assert n_maps == n_f and r % SUBLANES == 0 and SUBLANES % per_step == 0
    assert page_size >= MAX_DISTANCE
    steps_per_block = SUBLANES // per_step
    row = pl.BlockSpec((SUBLANES, width), lambda i, pt: (i // steps_per_block, 0))
    head_rows = pl.BlockSpec((SUBLANES * n_dh, e), lambda i, pt: (i // steps_per_block, 0))
    caches = [cdk, cdv, cfk, cfv, clf]
    value_side = (cdv, cfv)

    in_specs = [row, row, head_rows, row, row, row, _resident(logf_new.shape),
                _resident(rbt.shape), _resident(lam_p.shape), _resident(subg.shape)]
    operands = [qd, kd, vd, qf, kf, vf, logf_new, rbt, lam_p, subg]
    out_shape = [jax.ShapeDtypeStruct((r, width), F32)] * 2
    out_specs = [row, row]
    if rider is not None:
        in_specs += list(rider.in_specs)
        operands += list(rider.operands)
        out_shape.append(rider.out_shape)
        out_specs.append(rider.out_spec)
    in_specs += [pl.BlockSpec(memory_space=pl.ANY)] * len(caches)
    operands += caches
    return pl.pallas_call(
        functools.partial(_decode_step_kernel, n_pages=n_pages, page_size=page_size,
                          head_dim=head_dim, lambda_init=lambda_init,
                          rider=rider, n_rider_in=len(rider.operands) if rider else 0),
        out_shape=out_shape,
        grid_spec=pltpu.PrefetchScalarGridSpec(
            num_scalar_prefetch=1,
            grid=(r // per_step,),
            in_specs=in_specs,
            out_specs=out_specs,
            scratch_shapes=[pltpu.VMEM(((per_step,) if any(c is v for v in value_side) else ())
                                       + (n_pages,) + c.shape[1:], c.dtype) for c in caches]
                           + [pltpu.SemaphoreType.DMA((1 + per_step,)),
                              pltpu.VMEM((n_maps, page_size), F32)]),
        compiler_params=_params(("arbitrary",), DECODE_VMEM_LIMIT),
        name="decode_attention",
    )(page_table, *operands)


def _merge_rows(x, od_ref, of_ref, g1_ref, wg_ref, gb_ref, wa_ref, wb_ref, wo_ref):
    d = x.shape[-1]
    dt = wg_ref.dtype
    h = _rms(x, g1_ref[...]).astype(dt)
    ya = _dot(od_ref[...].astype(dt), wa_ref[...])
    yb = _dot(of_ref[...].astype(dt), wb_ref[...])
    gate_a = jax.nn.sigmoid(_dot(h, wg_ref[:, :d]) + gb_ref[:, :d])
    gate_b = jax.nn.sigmoid(_dot(h, wg_ref[:, d:]) + gb_ref[:, d:])
    merged = gate_a * ya + gate_b * yb
    return x + _dot(merged.astype(dt), wo_ref[...])


def _ffn_hidden(x, g2_ref, wgu_ref, wdn_ref):
    d_ff = wdn_ref.shape[0]
    h = _rms(x, g2_ref[...]).astype(wgu_ref.dtype)
    gate = _dot(h, wgu_ref[:, :d_ff])
    up = _dot(h, wgu_ref[:, d_ff:])
    return (gate * jax.nn.sigmoid(gate) * up).astype(wdn_ref.dtype)


N_MERGE_REFS = 9


def _tail_first(*refs):
    x_ref, *merge_refs = refs[:N_MERGE_REFS]
    return _merge_rows(x_ref[...], *merge_refs)


def _tail_second(x1, *refs):
    *ffn_refs, o_ref = refs[N_MERGE_REFS:]
    o_ref[...] = x1 + _dot(_ffn_hidden(x1, *ffn_refs), ffn_refs[-1][...])


def _tail_kernel(*refs):
    in_refs = refs[:-1]
    _tail_second(_tail_first(*in_refs), *refs)


def _tail(x, od, of, weights):
    m, d = x.shape
    tm = min(ROW_TILE, m)
    assert m % tm == 0
    row = lambda c: pl.BlockSpec((tm, c), lambda i: (i, 0))
    return pl.pallas_call(
        _tail_kernel,
        out_shape=jax.ShapeDtypeStruct((m, d), F32),
        grid=(m // tm,),
        in_specs=[row(d), row(od.shape[1]), row(of.shape[1])] + [_resident(w.shape) for w in weights],
        out_specs=row(d),
        compiler_params=_params(("arbitrary",)),
        name="tail",
    )(x, od, of, *weights)


def kernel(x_prompt, x_sample, cache_diff_k, cache_diff_v, cache_fox_k, cache_fox_v, cache_fox_logf,
           page_table, rel_bias, norm1_g, w_in, diff_q_g, diff_k_g, fox_q_g, fox_k_g, diff_lambda,
           fox_f_b, gate_b, diff_subln_g, w_branch_a, w_branch_b, w_out, norm2_g, w_gate_up, w_down):
    depth = w_in.shape[0]
    batch, seq, d = x_prompt.shape
    dec_batch, dec_seq, _ = x_sample.shape
    assert dec_seq == 1
    _, n_phys, page_size, n_dh, _, head_dim = cache_diff_k.shape
    n_f = cache_fox_k.shape[3]
    e = 2 * head_dim
    width = n_dh * e
    assert n_f * head_dim == width

    rb_flat = rel_bias.reshape(-1)
    rbt = jnp.repeat(rel_bias.T, 2, axis=0)

    yp, ys = x_prompt, x_sample.reshape(1, dec_batch, d)
    rows_p, rows_s = [], []
    for l in range(depth):
        lambda_init = 0.8 - 0.6 * math.exp(-0.3 * l)
        w_t = w_in[l].T
        fb = fox_f_b[l].reshape(n_f, 1)
        gains = jnp.stack([jnp.tile(g[l], width // head_dim)
                           for g in (diff_q_g, diff_k_g, fox_q_g, fox_k_g)]).reshape(4, width, 1)
        g1 = norm1_g[l].reshape(1, d)
        g2 = norm2_g[l].reshape(1, d)
        gb = gate_b[l].reshape(1, 2 * d)
        subg = diff_subln_g[l].reshape(1, e)
        wg = w_t[6 * width + n_f:].T
        merge_w = (g1, wg.astype(BF16), gb, w_branch_a[l].astype(BF16), w_branch_b[l].astype(BF16),
                   w_out[l].astype(BF16))
        ffn_w = (g2, w_gate_up[l].astype(BF16), w_down[l].astype(BF16))
        lam_p = diff_lambda[l]

        def cache_rows(kd_t, vd, kf_t, vf_t, logf_t):
            b, _, s = kd_t.shape
            return (jnp.transpose(kd_t.reshape(b, n_dh, 2, head_dim, s), (0, 4, 1, 2, 3)),
                    vd.reshape(b, s, n_dh, e),
                    jnp.transpose(kf_t.reshape(b, n_f, head_dim, s), (0, 3, 1, 2)),
                    jnp.transpose(vf_t.reshape(b, n_f, head_dim, s), (0, 3, 1, 2)),
                    jnp.transpose(logf_t, (0, 2, 1)))

        qd_t, kd_t, vd, vd_t, qf_t, kf_t, vf_t, logf_t = _proj(
            yp, g1, w_t, gains, fb, width=width, head_dim=head_dim, with_rows=False)
        od = _diff_attention(rb_flat, qd_t, kd_t, vd_t, lam_p, subg,
                             n_heads=n_dh, head_dim=head_dim, lambda_init=lambda_init)
        of = _fox_attention(qf_t, kf_t, vf_t, logf_t, head_dim=head_dim)
        m_p = batch * seq
        assert m_p % dec_batch == 0
        rows = lambda c: pl.BlockSpec((REQUESTS_PER_STEP * (m_p // dec_batch), c), lambda i, pt: (i, 0))
        tiled = (yp.reshape(m_p, d), od.reshape(m_p, width), of.reshape(m_p, width))
        weights = merge_w + ffn_w
        assert len(tiled) + len(merge_w) == N_MERGE_REFS
        rider = _Rider(_tail_first, _tail_second, len(tiled), tiled + weights,
                       tuple(rows(a.shape[1]) for a in tiled) + tuple(_resident(w.shape) for w in weights),
                       jax.ShapeDtypeStruct((m_p, d), F32), rows(d))
        rows_p.append(cache_rows(kd_t, vd, kf_t, vf_t, logf_t))

        (qd_t, kd_t, vd, _, qf_t, kf_t, vf_t, logf_t, qd_r, kd_r, qf_r, kf_r, vf_r) = _proj(
            ys, g1, w_t, gains, fb, width=width, head_dim=head_dim, with_rows=True)
        as_row = lambda a: a.reshape(dec_batch, width)
        cdk = jnp.transpose(cache_diff_k[l], (0, 2, 3, 4, 1)).reshape(n_phys, width, page_size)
        cdv = cache_diff_v[l].reshape(n_phys, page_size * n_dh, e)
        cfk = jnp.transpose(cache_fox_k[l], (0, 2, 3, 1)).reshape(n_phys, width, page_size)
        cfv = jnp.transpose(cache_fox_v[l], (0, 2, 3, 1)).reshape(n_phys, width, page_size)
        clf = jnp.transpose(cache_fox_logf[l], (0, 2, 1))
        od, of, yp = _decode_requests(
            page_table, as_row(qd_r), as_row(kd_r), vd.reshape(dec_batch * n_dh, e),
            as_row(qf_r), as_row(kf_r), as_row(vf_r), logf_t, rbt, lam_p, subg,
            cdk, cdv, cfk, cfv, clf, head_dim=head_dim, lambda_init=lambda_init, rider=rider)
        yp = yp.reshape(batch, seq, d)
        ys = _tail(ys.reshape(dec_batch, d), od, of, weights).reshape(1, dec_batch, d)
        rows_s.append(tuple(jnp.swapaxes(a, 0, 1) for a in cache_rows(kd_t, vd, kf_t, vf_t, logf_t)))

    stack = lambda rows, i: jnp.stack([r[i] for r in rows], axis=0)
    return (yp, ys.reshape(dec_batch, 1, d),
            *(stack(rows_p, i) for i in range(5)), *(stack(rows_s, i) for i in range(5)))
```
